```python
import jax
import jax.numpy as jnp
from jax import lax
import numpy as np

D_MODEL = 1024
BATCH = 8
SEQ = 2048
DEPTH = 1

HEAD_DIM = 64
A_HEADS = 8
A_KV_HEADS = 2
B_HEADS = 8
B_KV_HEADS = 2
WINDOW = 128
BLOCK = 128
GRID_W = 64
ROPE_THETA = 10000.0
N_EXPERTS = 32
TOP_K = 4
D_FF = 1024
MOE_BLOCK = 128
SWIGLU_LIMIT = 7.0
SWIGLU_ALPHA = 1.702
LN_EPS = 1e-5
RMS_EPS = 1e-6
NEG_INF = -1e30
DN_ALPHA = (2.0 * DEPTH) ** 0.25
DN_BETA = (8.0 * DEPTH) ** -0.25

A_Q = A_HEADS * HEAD_DIM
A_KV = A_KV_HEADS * HEAD_DIM
B_Q = B_HEADS * HEAD_DIM
B_KV = B_KV_HEADS * HEAD_DIM
IN_SPLITS = (A_Q, A_KV, A_KV, B_Q, B_KV, B_KV, D_MODEL, D_MODEL)
IN_WIDTH = sum(IN_SPLITS)
IN_OFFSETS = tuple(int(o) for o in np.cumsum(IN_SPLITS)[:-1])
ALIBI_SLOPES = tuple(2.0 ** (-8.0 * (h + 1) / A_HEADS) for h in range(A_HEADS))

kernel_name = "hybrid_window_grid_attn_moe_deepnorm"


def layer_norm(x, g, b):
    xf = x.astype(jnp.float32)
    mu = jnp.mean(xf, -1, keepdims=True)
    var = jnp.mean(jnp.square(xf - mu), -1, keepdims=True)
    y = (xf - mu) * lax.rsqrt(var + LN_EPS) * g.astype(jnp.float32) + b.astype(jnp.float32)
    return y.astype(x.dtype)


def rms_norm(x, g):
    xf = x.astype(jnp.float32)
    y = xf * lax.rsqrt(jnp.mean(xf * xf, -1, keepdims=True) + RMS_EPS) * g.astype(jnp.float32)
    return y.astype(x.dtype)


def rope_1d(xs, pos):
    dim = xs.shape[-1]
    half = dim // 2
    inv = ROPE_THETA ** (-jnp.arange(half, dtype=jnp.float32) * (2.0 / dim))
    ang = pos.astype(jnp.float32)[:, None] * inv[None, :]
    cos = jnp.cos(ang)[None, :, None, :]
    sin = jnp.sin(ang)[None, :, None, :]
    x1 = xs[..., :half].astype(jnp.float32)
    x2 = xs[..., half:].astype(jnp.float32)
    return jnp.concatenate([x1 * cos - x2 * sin, x2 * cos + x1 * sin], -1)


def axial_rope(x, row, col):
    half = x.shape[-1] // 2
    y = jnp.concatenate([rope_1d(x[..., :half], row), rope_1d(x[..., half:], col)], -1)
    return y.astype(x.dtype)


def window_attention(q, k, v, sink):
    bsz, seq, n_heads, hd = q.shape
    n_kv = k.shape[2]
    grp = n_heads // n_kv
    nb = seq // BLOCK
    qb = q.reshape(bsz, nb, BLOCK, n_kv, grp, hd)

    def band(t):
        tp = jnp.pad(t, ((0, 0), (BLOCK, BLOCK), (0, 0), (0, 0)))
        tp = tp.reshape(bsz, nb + 2, BLOCK, n_kv, hd)
        return jnp.concatenate([tp[:, :-2], tp[:, 1:-1], tp[:, 2:]], axis=2)

    kb = band(k)
    vb = band(v)
    s = jnp.einsum('bnqkgd,bnskd->bnkgqs', qb, kb).astype(jnp.float32) * (hd ** -0.5)
    blk = jnp.arange(nb)[:, None] * BLOCK
    q_pos = blk + jnp.arange(BLOCK)[None, :]
    k_pos = blk - BLOCK + jnp.arange(3 * BLOCK)[None, :]
    dist_i = jnp.abs(q_pos[:, :, None] - k_pos[:, None, :])
    valid = (dist_i <= WINDOW) & (k_pos[:, None, :] >= 0) & (k_pos[:, None, :] < seq)
    dist = dist_i.astype(jnp.float32)
    slopes = jnp.asarray(ALIBI_SLOPES, jnp.float32).reshape(n_kv, grp)
    bias = jnp.where(valid[:, None, None], -slopes[None, :, :, None, None] * dist[:, None, None], NEG_INF)
    s = s + bias[None]
    sk = sink.astype(jnp.float32).reshape(1, 1, n_kv, grp, 1, 1)
    m = jnp.maximum(jnp.max(s, -1, keepdims=True), sk)
    p = jnp.exp(s - m)
    p = p / (jnp.sum(p, -1, keepdims=True) + jnp.exp(sk - m))
    o = jnp.einsum('bnkgqs,bnskd->bnqkgd', p.astype(v.dtype), vb)
    return o.reshape(bsz, seq, n_heads * hd)


def grid_attention(q, k, v):
    bsz, seq, n_heads, hd = q.shape
    n_kv = k.shape[2]
    grp = n_heads // n_kv
    nb = seq // BLOCK
    q_blocks = jnp.moveaxis(q.reshape(bsz, nb, BLOCK, n_kv, grp, hd), 1, 0)

    def one_block(qi):
        s = jnp.einsum('bqkgd,bskd->bkgqs', qi, k).astype(jnp.float32) * (hd ** -0.5)
        p = jax.nn.softmax(s, axis=-1)
        return jnp.einsum('bkgqs,bskd->bqkgd', p.astype(v.dtype), v)

    o = lax.map(one_block, q_blocks)
    return jnp.moveaxis(o, 0, 1).reshape(bsz, seq, n_heads * hd)


def token_mixer(h, w_in, a_sink, b_q_norm, b_k_norm, w_branch_a, w_branch_b, w_out):
    bsz, seq, _ = h.shape
    qa, ka, va, qg, kg, vg, ga, gb = jnp.split(h @ w_in, IN_OFFSETS, axis=-1)

    def heads(t):
        return t.reshape(bsz, seq, -1, HEAD_DIM)

    out_a = window_attention(heads(qa), heads(ka), heads(va), a_sink) @ w_branch_a
    rows = seq // GRID_W
    t = jnp.arange(rows * GRID_W)
    row = t // GRID_W
    col = t % GRID_W
    q_b = axial_rope(rms_norm(heads(qg), b_q_norm), row, col)
    k_b = axial_rope(rms_norm(heads(kg), b_k_norm), row, col)
    out_b = grid_attention(q_b, k_b, heads(vg)) @ w_branch_b
    merged = jax.nn.sigmoid(ga) * out_a + jax.nn.sigmoid(gb) * out_b
    return merged @ w_out


def routed_experts(t, w_router, b_router, w_gate, b_gate, w_up, b_up, w_down, b_down):
    n_tok, d = t.shape
    logits = (t @ w_router + b_router).astype(jnp.float32)
    top_v, top_i = lax.top_k(logits, TOP_K)
    top_w = jax.nn.softmax(top_v, axis=-1)
    n_slots = n_tok * TOP_K
    flat_e = top_i.reshape(-1)
    flat_tok = jnp.arange(n_slots) // TOP_K
    flat_w = top_w.reshape(-1)
    order = jnp.argsort(flat_e)
    e_sorted = flat_e[order]
    counts = jnp.bincount(flat_e, length=N_EXPERTS)
    padded = (counts + MOE_BLOCK - 1) // MOE_BLOCK * MOE_BLOCK
    start = jnp.cumsum(counts) - counts
    pad_end = jnp.cumsum(padded)
    pad_start = pad_end - padded
    dest = pad_start[e_sorted] + (jnp.arange(n_slots) - start[e_sorted])
    n_rows = n_slots + N_EXPERTS * MOE_BLOCK
    n_blocks = n_rows // MOE_BLOCK
    row_tok = jnp.zeros((n_rows,), jnp.int32).at[dest].set(flat_tok[order].astype(jnp.int32))
    row_w = jnp.zeros((n_rows,), jnp.float32).at[dest].set(flat_w[order])
    block_e = jnp.minimum(jnp.searchsorted(pad_end, jnp.arange(n_blocks) * MOE_BLOCK, side='right'), N_EXPERTS - 1)
    xs = t[row_tok].reshape(n_blocks, MOE_BLOCK, d)

    def expert_block(args):
        xb, e = args
        g = xb @ w_gate[e] + b_gate[e]
        u = xb @ w_up[e] + b_up[e]
        g = jnp.minimum(g, SWIGLU_LIMIT)
        u = jnp.clip(u, -SWIGLU_LIMIT, SWIGLU_LIMIT)
        act = g * jax.nn.sigmoid(SWIGLU_ALPHA * g) * (u + 1.0)
        return act @ w_down[e] + b_down[e]

    ys = lax.map(expert_block, (xs, block_e)).reshape(n_rows, d)
    out = jax.ops.segment_sum(ys.astype(jnp.float32) * row_w[:, None], row_tok, num_segments=n_tok)
    return out.astype(t.dtype)


def setup_inputs(seed: int = 0) -> dict:
    key = jax.random.key(seed)
    ks = jax.random.split(key, 24)

    def nrm(k, shape, scale):
        return jax.random.normal(k, shape, jnp.float32) * scale

    col_scale = jnp.concatenate([
        jnp.ones((A_Q + A_KV,), jnp.float32), jnp.full((A_KV,), DN_BETA, jnp.float32),
        jnp.ones((B_Q + B_KV,), jnp.float32), jnp.full((B_KV,), DN_BETA, jnp.float32),
        jnp.ones((2 * D_MODEL,), jnp.float32)])
    return {
        "x": nrm(ks[0], (BATCH, SEQ, D_MODEL), 1.0),
        "ln0_g": 1.0 + nrm(ks[1], (D_MODEL,), 0.02),
        "ln0_b": nrm(ks[2], (D_MODEL,), 0.02),
        "w_in": nrm(ks[3], (DEPTH, D_MODEL, IN_WIDTH), D_MODEL ** -0.5) * col_scale,
        "a_sink": nrm(ks[4], (DEPTH, A_HEADS), 0.5),
        "b_q_norm": 1.0 + nrm(ks[5], (DEPTH, HEAD_DIM), 0.02),
        "b_k_norm": 1.0 + nrm(ks[6], (DEPTH, HEAD_DIM), 0.02),
        "w_branch_a": nrm(ks[7], (DEPTH, A_Q, D_MODEL), A_Q ** -0.5),
        "w_branch_b": nrm(ks[8], (DEPTH, B_Q, D_MODEL), B_Q ** -0.5),
        "w_out": nrm(ks[9], (DEPTH, D_MODEL, D_MODEL), D_MODEL ** -0.5 * DN_BETA),
        "ln1_g": 1.0 + nrm(ks[10], (DEPTH, D_MODEL), 0.02),
        "ln1_b": nrm(ks[11], (DEPTH, D_MODEL), 0.02),
        "w_router": nrm(ks[12], (DEPTH, D_MODEL, N_EXPERTS), D_MODEL ** -0.5),
        "b_router": nrm(ks[13], (DEPTH, N_EXPERTS), 0.01),
        "w_gate": nrm(ks[14], (DEPTH, N_EXPERTS, D_MODEL, D_FF), D_MODEL ** -0.5),
        "b_gate": nrm(ks[15], (DEPTH, N_EXPERTS, D_FF), 0.01),
        "w_up": nrm(ks[16], (DEPTH, N_EXPERTS, D_MODEL, D_FF), D_MODEL ** -0.5 * DN_BETA),
        "b_up": nrm(ks[17], (DEPTH, N_EXPERTS, D_FF), 0.01),
        "w_down": nrm(ks[18], (DEPTH, N_EXPERTS, D_FF, D_MODEL), D_FF ** -0.5 * DN_BETA),
        "b_down": nrm(ks[19], (DEPTH, N_EXPERTS, D_MODEL), 0.01),
        "ln2_g": 1.0 + nrm(ks[20], (DEPTH, D_MODEL), 0.02),
        "ln2_b": nrm(ks[21], (DEPTH, D_MODEL), 0.02),
    }


def reference(x, ln0_g, ln0_b, w_in, a_sink, b_q_norm, b_k_norm, w_branch_a, w_branch_b, w_out,
              ln1_g, ln1_b, w_router, b_router, w_gate, b_gate, w_up, b_up, w_down, b_down,
              ln2_g, ln2_b):
    h = layer_norm(x, ln0_g, ln0_b)
    bsz, seq, d = h.shape
    for l in range(DEPTH):
        mix = token_mixer(h, w_in[l], a_sink[l], b_q_norm[l], b_k_norm[l],
                          w_branch_a[l], w_branch_b[l], w_out[l])
        h = layer_norm(DN_ALPHA * h + mix, ln1_g[l], ln1_b[l])
        ffn = routed_experts(h.reshape(bsz * seq, d), w_router[l], b_router[l], w_gate[l], b_gate[l],
                             w_up[l], b_up[l], w_down[l], b_down[l]).reshape(bsz, seq, d)
        h = layer_norm(DN_ALPHA * h + ffn, ln2_g[l], ln2_b[l])
    return h
```

```python
import functools

import jax
import jax.numpy as jnp
import numpy as np
from jax import lax
from jax.experimental import pallas as pl
from jax.experimental.pallas import tpu as pltpu

HEAD_DIM = 64
N_HEADS = 8
N_KV = 2
WINDOW = 128
BLOCK = 128
GRID_W = 64
ROPE_THETA = 10000.0
N_EXPERTS = 32
TOP_K = 4
SWIGLU_LIMIT = 7.0
SWIGLU_ALPHA = 1.702
LN_EPS = 1e-5
RMS_EPS = 1e-6
NEG_INF = -1e30
DEPTH = 1
DN_ALPHA = (2.0 * DEPTH) ** 0.25
ALIBI_SLOPES = tuple(2.0 ** (-8.0 * (h + 1) / N_HEADS) for h in range(N_HEADS))
QK_SCALE = HEAD_DIM ** -0.5

LANES = 128
Q_W = N_HEADS * HEAD_DIM
KV_W = N_KV * HEAD_DIM

TM_PROJ = 512
TQ_GRID = 256
BM_EXPERT = 512
TM_DISPATCH = 512
TM_COMBINE = 256

F32 = jnp.float32
BF16 = jnp.bfloat16

_PAIRED = np.array([(j if c == 0 else 4 + j) * HEAD_DIM + d
                    for j in range(4) for c in range(2) for d in range(HEAD_DIM)], np.int32)


def _ln(x, g, b):
    mu = jnp.mean(x, -1, keepdims=True)
    xc = x - mu
    var = jnp.mean(xc * xc, -1, keepdims=True)
    return xc * lax.rsqrt(var + LN_EPS) * g + b


def _dot(a, b):
    return jnp.dot(a, b, preferred_element_type=F32)


def _dot_nt(a, b):
    return lax.dot_general(a, b, (((1,), (1,)), ((), ())), preferred_element_type=F32)


def _in_proj_body(x_ref, g0_ref, b0_ref, w_ref, bd_ref, cq_ref, s1q_ref, s2q_ref,
                  ck_ref, s1k_ref, s2k_ref,
                  qa_ref, ka_ref, vat_ref, qb_ref, kb_ref, vbt_ref, sga_ref, sgb_ref):
    h = _ln(x_ref[0], g0_ref[...], b0_ref[...])
    hb = h.astype(BF16)

    def proj(lo, hi):
        return _dot(hb, w_ref[:, lo:hi])

    def norm_rope(t, width, c_ref, s1_ref, s2_ref):
        ss = _dot((t * t).astype(BF16), bd_ref[:width, :width])
        r = lax.rsqrt(ss * (1.0 / HEAD_DIM) + RMS_EPS)
        rot = (t * c_ref[...] + pltpu.roll(t, width - 16, 1) * s1_ref[...]
               + pltpu.roll(t, 16, 1) * s2_ref[...])
        return rot * r

    o = 0
    qa_ref[0] = (proj(o, o + Q_W) * QK_SCALE).astype(BF16)
    o += Q_W
    ka_ref[0] = proj(o, o + KV_W).astype(BF16)
    o += KV_W
    qb = norm_rope(proj(o, o + Q_W), Q_W, cq_ref, s1q_ref, s2q_ref)
    qb_ref[0] = (qb * QK_SCALE).astype(BF16)
    o += Q_W
    kb_ref[0] = norm_rope(proj(o, o + KV_W), KV_W, ck_ref, s1k_ref, s2k_ref).astype(BF16)
    o += KV_W
    vat_ref[0] = proj(o, o + KV_W).T.astype(BF16)
    o += KV_W
    vbt_ref[0] = proj(o, o + KV_W).T.astype(BF16)
    o += KV_W
    d = sga_ref.shape[-1]
    sga_ref[0] = jax.nn.sigmoid(proj(o, o + d)).astype(BF16)
    o += d
    sgb_ref[0] = jax.nn.sigmoid(proj(o, o + d)).astype(BF16)


def _in_proj(x, g0, b0, w, bd, tabs):
    bsz, seq, d = x.shape
    tm = TM_PROJ
    cq, s1q, s2q, ck, s1k, s2k = tabs
    n_in = w.shape[1]
    const = lambda i, j: (0, 0)
    tok3 = lambda i, j: (j, i, 0)
    tab = lambda i, j: (i, 0)
    in_specs = [
        pl.BlockSpec((1, tm, d), tok3),
        pl.BlockSpec((1, d), const), pl.BlockSpec((1, d), const),
        pl.BlockSpec((d, n_in), const),
        pl.BlockSpec((Q_W, Q_W), const),
        pl.BlockSpec((tm, Q_W), tab), pl.BlockSpec((tm, Q_W), tab), pl.BlockSpec((tm, Q_W), tab),
        pl.BlockSpec((tm, KV_W), tab), pl.BlockSpec((tm, KV_W), tab), pl.BlockSpec((tm, KV_W), tab),
    ]
    tr3 = lambda i, j: (j, 0, i)
    out_specs = [
        pl.BlockSpec((1, tm, Q_W), tok3), pl.BlockSpec((1, tm, KV_W), tok3),
        pl.BlockSpec((1, KV_W, tm), tr3),
        pl.BlockSpec((1, tm, Q_W), tok3), pl.BlockSpec((1, tm, KV_W), tok3),
        pl.BlockSpec((1, KV_W, tm), tr3),
        pl.BlockSpec((1, tm, d), tok3), pl.BlockSpec((1, tm, d), tok3),
    ]
    out_shape = [
        jax.ShapeDtypeStruct((bsz, seq, Q_W), BF16), jax.ShapeDtypeStruct((bsz, seq, KV_W), BF16),
        jax.ShapeDtypeStruct((bsz, KV_W, seq), BF16),
        jax.ShapeDtypeStruct((bsz, seq, Q_W), BF16), jax.ShapeDtypeStruct((bsz, seq, KV_W), BF16),
        jax.ShapeDtypeStruct((bsz, KV_W, seq), BF16),
        jax.ShapeDtypeStruct((bsz, seq, d), BF16), jax.ShapeDtypeStruct((bsz, seq, d), BF16),
    ]
    return pl.pallas_call(
        _in_proj_body, grid=(seq // tm, bsz), in_specs=in_specs, out_specs=out_specs,
        out_shape=out_shape, name="in_proj",
        compiler_params=pltpu.CompilerParams(
            dimension_semantics=("arbitrary", "arbitrary"), vmem_limit_bytes=48 * 1024 * 1024),
    )(x, g0, b0, w, bd, cq, s1q, s2q, ck, s1k, s2k)


def _half_mask(rows, c):
    lane = lax.broadcasted_iota(jnp.int32, (rows, LANES), 1)
    return (lane >= HEAD_DIM) if c == 1 else (lane < HEAD_DIM)


def _win_attn_body(sink_ref, q_ref, k0_ref, k1_ref, k2_ref, v0_ref, v1_ref, v2_ref, o_ref, *, seq):
    n = pl.program_id(1)
    k = jnp.concatenate([k0_ref[0], k1_ref[0], k2_ref[0]], axis=0)
    vt = jnp.concatenate([v0_ref[0], v1_ref[0], v2_ref[0]], axis=1)
    kk = lax.broadcasted_iota(jnp.int32, (3 * BLOCK, BLOCK), 0)
    qq = lax.broadcasted_iota(jnp.int32, (3 * BLOCK, BLOCK), 1)
    dist_i = jnp.abs(kk - BLOCK - qq)
    k_pos = n * BLOCK - BLOCK + kk
    valid = (dist_i <= WINDOW) & (k_pos >= 0) & (k_pos < seq)
    dist = dist_i.astype(F32)
    for j in range(4):
        slab = q_ref[0, :, j * LANES:(j + 1) * LANES]
        outs = []
        for c in range(2):
            head = j + 4 * c
            qm = jnp.where(_half_mask(BLOCK, c), slab, jnp.zeros_like(slab))
            st = _dot_nt(k, qm)
            st = st + jnp.where(valid, -ALIBI_SLOPES[head] * dist, NEG_INF)
            sk = sink_ref[head]
            m = jnp.maximum(jnp.max(st, axis=0, keepdims=True), sk)
            p = jnp.exp(st - m)
            den = jnp.sum(p, axis=0, keepdims=True) + jnp.exp(sk - m)
            ot = _dot(vt[c * HEAD_DIM:(c + 1) * HEAD_DIM, :], p.astype(BF16))
            outs.append(ot / den)
        o_ref[0, :, j * LANES:(j + 1) * LANES] = jnp.concatenate(outs, axis=0).T.astype(BF16)


def _win_attn(sink, qa, ka, vat):
    bsz, seq, _ = qa.shape
    nb = seq // BLOCK
    qmap = lambda b, n: (b, n, 0)
    kspec = lambda f: pl.BlockSpec((1, BLOCK, KV_W), lambda b, n: (b, f(n), 0))
    vspec = lambda f: pl.BlockSpec((1, KV_W, BLOCK), lambda b, n: (b, 0, f(n)))
    prev = lambda n: jnp.maximum(n - 1, 0)
    cur = lambda n: n
    nxt = lambda n: jnp.minimum(n + 1, nb - 1)
    return pl.pallas_call(
        functools.partial(_win_attn_body, seq=seq), grid=(bsz, nb),
        in_specs=[pl.BlockSpec(memory_space=pltpu.SMEM),
                  pl.BlockSpec((1, BLOCK, Q_W), qmap),
                  kspec(prev), kspec(cur), kspec(nxt), vspec(prev), vspec(cur), vspec(nxt)],
        out_specs=pl.BlockSpec((1, BLOCK, Q_W), qmap),
        out_shape=jax.ShapeDtypeStruct((bsz, seq, Q_W), BF16), name="win_attn",
        compiler_params=pltpu.CompilerParams(dimension_semantics=("arbitrary", "arbitrary")),
    )(sink, qa, ka, ka, ka, vat, vat, vat)


def _grid_attn_body(q_ref, k_ref, vt_ref, o_ref):
    k = k_ref[0]
    tq = q_ref.shape[1]
    for j in range(4):
        slab = q_ref[0, :, j * LANES:(j + 1) * LANES]
        outs = []
        for c in range(2):
            qm = jnp.where(_half_mask(tq, c), slab, jnp.zeros_like(slab))
            st = _dot_nt(k, qm)
            m = jnp.max(st, axis=0, keepdims=True)
            p = jnp.exp(st - m)
            den = jnp.sum(p, axis=0, keepdims=True)
            ot = _dot(vt_ref[0, c * HEAD_DIM:(c + 1) * HEAD_DIM, :], p.astype(BF16))
            outs.append(ot / den)
        o_ref[0, :, j * LANES:(j + 1) * LANES] = jnp.concatenate(outs, axis=0).T.astype(BF16)


def _grid_attn(qb, kb, vbt):
    bsz, seq, _ = qb.shape
    tq = TQ_GRID
    return pl.pallas_call(
        _grid_attn_body, grid=(bsz, seq // tq),
        in_specs=[pl.BlockSpec((1, tq, Q_W), lambda b, n: (b, n, 0)),
                  pl.BlockSpec((1, seq, KV_W), lambda b, n: (b, 0, 0)),
                  pl.BlockSpec((1, KV_W, seq), lambda b, n: (b, 0, 0))],
        out_specs=pl.BlockSpec((1, tq, Q_W), lambda b, n: (b, n, 0)),
        out_shape=jax.ShapeDtypeStruct((bsz, seq, Q_W), BF16), name="grid_attn",
        compiler_params=pltpu.CompilerParams(
            dimension_semantics=("arbitrary", "arbitrary"), vmem_limit_bytes=40 * 1024 * 1024),
    )(qb, kb, vbt)


def _post_attn_body(x_ref, oa_ref, ob_ref, sga_ref, sgb_ref, g0_ref, b0_ref, wa_ref, wb_ref, wo_ref,
                    g1_ref, b1_ref, wr_ref, br_ref, h1_ref, ti_ref, tw_ref):
    h0 = _ln(x_ref[0], g0_ref[...], b0_ref[...])
    out_a = _dot(oa_ref[0], wa_ref[...])
    out_b = _dot(ob_ref[0], wb_ref[...])
    merged = sga_ref[0].astype(F32) * out_a + sgb_ref[0].astype(F32) * out_b
    mix = _dot(merged.astype(BF16), wo_ref[...])
    h1 = _ln(DN_ALPHA * h0 + mix, g1_ref[...], b1_ref[...])
    h1_ref[0] = h1

    tm = h1.shape[0]
    logits = _dot(h1.astype(BF16), wr_ref[...]) + br_ref[...]
    lane = lax.broadcasted_iota(jnp.int32, (tm, LANES), 1)
    cur = jnp.where(lane < N_EXPERTS, logits, -jnp.inf)
    vals, idxs = [], []
    for _ in range(TOP_K):
        mv = jnp.max(cur, axis=-1, keepdims=True)
        ix = jnp.min(jnp.where(cur == mv, lane, LANES), axis=-1, keepdims=True)
        vals.append(mv)
        idxs.append(ix)
        cur = jnp.where(lane == ix, -jnp.inf, cur)
    es = [jnp.exp(v - vals[0]) for v in vals]
    tot = es[0] + es[1] + es[2] + es[3]
    ti = jnp.zeros((tm, LANES), jnp.int32)
    tw = jnp.zeros((tm, LANES), F32)
    for kx in range(TOP_K):
        ti = jnp.where(lane == kx, idxs[kx], ti)
        tw = jnp.where(lane == kx, es[kx] / tot, tw)
    ti_ref[0] = ti
    tw_ref[0] = tw


def _post_attn(x, oa, ob, sga, sgb, g0, b0, wa, wb, wo, g1, b1, wr, br):
    bsz, seq, d = x.shape
    tm = TM_PROJ
    tok3 = lambda b, i: (b, i, 0)
    const = lambda b, i: (0, 0)
    full = lambda a: pl.BlockSpec(a.shape, const)
    return pl.pallas_call(
        _post_attn_body, grid=(bsz, seq // tm),
        in_specs=[pl.BlockSpec((1, tm, d), tok3),
                  pl.BlockSpec((1, tm, Q_W), tok3), pl.BlockSpec((1, tm, Q_W), tok3),
                  pl.BlockSpec((1, tm, d), tok3), pl.BlockSpec((1, tm, d), tok3),
                  full(g0), full(b0), full(wa), full(wb), full(wo), full(g1), full(b1),
                  full(wr), full(br)],
        out_specs=[pl.BlockSpec((1, tm, d), tok3), pl.BlockSpec((1, tm, LANES), tok3),
                   pl.BlockSpec((1, tm, LANES), tok3)],
        out_shape=[jax.ShapeDtypeStruct((bsz, seq, d), F32),
                   jax.ShapeDtypeStruct((bsz, seq, LANES), jnp.int32),
                   jax.ShapeDtypeStruct((bsz, seq, LANES), F32)],
        name="post_attn",
        compiler_params=pltpu.CompilerParams(
            dimension_semantics=("arbitrary", "arbitrary"), vmem_limit_bytes=48 * 1024 * 1024),
    )(x, oa, ob, sga, sgb, g0, b0, wa, wb, wo, g1, b1, wr, br)


def _dispatch_body(dest_ref, h1_hbm, xs_in_hbm, xs_hbm, sem):
    del xs_in_hbm
    tm = TM_DISPATCH
    base = pl.program_id(0) * tm

    def issue(t, carry):
        for kx in range(TOP_K):
            d = dest_ref[TOP_K * t + kx]
            pltpu.make_async_copy(h1_hbm.at[pl.ds(base + t, 1)], xs_hbm.at[pl.ds(d, 1)], sem).start()
        return carry

    lax.fori_loop(0, tm, issue, 0)
    pltpu.make_async_copy(h1_hbm.at[pl.ds(0, TOP_K * tm)], xs_hbm.at[pl.ds(0, TOP_K * tm)], sem).wait()


def _dispatch(dest_flat, h1, n_rows):
    n_tok, d = h1.shape
    tm = TM_DISPATCH
    xs0 = jnp.zeros((n_rows, d), F32)
    return pl.pallas_call(
        _dispatch_body, grid=(n_tok // tm,),
        in_specs=[pl.BlockSpec((TOP_K * tm,), lambda i: (i,), memory_space=pltpu.SMEM),
                  pl.BlockSpec(memory_space=pl.ANY), pl.BlockSpec(memory_space=pl.ANY)],
        out_specs=pl.BlockSpec(memory_space=pl.ANY),
        out_shape=jax.ShapeDtypeStruct((n_rows, d), F32),
        scratch_shapes=[pltpu.SemaphoreType.DMA(())],
        input_output_aliases={2: 0}, name="dispatch",
        compiler_params=pltpu.CompilerParams(dimension_semantics=("arbitrary",)),
    )(dest_flat, h1, xs0)


def _experts_body(be_ref, nu_ref, xs_ref, wg_ref, bg_ref, wu_ref, bu_ref, wd_ref, bd_ref, ys_ref,
                  wg_s, wu_s, wd_s):
    i = pl.program_id(0)
    used = i < nu_ref[0]
    prev = be_ref[jnp.maximum(i - 1, 0)]
    fresh = (i == 0) | (be_ref[i] != prev)

    @pl.when(used & fresh)
    def _():
        wg_s[...] = wg_ref[0].astype(BF16)
        wu_s[...] = wu_ref[0].astype(BF16)
        wd_s[...] = wd_ref[0].astype(BF16)

    @pl.when(used)
    def _():
        xb = xs_ref[...].astype(BF16)
        g = _dot(xb, wg_s[...]) + bg_ref[0]
        u = _dot(xb, wu_s[...]) + bu_ref[0]
        g = jnp.minimum(g, SWIGLU_LIMIT)
        u = jnp.clip(u, -SWIGLU_LIMIT, SWIGLU_LIMIT)
        act = g * jax.nn.sigmoid(SWIGLU_ALPHA * g) * (u + 1.0)
        ys_ref[...] = _dot(act.astype(BF16), wd_s[...]) + bd_ref[0]

    @pl.when(jnp.logical_not(used))
    def _():
        ys_ref[...] = jnp.zeros_like(ys_ref)


def _experts(block_e, n_used, xs, wg, bg, wu, bu, wd, bd):
    n_rows, d = xs.shape
    bm = BM_EXPERT
    n_e, _, d_ff = wg.shape
    row = lambda i, be, nu: (jnp.minimum(i, nu[0] - 1), 0)
    exp3 = lambda i, be, nu: (be[jnp.minimum(i, nu[0] - 1)], 0, 0)
    grid_spec = pltpu.PrefetchScalarGridSpec(
        num_scalar_prefetch=2, grid=(n_rows // bm,),
        in_specs=[pl.BlockSpec((bm, d), row),
                  pl.BlockSpec((1, d, d_ff), exp3), pl.BlockSpec((1, 1, d_ff), exp3),
                  pl.BlockSpec((1, d, d_ff), exp3), pl.BlockSpec((1, 1, d_ff), exp3),
                  pl.BlockSpec((1, d_ff, d), exp3), pl.BlockSpec((1, 1, d), exp3)],
        out_specs=pl.BlockSpec((bm, d), lambda i, be, nu: (i, 0)),
        scratch_shapes=[pltpu.VMEM((d, d_ff), BF16), pltpu.VMEM((d, d_ff), BF16),
                        pltpu.VMEM((d_ff, d), BF16)])
    return pl.pallas_call(
        _experts_body, grid_spec=grid_spec,
        out_shape=jax.ShapeDtypeStruct((n_rows, d), F32), name="experts",
        compiler_params=pltpu.CompilerParams(
            dimension_semantics=("arbitrary",), vmem_limit_bytes=56 * 1024 * 1024),
    )(block_e, n_used, xs, wg, bg.reshape(n_e, 1, d_ff), wu, bu.reshape(n_e, 1, d_ff),
      wd, bd.reshape(n_e, 1, d))


def _combine_body(dest_ref, h1_ref, tw_ref, g2_ref, b2_ref, ys_hbm, out_ref, buf, sem):
    tm = TM_COMBINE

    def issue(t, carry):
        for kx in range(TOP_K):
            d = dest_ref[TOP_K * t + kx]
            pltpu.make_async_copy(ys_hbm.at[pl.ds(d, 1)], buf.at[kx, pl.ds(t, 1)], sem).start()
        return carry

    lax.fori_loop(0, tm, issue, 0)
    for kx in range(TOP_K):
        pltpu.make_async_copy(ys_hbm.at[pl.ds(0, tm)], buf.at[kx], sem).wait()
    tw = tw_ref[...]
    ffn = tw[:, 0:1] * buf[0]
    for kx in range(1, TOP_K):
        ffn = ffn + tw[:, kx:kx + 1] * buf[kx]
    out_ref[...] = _ln(DN_ALPHA * h1_ref[...] + ffn, g2_ref[...], b2_ref[...])


def _combine(dest_flat, h1, tw, g2, b2, ys):
    n_tok, d = h1.shape
    tm = TM_COMBINE
    const = lambda i: (0, 0)
    return pl.pallas_call(
        _combine_body, grid=(n_tok // tm,),
        in_specs=[pl.BlockSpec((TOP_K * tm,), lambda i: (i,), memory_space=pltpu.SMEM),
                  pl.BlockSpec((tm, d), lambda i: (i, 0)),
                  pl.BlockSpec((tm, LANES), lambda i: (i, 0)),
                  pl.BlockSpec((1, d), const), pl.BlockSpec((1, d), const),
                  pl.BlockSpec(memory_space=pl.ANY)],
        out_specs=pl.BlockSpec((tm, d), lambda i: (i, 0)),
        out_shape=jax.ShapeDtypeStruct((n_tok, d), F32),
        scratch_shapes=[pltpu.VMEM((TOP_K, tm, d), F32), pltpu.SemaphoreType.DMA(())],
        name="combine",
        compiler_params=pltpu.CompilerParams(dimension_semantics=("arbitrary",)),
    )(dest_flat, h1, tw, g2, b2, ys)


def _rope_tables(seq, gain):
    t = jnp.arange(seq)
    row = (t // GRID_W).astype(F32)
    col = (t % GRID_W).astype(F32)
    half = HEAD_DIM // 2
    quarter = half // 2
    inv = ROPE_THETA ** (-jnp.arange(quarter, dtype=F32) * (2.0 / half))
    ang_r = row[:, None] * inv[None, :]
    ang_c = col[:, None] * inv[None, :]
    zeros = jnp.zeros_like(ang_r)
    cos = jnp.concatenate([jnp.cos(ang_r), jnp.cos(ang_r), jnp.cos(ang_c), jnp.cos(ang_c)], -1)
    s_lo = jnp.concatenate([-jnp.sin(ang_r), zeros, -jnp.sin(ang_c), zeros], -1)
    s_hi = jnp.concatenate([zeros, jnp.sin(ang_r), zeros, jnp.sin(ang_c)], -1)
    g = gain.astype(F32)
    return cos * g[None, :], s_lo * jnp.roll(g, -quarter)[None, :], s_hi * jnp.roll(g, quarter)[None, :]


def _routing(top_i, bm):
    n_tok = top_i.shape[0]
    onehot = (top_i[:, :, None] == jnp.arange(N_EXPERTS, dtype=jnp.int32)[None, None, :])
    mask = jnp.sum(onehot.astype(jnp.int32), axis=1)
    csum = jnp.cumsum(mask, axis=0)
    counts = csum[-1]
    padded = (counts + bm - 1) // bm * bm
    pad_end = jnp.cumsum(padded)
    pad_start = pad_end - padded
    base = pad_start[None, :] + csum - mask
    dest = jnp.take_along_axis(base, top_i, axis=1).astype(jnp.int32)
    n_rows = n_tok * TOP_K + N_EXPERTS * bm
    n_blocks = n_rows // bm
    block_e = jnp.minimum(
        jnp.searchsorted(pad_end, jnp.arange(n_blocks, dtype=jnp.int32) * bm, side='right'),
        N_EXPERTS - 1).astype(jnp.int32)
    n_used = (pad_end[-1] // bm).astype(jnp.int32).reshape(1)
    return dest.reshape(-1), block_e, n_used, n_rows


def kernel(x, ln0_g, ln0_b, w_in, a_sink, b_q_norm, b_k_norm, w_branch_a, w_branch_b, w_out,
           ln1_g, ln1_b, w_router, b_router, w_gate, b_gate, w_up, b_up, w_down, b_down,
           ln2_g, ln2_b):
    bsz, seq, d = x.shape
    assert w_in.shape[0] == DEPTH == 1
    assert seq % TM_PROJ == 0 and seq % TQ_GRID == 0 and seq == (seq // GRID_W) * GRID_W
    n_tok = bsz * seq
    row2 = lambda a: a.reshape(1, -1)

    w = w_in[0]
    o_ka, o_va, o_qb, o_kb, o_vb, o_g = Q_W, Q_W + KV_W, Q_W + 2 * KV_W, 2 * Q_W + 2 * KV_W, \
        2 * Q_W + 3 * KV_W, 2 * Q_W + 4 * KV_W
    w_perm = jnp.concatenate([
        w[:, :Q_W][:, _PAIRED], w[:, o_ka:o_va],
        w[:, o_qb:o_kb][:, _PAIRED], w[:, o_kb:o_vb],
        w[:, o_va:o_qb], w[:, o_vb:o_g], w[:, o_g:]], axis=1).astype(BF16)
    head_id = np.arange(Q_W) // HEAD_DIM
    bd = jnp.asarray(head_id[:, None] == head_id[None, :], BF16)
    cq, s1q, s2q = (jnp.tile(t, (1, N_HEADS)) for t in _rope_tables(seq, b_q_norm[0]))
    ck, s1k, s2k = (jnp.tile(t, (1, N_KV)) for t in _rope_tables(seq, b_k_norm[0]))

    qa, ka, vat, qb, kb, vbt, sga, sgb = _in_proj(
        x, row2(ln0_g), row2(ln0_b), w_perm, bd, (cq, s1q, s2q, ck, s1k, s2k))
    oa = _win_attn(a_sink[0].astype(F32), qa, ka, vat)
    ob = _grid_attn(qb, kb, vbt)

    wr = jnp.zeros((d, LANES), BF16).at[:, :N_EXPERTS].set(w_router[0].astype(BF16))
    br = jnp.zeros((1, LANES), F32).at[0, :N_EXPERTS].set(b_router[0])
    h1, ti, tw = _post_attn(
        x, oa, ob, sga, sgb, row2(ln0_g), row2(ln0_b),
        w_branch_a[0][_PAIRED].astype(BF16), w_branch_b[0][_PAIRED].astype(BF16),
        w_out[0].astype(BF16), row2(ln1_g[0]), row2(ln1_b[0]), wr, br)
    h1 = h1.reshape(n_tok, d)
    top_i = ti.reshape(n_tok, LANES)[:, :TOP_K]
    tw = tw.reshape(n_tok, LANES)

    dest, block_e, n_used, n_rows = _routing(top_i, BM_EXPERT)
    xs = _dispatch(dest, h1, n_rows)
    ys = _experts(block_e, n_used, xs, w_gate[0], b_gate[0], w_up[0], b_up[0], w_down[0], b_down[0])
    out = _combine(dest, h1, tw, row2(ln2_g[0]), row2(ln2_b[0]), ys)
    return out.reshape(bsz, seq, d)
```

```python
import functools

import jax
import jax.numpy as jnp
import numpy as np
from jax import lax
from jax.experimental import pallas as pl
from jax.experimental.pallas import tpu as pltpu

HEAD_DIM = 64
N_HEADS = 8
N_KV = 2
WINDOW = 128
BLOCK = 128
GRID_W = 64
ROPE_THETA = 10000.0
N_EXPERTS = 32
TOP_K = 4
SWIGLU_LIMIT = 7.0
SWIGLU_ALPHA = 1.702
LN_EPS = 1e-5
RMS_EPS = 1e-6
NEG_INF = -1e30
DEPTH = 1
DN_ALPHA = (2.0 * DEPTH) ** 0.25
ALIBI_SLOPES = tuple(2.0 ** (-8.0 * (h + 1) / N_HEADS) for h in range(N_HEADS))
QK_SCALE = HEAD_DIM ** -0.5

LANES = 128
ROW_TILE = 8
Q_W = N_HEADS * HEAD_DIM
KV_W = N_KV * HEAD_DIM

TM_PROJ = 512
TQ_GRID = 256
BM_EXPERT = 512
TM_DISPATCH = 512
TM_COMBINE = 256

F32 = jnp.float32
BF16 = jnp.bfloat16

_PAIRED = np.array([(j if c == 0 else 4 + j) * HEAD_DIM + d
                    for j in range(4) for c in range(2) for d in range(HEAD_DIM)], np.int32)


def _ln(x, g, b):
    mu = jnp.mean(x, -1, keepdims=True)
    xc = x - mu
    var = jnp.mean(xc * xc, -1, keepdims=True)
    return xc * lax.rsqrt(var + LN_EPS) * g + b


def _dot(a, b):
    return jnp.dot(a, b, preferred_element_type=F32)


def _dot_nt(a, b):
    return lax.dot_general(a, b, (((1,), (1,)), ((), ())), preferred_element_type=F32)


def _load_row_tiles(ref, rows):
    return jnp.concatenate([ref[pl.ds(c, rows, stride=ROW_TILE), :] for c in range(ROW_TILE)], axis=1)


def _store_row_tiles(ref, val, rows):
    for c in range(ROW_TILE):
        ref[pl.ds(c, rows, stride=ROW_TILE), :] = val[:, c * LANES:(c + 1) * LANES]


def _tile(ref, row):
    return ref.at[pl.ds(pl.multiple_of(row * ROW_TILE, ROW_TILE), ROW_TILE)]


def _in_proj_body(x_ref, g0_ref, b0_ref, w_ref, bd_ref, cq_ref, s1q_ref, s2q_ref,
                  ck_ref, s1k_ref, s2k_ref,
                  qa_ref, ka_ref, vat_ref, qb_ref, kb_ref, vbt_ref, sga_ref, sgb_ref):
    h = _ln(x_ref[0], g0_ref[...], b0_ref[...])
    hb = h.astype(BF16)

    def proj(lo, hi):
        return _dot(hb, w_ref[:, lo:hi])

    def norm_rope(t, width, c_ref, s1_ref, s2_ref):
        ss = _dot((t * t).astype(BF16), bd_ref[:width, :width])
        r = lax.rsqrt(ss * (1.0 / HEAD_DIM) + RMS_EPS)
        rot = (t * c_ref[...] + pltpu.roll(t, width - 16, 1) * s1_ref[...]
               + pltpu.roll(t, 16, 1) * s2_ref[...])
        return rot * r

    o = 0
    qa_ref[0] = (proj(o, o + Q_W) * QK_SCALE).astype(BF16)
    o += Q_W
    ka_ref[0] = proj(o, o + KV_W).astype(BF16)
    o += KV_W
    qb = norm_rope(proj(o, o + Q_W), Q_W, cq_ref, s1q_ref, s2q_ref)
    qb_ref[0] = (qb * QK_SCALE).astype(BF16)
    o += Q_W
    kb_ref[0] = norm_rope(proj(o, o + KV_W), KV_W, ck_ref, s1k_ref, s2k_ref).astype(BF16)
    o += KV_W
    vat_ref[0] = proj(o, o + KV_W).T.astype(BF16)
    o += KV_W
    vbt_ref[0] = proj(o, o + KV_W).T.astype(BF16)
    o += KV_W
    d = sga_ref.shape[-1]
    sga_ref[0] = jax.nn.sigmoid(proj(o, o + d)).astype(BF16)
    o += d
    sgb_ref[0] = jax.nn.sigmoid(proj(o, o + d)).astype(BF16)


def _in_proj(x, g0, b0, w, bd, tabs):
    bsz, seq, d = x.shape
    tm = TM_PROJ
    cq, s1q, s2q, ck, s1k, s2k = tabs
    n_in = w.shape[1]
    const = lambda i, j: (0, 0)
    tok3 = lambda i, j: (j, i, 0)
    tab = lambda i, j: (i, 0)
    in_specs = [
        pl.BlockSpec((1, tm, d), tok3),
        pl.BlockSpec((1, d), const), pl.BlockSpec((1, d), const),
        pl.BlockSpec((d, n_in), const),
        pl.BlockSpec((Q_W, Q_W), const),
        pl.BlockSpec((tm, Q_W), tab), pl.BlockSpec((tm, Q_W), tab), pl.BlockSpec((tm, Q_W), tab),
        pl.BlockSpec((tm, KV_W), tab), pl.BlockSpec((tm, KV_W), tab), pl.BlockSpec((tm, KV_W), tab),
    ]
    tr3 = lambda i, j: (j, 0, i)
    out_specs = [
        pl.BlockSpec((1, tm, Q_W), tok3), pl.BlockSpec((1, tm, KV_W), tok3),
        pl.BlockSpec((1, KV_W, tm), tr3),
        pl.BlockSpec((1, tm, Q_W), tok3), pl.BlockSpec((1, tm, KV_W), tok3),
        pl.BlockSpec((1, KV_W, tm), tr3),
        pl.BlockSpec((1, tm, d), tok3), pl.BlockSpec((1, tm, d), tok3),
    ]
    out_shape = [
        jax.ShapeDtypeStruct((bsz, seq, Q_W), BF16), jax.ShapeDtypeStruct((bsz, seq, KV_W), BF16),
        jax.ShapeDtypeStruct((bsz, KV_W, seq), BF16),
        jax.ShapeDtypeStruct((bsz, seq, Q_W), BF16), jax.ShapeDtypeStruct((bsz, seq, KV_W), BF16),
        jax.ShapeDtypeStruct((bsz, KV_W, seq), BF16),
        jax.ShapeDtypeStruct((bsz, seq, d), BF16), jax.ShapeDtypeStruct((bsz, seq, d), BF16),
    ]
    return pl.pallas_call(
        _in_proj_body, grid=(seq // tm, bsz), in_specs=in_specs, out_specs=out_specs,
        out_shape=out_shape, name="in_proj",
        compiler_params=pltpu.CompilerParams(
            dimension_semantics=("arbitrary", "arbitrary"), vmem_limit_bytes=48 * 1024 * 1024),
    )(x, g0, b0, w, bd, cq, s1q, s2q, ck, s1k, s2k)


def _half_mask(rows, c):
    lane = lax.broadcasted_iota(jnp.int32, (rows, LANES), 1)
    return (lane >= HEAD_DIM) if c == 1 else (lane < HEAD_DIM)


def _win_attn_body(sink_ref, q_ref, k0_ref, k1_ref, k2_ref, v0_ref, v1_ref, v2_ref, o_ref, *, seq):
    n = pl.program_id(1)
    k = jnp.concatenate([k0_ref[0], k1_ref[0], k2_ref[0]], axis=0)
    vt = jnp.concatenate([v0_ref[0], v1_ref[0], v2_ref[0]], axis=1)
    kk = lax.broadcasted_iota(jnp.int32, (3 * BLOCK, BLOCK), 0)
    qq = lax.broadcasted_iota(jnp.int32, (3 * BLOCK, BLOCK), 1)
    dist_i = jnp.abs(kk - BLOCK - qq)
    k_pos = n * BLOCK - BLOCK + kk
    valid = (dist_i <= WINDOW) & (k_pos >= 0) & (k_pos < seq)
    dist = dist_i.astype(F32)
    for j in range(4):
        slab = q_ref[0, :, j * LANES:(j + 1) * LANES]
        outs = []
        for c in range(2):
            head = j + 4 * c
            qm = jnp.where(_half_mask(BLOCK, c), slab, jnp.zeros_like(slab))
            st = _dot_nt(k, qm)
            st = st + jnp.where(valid, -ALIBI_SLOPES[head] * dist, NEG_INF)
            sk = sink_ref[head]
            m = jnp.maximum(jnp.max(st, axis=0, keepdims=True), sk)
            p = jnp.exp(st - m)
            den = jnp.sum(p, axis=0, keepdims=True) + jnp.exp(sk - m)
            ot = _dot(vt[c * HEAD_DIM:(c + 1) * HEAD_DIM, :], p.astype(BF16))
            outs.append(ot / den)
        o_ref[0, :, j * LANES:(j + 1) * LANES] = jnp.concatenate(outs, axis=0).T.astype(BF16)


def _win_attn(sink, qa, ka, vat):
    bsz, seq, _ = qa.shape
    nb = seq // BLOCK
    qmap = lambda b, n: (b, n, 0)
    kspec = lambda f: pl.BlockSpec((1, BLOCK, KV_W), lambda b, n: (b, f(n), 0))
    vspec = lambda f: pl.BlockSpec((1, KV_W, BLOCK), lambda b, n: (b, 0, f(n)))
    prev = lambda n: jnp.maximum(n - 1, 0)
    cur = lambda n: n
    nxt = lambda n: jnp.minimum(n + 1, nb - 1)
    return pl.pallas_call(
        functools.partial(_win_attn_body, seq=seq), grid=(bsz, nb),
        in_specs=[pl.BlockSpec(memory_space=pltpu.SMEM),
                  pl.BlockSpec((1, BLOCK, Q_W), qmap),
                  kspec(prev), kspec(cur), kspec(nxt), vspec(prev), vspec(cur), vspec(nxt)],
        out_specs=pl.BlockSpec((1, BLOCK, Q_W), qmap),
        out_shape=jax.ShapeDtypeStruct((bsz, seq, Q_W), BF16), name="win_attn",
        compiler_params=pltpu.CompilerParams(dimension_semantics=("arbitrary", "arbitrary")),
    )(sink, qa, ka, ka, ka, vat, vat, vat)


def _grid_attn_body(q_ref, k_ref, vt_ref, o_ref):
    k = k_ref[0]
    tq = q_ref.shape[1]
    for j in range(4):
        slab = q_ref[0, :, j * LANES:(j + 1) * LANES]
        outs = []
        for c in range(2):
            qm = jnp.where(_half_mask(tq, c), slab, jnp.zeros_like(slab))
            st = _dot_nt(k, qm)
            m = jnp.max(st, axis=0, keepdims=True)
            p = jnp.exp(st - m)
            den = jnp.sum(p, axis=0, keepdims=True)
            ot = _dot(vt_ref[0, c * HEAD_DIM:(c + 1) * HEAD_DIM, :], p.astype(BF16))
            outs.append(ot / den)
        o_ref[0, :, j * LANES:(j + 1) * LANES] = jnp.concatenate(outs, axis=0).T.astype(BF16)


def _grid_attn(qb, kb, vbt):
    bsz, seq, _ = qb.shape
    tq = TQ_GRID
    return pl.pallas_call(
        _grid_attn_body, grid=(bsz, seq // tq),
        in_specs=[pl.BlockSpec((1, tq, Q_W), lambda b, n: (b, n, 0)),
                  pl.BlockSpec((1, seq, KV_W), lambda b, n: (b, 0, 0)),
                  pl.BlockSpec((1, KV_W, seq), lambda b, n: (b, 0, 0))],
        out_specs=pl.BlockSpec((1, tq, Q_W), lambda b, n: (b, n, 0)),
        out_shape=jax.ShapeDtypeStruct((bsz, seq, Q_W), BF16), name="grid_attn",
        compiler_params=pltpu.CompilerParams(
            dimension_semantics=("arbitrary", "arbitrary"), vmem_limit_bytes=40 * 1024 * 1024),
    )(qb, kb, vbt)


def _post_attn_body(x_ref, oa_ref, ob_ref, sga_ref, sgb_ref, g0_ref, b0_ref, wa_ref, wb_ref, wo_ref,
                    g1_ref, b1_ref, wr_ref, br_ref, h1t_ref, ti_ref, tw_ref):
    h0 = _ln(x_ref[0], g0_ref[...], b0_ref[...])
    out_a = _dot(oa_ref[0], wa_ref[...])
    out_b = _dot(ob_ref[0], wb_ref[...])
    merged = sga_ref[0].astype(F32) * out_a + sgb_ref[0].astype(F32) * out_b
    mix = _dot(merged.astype(BF16), wo_ref[...])
    h1 = _ln(DN_ALPHA * h0 + mix, g1_ref[...], b1_ref[...])
    tm = h1.shape[0]
    _store_row_tiles(h1t_ref.at[0], h1, tm)

    logits = _dot(h1.astype(BF16), wr_ref[...]) + br_ref[...]
    lane = lax.broadcasted_iota(jnp.int32, (tm, LANES), 1)
    cur = jnp.where(lane < N_EXPERTS, logits, -jnp.inf)
    vals, idxs = [], []
    for _ in range(TOP_K):
        mv = jnp.max(cur, axis=-1, keepdims=True)
        ix = jnp.min(jnp.where(cur == mv, lane, LANES), axis=-1, keepdims=True)
        vals.append(mv)
        idxs.append(ix)
        cur = jnp.where(lane == ix, -jnp.inf, cur)
    es = [jnp.exp(v - vals[0]) for v in vals]
    tot = es[0] + es[1] + es[2] + es[3]
    ti = jnp.zeros((tm, LANES), jnp.int32)
    tw = jnp.zeros((tm, LANES), F32)
    for kx in range(TOP_K):
        ti = jnp.where(lane == kx, idxs[kx], ti)
        tw = jnp.where(lane == kx, es[kx] / tot, tw)
    ti_ref[0] = ti
    tw_ref[0] = tw


def _post_attn(x, oa, ob, sga, sgb, g0, b0, wa, wb, wo, g1, b1, wr, br):
    bsz, seq, d = x.shape
    tm = TM_PROJ
    tok3 = lambda b, i: (b, i, 0)
    const = lambda b, i: (0, 0)
    full = lambda a: pl.BlockSpec(a.shape, const)
    return pl.pallas_call(
        _post_attn_body, grid=(bsz, seq // tm),
        in_specs=[pl.BlockSpec((1, tm, d), tok3),
                  pl.BlockSpec((1, tm, Q_W), tok3), pl.BlockSpec((1, tm, Q_W), tok3),
                  pl.BlockSpec((1, tm, d), tok3), pl.BlockSpec((1, tm, d), tok3),
                  full(g0), full(b0), full(wa), full(wb), full(wo), full(g1), full(b1),
                  full(wr), full(br)],
        out_specs=[pl.BlockSpec((1, tm * ROW_TILE, LANES), tok3), pl.BlockSpec((1, tm, LANES), tok3),
                   pl.BlockSpec((1, tm, LANES), tok3)],
        out_shape=[jax.ShapeDtypeStruct((bsz, seq * ROW_TILE, LANES), F32),
                   jax.ShapeDtypeStruct((bsz, seq, LANES), jnp.int32),
                   jax.ShapeDtypeStruct((bsz, seq, LANES), F32)],
        name="post_attn",
        compiler_params=pltpu.CompilerParams(
            dimension_semantics=("arbitrary", "arbitrary"), vmem_limit_bytes=48 * 1024 * 1024),
    )(x, oa, ob, sga, sgb, g0, b0, wa, wb, wo, g1, b1, wr, br)


def _dispatch_body(dest_ref, pend_ref, padded_ref, nu_ref, h1t_ref, xs_hbm, zbuf, sem, zsem):
    tm = TM_DISPATCH
    zrows = BM_EXPERT * ROW_TILE
    n_blocks = xs_hbm.shape[0] // zrows

    @pl.when(pl.program_id(0) == 0)
    def _():
        zbuf[...] = jnp.zeros_like(zbuf)
        zero_wait = pltpu.make_async_copy(zbuf, xs_hbm.at[pl.ds(0, zrows)], zsem).wait
        for e in range(N_EXPERTS):
            @pl.when(padded_ref[e] > 0)
            def _():
                start = pl.multiple_of((pend_ref[e] - BM_EXPERT) * ROW_TILE, ROW_TILE)
                pltpu.make_async_copy(zbuf, xs_hbm.at[pl.ds(start, zrows)], zsem).start()
        for b in range(n_blocks - N_EXPERTS, n_blocks):
            @pl.when(b >= nu_ref[0])
            def _():
                pltpu.make_async_copy(zbuf, xs_hbm.at[pl.ds(b * zrows, zrows)], zsem).start()
        for e in range(N_EXPERTS):
            pl.when(padded_ref[e] > 0)(zero_wait)
        for b in range(n_blocks - N_EXPERTS, n_blocks):
            pl.when(b >= nu_ref[0])(zero_wait)

    def issue(t, carry):
        for kx in range(TOP_K):
            d = dest_ref[TOP_K * t + kx]
            pltpu.make_async_copy(_tile(h1t_ref, t), _tile(xs_hbm, d), sem).start()
        return carry

    lax.fori_loop(0, tm, issue, 0)
    for _ in range(TOP_K):
        pltpu.make_async_copy(h1t_ref, xs_hbm.at[pl.ds(0, tm * ROW_TILE)], sem).wait()


def _dispatch(dest_flat, pad_end, padded, n_used, h1t, n_rows):
    n_tok = h1t.shape[0] // ROW_TILE
    tm = TM_DISPATCH
    return pl.pallas_call(
        _dispatch_body, grid=(n_tok // tm,),
        in_specs=[pl.BlockSpec((TOP_K * tm,), lambda i: (i,), memory_space=pltpu.SMEM),
                  pl.BlockSpec(memory_space=pltpu.SMEM), pl.BlockSpec(memory_space=pltpu.SMEM),
                  pl.BlockSpec(memory_space=pltpu.SMEM),
                  pl.BlockSpec((tm * ROW_TILE, LANES), lambda i: (i, 0))],
        out_specs=pl.BlockSpec(memory_space=pl.ANY),
        out_shape=jax.ShapeDtypeStruct((n_rows * ROW_TILE, LANES), F32),
        scratch_shapes=[pltpu.VMEM((BM_EXPERT * ROW_TILE, LANES), F32),
                        pltpu.SemaphoreType.DMA(()), pltpu.SemaphoreType.DMA(())],
        name="dispatch",
        compiler_params=pltpu.CompilerParams(dimension_semantics=("arbitrary",)),
    )(dest_flat, pad_end, padded, n_used, h1t)


def _experts_body(be_ref, nu_ref, xs_ref, wg_ref, bg_ref, wu_ref, bu_ref, wd_ref, bd_ref, ys_ref,
                  wg_s, wu_s, wd_s):
    i = pl.program_id(0)
    bm = BM_EXPERT
    used = i < nu_ref[0]
    prev = be_ref[jnp.maximum(i - 1, 0)]
    fresh = (i == 0) | (be_ref[i] != prev)

    @pl.when(used & fresh)
    def _():
        wg_s[...] = wg_ref[0].astype(BF16)
        wu_s[...] = wu_ref[0].astype(BF16)
        wd_s[...] = wd_ref[0].astype(BF16)

    @pl.when(used)
    def _():
        xb = _load_row_tiles(xs_ref, bm).astype(BF16)
        g = _dot(xb, wg_s[...]) + bg_ref[0]
        u = _dot(xb, wu_s[...]) + bu_ref[0]
        g = jnp.minimum(g, SWIGLU_LIMIT)
        u = jnp.clip(u, -SWIGLU_LIMIT, SWIGLU_LIMIT)
        act = g * jax.nn.sigmoid(SWIGLU_ALPHA * g) * (u + 1.0)
        _store_row_tiles(ys_ref, _dot(act.astype(BF16), wd_s[...]) + bd_ref[0], bm)

    @pl.when(jnp.logical_not(used))
    def _():
        ys_ref[...] = jnp.zeros_like(ys_ref)


def _experts(block_e, n_used, xs, wg, bg, wu, bu, wd, bd):
    n_rows = xs.shape[0] // ROW_TILE
    bm = BM_EXPERT
    n_e, d, d_ff = wg.shape
    row = lambda i, be, nu: (jnp.minimum(i, nu[0] - 1), 0)
    exp3 = lambda i, be, nu: (be[jnp.minimum(i, nu[0] - 1)], 0, 0)
    grid_spec = pltpu.PrefetchScalarGridSpec(
        num_scalar_prefetch=2, grid=(n_rows // bm,),
        in_specs=[pl.BlockSpec((bm * ROW_TILE, LANES), row),
                  pl.BlockSpec((1, d, d_ff), exp3), pl.BlockSpec((1, 1, d_ff), exp3),
                  pl.BlockSpec((1, d, d_ff), exp3), pl.BlockSpec((1, 1, d_ff), exp3),
                  pl.BlockSpec((1, d_ff, d), exp3), pl.BlockSpec((1, 1, d), exp3)],
        out_specs=pl.BlockSpec((bm * ROW_TILE, LANES), lambda i, be, nu: (i, 0)),
        scratch_shapes=[pltpu.VMEM((d, d_ff), BF16), pltpu.VMEM((d, d_ff), BF16),
                        pltpu.VMEM((d_ff, d), BF16)])
    return pl.pallas_call(
        _experts_body, grid_spec=grid_spec,
        out_shape=jax.ShapeDtypeStruct((n_rows * ROW_TILE, LANES), F32), name="experts",
        compiler_params=pltpu.CompilerParams(
            dimension_semantics=("arbitrary",), vmem_limit_bytes=56 * 1024 * 1024),
    )(block_e, n_used, xs, wg, bg.reshape(n_e, 1, d_ff), wu, bu.reshape(n_e, 1, d_ff),
      wd, bd.reshape(n_e, 1, d))


def _combine_body(dest_ref, h1t_ref, tw_ref, g2_ref, b2_ref, ys_hbm, out_ref, buf, sem):
    tm = TM_COMBINE

    def issue(t, carry):
        for kx in range(TOP_K):
            d = dest_ref[TOP_K * t + kx]
            pltpu.make_async_copy(_tile(ys_hbm, d), _tile(buf.at[kx], t), sem).start()
        return carry

    lax.fori_loop(0, tm, issue, 0)
    for kx in range(TOP_K):
        pltpu.make_async_copy(ys_hbm.at[pl.ds(0, tm * ROW_TILE)], buf.at[kx], sem).wait()
    tw = tw_ref[...]
    ffn = tw[:, 0:1] * _load_row_tiles(buf.at[0], tm)
    for kx in range(1, TOP_K):
        ffn = ffn + tw[:, kx:kx + 1] * _load_row_tiles(buf.at[kx], tm)
    h1 = _load_row_tiles(h1t_ref, tm)
    out_ref[...] = _ln(DN_ALPHA * h1 + ffn, g2_ref[...], b2_ref[...])


def _combine(dest_flat, h1t, tw, g2, b2, ys):
    n_tok = h1t.shape[0] // ROW_TILE
    d = ROW_TILE * LANES
    tm = TM_COMBINE
    const = lambda i: (0, 0)
    return pl.pallas_call(
        _combine_body, grid=(n_tok // tm,),
        in_specs=[pl.BlockSpec((TOP_K * tm,), lambda i: (i,), memory_space=pltpu.SMEM),
                  pl.BlockSpec((tm * ROW_TILE, LANES), lambda i: (i, 0)),
                  pl.BlockSpec((tm, LANES), lambda i: (i, 0)),
                  pl.BlockSpec((1, d), const), pl.BlockSpec((1, d), const),
                  pl.BlockSpec(memory_space=pl.ANY)],
        out_specs=pl.BlockSpec((tm, d), lambda i: (i, 0)),
        out_shape=jax.ShapeDtypeStruct((n_tok, d), F32),
        scratch_shapes=[pltpu.VMEM((TOP_K, tm * ROW_TILE, LANES), F32), pltpu.SemaphoreType.DMA(())],
        name="combine",
        compiler_params=pltpu.CompilerParams(dimension_semantics=("arbitrary",)),
    )(dest_flat, h1t, tw, g2, b2, ys)


def _rope_tables(seq, gain):
    t = jnp.arange(seq)
    row = (t // GRID_W).astype(F32)
    col = (t % GRID_W).astype(F32)
    half = HEAD_DIM // 2
    quarter = half // 2
    inv = ROPE_THETA ** (-jnp.arange(quarter, dtype=F32) * (2.0 / half))
    ang_r = row[:, None] * inv[None, :]
    ang_c = col[:, None] * inv[None, :]
    zeros = jnp.zeros_like(ang_r)
    cos = jnp.concatenate([jnp.cos(ang_r), jnp.cos(ang_r), jnp.cos(ang_c), jnp.cos(ang_c)], -1)
    s_lo = jnp.concatenate([-jnp.sin(ang_r), zeros, -jnp.sin(ang_c), zeros], -1)
    s_hi = jnp.concatenate([zeros, jnp.sin(ang_r), zeros, jnp.sin(ang_c)], -1)
    g = gain.astype(F32)
    return cos * g[None, :], s_lo * jnp.roll(g, -quarter)[None, :], s_hi * jnp.roll(g, quarter)[None, :]


def _routing(top_i, bm):
    n_tok = top_i.shape[0]
    onehot = (top_i[:, :, None] == jnp.arange(N_EXPERTS, dtype=jnp.int32)[None, None, :])
    mask = jnp.sum(onehot.astype(jnp.int32), axis=1)
    csum = jnp.cumsum(mask, axis=0)
    counts = csum[-1]
    padded = (counts + bm - 1) // bm * bm
    pad_end = jnp.cumsum(padded)
    pad_start = pad_end - padded
    base = pad_start[None, :] + csum - mask
    dest = jnp.take_along_axis(base, top_i, axis=1).astype(jnp.int32)
    n_rows = n_tok * TOP_K + N_EXPERTS * bm
    n_blocks = n_rows // bm
    block_e = jnp.minimum(
        jnp.searchsorted(pad_end, jnp.arange(n_blocks, dtype=jnp.int32) * bm, side='right'),
        N_EXPERTS - 1).astype(jnp.int32)
    n_used = (pad_end[-1] // bm).astype(jnp.int32).reshape(1)
    return dest.reshape(-1), block_e, n_used, n_rows, pad_end.astype(jnp.int32), padded.astype(jnp.int32)


def kernel(x, ln0_g, ln0_b, w_in, a_sink, b_q_norm, b_k_norm, w_branch_a, w_branch_b, w_out,
           ln1_g, ln1_b, w_router, b_router, w_gate, b_gate, w_up, b_up, w_down, b_down,
           ln2_g, ln2_b):
    bsz, seq, d = x.shape
    assert w_in.shape[0] == DEPTH == 1
    assert seq % TM_PROJ == 0 and seq % TQ_GRID == 0 and seq == (seq // GRID_W) * GRID_W
    n_tok = bsz * seq
    row2 = lambda a: a.reshape(1, -1)

    w = w_in[0]
    o_ka, o_va, o_qb, o_kb, o_vb, o_g = Q_W, Q_W + KV_W, Q_W + 2 * KV_W, 2 * Q_W + 2 * KV_W, \
        2 * Q_W + 3 * KV_W, 2 * Q_W + 4 * KV_W
    w_perm = jnp.concatenate([
        w[:, :Q_W][:, _PAIRED], w[:, o_ka:o_va],
        w[:, o_qb:o_kb][:, _PAIRED], w[:, o_kb:o_vb],
        w[:, o_va:o_qb], w[:, o_vb:o_g], w[:, o_g:]], axis=1).astype(BF16)
    head_id = np.arange(Q_W) // HEAD_DIM
    bd = jnp.asarray(head_id[:, None] == head_id[None, :], BF16)
    cq, s1q, s2q = (jnp.tile(t, (1, N_HEADS)) for t in _rope_tables(seq, b_q_norm[0]))
    ck, s1k, s2k = (jnp.tile(t, (1, N_KV)) for t in _rope_tables(seq, b_k_norm[0]))

    qa, ka, vat, qb, kb, vbt, sga, sgb = _in_proj(
        x, row2(ln0_g), row2(ln0_b), w_perm, bd, (cq, s1q, s2q, ck, s1k, s2k))
    oa = _win_attn(a_sink[0].astype(F32), qa, ka, vat)
    ob = _grid_attn(qb, kb, vbt)

    wr = jnp.zeros((d, LANES), BF16).at[:, :N_EXPERTS].set(w_router[0].astype(BF16))
    br = jnp.zeros((1, LANES), F32).at[0, :N_EXPERTS].set(b_router[0])
    h1t, ti, tw = _post_attn(
        x, oa, ob, sga, sgb, row2(ln0_g), row2(ln0_b),
        w_branch_a[0][_PAIRED].astype(BF16), w_branch_b[0][_PAIRED].astype(BF16),
        w_out[0].astype(BF16), row2(ln1_g[0]), row2(ln1_b[0]), wr, br)
    h1t = h1t.reshape(n_tok * ROW_TILE, LANES)
    top_i = ti.reshape(n_tok, LANES)[:, :TOP_K]
    tw = tw.reshape(n_tok, LANES)

    dest, block_e, n_used, n_rows, pad_end, padded = _routing(top_i, BM_EXPERT)
    xs = _dispatch(dest, pad_end, padded, n_used, h1t, n_rows)
    ys = _experts(block_e, n_used, xs, w_gate[0], b_gate[0], w_up[0], b_up[0], w_down[0], b_down[0])
    out = _combine(dest, h1t, tw, row2(ln2_g[0]), row2(ln2_b[0]), ys)
    return out.reshape(bsz, seq, d)
```

```python
import functools

import jax
import jax.numpy as jnp
import numpy as np
from jax import lax
from jax.experimental import pallas as pl
from jax.experimental.pallas import tpu as pltpu

HEAD_DIM = 64
N_HEADS = 8
N_KV = 2
WINDOW = 128
BLOCK = 128
GRID_W = 64
ROPE_THETA = 10000.0
N_EXPERTS = 32
TOP_K = 4
SWIGLU_LIMIT = 7.0
SWIGLU_ALPHA = 1.702
LN_EPS = 1e-5
RMS_EPS = 1e-6
NEG_INF = -1e30
DEPTH = 1
DN_ALPHA = (2.0 * DEPTH) ** 0.25
ALIBI_SLOPES = tuple(2.0 ** (-8.0 * (h + 1) / N_HEADS) for h in range(N_HEADS))
QK_SCALE = HEAD_DIM ** -0.5
LOG2_E = 1.4426950408889634

LANES = 128
ROW_TILE = 8
Q_W = N_HEADS * HEAD_DIM
KV_W = N_KV * HEAD_DIM

TM_PROJ = 512
TQ_GRID = 256
KEY_CHUNK = 256
BM_EXPERT = 512
TM_DISPATCH = 512
TM_COMBINE = 256

F32 = jnp.float32
BF16 = jnp.bfloat16

_PAIRED = np.array([(j if c == 0 else 4 + j) * HEAD_DIM + d
                    for j in range(4) for c in range(2) for d in range(HEAD_DIM)], np.int32)


def _ln(x, g, b):
    mu = jnp.mean(x, -1, keepdims=True)
    xc = x - mu
    var = jnp.mean(xc * xc, -1, keepdims=True)
    return xc * lax.rsqrt(var + LN_EPS) * g + b


def _dot(a, b):
    return jnp.dot(a, b, preferred_element_type=F32)


def _dot_nt(a, b):
    return lax.dot_general(a, b, (((1,), (1,)), ((), ())), preferred_element_type=F32)


def _load_row_tiles(ref, rows):
    return jnp.concatenate([ref[pl.ds(c, rows, stride=ROW_TILE), :] for c in range(ROW_TILE)], axis=1)


def _store_row_tiles(ref, val, rows):
    for c in range(ROW_TILE):
        ref[pl.ds(c, rows, stride=ROW_TILE), :] = val[:, c * LANES:(c + 1) * LANES]


def _tile(ref, row):
    return ref.at[pl.ds(pl.multiple_of(row * ROW_TILE, ROW_TILE), ROW_TILE)]


def _in_proj_body(x_ref, g0_ref, b0_ref, w_ref, bd_ref, cq_ref, s1q_ref, s2q_ref,
                  ck_ref, s1k_ref, s2k_ref,
                  qa_ref, ka_ref, vat_ref, qb_ref, kb_ref, vbt_ref, sga_ref, sgb_ref):
    h = _ln(x_ref[0], g0_ref[...], b0_ref[...])
    hb = h.astype(BF16)

    def proj(lo, hi):
        return _dot(hb, w_ref[:, lo:hi])

    def norm_rope(t, width, c_ref, s1_ref, s2_ref):
        ss = _dot((t * t).astype(BF16), bd_ref[:width, :width])
        r = lax.rsqrt(ss * (1.0 / HEAD_DIM) + RMS_EPS)
        rot = (t * c_ref[...] + pltpu.roll(t, width - 16, 1) * s1_ref[...]
               + pltpu.roll(t, 16, 1) * s2_ref[...])
        return rot * r

    o = 0
    qa_ref[0] = (proj(o, o + Q_W) * QK_SCALE).astype(BF16)
    o += Q_W
    ka_ref[0] = proj(o, o + KV_W).astype(BF16)
    o += KV_W
    qb = norm_rope(proj(o, o + Q_W), Q_W, cq_ref, s1q_ref, s2q_ref)
    qb_ref[0] = (qb * (QK_SCALE * LOG2_E)).astype(BF16)
    o += Q_W
    kb_ref[0] = norm_rope(proj(o, o + KV_W), KV_W, ck_ref, s1k_ref, s2k_ref).astype(BF16)
    o += KV_W
    vat_ref[0] = proj(o, o + KV_W).T.astype(BF16)
    o += KV_W
    vbt_ref[0] = proj(o, o + KV_W).T.astype(BF16)
    o += KV_W
    d = sga_ref.shape[-1]
    sga_ref[0] = jax.nn.sigmoid(proj(o, o + d)).astype(BF16)
    o += d
    sgb_ref[0] = jax.nn.sigmoid(proj(o, o + d)).astype(BF16)


def _in_proj(x, g0, b0, w, bd, tabs):
    bsz, seq, d = x.shape
    tm = TM_PROJ
    cq, s1q, s2q, ck, s1k, s2k = tabs
    n_in = w.shape[1]
    const = lambda i, j: (0, 0)
    tok3 = lambda i, j: (j, i, 0)
    tab = lambda i, j: (i, 0)
    in_specs = [
        pl.BlockSpec((1, tm, d), tok3),
        pl.BlockSpec((1, d), const), pl.BlockSpec((1, d), const),
        pl.BlockSpec((d, n_in), const),
        pl.BlockSpec((Q_W, Q_W), const),
        pl.BlockSpec((tm, Q_W), tab), pl.BlockSpec((tm, Q_W), tab), pl.BlockSpec((tm, Q_W), tab),
        pl.BlockSpec((tm, KV_W), tab), pl.BlockSpec((tm, KV_W), tab), pl.BlockSpec((tm, KV_W), tab),
    ]
    tr3 = lambda i, j: (j, 0, i)
    out_specs = [
        pl.BlockSpec((1, tm, Q_W), tok3), pl.BlockSpec((1, tm, KV_W), tok3),
        pl.BlockSpec((1, KV_W, tm), tr3),
        pl.BlockSpec((1, tm, Q_W), tok3), pl.BlockSpec((1, tm, KV_W), tok3),
        pl.BlockSpec((1, KV_W, tm), tr3),
        pl.BlockSpec((1, tm, d), tok3), pl.BlockSpec((1, tm, d), tok3),
    ]
    out_shape = [
        jax.ShapeDtypeStruct((bsz, seq, Q_W), BF16), jax.ShapeDtypeStruct((bsz, seq, KV_W), BF16),
        jax.ShapeDtypeStruct((bsz, KV_W, seq), BF16),
        jax.ShapeDtypeStruct((bsz, seq, Q_W), BF16), jax.ShapeDtypeStruct((bsz, seq, KV_W), BF16),
        jax.ShapeDtypeStruct((bsz, KV_W, seq), BF16),
        jax.ShapeDtypeStruct((bsz, seq, d), BF16), jax.ShapeDtypeStruct((bsz, seq, d), BF16),
    ]
    return pl.pallas_call(
        _in_proj_body, grid=(seq // tm, bsz), in_specs=in_specs, out_specs=out_specs,
        out_shape=out_shape, name="in_proj",
        compiler_params=pltpu.CompilerParams(
            dimension_semantics=("arbitrary", "arbitrary"), vmem_limit_bytes=48 * 1024 * 1024),
    )(x, g0, b0, w, bd, cq, s1q, s2q, ck, s1k, s2k)


def _half_mask(rows, c):
    lane = lax.broadcasted_iota(jnp.int32, (rows, LANES), 1)
    return (lane >= HEAD_DIM) if c == 1 else (lane < HEAD_DIM)


def _win_attn_body(sink_ref, q_ref, k0_ref, k1_ref, k2_ref, v0_ref, v1_ref, v2_ref, o_ref, *, seq):
    n = pl.program_id(1)
    k = jnp.concatenate([k0_ref[0], k1_ref[0], k2_ref[0]], axis=0)
    vt = jnp.concatenate([v0_ref[0], v1_ref[0], v2_ref[0]], axis=1)
    kk = lax.broadcasted_iota(jnp.int32, (3 * BLOCK, BLOCK), 0)
    qq = lax.broadcasted_iota(jnp.int32, (3 * BLOCK, BLOCK), 1)
    dist_i = jnp.abs(kk - BLOCK - qq)
    k_pos = n * BLOCK - BLOCK + kk
    valid = (dist_i <= WINDOW) & (k_pos >= 0) & (k_pos < seq)
    dist = dist_i.astype(F32)
    slabs = [q_ref[0, :, j * LANES:(j + 1) * LANES] for j in range(4)]
    qm = jnp.concatenate(
        [jnp.where(_half_mask(BLOCK, c), s, jnp.zeros_like(s)) for c in range(2) for s in slabs], axis=0)
    st_all = _dot_nt(k, qm)
    ots = []
    for c in range(2):
        ps, dens = [], []
        for j in range(4):
            head = j + 4 * c
            col = (4 * c + j) * BLOCK
            st = st_all[:, col:col + BLOCK] + jnp.where(valid, -ALIBI_SLOPES[head] * dist, NEG_INF)
            sk = sink_ref[head]
            m = jnp.maximum(jnp.max(st, axis=0, keepdims=True), sk)
            p = jnp.exp(st - m)
            dens.append(jnp.sum(p, axis=0, keepdims=True) + jnp.exp(sk - m))
            ps.append(p.astype(BF16))
        ot = _dot(vt[c * HEAD_DIM:(c + 1) * HEAD_DIM, :], jnp.concatenate(ps, axis=1))
        ots.append(ot / jnp.concatenate(dens, axis=1))
    for j in range(4):
        pair = jnp.concatenate([ot[:, j * BLOCK:(j + 1) * BLOCK] for ot in ots], axis=0)
        o_ref[0, :, j * LANES:(j + 1) * LANES] = pair.T.astype(BF16)


def _win_attn(sink, qa, ka, vat):
    bsz, seq, _ = qa.shape
    nb = seq // BLOCK
    qmap = lambda b, n: (b, n, 0)
    kspec = lambda f: pl.BlockSpec((1, BLOCK, KV_W), lambda b, n: (b, f(n), 0))
    vspec = lambda f: pl.BlockSpec((1, KV_W, BLOCK), lambda b, n: (b, 0, f(n)))
    prev = lambda n: jnp.maximum(n - 1, 0)
    cur = lambda n: n
    nxt = lambda n: jnp.minimum(n + 1, nb - 1)
    return pl.pallas_call(
        functools.partial(_win_attn_body, seq=seq), grid=(bsz, nb),
        in_specs=[pl.BlockSpec(memory_space=pltpu.SMEM),
                  pl.BlockSpec((1, BLOCK, Q_W), qmap),
                  kspec(prev), kspec(cur), kspec(nxt), vspec(prev), vspec(cur), vspec(nxt)],
        out_specs=pl.BlockSpec((1, BLOCK, Q_W), qmap),
        out_shape=jax.ShapeDtypeStruct((bsz, seq, Q_W), BF16), name="win_attn",
        compiler_params=pltpu.CompilerParams(dimension_semantics=("arbitrary", "arbitrary")),
    )(sink, qa, ka, ka, ka, vat, vat, vat)


def _grid_attn_body(q_ref, k_ref, vt_ref, o_ref, s0_ref, s1_ref, p0_ref, p1_ref):
    tq = q_ref.shape[1]
    seq = k_ref.shape[1]
    kc = KEY_CHUNK
    n_chunks = seq // kc
    s_bufs, p_bufs = (s0_ref, s1_ref), (p0_ref, p1_ref)
    heads = [(j, c) for j in range(4) for c in range(2)]
    ones = jnp.ones((2 * ROW_TILE, seq), BF16)

    def masked_q(h):
        j, c = heads[h]
        slab = q_ref[0, :, j * LANES:(j + 1) * LANES]
        return jnp.where(_half_mask(tq, c), slab, jnp.zeros_like(slab))

    def score_chunk(h, qm, kb, m8):
        sc = _dot_nt(k_ref[0, kb * kc:(kb + 1) * kc, :], qm)
        s_bufs[h % 2][kb * kc:(kb + 1) * kc, :] = sc
        cm = jnp.max(sc.reshape(kc // ROW_TILE, ROW_TILE, tq), axis=0)
        return cm if m8 is None else jnp.maximum(m8, cm)

    def prob_chunk(h, kb, m):
        x = s_bufs[h % 2][kb * kc:(kb + 1) * kc, :] - m
        p_bufs[h % 2][kb * kc:(kb + 1) * kc, :] = jnp.exp2(x.astype(BF16))

    qm = masked_q(0)
    m8 = None
    for kb in range(n_chunks):
        m8 = score_chunk(0, qm, kb, m8)
    outs = []
    for h in range(len(heads)):
        m = jnp.max(m8, axis=0, keepdims=True)
        nxt = h + 1 < len(heads)
        if nxt:
            qm = masked_q(h + 1)
            m8 = None
        for kb in range(n_chunks):
            if nxt:
                m8 = score_chunk(h + 1, qm, kb, m8)
            prob_chunk(h, kb, m)
        j, c = heads[h]
        va = jnp.concatenate([vt_ref[0, c * HEAD_DIM:(c + 1) * HEAD_DIM, :], ones], axis=0)
        ot = _dot(va, p_bufs[h % 2][...])
        outs.append(ot[:HEAD_DIM] / ot[HEAD_DIM:HEAD_DIM + 1])
        if c == 1:
            o_ref[0, :, j * LANES:(j + 1) * LANES] = jnp.concatenate(outs, axis=0).T.astype(BF16)
            outs = []


def _grid_attn(qb, kb, vbt):
    bsz, seq, _ = qb.shape
    tq = TQ_GRID
    return pl.pallas_call(
        _grid_attn_body, grid=(bsz, seq // tq),
        in_specs=[pl.BlockSpec((1, tq, Q_W), lambda b, n: (b, n, 0)),
                  pl.BlockSpec((1, seq, KV_W), lambda b, n: (b, 0, 0)),
                  pl.BlockSpec((1, KV_W, seq), lambda b, n: (b, 0, 0))],
        out_specs=pl.BlockSpec((1, tq, Q_W), lambda b, n: (b, n, 0)),
        scratch_shapes=[pltpu.VMEM((seq, tq), F32), pltpu.VMEM((seq, tq), F32),
                        pltpu.VMEM((seq, tq), BF16), pltpu.VMEM((seq, tq), BF16)],
        out_shape=jax.ShapeDtypeStruct((bsz, seq, Q_W), BF16), name="grid_attn",
        compiler_params=pltpu.CompilerParams(
            dimension_semantics=("arbitrary", "arbitrary"), vmem_limit_bytes=40 * 1024 * 1024),
    )(qb, kb, vbt)


def _post_attn_body(x_ref, oa_ref, ob_ref, sga_ref, sgb_ref, g0_ref, b0_ref, wa_ref, wb_ref, wo_ref,
                    g1_ref, b1_ref, wr_ref, br_ref, h1t_ref, ti_ref, tw_ref, rk_ref, cnt_out_ref, cnt_ref):
    h0 = _ln(x_ref[0], g0_ref[...], b0_ref[...])
    out_a = _dot(oa_ref[0], wa_ref[...])
    out_b = _dot(ob_ref[0], wb_ref[...])
    merged = sga_ref[0].astype(F32) * out_a + sgb_ref[0].astype(F32) * out_b
    mix = _dot(merged.astype(BF16), wo_ref[...])
    h1 = _ln(DN_ALPHA * h0 + mix, g1_ref[...], b1_ref[...])
    tm = h1.shape[0]
    _store_row_tiles(h1t_ref.at[0], h1, tm)

    logits = _dot(h1.astype(BF16), wr_ref[...]) + br_ref[...]
    lane = lax.broadcasted_iota(jnp.int32, (tm, LANES), 1)
    cur = jnp.where(lane < N_EXPERTS, logits, -jnp.inf)
    vals, idxs = [], []
    for _ in range(TOP_K):
        mv = jnp.max(cur, axis=-1, keepdims=True)
        ix = jnp.min(jnp.where(cur == mv, lane, LANES), axis=-1, keepdims=True)
        vals.append(mv)
        idxs.append(ix)
        cur = jnp.where(lane == ix, -jnp.inf, cur)
    es = [jnp.exp(v - vals[0]) for v in vals]
    tot = es[0] + es[1] + es[2] + es[3]

    @pl.when((pl.program_id(0) == 0) & (pl.program_id(1) == 0))
    def _():
        cnt_ref[...] = jnp.zeros_like(cnt_ref)

    sel = jnp.zeros((tm, LANES), F32)
    for kx in range(TOP_K):
        sel = sel + (lane == idxs[kx]).astype(F32)
    r_i = lax.broadcasted_iota(jnp.int32, (tm, tm), 0)
    c_i = lax.broadcasted_iota(jnp.int32, (tm, tm), 1)
    tri = (c_i < r_i).astype(BF16)
    rank = _dot(tri, sel.astype(BF16)) + cnt_ref[...]
    cnt_ref[...] = cnt_ref[...] + jnp.sum(sel, axis=0, keepdims=True)
    cnt_out_ref[...] = cnt_ref[...]

    ti = jnp.zeros((tm, LANES), jnp.int32)
    tw = jnp.zeros((tm, LANES), F32)
    rk = jnp.zeros((tm, LANES), jnp.int32)
    for kx in range(TOP_K):
        ti = jnp.where(lane == kx, idxs[kx], ti)
        tw = jnp.where(lane == kx, es[kx] / tot, tw)
        rk_k = jnp.sum(jnp.where(lane == idxs[kx], rank, 0.0), axis=-1, keepdims=True)
        rk = jnp.where(lane == kx, rk_k.astype(jnp.int32), rk)
    ti_ref[0] = ti
    tw_ref[0] = tw
    rk_ref[0] = rk


def _post_attn(x, oa, ob, sga, sgb, g0, b0, wa, wb, wo, g1, b1, wr, br):
    bsz, seq, d = x.shape
    tm = TM_PROJ
    tok3 = lambda b, i: (b, i, 0)
    const = lambda b, i: (0, 0)
    full = lambda a: pl.BlockSpec(a.shape, const)
    return pl.pallas_call(
        _post_attn_body, grid=(bsz, seq // tm),
        in_specs=[pl.BlockSpec((1, tm, d), tok3),
                  pl.BlockSpec((1, tm, Q_W), tok3), pl.BlockSpec((1, tm, Q_W), tok3),
                  pl.BlockSpec((1, tm, d), tok3), pl.BlockSpec((1, tm, d), tok3),
                  full(g0), full(b0), full(wa), full(wb), full(wo), full(g1), full(b1),
                  full(wr), full(br)],
        out_specs=[pl.BlockSpec((1, tm * ROW_TILE, LANES), tok3), pl.BlockSpec((1, tm, LANES), tok3),
                   pl.BlockSpec((1, tm, LANES), tok3), pl.BlockSpec((1, tm, LANES), tok3),
                   pl.BlockSpec((1, LANES), const)],
        out_shape=[jax.ShapeDtypeStruct((bsz, seq * ROW_TILE, LANES), F32),
                   jax.ShapeDtypeStruct((bsz, seq, LANES), jnp.int32),
                   jax.ShapeDtypeStruct((bsz, seq, LANES), F32),
                   jax.ShapeDtypeStruct((bsz, seq, LANES), jnp.int32),
                   jax.ShapeDtypeStruct((1, LANES), F32)],
        scratch_shapes=[pltpu.VMEM((1, LANES), F32)],
        name="post_attn",
        compiler_params=pltpu.CompilerParams(
            dimension_semantics=("arbitrary", "arbitrary"), vmem_limit_bytes=48 * 1024 * 1024),
    )(x, oa, ob, sga, sgb, g0, b0, wa, wb, wo, g1, b1, wr, br)


def _dispatch_body(dest_ref, pend_ref, padded_ref, nu_ref, h1t_ref, xs_hbm, zbuf, sem, zsem):
    tm = TM_DISPATCH
    zrows = BM_EXPERT * ROW_TILE
    n_blocks = xs_hbm.shape[0] // zrows

    @pl.when(pl.program_id(0) == 0)
    def _():
        zbuf[...] = jnp.zeros_like(zbuf)
        zero_wait = pltpu.make_async_copy(zbuf, xs_hbm.at[pl.ds(0, zrows)], zsem).wait
        for e in range(N_EXPERTS):
            @pl.when(padded_ref[e] > 0)
            def _():
                start = pl.multiple_of((pend_ref[e] - BM_EXPERT) * ROW_TILE, ROW_TILE)
                pltpu.make_async_copy(zbuf, xs_hbm.at[pl.ds(start, zrows)], zsem).start()
        for b in range(n_blocks - N_EXPERTS, n_blocks):
            @pl.when(b >= nu_ref[0])
            def _():
                pltpu.make_async_copy(zbuf, xs_hbm.at[pl.ds(b * zrows, zrows)], zsem).start()
        for e in range(N_EXPERTS):
            pl.when(padded_ref[e] > 0)(zero_wait)
        for b in range(n_blocks - N_EXPERTS, n_blocks):
            pl.when(b >= nu_ref[0])(zero_wait)

    def issue(t, carry):
        for kx in range(TOP_K):
            d = dest_ref[TOP_K * t + kx]
            pltpu.make_async_copy(_tile(h1t_ref, t), _tile(xs_hbm, d), sem).start()
        return carry

    lax.fori_loop(0, tm, issue, 0)
    for _ in range(TOP_K):
        pltpu.make_async_copy(h1t_ref, xs_hbm.at[pl.ds(0, tm * ROW_TILE)], sem).wait()


def _dispatch(dest_flat, pad_end, padded, n_used, h1t, n_rows):
    n_tok = h1t.shape[0] // ROW_TILE
    tm = TM_DISPATCH
    return pl.pallas_call(
        _dispatch_body, grid=(n_tok // tm,),
        in_specs=[pl.BlockSpec((TOP_K * tm,), lambda i: (i,), memory_space=pltpu.SMEM),
                  pl.BlockSpec(memory_space=pltpu.SMEM), pl.BlockSpec(memory_space=pltpu.SMEM),
                  pl.BlockSpec(memory_space=pltpu.SMEM),
                  pl.BlockSpec((tm * ROW_TILE, LANES), lambda i: (i, 0))],
        out_specs=pl.BlockSpec(memory_space=pl.ANY),
        out_shape=jax.ShapeDtypeStruct((n_rows * ROW_TILE, LANES), F32),
        scratch_shapes=[pltpu.VMEM((BM_EXPERT * ROW_TILE, LANES), F32),
                        pltpu.SemaphoreType.DMA(()), pltpu.SemaphoreType.DMA(())],
        name="dispatch",
        compiler_params=pltpu.CompilerParams(dimension_semantics=("arbitrary",)),
    )(dest_flat, pad_end, padded, n_used, h1t)


def _experts_body(be_ref, nu_ref, xs_ref, wg_ref, bg_ref, wu_ref, bu_ref, wd_ref, bd_ref, ys_ref,
                  wg_s, wu_s, wd_s):
    i = pl.program_id(0)
    bm = BM_EXPERT
    used = i < nu_ref[0]
    prev = be_ref[jnp.maximum(i - 1, 0)]
    fresh = (i == 0) | (be_ref[i] != prev)

    @pl.when(used & fresh)
    def _():
        wg_s[...] = wg_ref[0].astype(BF16)
        wu_s[...] = wu_ref[0].astype(BF16)
        wd_s[...] = wd_ref[0].astype(BF16)

    @pl.when(used)
    def _():
        xb = _load_row_tiles(xs_ref, bm).astype(BF16)
        g = _dot(xb, wg_s[...]) + bg_ref[0]
        u = _dot(xb, wu_s[...]) + bu_ref[0]
        g = jnp.minimum(g, SWIGLU_LIMIT)
        u = jnp.clip(u, -SWIGLU_LIMIT, SWIGLU_LIMIT)
        act = g * jax.nn.sigmoid(SWIGLU_ALPHA * g) * (u + 1.0)
        _store_row_tiles(ys_ref, _dot(act.astype(BF16), wd_s[...]) + bd_ref[0], bm)

    @pl.when(jnp.logical_not(used))
    def _():
        ys_ref[...] = jnp.zeros_like(ys_ref)


def _experts(block_e, n_used, xs, wg, bg, wu, bu, wd, bd):
    n_rows = xs.shape[0] // ROW_TILE
    bm = BM_EXPERT
    n_e, d, d_ff = wg.shape
    row = lambda i, be, nu: (jnp.minimum(i, nu[0] - 1), 0)
    exp3 = lambda i, be, nu: (be[jnp.minimum(i, nu[0] - 1)], 0, 0)
    grid_spec = pltpu.PrefetchScalarGridSpec(
        num_scalar_prefetch=2, grid=(n_rows // bm,),
        in_specs=[pl.BlockSpec((bm * ROW_TILE, LANES), row),
                  pl.BlockSpec((1, d, d_ff), exp3), pl.BlockSpec((1, 1, d_ff), exp3),
                  pl.BlockSpec((1, d, d_ff), exp3), pl.BlockSpec((1, 1, d_ff), exp3),
                  pl.BlockSpec((1, d_ff, d), exp3), pl.BlockSpec((1, 1, d), exp3)],
        out_specs=pl.BlockSpec((bm * ROW_TILE, LANES), lambda i, be, nu: (i, 0)),
        scratch_shapes=[pltpu.VMEM((d, d_ff), BF16), pltpu.VMEM((d, d_ff), BF16),
                        pltpu.VMEM((d_ff, d), BF16)])
    return pl.pallas_call(
        _experts_body, grid_spec=grid_spec,
        out_shape=jax.ShapeDtypeStruct((n_rows * ROW_TILE, LANES), F32), name="experts",
        compiler_params=pltpu.CompilerParams(
            dimension_semantics=("arbitrary",), vmem_limit_bytes=56 * 1024 * 1024),
    )(block_e, n_used, xs, wg, bg.reshape(n_e, 1, d_ff), wu, bu.reshape(n_e, 1, d_ff),
      wd, bd.reshape(n_e, 1, d))


def _combine_body(dest_ref, h1t_ref, tw_ref, g2_ref, b2_ref, ys_hbm, out_ref, buf, sem):
    tm = TM_COMBINE

    def issue(t, carry):
        for kx in range(TOP_K):
            d = dest_ref[TOP_K * t + kx]
            pltpu.make_async_copy(_tile(ys_hbm, d), _tile(buf.at[kx], t), sem).start()
        return carry

    lax.fori_loop(0, tm, issue, 0)
    for kx in range(TOP_K):
        pltpu.make_async_copy(ys_hbm.at[pl.ds(0, tm * ROW_TILE)], buf.at[kx], sem).wait()
    tw = tw_ref[...]
    ffn = tw[:, 0:1] * _load_row_tiles(buf.at[0], tm)
    for kx in range(1, TOP_K):
        ffn = ffn + tw[:, kx:kx + 1] * _load_row_tiles(buf.at[kx], tm)
    h1 = _load_row_tiles(h1t_ref, tm)
    out_ref[...] = _ln(DN_ALPHA * h1 + ffn, g2_ref[...], b2_ref[...])


def _combine(dest_flat, h1t, tw, g2, b2, ys):
    n_tok = h1t.shape[0] // ROW_TILE
    d = ROW_TILE * LANES
    tm = TM_COMBINE
    const = lambda i: (0, 0)
    return pl.pallas_call(
        _combine_body, grid=(n_tok // tm,),
        in_specs=[pl.BlockSpec((TOP_K * tm,), lambda i: (i,), memory_space=pltpu.SMEM),
                  pl.BlockSpec((tm * ROW_TILE, LANES), lambda i: (i, 0)),
                  pl.BlockSpec((tm, LANES), lambda i: (i, 0)),
                  pl.BlockSpec((1, d), const), pl.BlockSpec((1, d), const),
                  pl.BlockSpec(memory_space=pl.ANY)],
        out_specs=pl.BlockSpec((tm, d), lambda i: (i, 0)),
        out_shape=jax.ShapeDtypeStruct((n_tok, d), F32),
        scratch_shapes=[pltpu.VMEM((TOP_K, tm * ROW_TILE, LANES), F32), pltpu.SemaphoreType.DMA(())],
        name="combine",
        compiler_params=pltpu.CompilerParams(dimension_semantics=("arbitrary",)),
    )(dest_flat, h1t, tw, g2, b2, ys)


def _rope_tables(seq, gain):
    t = np.arange(seq)
    row = (t // GRID_W).astype(np.float32)
    col = (t % GRID_W).astype(np.float32)
    half = HEAD_DIM // 2
    quarter = half // 2
    inv = (ROPE_THETA ** (-np.arange(quarter, dtype=np.float32) * np.float32(2.0 / half))).astype(np.float32)
    ang_r = row[:, None] * inv[None, :]
    ang_c = col[:, None] * inv[None, :]
    zeros = np.zeros_like(ang_r)
    cos = np.concatenate([np.cos(ang_r), np.cos(ang_r), np.cos(ang_c), np.cos(ang_c)], -1)
    s_lo = np.concatenate([-np.sin(ang_r), zeros, -np.sin(ang_c), zeros], -1)
    s_hi = np.concatenate([zeros, np.sin(ang_r), zeros, np.sin(ang_c)], -1)
    g = gain.astype(F32)
    return (jnp.asarray(cos, F32) * g[None, :], jnp.asarray(s_lo, F32) * jnp.roll(g, -quarter)[None, :],
            jnp.asarray(s_hi, F32) * jnp.roll(g, quarter)[None, :])


def _routing(top_i, rank, counts, bm):
    n_tok = top_i.shape[0]
    padded = (counts + bm - 1) // bm * bm
    pad_end = jnp.cumsum(padded)
    pad_start = pad_end - padded
    dest = (jnp.take(pad_start, top_i) + rank).astype(jnp.int32)
    n_rows = n_tok * TOP_K + N_EXPERTS * bm
    n_blocks = n_rows // bm
    block_e = jnp.minimum(
        jnp.searchsorted(pad_end, jnp.arange(n_blocks, dtype=jnp.int32) * bm, side='right'),
        N_EXPERTS - 1).astype(jnp.int32)
    n_used = (pad_end[-1] // bm).astype(jnp.int32).reshape(1)
    return dest.reshape(-1), block_e, n_used, n_rows, pad_end.astype(jnp.int32), padded.astype(jnp.int32)


def kernel(x, ln0_g, ln0_b, w_in, a_sink, b_q_norm, b_k_norm, w_branch_a, w_branch_b, w_out,
           ln1_g, ln1_b, w_router, b_router, w_gate, b_gate, w_up, b_up, w_down, b_down,
           ln2_g, ln2_b):
    bsz, seq, d = x.shape
    assert w_in.shape[0] == DEPTH == 1
    assert seq % TM_PROJ == 0 and seq % TQ_GRID == 0 and seq == (seq // GRID_W) * GRID_W
    n_tok = bsz * seq
    row2 = lambda a: a.reshape(1, -1)

    w = w_in[0]
    o_ka, o_va, o_qb, o_kb, o_vb, o_g = Q_W, Q_W + KV_W, Q_W + 2 * KV_W, 2 * Q_W + 2 * KV_W, \
        2 * Q_W + 3 * KV_W, 2 * Q_W + 4 * KV_W
    w_perm = jnp.concatenate([
        w[:, :Q_W][:, _PAIRED], w[:, o_ka:o_va],
        w[:, o_qb:o_kb][:, _PAIRED], w[:, o_kb:o_vb],
        w[:, o_va:o_qb], w[:, o_vb:o_g], w[:, o_g:]], axis=1).astype(BF16)
    head_id = np.arange(Q_W) // HEAD_DIM
    bd = jnp.asarray(head_id[:, None] == head_id[None, :], BF16)
    cq, s1q, s2q = (jnp.tile(t, (1, N_HEADS)) for t in _rope_tables(seq, b_q_norm[0]))
    ck, s1k, s2k = (jnp.tile(t, (1, N_KV)) for t in _rope_tables(seq, b_k_norm[0]))

    qa, ka, vat, qb, kb, vbt, sga, sgb = _in_proj(
        x, row2(ln0_g), row2(ln0_b), w_perm, bd, (cq, s1q, s2q, ck, s1k, s2k))
    oa = _win_attn(a_sink[0].astype(F32), qa, ka, vat)
    ob = _grid_attn(qb, kb, vbt)

    wr = jnp.zeros((d, LANES), BF16).at[:, :N_EXPERTS].set(w_router[0].astype(BF16))
    br = jnp.zeros((1, LANES), F32).at[0, :N_EXPERTS].set(b_router[0])
    h1t, ti, tw, rk, cnt = _post_attn(
        x, oa, ob, sga, sgb, row2(ln0_g), row2(ln0_b),
        w_branch_a[0][_PAIRED].astype(BF16), w_branch_b[0][_PAIRED].astype(BF16),
        w_out[0].astype(BF16), row2(ln1_g[0]), row2(ln1_b[0]), wr, br)
    h1t = h1t.reshape(n_tok * ROW_TILE, LANES)
    top_i = ti.reshape(n_tok, LANES)[:, :TOP_K]
    tw = tw.reshape(n_tok, LANES)

    rank = rk.reshape(n_tok, LANES)[:, :TOP_K]
    counts = cnt[0, :N_EXPERTS].astype(jnp.int32)
    dest, block_e, n_used, n_rows, pad_end, padded = _routing(top_i, rank, counts, BM_EXPERT)
    xs = _dispatch(dest, pad_end, padded, n_used, h1t, n_rows)
    ys = _experts(block_e, n_used, xs, w_gate[0], b_gate[0], w_up[0], b_up[0], w_down[0], b_down[0])
    out = _combine(dest, h1t, tw, row2(ln2_g[0]), row2(ln2_b[0]), ys)
    return out.reshape(bsz, seq, d)
```

```python
import functools

import jax
import jax.numpy as jnp
import numpy as np
from jax import lax
from jax.experimental import pallas as pl
from jax.experimental.pallas import tpu as pltpu

HEAD_DIM = 64
N_HEADS = 8
N_KV = 2
WINDOW = 128
BLOCK = 128
GRID_W = 64
ROPE_THETA = 10000.0
N_EXPERTS = 32
TOP_K = 4
SWIGLU_LIMIT = 7.0
SWIGLU_ALPHA = 1.702
LN_EPS = 1e-5
RMS_EPS = 1e-6
NEG_INF = -1e30
DEPTH = 1
DN_ALPHA = (2.0 * DEPTH) ** 0.25
ALIBI_SLOPES = tuple(2.0 ** (-8.0 * (h + 1) / N_HEADS) for h in range(N_HEADS))
QK_SCALE = HEAD_DIM ** -0.5
LOG2_E = 1.4426950408889634

LANES = 128
ROW_TILE = 8
Q_W = N_HEADS * HEAD_DIM
KV_W = N_KV * HEAD_DIM

TM_PROJ = 512
TQ_GRID = 256
KEY_CHUNK = 256
BM_EXPERT = 512
TM_DISPATCH = 512
TM_COMBINE = 256

F32 = jnp.float32
BF16 = jnp.bfloat16

_PAIRED = np.array([(j if c == 0 else 4 + j) * HEAD_DIM + d
                    for j in range(4) for c in range(2) for d in range(HEAD_DIM)], np.int32)


def _ln(x, g, b):
    mu = jnp.mean(x, -1, keepdims=True)
    xc = x - mu
    var = jnp.mean(xc * xc, -1, keepdims=True)
    return xc * lax.rsqrt(var + LN_EPS) * g + b


def _dot(a, b):
    return jnp.dot(a, b, preferred_element_type=F32)


def _dot_nt(a, b):
    return lax.dot_general(a, b, (((1,), (1,)), ((), ())), preferred_element_type=F32)


def _load_row_tiles(ref, rows):
    return jnp.concatenate([ref[pl.ds(c, rows, stride=ROW_TILE), :] for c in range(ROW_TILE)], axis=1)


def _store_row_tiles(ref, val, rows):
    for c in range(ROW_TILE):
        ref[pl.ds(c, rows, stride=ROW_TILE), :] = val[:, c * LANES:(c + 1) * LANES]


def _tile(ref, row):
    return ref.at[pl.ds(pl.multiple_of(row * ROW_TILE, ROW_TILE), ROW_TILE)]


def _in_proj_body(x_ref, g0_ref, b0_ref, w_ref, bd_ref, cq_ref, s1q_ref, s2q_ref,
                  ck_ref, s1k_ref, s2k_ref,
                  qa_ref, ka_ref, vat_ref, qb_ref, kb_ref, vbt_ref, sga_ref, sgb_ref):
    h = _ln(x_ref[0], g0_ref[...], b0_ref[...])
    hb = h.astype(BF16)

    def proj(lo, hi):
        return _dot(hb, w_ref[:, lo:hi])

    def norm_rope(t, width, c_ref, s1_ref, s2_ref):
        ss = _dot((t * t).astype(BF16), bd_ref[:width, :width])
        r = lax.rsqrt(ss * (1.0 / HEAD_DIM) + RMS_EPS)
        rot = (t * c_ref[...] + pltpu.roll(t, width - 16, 1) * s1_ref[...]
               + pltpu.roll(t, 16, 1) * s2_ref[...])
        return rot * r

    o = 0
    qa_ref[0] = (proj(o, o + Q_W) * QK_SCALE).astype(BF16)
    o += Q_W
    ka_ref[0] = proj(o, o + KV_W).astype(BF16)
    o += KV_W
    qb = norm_rope(proj(o, o + Q_W), Q_W, cq_ref, s1q_ref, s2q_ref)
    qb_ref[0] = (qb * (QK_SCALE * LOG2_E)).astype(BF16)
    o += Q_W
    kb_ref[0] = norm_rope(proj(o, o + KV_W), KV_W, ck_ref, s1k_ref, s2k_ref).astype(BF16)
    o += KV_W
    vat_ref[0] = proj(o, o + KV_W).T.astype(BF16)
    o += KV_W
    vbt_ref[0] = proj(o, o + KV_W).T.astype(BF16)
    o += KV_W
    d = sga_ref.shape[-1]
    sga_ref[0] = jax.nn.sigmoid(proj(o, o + d)).astype(BF16)
    o += d
    sgb_ref[0] = jax.nn.sigmoid(proj(o, o + d)).astype(BF16)


def _in_proj(x, g0, b0, w, bd, tabs):
    bsz, seq, d = x.shape
    tm = TM_PROJ
    cq, s1q, s2q, ck, s1k, s2k = tabs
    n_in = w.shape[1]
    const = lambda i, j: (0, 0)
    tok3 = lambda i, j: (j, i, 0)
    tab = lambda i, j: (i, 0)
    in_specs = [
        pl.BlockSpec((1, tm, d), tok3),
        pl.BlockSpec((1, d), const), pl.BlockSpec((1, d), const),
        pl.BlockSpec((d, n_in), const),
        pl.BlockSpec((Q_W, Q_W), const),
        pl.BlockSpec((tm, Q_W), tab), pl.BlockSpec((tm, Q_W), tab), pl.BlockSpec((tm, Q_W), tab),
        pl.BlockSpec((tm, KV_W), tab), pl.BlockSpec((tm, KV_W), tab), pl.BlockSpec((tm, KV_W), tab),
    ]
    tr3 = lambda i, j: (j, 0, i)
    out_specs = [
        pl.BlockSpec((1, tm, Q_W), tok3), pl.BlockSpec((1, tm, KV_W), tok3),
        pl.BlockSpec((1, KV_W, tm), tr3),
        pl.BlockSpec((1, tm, Q_W), tok3), pl.BlockSpec((1, tm, KV_W), tok3),
        pl.BlockSpec((1, KV_W, tm), tr3),
        pl.BlockSpec((1, tm, d), tok3), pl.BlockSpec((1, tm, d), tok3),
    ]
    out_shape = [
        jax.ShapeDtypeStruct((bsz, seq, Q_W), BF16), jax.ShapeDtypeStruct((bsz, seq, KV_W), BF16),
        jax.ShapeDtypeStruct((bsz, KV_W, seq), BF16),
        jax.ShapeDtypeStruct((bsz, seq, Q_W), BF16), jax.ShapeDtypeStruct((bsz, seq, KV_W), BF16),
        jax.ShapeDtypeStruct((bsz, KV_W, seq), BF16),
        jax.ShapeDtypeStruct((bsz, seq, d), BF16), jax.ShapeDtypeStruct((bsz, seq, d), BF16),
    ]
    return pl.pallas_call(
        _in_proj_body, grid=(seq // tm, bsz), in_specs=in_specs, out_specs=out_specs,
        out_shape=out_shape, name="in_proj",
        compiler_params=pltpu.CompilerParams(
            dimension_semantics=("arbitrary", "arbitrary"), vmem_limit_bytes=48 * 1024 * 1024),
    )(x, g0, b0, w, bd, cq, s1q, s2q, ck, s1k, s2k)


def _half_mask(rows, c):
    lane = lax.broadcasted_iota(jnp.int32, (rows, LANES), 1)
    return (lane >= HEAD_DIM) if c == 1 else (lane < HEAD_DIM)


def _win_attn_body(sink_ref, q_ref, k0_ref, k1_ref, k2_ref, v0_ref, v1_ref, v2_ref, o_ref, *, seq):
    n = pl.program_id(1)
    k = jnp.concatenate([k0_ref[0], k1_ref[0], k2_ref[0]], axis=0)
    vt = jnp.concatenate([v0_ref[0], v1_ref[0], v2_ref[0]], axis=1)
    kk = lax.broadcasted_iota(jnp.int32, (3 * BLOCK, BLOCK), 0)
    qq = lax.broadcasted_iota(jnp.int32, (3 * BLOCK, BLOCK), 1)
    dist_i = jnp.abs(kk - BLOCK - qq)
    k_pos = n * BLOCK - BLOCK + kk
    valid = (dist_i <= WINDOW) & (k_pos >= 0) & (k_pos < seq)
    dist = dist_i.astype(F32)
    slabs = [q_ref[0, :, j * LANES:(j + 1) * LANES] for j in range(4)]
    qm = jnp.concatenate(
        [jnp.where(_half_mask(BLOCK, c), s, jnp.zeros_like(s)) for c in range(2) for s in slabs], axis=0)
    st_all = _dot_nt(k, qm)
    ots = []
    for c in range(2):
        ps, dens = [], []
        for j in range(4):
            head = j + 4 * c
            col = (4 * c + j) * BLOCK
            st = st_all[:, col:col + BLOCK] + jnp.where(valid, -ALIBI_SLOPES[head] * dist, NEG_INF)
            sk = sink_ref[head]
            m = jnp.maximum(jnp.max(st, axis=0, keepdims=True), sk)
            p = jnp.exp(st - m)
            dens.append(jnp.sum(p, axis=0, keepdims=True) + jnp.exp(sk - m))
            ps.append(p.astype(BF16))
        ot = _dot(vt[c * HEAD_DIM:(c + 1) * HEAD_DIM, :], jnp.concatenate(ps, axis=1))
        ots.append(ot / jnp.concatenate(dens, axis=1))
    for j in range(4):
        pair = jnp.concatenate([ot[:, j * BLOCK:(j + 1) * BLOCK] for ot in ots], axis=0)
        o_ref[0, :, j * LANES:(j + 1) * LANES] = pair.T.astype(BF16)


def _win_attn(sink, qa, ka, vat):
    bsz, seq, _ = qa.shape
    nb = seq // BLOCK
    qmap = lambda b, n: (b, n, 0)
    kspec = lambda f: pl.BlockSpec((1, BLOCK, KV_W), lambda b, n: (b, f(n), 0))
    vspec = lambda f: pl.BlockSpec((1, KV_W, BLOCK), lambda b, n: (b, 0, f(n)))
    prev = lambda n: jnp.maximum(n - 1, 0)
    cur = lambda n: n
    nxt = lambda n: jnp.minimum(n + 1, nb - 1)
    return pl.pallas_call(
        functools.partial(_win_attn_body, seq=seq), grid=(bsz, nb),
        in_specs=[pl.BlockSpec(memory_space=pltpu.SMEM),
                  pl.BlockSpec((1, BLOCK, Q_W), qmap),
                  kspec(prev), kspec(cur), kspec(nxt), vspec(prev), vspec(cur), vspec(nxt)],
        out_specs=pl.BlockSpec((1, BLOCK, Q_W), qmap),
        out_shape=jax.ShapeDtypeStruct((bsz, seq, Q_W), BF16), name="win_attn",
        compiler_params=pltpu.CompilerParams(dimension_semantics=("arbitrary", "arbitrary")),
    )(sink, qa, ka, ka, ka, vat, vat, vat)


def _grid_attn_body(q_ref, k_ref, vt_ref, o_ref, s0_ref, s1_ref, p0_ref, p1_ref):
    tq = q_ref.shape[1]
    seq = k_ref.shape[1]
    kc = KEY_CHUNK
    n_chunks = seq // kc
    s_bufs, p_bufs = (s0_ref, s1_ref), (p0_ref, p1_ref)
    heads = [(j, c) for j in range(4) for c in range(2)]
    ones = jnp.ones((2 * ROW_TILE, seq), BF16)

    def masked_q(h):
        j, c = heads[h]
        slab = q_ref[0, :, j * LANES:(j + 1) * LANES]
        return jnp.where(_half_mask(tq, c), slab, jnp.zeros_like(slab))

    def score_chunk(h, qm, kb, m8):
        sc = _dot_nt(k_ref[0, kb * kc:(kb + 1) * kc, :], qm)
        s_bufs[h % 2][kb * kc:(kb + 1) * kc, :] = sc
        cm = jnp.max(sc.reshape(kc // ROW_TILE, ROW_TILE, tq), axis=0)
        return cm if m8 is None else jnp.maximum(m8, cm)

    def prob_chunk(h, kb, m):
        x = s_bufs[h % 2][kb * kc:(kb + 1) * kc, :] - m
        p_bufs[h % 2][kb * kc:(kb + 1) * kc, :] = jnp.exp2(x.astype(BF16))

    qm = masked_q(0)
    m8 = None
    for kb in range(n_chunks):
        m8 = score_chunk(0, qm, kb, m8)
    outs = []
    for h in range(len(heads)):
        m = jnp.max(m8, axis=0, keepdims=True)
        nxt = h + 1 < len(heads)
        if nxt:
            qm = masked_q(h + 1)
            m8 = None
        for kb in range(n_chunks):
            if nxt:
                m8 = score_chunk(h + 1, qm, kb, m8)
            prob_chunk(h, kb, m)
        j, c = heads[h]
        va = jnp.concatenate([vt_ref[0, c * HEAD_DIM:(c + 1) * HEAD_DIM, :], ones], axis=0)
        ot = _dot(va, p_bufs[h % 2][...])
        outs.append(ot[:HEAD_DIM] / ot[HEAD_DIM:HEAD_DIM + 1])
        if c == 1:
            o_ref[0, :, j * LANES:(j + 1) * LANES] = jnp.concatenate(outs, axis=0).T.astype(BF16)
            outs = []


def _grid_attn(qb, kb, vbt):
    bsz, seq, _ = qb.shape
    tq = TQ_GRID
    return pl.pallas_call(
        _grid_attn_body, grid=(bsz, seq // tq),
        in_specs=[pl.BlockSpec((1, tq, Q_W), lambda b, n: (b, n, 0)),
                  pl.BlockSpec((1, seq, KV_W), lambda b, n: (b, 0, 0)),
                  pl.BlockSpec((1, KV_W, seq), lambda b, n: (b, 0, 0))],
        out_specs=pl.BlockSpec((1, tq, Q_W), lambda b, n: (b, n, 0)),
        scratch_shapes=[pltpu.VMEM((seq, tq), F32), pltpu.VMEM((seq, tq), F32),
                        pltpu.VMEM((seq, tq), BF16), pltpu.VMEM((seq, tq), BF16)],
        out_shape=jax.ShapeDtypeStruct((bsz, seq, Q_W), BF16), name="grid_attn",
        compiler_params=pltpu.CompilerParams(
            dimension_semantics=("arbitrary", "arbitrary"), vmem_limit_bytes=40 * 1024 * 1024),
    )(qb, kb, vbt)


def _post_attn_body(x_ref, oa_ref, ob_ref, sga_ref, sgb_ref, g0_ref, b0_ref, wa_ref, wb_ref, wo_ref,
                    g1_ref, b1_ref, wr_ref, br_ref, h1t_ref, ti_ref, tw_ref, rk_ref, cnt_out_ref, cnt_ref):
    h0 = _ln(x_ref[0], g0_ref[...], b0_ref[...])
    out_a = _dot(oa_ref[0], wa_ref[...])
    out_b = _dot(ob_ref[0], wb_ref[...])
    merged = sga_ref[0].astype(F32) * out_a + sgb_ref[0].astype(F32) * out_b
    mix = _dot(merged.astype(BF16), wo_ref[...])
    h1 = _ln(DN_ALPHA * h0 + mix, g1_ref[...], b1_ref[...])
    tm = h1.shape[0]
    _store_row_tiles(h1t_ref.at[0], h1, tm)

    logits = _dot(h1.astype(BF16), wr_ref[...]) + br_ref[...]
    lane = lax.broadcasted_iota(jnp.int32, (tm, LANES), 1)
    cur = jnp.where(lane < N_EXPERTS, logits, -jnp.inf)
    vals, idxs = [], []
    for _ in range(TOP_K):
        mv = jnp.max(cur, axis=-1, keepdims=True)
        ix = jnp.min(jnp.where(cur == mv, lane, LANES), axis=-1, keepdims=True)
        vals.append(mv)
        idxs.append(ix)
        cur = jnp.where(lane == ix, -jnp.inf, cur)
    es = [jnp.exp(v - vals[0]) for v in vals]
    tot = es[0] + es[1] + es[2] + es[3]

    @pl.when((pl.program_id(0) == 0) & (pl.program_id(1) == 0))
    def _():
        cnt_ref[...] = jnp.zeros_like(cnt_ref)

    sel = jnp.zeros((tm, LANES), F32)
    for kx in range(TOP_K):
        sel = sel + (lane == idxs[kx]).astype(F32)
    r_i = lax.broadcasted_iota(jnp.int32, (tm, tm), 0)
    c_i = lax.broadcasted_iota(jnp.int32, (tm, tm), 1)
    tri = (c_i < r_i).astype(BF16)
    rank = _dot(tri, sel.astype(BF16)) + cnt_ref[...]
    cnt_ref[...] = cnt_ref[...] + jnp.sum(sel, axis=0, keepdims=True)
    cnt_out_ref[...] = cnt_ref[...]

    ti = jnp.zeros((tm, LANES), jnp.int32)
    tw = jnp.zeros((tm, LANES), F32)
    rk = jnp.zeros((tm, LANES), jnp.int32)
    for kx in range(TOP_K):
        ti = jnp.where(lane == kx, idxs[kx], ti)
        tw = jnp.where(lane == kx, es[kx] / tot, tw)
        rk_k = jnp.sum(jnp.where(lane == idxs[kx], rank, 0.0), axis=-1, keepdims=True)
        rk = jnp.where(lane == kx, rk_k.astype(jnp.int32), rk)
    ti_ref[0] = ti
    tw_ref[0] = tw
    rk_ref[0] = rk


def _post_attn(x, oa, ob, sga, sgb, g0, b0, wa, wb, wo, g1, b1, wr, br):
    bsz, seq, d = x.shape
    tm = TM_PROJ
    tok3 = lambda b, i: (b, i, 0)
    const = lambda b, i: (0, 0)
    full = lambda a: pl.BlockSpec(a.shape, const)
    return pl.pallas_call(
        _post_attn_body, grid=(bsz, seq // tm),
        in_specs=[pl.BlockSpec((1, tm, d), tok3),
                  pl.BlockSpec((1, tm, Q_W), tok3), pl.BlockSpec((1, tm, Q_W), tok3),
                  pl.BlockSpec((1, tm, d), tok3), pl.BlockSpec((1, tm, d), tok3),
                  full(g0), full(b0), full(wa), full(wb), full(wo), full(g1), full(b1),
                  full(wr), full(br)],
        out_specs=[pl.BlockSpec((1, tm * ROW_TILE, LANES), tok3), pl.BlockSpec((1, tm, LANES), tok3),
                   pl.BlockSpec((1, tm, LANES), tok3), pl.BlockSpec((1, tm, LANES), tok3),
                   pl.BlockSpec((1, LANES), const)],
        out_shape=[jax.ShapeDtypeStruct((bsz, seq * ROW_TILE, LANES), F32),
                   jax.ShapeDtypeStruct((bsz, seq, LANES), jnp.int32),
                   jax.ShapeDtypeStruct((bsz, seq, LANES), F32),
                   jax.ShapeDtypeStruct((bsz, seq, LANES), jnp.int32),
                   jax.ShapeDtypeStruct((1, LANES), F32)],
        scratch_shapes=[pltpu.VMEM((1, LANES), F32)],
        name="post_attn",
        compiler_params=pltpu.CompilerParams(
            dimension_semantics=("arbitrary", "arbitrary"), vmem_limit_bytes=48 * 1024 * 1024),
    )(x, oa, ob, sga, sgb, g0, b0, wa, wb, wo, g1, b1, wr, br)


def _dispatch_body(dest_ref, pend_ref, padded_ref, nu_ref, h1t_ref, xs_hbm, zbuf, sem, zsem):
    tm = TM_DISPATCH
    zrows = BM_EXPERT * ROW_TILE
    n_blocks = xs_hbm.shape[0] // zrows

    @pl.when(pl.program_id(0) == 0)
    def _():
        zbuf[...] = jnp.zeros_like(zbuf)
        zero_wait = pltpu.make_async_copy(zbuf, xs_hbm.at[pl.ds(0, zrows)], zsem).wait
        for e in range(N_EXPERTS):
            @pl.when(padded_ref[e] > 0)
            def _():
                start = pl.multiple_of((pend_ref[e] - BM_EXPERT) * ROW_TILE, ROW_TILE)
                pltpu.make_async_copy(zbuf, xs_hbm.at[pl.ds(start, zrows)], zsem).start()
        for b in range(n_blocks - N_EXPERTS, n_blocks):
            @pl.when(b >= nu_ref[0])
            def _():
                pltpu.make_async_copy(zbuf, xs_hbm.at[pl.ds(b * zrows, zrows)], zsem).start()
        for e in range(N_EXPERTS):
            pl.when(padded_ref[e] > 0)(zero_wait)
        for b in range(n_blocks - N_EXPERTS, n_blocks):
            pl.when(b >= nu_ref[0])(zero_wait)

    def issue(t, carry):
        for kx in range(TOP_K):
            d = dest_ref[TOP_K * t + kx]
            pltpu.make_async_copy(_tile(h1t_ref, t), _tile(xs_hbm, d), sem).start(priority=kx % 2)
        return carry

    lax.fori_loop(0, tm, issue, 0)
    for _ in range(TOP_K):
        pltpu.make_async_copy(h1t_ref, xs_hbm.at[pl.ds(0, tm * ROW_TILE)], sem).wait()


def _dispatch(dest_flat, pad_end, padded, n_used, h1t, n_rows):
    n_tok = h1t.shape[0] // ROW_TILE
    tm = TM_DISPATCH
    return pl.pallas_call(
        _dispatch_body, grid=(n_tok // tm,),
        in_specs=[pl.BlockSpec((TOP_K * tm,), lambda i: (i,), memory_space=pltpu.SMEM),
                  pl.BlockSpec(memory_space=pltpu.SMEM), pl.BlockSpec(memory_space=pltpu.SMEM),
                  pl.BlockSpec(memory_space=pltpu.SMEM),
                  pl.BlockSpec((tm * ROW_TILE, LANES), lambda i: (i, 0))],
        out_specs=pl.BlockSpec(memory_space=pl.ANY),
        out_shape=jax.ShapeDtypeStruct((n_rows * ROW_TILE, LANES), F32),
        scratch_shapes=[pltpu.VMEM((BM_EXPERT * ROW_TILE, LANES), F32),
                        pltpu.SemaphoreType.DMA(()), pltpu.SemaphoreType.DMA(())],
        name="dispatch",
        compiler_params=pltpu.CompilerParams(dimension_semantics=("arbitrary",)),
    )(dest_flat, pad_end, padded, n_used, h1t)


def _experts_body(be_ref, nu_ref, xs_ref, wg_ref, bg_ref, wu_ref, bu_ref, wd_ref, bd_ref, ys_ref,
                  wg_s, wu_s, wd_s):
    i = pl.program_id(0)
    bm = BM_EXPERT
    used = i < nu_ref[0]
    prev = be_ref[jnp.maximum(i - 1, 0)]
    fresh = (i == 0) | (be_ref[i] != prev)

    @pl.when(used & fresh)
    def _():
        wg_s[...] = wg_ref[0].astype(BF16)
        wu_s[...] = wu_ref[0].astype(BF16)
        wd_s[...] = wd_ref[0].astype(BF16)

    @pl.when(used)
    def _():
        xb = _load_row_tiles(xs_ref, bm).astype(BF16)
        g = _dot(xb, wg_s[...]) + bg_ref[0]
        u = _dot(xb, wu_s[...]) + bu_ref[0]
        g = jnp.minimum(g, SWIGLU_LIMIT)
        u = jnp.clip(u, -SWIGLU_LIMIT, SWIGLU_LIMIT)
        act = g * jax.nn.sigmoid(SWIGLU_ALPHA * g) * (u + 1.0)
        _store_row_tiles(ys_ref, _dot(act.astype(BF16), wd_s[...]) + bd_ref[0], bm)

    @pl.when(jnp.logical_not(used))
    def _():
        ys_ref[...] = jnp.zeros_like(ys_ref)


def _experts(block_e, n_used, xs, wg, bg, wu, bu, wd, bd):
    n_rows = xs.shape[0] // ROW_TILE
    bm = BM_EXPERT
    n_e, d, d_ff = wg.shape
    row = lambda i, be, nu: (jnp.minimum(i, nu[0] - 1), 0)
    exp3 = lambda i, be, nu: (be[jnp.minimum(i, nu[0] - 1)], 0, 0)
    grid_spec = pltpu.PrefetchScalarGridSpec(
        num_scalar_prefetch=2, grid=(n_rows // bm,),
        in_specs=[pl.BlockSpec((bm * ROW_TILE, LANES), row),
                  pl.BlockSpec((1, d, d_ff), exp3), pl.BlockSpec((1, 1, d_ff), exp3),
                  pl.BlockSpec((1, d, d_ff), exp3), pl.BlockSpec((1, 1, d_ff), exp3),
                  pl.BlockSpec((1, d_ff, d), exp3), pl.BlockSpec((1, 1, d), exp3)],
        out_specs=pl.BlockSpec((bm * ROW_TILE, LANES), lambda i, be, nu: (i, 0)),
        scratch_shapes=[pltpu.VMEM((d, d_ff), BF16), pltpu.VMEM((d, d_ff), BF16),
                        pltpu.VMEM((d_ff, d), BF16)])
    return pl.pallas_call(
        _experts_body, grid_spec=grid_spec,
        out_shape=jax.ShapeDtypeStruct((n_rows * ROW_TILE, LANES), F32), name="experts",
        compiler_params=pltpu.CompilerParams(
            dimension_semantics=("arbitrary",), vmem_limit_bytes=56 * 1024 * 1024),
    )(block_e, n_used, xs, wg, bg.reshape(n_e, 1, d_ff), wu, bu.reshape(n_e, 1, d_ff),
      wd, bd.reshape(n_e, 1, d))


def _combine_body(dest_ref, dest_next_ref, h1t_ref, tw_ref, g2_ref, b2_ref, ys_hbm, out_ref, buf, sems):
    tm = TM_COMBINE
    i = pl.program_id(0)
    slot = i % 2

    def gather(d_ref, s):
        def issue(t, carry):
            for kx in range(TOP_K):
                d = d_ref[TOP_K * t + kx]
                pltpu.make_async_copy(_tile(ys_hbm, d), _tile(buf.at[s, kx], t), sems.at[s]).start(
                    priority=kx % 2)
            return carry
        lax.fori_loop(0, tm, issue, 0)

    @pl.when(i == 0)
    def _():
        gather(dest_ref, 0)

    @pl.when(i + 1 < pl.num_programs(0))
    def _():
        gather(dest_next_ref, 1 - slot)

    for kx in range(TOP_K):
        pltpu.make_async_copy(ys_hbm.at[pl.ds(0, tm * ROW_TILE)], buf.at[slot, kx], sems.at[slot]).wait()
    tw = tw_ref[...]
    ffn = tw[:, 0:1] * _load_row_tiles(buf.at[slot, 0], tm)
    for kx in range(1, TOP_K):
        ffn = ffn + tw[:, kx:kx + 1] * _load_row_tiles(buf.at[slot, kx], tm)
    h1 = _load_row_tiles(h1t_ref, tm)
    out_ref[...] = _ln(DN_ALPHA * h1 + ffn, g2_ref[...], b2_ref[...])


def _combine(dest_flat, h1t, tw, g2, b2, ys):
    n_tok = h1t.shape[0] // ROW_TILE
    d = ROW_TILE * LANES
    tm = TM_COMBINE
    n_steps = n_tok // tm
    const = lambda i: (0, 0)
    return pl.pallas_call(
        _combine_body, grid=(n_steps,),
        in_specs=[pl.BlockSpec((TOP_K * tm,), lambda i: (i,), memory_space=pltpu.SMEM),
                  pl.BlockSpec((TOP_K * tm,), lambda i: (jnp.minimum(i + 1, n_steps - 1),),
                               memory_space=pltpu.SMEM),
                  pl.BlockSpec((tm * ROW_TILE, LANES), lambda i: (i, 0)),
                  pl.BlockSpec((tm, LANES), lambda i: (i, 0)),
                  pl.BlockSpec((1, d), const), pl.BlockSpec((1, d), const),
                  pl.BlockSpec(memory_space=pl.ANY)],
        out_specs=pl.BlockSpec((tm, d), lambda i: (i, 0)),
        out_shape=jax.ShapeDtypeStruct((n_tok, d), F32),
        scratch_shapes=[pltpu.VMEM((2, TOP_K, tm * ROW_TILE, LANES), F32), pltpu.SemaphoreType.DMA((2,))],
        name="combine",
        compiler_params=pltpu.CompilerParams(
            dimension_semantics=("arbitrary",), vmem_limit_bytes=40 * 1024 * 1024),
    )(dest_flat, dest_flat, h1t, tw, g2, b2, ys)


def _rope_tables(seq, gain):
    t = np.arange(seq)
    row = (t // GRID_W).astype(np.float32)
    col = (t % GRID_W).astype(np.float32)
    half = HEAD_DIM // 2
    quarter = half // 2
    inv = (ROPE_THETA ** (-np.arange(quarter, dtype=np.float32) * np.float32(2.0 / half))).astype(np.float32)
    ang_r = row[:, None] * inv[None, :]
    ang_c = col[:, None] * inv[None, :]
    zeros = np.zeros_like(ang_r)
    cos = np.concatenate([np.cos(ang_r), np.cos(ang_r), np.cos(ang_c), np.cos(ang_c)], -1)
    s_lo = np.concatenate([-np.sin(ang_r), zeros, -np.sin(ang_c), zeros], -1)
    s_hi = np.concatenate([zeros, np.sin(ang_r), zeros, np.sin(ang_c)], -1)
    g = gain.astype(F32)
    return (jnp.asarray(cos, F32) * g[None, :], jnp.asarray(s_lo, F32) * jnp.roll(g, -quarter)[None, :],
            jnp.asarray(s_hi, F32) * jnp.roll(g, quarter)[None, :])


def _routing(top_i, rank, counts, bm):
    n_tok = top_i.shape[0]
    padded = (counts + bm - 1) // bm * bm
    pad_end = jnp.cumsum(padded)
    pad_start = pad_end - padded
    dest = (jnp.take(pad_start, top_i) + rank).astype(jnp.int32)
    n_rows = n_tok * TOP_K + N_EXPERTS * bm
    n_blocks = n_rows // bm
    block_start = jnp.arange(n_blocks, dtype=jnp.int32) * bm
    block_e = jnp.minimum(jnp.sum((pad_end[None, :] <= block_start[:, None]).astype(jnp.int32), axis=1),
                          N_EXPERTS - 1).astype(jnp.int32)
    n_used = (pad_end[-1] // bm).astype(jnp.int32).reshape(1)
    return dest.reshape(-1), block_e, n_used, n_rows, pad_end.astype(jnp.int32), padded.astype(jnp.int32)


def kernel(x, ln0_g, ln0_b, w_in, a_sink, b_q_norm, b_k_norm, w_branch_a, w_branch_b, w_out,
           ln1_g, ln1_b, w_router, b_router, w_gate, b_gate, w_up, b_up, w_down, b_down,
           ln2_g, ln2_b):
    bsz, seq, d = x.shape
    assert w_in.shape[0] == DEPTH == 1
    assert seq % TM_PROJ == 0 and seq % TQ_GRID == 0 and seq == (seq // GRID_W) * GRID_W
    n_tok = bsz * seq
    row2 = lambda a: a.reshape(1, -1)

    w = w_in[0]
    o_ka, o_va, o_qb, o_kb, o_vb, o_g = Q_W, Q_W + KV_W, Q_W + 2 * KV_W, 2 * Q_W + 2 * KV_W, \
        2 * Q_W + 3 * KV_W, 2 * Q_W + 4 * KV_W
    w_perm = jnp.concatenate([
        w[:, :Q_W][:, _PAIRED], w[:, o_ka:o_va],
        w[:, o_qb:o_kb][:, _PAIRED], w[:, o_kb:o_vb],
        w[:, o_va:o_qb], w[:, o_vb:o_g], w[:, o_g:]], axis=1).astype(BF16)
    head_id = np.arange(Q_W) // HEAD_DIM
    bd = jnp.asarray(head_id[:, None] == head_id[None, :], BF16)
    cq, s1q, s2q = (jnp.tile(t, (1, N_HEADS)) for t in _rope_tables(seq, b_q_norm[0]))
    ck, s1k, s2k = (jnp.tile(t, (1, N_KV)) for t in _rope_tables(seq, b_k_norm[0]))

    qa, ka, vat, qb, kb, vbt, sga, sgb = _in_proj(
        x, row2(ln0_g), row2(ln0_b), w_perm, bd, (cq, s1q, s2q, ck, s1k, s2k))
    oa = _win_attn(a_sink[0].astype(F32), qa, ka, vat)
    ob = _grid_attn(qb, kb, vbt)

    wr = jnp.zeros((d, LANES), BF16).at[:, :N_EXPERTS].set(w_router[0].astype(BF16))
    br = jnp.zeros((1, LANES), F32).at[0, :N_EXPERTS].set(b_router[0])
    h1t, ti, tw, rk, cnt = _post_attn(
        x, oa, ob, sga, sgb, row2(ln0_g), row2(ln0_b),
        w_branch_a[0][_PAIRED].astype(BF16), w_branch_b[0][_PAIRED].astype(BF16),
        w_out[0].astype(BF16), row2(ln1_g[0]), row2(ln1_b[0]), wr, br)
    h1t = h1t.reshape(n_tok * ROW_TILE, LANES)
    top_i = ti.reshape(n_tok, LANES)[:, :TOP_K]
    tw = tw.reshape(n_tok, LANES)

    rank = rk.reshape(n_tok, LANES)[:, :TOP_K]
    counts = cnt[0, :N_EXPERTS].astype(jnp.int32)
    dest, block_e, n_used, n_rows, pad_end, padded = _routing(top_i, rank, counts, BM_EXPERT)
    xs = _dispatch(dest, pad_end, padded, n_used, h1t, n_rows)
    ys = _experts(block_e, n_used, xs, w_gate[0], b_gate[0], w_up[0], b_up[0], w_down[0], b_down[0])
    out = _combine(dest, h1t, tw, row2(ln2_g[0]), row2(ln2_b[0]), ys)
    return out.reshape(bsz, seq, d)
```

```python
import functools

import jax
import jax.numpy as jnp
import numpy as np
from jax import lax
from jax.experimental import pallas as pl
from jax.experimental.pallas import tpu as pltpu

HEAD_DIM = 64
N_HEADS = 8
N_KV = 2
WINDOW = 128
BLOCK = 128
GRID_W = 64
ROPE_THETA = 10000.0
N_EXPERTS = 32
TOP_K = 4
SWIGLU_LIMIT = 7.0
SWIGLU_ALPHA = 1.702
LN_EPS = 1e-5
RMS_EPS = 1e-6
NEG_INF = -1e30
DEPTH = 1
DN_ALPHA = (2.0 * DEPTH) ** 0.25
ALIBI_SLOPES = tuple(2.0 ** (-8.0 * (h + 1) / N_HEADS) for h in range(N_HEADS))
QK_SCALE = HEAD_DIM ** -0.5
LOG2_E = 1.4426950408889634

LANES = 128
ROW_TILE = 8
Q_W = N_HEADS * HEAD_DIM
KV_W = N_KV * HEAD_DIM

TM_PROJ = 512
TQ_GRID = 256
KEY_CHUNK = 256
BM_EXPERT = 512
TM_DISPATCH = 512
TM_COMBINE = 256
COMBINE_ROW_CHUNK = 32

F32 = jnp.float32
BF16 = jnp.bfloat16

_PAIRED = np.array([(j if c == 0 else 4 + j) * HEAD_DIM + d
                    for j in range(4) for c in range(2) for d in range(HEAD_DIM)], np.int32)


def _ln(x, g, b):
    mu = jnp.mean(x, -1, keepdims=True)
    xc = x - mu
    var = jnp.mean(xc * xc, -1, keepdims=True)
    return xc * lax.rsqrt(var + LN_EPS) * g + b


def _dot(a, b):
    return jnp.dot(a, b, preferred_element_type=F32)


def _dot_nt(a, b):
    return lax.dot_general(a, b, (((1,), (1,)), ((), ())), preferred_element_type=F32)


def _load_row_tiles(ref, rows):
    return jnp.concatenate([ref[pl.ds(c, rows, stride=ROW_TILE), :] for c in range(ROW_TILE)], axis=1)


def _store_row_tiles(ref, val, rows):
    for c in range(ROW_TILE):
        ref[pl.ds(c, rows, stride=ROW_TILE), :] = val[:, c * LANES:(c + 1) * LANES]


def _tile(ref, row):
    start = row * ROW_TILE
    if not isinstance(row, int):
        start = pl.multiple_of(start, ROW_TILE)
    return ref.at[pl.ds(start, ROW_TILE)]


def _in_proj_body(x_ref, g0_ref, b0_ref, w_ref, bd_ref, cq_ref, s1q_ref, s2q_ref,
                  ck_ref, s1k_ref, s2k_ref,
                  qa_ref, ka_ref, vat_ref, qb_ref, kb_ref, vbt_ref, sga_ref, sgb_ref):
    h = _ln(x_ref[0], g0_ref[...], b0_ref[...])
    hb = h.astype(BF16)

    def proj(lo, hi):
        return _dot(hb, w_ref[:, lo:hi])

    def norm_rope(t, width, c_ref, s1_ref, s2_ref):
        ss = _dot((t * t).astype(BF16), bd_ref[:width, :width])
        r = lax.rsqrt(ss * (1.0 / HEAD_DIM) + RMS_EPS)
        rot = (t * c_ref[...] + pltpu.roll(t, width - 16, 1) * s1_ref[...]
               + pltpu.roll(t, 16, 1) * s2_ref[...])
        return rot * r

    o = 0
    qa_ref[0] = (proj(o, o + Q_W) * QK_SCALE).astype(BF16)
    o += Q_W
    ka_ref[0] = proj(o, o + KV_W).astype(BF16)
    o += KV_W
    qb = norm_rope(proj(o, o + Q_W), Q_W, cq_ref, s1q_ref, s2q_ref)
    qb_ref[0] = (qb * (QK_SCALE * LOG2_E)).astype(BF16)
    o += Q_W
    kb_ref[0] = norm_rope(proj(o, o + KV_W), KV_W, ck_ref, s1k_ref, s2k_ref).astype(BF16)
    o += KV_W
    vat_ref[0] = proj(o, o + KV_W).T.astype(BF16)
    o += KV_W
    vbt_ref[0] = proj(o, o + KV_W).T.astype(BF16)
    o += KV_W
    d = sga_ref.shape[-1]
    sga_ref[0] = jax.nn.sigmoid(proj(o, o + d)).astype(BF16)
    o += d
    sgb_ref[0] = jax.nn.sigmoid(proj(o, o + d)).astype(BF16)


def _in_proj(x, g0, b0, w, bd, tabs):
    bsz, seq, d = x.shape
    tm = TM_PROJ
    cq, s1q, s2q, ck, s1k, s2k = tabs
    n_in = w.shape[1]
    const = lambda i, j: (0, 0)
    tok3 = lambda i, j: (j, i, 0)
    tab = lambda i, j: (i, 0)
    in_specs = [
        pl.BlockSpec((1, tm, d), tok3),
        pl.BlockSpec((1, d), const), pl.BlockSpec((1, d), const),
        pl.BlockSpec((d, n_in), const),
        pl.BlockSpec((Q_W, Q_W), const),
        pl.BlockSpec((tm, Q_W), tab), pl.BlockSpec((tm, Q_W), tab), pl.BlockSpec((tm, Q_W), tab),
        pl.BlockSpec((tm, KV_W), tab), pl.BlockSpec((tm, KV_W), tab), pl.BlockSpec((tm, KV_W), tab),
    ]
    tr3 = lambda i, j: (j, 0, i)
    out_specs = [
        pl.BlockSpec((1, tm, Q_W), tok3), pl.BlockSpec((1, tm, KV_W), tok3),
        pl.BlockSpec((1, KV_W, tm), tr3),
        pl.BlockSpec((1, tm, Q_W), tok3), pl.BlockSpec((1, tm, KV_W), tok3),
        pl.BlockSpec((1, KV_W, tm), tr3),
        pl.BlockSpec((1, tm, d), tok3), pl.BlockSpec((1, tm, d), tok3),
    ]
    out_shape = [
        jax.ShapeDtypeStruct((bsz, seq, Q_W), BF16), jax.ShapeDtypeStruct((bsz, seq, KV_W), BF16),
        jax.ShapeDtypeStruct((bsz, KV_W, seq), BF16),
        jax.ShapeDtypeStruct((bsz, seq, Q_W), BF16), jax.ShapeDtypeStruct((bsz, seq, KV_W), BF16),
        jax.ShapeDtypeStruct((bsz, KV_W, seq), BF16),
        jax.ShapeDtypeStruct((bsz, seq, d), BF16), jax.ShapeDtypeStruct((bsz, seq, d), BF16),
    ]
    return pl.pallas_call(
        _in_proj_body, grid=(seq // tm, bsz), in_specs=in_specs, out_specs=out_specs,
        out_shape=out_shape, name="in_proj",
        compiler_params=pltpu.CompilerParams(
            dimension_semantics=("arbitrary", "arbitrary"), vmem_limit_bytes=48 * 1024 * 1024),
    )(x, g0, b0, w, bd, cq, s1q, s2q, ck, s1k, s2k)


def _half_mask(rows, c):
    lane = lax.broadcasted_iota(jnp.int32, (rows, LANES), 1)
    return (lane >= HEAD_DIM) if c == 1 else (lane < HEAD_DIM)


def _win_attn_body(sink_ref, q_ref, k0_ref, k1_ref, k2_ref, v0_ref, v1_ref, v2_ref, o_ref, *, seq):
    n = pl.program_id(1)
    k = jnp.concatenate([k0_ref[0], k1_ref[0], k2_ref[0]], axis=0)
    vt = jnp.concatenate([v0_ref[0], v1_ref[0], v2_ref[0]], axis=1)
    kk = lax.broadcasted_iota(jnp.int32, (3 * BLOCK, BLOCK), 0)
    qq = lax.broadcasted_iota(jnp.int32, (3 * BLOCK, BLOCK), 1)
    dist_i = jnp.abs(kk - BLOCK - qq)
    k_pos = n * BLOCK - BLOCK + kk
    valid = (dist_i <= WINDOW) & (k_pos >= 0) & (k_pos < seq)
    dist = dist_i.astype(F32)
    slabs = [q_ref[0, :, j * LANES:(j + 1) * LANES] for j in range(4)]
    qm = jnp.concatenate(
        [jnp.where(_half_mask(BLOCK, c), s, jnp.zeros_like(s)) for c in range(2) for s in slabs], axis=0)
    st_all = _dot_nt(k, qm)
    ots = []
    for c in range(2):
        ps, dens = [], []
        for j in range(4):
            head = j + 4 * c
            col = (4 * c + j) * BLOCK
            st = st_all[:, col:col + BLOCK] + jnp.where(valid, -ALIBI_SLOPES[head] * dist, NEG_INF)
            sk = sink_ref[head]
            m = jnp.maximum(jnp.max(st, axis=0, keepdims=True), sk)
            p = jnp.exp(st - m)
            dens.append(jnp.sum(p, axis=0, keepdims=True) + jnp.exp(sk - m))
            ps.append(p.astype(BF16))
        ot = _dot(vt[c * HEAD_DIM:(c + 1) * HEAD_DIM, :], jnp.concatenate(ps, axis=1))
        ots.append(ot / jnp.concatenate(dens, axis=1))
    for j in range(4):
        pair = jnp.concatenate([ot[:, j * BLOCK:(j + 1) * BLOCK] for ot in ots], axis=0)
        o_ref[0, :, j * LANES:(j + 1) * LANES] = pair.T.astype(BF16)


def _win_attn(sink, qa, ka, vat):
    bsz, seq, _ = qa.shape
    nb = seq // BLOCK
    qmap = lambda b, n: (b, n, 0)
    kspec = lambda f: pl.BlockSpec((1, BLOCK, KV_W), lambda b, n: (b, f(n), 0))
    vspec = lambda f: pl.BlockSpec((1, KV_W, BLOCK), lambda b, n: (b, 0, f(n)))
    prev = lambda n: jnp.maximum(n - 1, 0)
    cur = lambda n: n
    nxt = lambda n: jnp.minimum(n + 1, nb - 1)
    return pl.pallas_call(
        functools.partial(_win_attn_body, seq=seq), grid=(bsz, nb),
        in_specs=[pl.BlockSpec(memory_space=pltpu.SMEM),
                  pl.BlockSpec((1, BLOCK, Q_W), qmap),
                  kspec(prev), kspec(cur), kspec(nxt), vspec(prev), vspec(cur), vspec(nxt)],
        out_specs=pl.BlockSpec((1, BLOCK, Q_W), qmap),
        out_shape=jax.ShapeDtypeStruct((bsz, seq, Q_W), BF16), name="win_attn",
        compiler_params=pltpu.CompilerParams(dimension_semantics=("arbitrary", "arbitrary")),
    )(sink, qa, ka, ka, ka, vat, vat, vat)


def _grid_attn_body(q_ref, k_ref, vt_ref, o_ref, s0_ref, s1_ref, p0_ref, p1_ref):
    tq = q_ref.shape[1]
    seq = k_ref.shape[1]
    kc = KEY_CHUNK
    n_chunks = seq // kc
    s_bufs, p_bufs = (s0_ref, s1_ref), (p0_ref, p1_ref)
    heads = [(j, c) for j in range(4) for c in range(2)]
    ones = jnp.ones((2 * ROW_TILE, seq), BF16)

    def masked_q(h):
        j, c = heads[h]
        slab = q_ref[0, :, j * LANES:(j + 1) * LANES]
        return jnp.where(_half_mask(tq, c), slab, jnp.zeros_like(slab))

    def score_chunk(h, qm, kb, m8):
        sc = _dot_nt(k_ref[0, kb * kc:(kb + 1) * kc, :], qm)
        s_bufs[h % 2][kb * kc:(kb + 1) * kc, :] = sc
        cm = jnp.max(sc.reshape(kc // ROW_TILE, ROW_TILE, tq), axis=0)
        return cm if m8 is None else jnp.maximum(m8, cm)

    def prob_chunk(h, kb, m):
        x = s_bufs[h % 2][kb * kc:(kb + 1) * kc, :] - m
        p_bufs[h % 2][kb * kc:(kb + 1) * kc, :] = jnp.exp2(x.astype(BF16))

    qm = masked_q(0)
    m8 = None
    for kb in range(n_chunks):
        m8 = score_chunk(0, qm, kb, m8)
    outs = []
    for h in range(len(heads)):
        m = jnp.max(m8, axis=0, keepdims=True)
        nxt = h + 1 < len(heads)
        if nxt:
            qm = masked_q(h + 1)
            m8 = None
        for kb in range(n_chunks):
            if nxt:
                m8 = score_chunk(h + 1, qm, kb, m8)
            prob_chunk(h, kb, m)
        j, c = heads[h]
        va = jnp.concatenate([vt_ref[0, c * HEAD_DIM:(c + 1) * HEAD_DIM, :], ones], axis=0)
        ot = _dot(va, p_bufs[h % 2][...])
        outs.append(ot[:HEAD_DIM] / ot[HEAD_DIM:HEAD_DIM + 1])
        if c == 1:
            o_ref[0, :, j * LANES:(j + 1) * LANES] = jnp.concatenate(outs, axis=0).T.astype(BF16)
            outs = []


def _grid_attn(qb, kb, vbt):
    bsz, seq, _ = qb.shape
    tq = TQ_GRID
    return pl.pallas_call(
        _grid_attn_body, grid=(bsz, seq // tq),
        in_specs=[pl.BlockSpec((1, tq, Q_W), lambda b, n: (b, n, 0)),
                  pl.BlockSpec((1, seq, KV_W), lambda b, n: (b, 0, 0)),
                  pl.BlockSpec((1, KV_W, seq), lambda b, n: (b, 0, 0))],
        out_specs=pl.BlockSpec((1, tq, Q_W), lambda b, n: (b, n, 0)),
        scratch_shapes=[pltpu.VMEM((seq, tq), F32), pltpu.VMEM((seq, tq), F32),
                        pltpu.VMEM((seq, tq), BF16), pltpu.VMEM((seq, tq), BF16)],
        out_shape=jax.ShapeDtypeStruct((bsz, seq, Q_W), BF16), name="grid_attn",
        compiler_params=pltpu.CompilerParams(
            dimension_semantics=("arbitrary", "arbitrary"), vmem_limit_bytes=40 * 1024 * 1024),
    )(qb, kb, vbt)


def _post_attn_body(x_ref, oa_ref, ob_ref, sga_ref, sgb_ref, g0_ref, b0_ref, wa_ref, wb_ref, wo_ref,
                    g1_ref, b1_ref, wr_ref, br_ref, h1t_ref, ti_ref, tw_ref, rk_ref, cnt_out_ref, cnt_ref):
    h0 = _ln(x_ref[0], g0_ref[...], b0_ref[...])
    out_a = _dot(oa_ref[0], wa_ref[...])
    out_b = _dot(ob_ref[0], wb_ref[...])
    merged = sga_ref[0].astype(F32) * out_a + sgb_ref[0].astype(F32) * out_b
    mix = _dot(merged.astype(BF16), wo_ref[...])
    h1 = _ln(DN_ALPHA * h0 + mix, g1_ref[...], b1_ref[...])
    tm = h1.shape[0]
    _store_row_tiles(h1t_ref.at[0], h1, tm)

    logits = _dot(h1.astype(BF16), wr_ref[...]) + br_ref[...]
    lane = lax.broadcasted_iota(jnp.int32, (tm, LANES), 1)
    cur = jnp.where(lane < N_EXPERTS, logits, -jnp.inf)
    vals, idxs = [], []
    for _ in range(TOP_K):
        mv = jnp.max(cur, axis=-1, keepdims=True)
        ix = jnp.min(jnp.where(cur == mv, lane, LANES), axis=-1, keepdims=True)
        vals.append(mv)
        idxs.append(ix)
        cur = jnp.where(lane == ix, -jnp.inf, cur)
    es = [jnp.exp(v - vals[0]) for v in vals]
    tot = es[0] + es[1] + es[2] + es[3]

    @pl.when((pl.program_id(0) == 0) & (pl.program_id(1) == 0))
    def _():
        cnt_ref[...] = jnp.zeros_like(cnt_ref)

    sel = jnp.zeros((tm, LANES), F32)
    for kx in range(TOP_K):
        sel = sel + (lane == idxs[kx]).astype(F32)
    r_i = lax.broadcasted_iota(jnp.int32, (tm, tm), 0)
    c_i = lax.broadcasted_iota(jnp.int32, (tm, tm), 1)
    tri = (c_i < r_i).astype(BF16)
    rank = _dot(tri, sel.astype(BF16)) + cnt_ref[...]
    cnt_ref[...] = cnt_ref[...] + jnp.sum(sel, axis=0, keepdims=True)
    cnt_out_ref[...] = cnt_ref[...]

    ti = jnp.zeros((tm, LANES), jnp.int32)
    tw = jnp.zeros((tm, LANES), F32)
    rk = jnp.zeros((tm, LANES), jnp.int32)
    for kx in range(TOP_K):
        ti = jnp.where(lane == kx, idxs[kx], ti)
        tw = jnp.where(lane == kx, es[kx] / tot, tw)
        rk_k = jnp.sum(jnp.where(lane == idxs[kx], rank, 0.0), axis=-1, keepdims=True)
        rk = jnp.where(lane == kx, rk_k.astype(jnp.int32), rk)
    ti_ref[0] = ti
    tw_ref[0] = tw
    rk_ref[0] = rk


def _post_attn(x, oa, ob, sga, sgb, g0, b0, wa, wb, wo, g1, b1, wr, br):
    bsz, seq, d = x.shape
    tm = TM_PROJ
    tok3 = lambda b, i: (b, i, 0)
    const = lambda b, i: (0, 0)
    full = lambda a: pl.BlockSpec(a.shape, const)
    return pl.pallas_call(
        _post_attn_body, grid=(bsz, seq // tm),
        in_specs=[pl.BlockSpec((1, tm, d), tok3),
                  pl.BlockSpec((1, tm, Q_W), tok3), pl.BlockSpec((1, tm, Q_W), tok3),
                  pl.BlockSpec((1, tm, d), tok3), pl.BlockSpec((1, tm, d), tok3),
                  full(g0), full(b0), full(wa), full(wb), full(wo), full(g1), full(b1),
                  full(wr), full(br)],
        out_specs=[pl.BlockSpec((1, tm * ROW_TILE, LANES), tok3), pl.BlockSpec((1, tm, LANES), tok3),
                   pl.BlockSpec((1, tm, LANES), tok3), pl.BlockSpec((1, tm, LANES), tok3),
                   pl.BlockSpec((1, LANES), const)],
        out_shape=[jax.ShapeDtypeStruct((bsz, seq * ROW_TILE, LANES), F32),
                   jax.ShapeDtypeStruct((bsz, seq, LANES), jnp.int32),
                   jax.ShapeDtypeStruct((bsz, seq, LANES), F32),
                   jax.ShapeDtypeStruct((bsz, seq, LANES), jnp.int32),
                   jax.ShapeDtypeStruct((1, LANES), F32)],
        scratch_shapes=[pltpu.VMEM((1, LANES), F32)],
        name="post_attn",
        compiler_params=pltpu.CompilerParams(
            dimension_semantics=("arbitrary", "arbitrary"), vmem_limit_bytes=48 * 1024 * 1024),
    )(x, oa, ob, sga, sgb, g0, b0, wa, wb, wo, g1, b1, wr, br)


def _experts_body(be_ref, nu_ref, tok0_ref, tok_next_ref, dst_prev_ref,
                  wg_ref, bg_ref, wu_ref, bu_ref, wd_ref, bd_ref, h1t_hbm, ys_hbm,
                  xbuf, ybuf, wg_s, wu_s, wd_s, gsem, ssem):
    i = pl.program_id(0)
    bm = BM_EXPERT
    rows = bm * ROW_TILE
    drain = nu_ref[0]
    slot = i % 2
    dump = ys_hbm.shape[0] - 2 * rows

    def gather_wait(s):
        pltpu.make_async_copy(h1t_hbm.at[pl.ds(0, rows)], xbuf.at[s], gsem.at[s]).wait()

    def scatter_wait(s):
        pltpu.make_async_copy(ybuf.at[s], ys_hbm.at[pl.ds(0, rows)], ssem.at[s]).wait()

    @pl.when(i == 0)
    def _():
        ybuf[...] = jnp.zeros_like(ybuf)
        for s in range(2):
            pltpu.make_async_copy(ybuf.at[s], ys_hbm.at[pl.ds(dump + s * rows, rows)], ssem.at[s]).start()
        for s in range(2):
            scatter_wait(s)

        def first(r, carry):
            pltpu.make_async_copy(_tile(h1t_hbm, tok0_ref[r]), _tile(xbuf.at[0], r), gsem.at[0]).start()
            return carry
        lax.fori_loop(0, bm, first, 0)

    fresh = (i == 0) | (be_ref[i] != be_ref[jnp.maximum(i - 1, 0)])

    @pl.when((i <= drain) & fresh)
    def _():
        wg_s[...] = wg_ref[0].astype(BF16)
        wu_s[...] = wu_ref[0].astype(BF16)
        wd_s[...] = wd_ref[0].astype(BF16)

    @pl.when(i <= drain)
    def _():
        gather_wait(slot)

        @pl.when(i >= 1)
        def _():
            scatter_wait(slot)

        def issue(q, carry):
            for par in range(2):
                r = 2 * q + par
                pltpu.make_async_copy(_tile(h1t_hbm, tok_next_ref[r]), _tile(xbuf.at[1 - slot], r),
                                      gsem.at[1 - slot]).start(priority=par)
                pltpu.make_async_copy(_tile(ybuf.at[1 - slot], r), _tile(ys_hbm, dst_prev_ref[r]),
                                      ssem.at[1 - slot]).start(priority=par)
            return carry
        lax.fori_loop(0, bm // 2, issue, 0)

        xb = _load_row_tiles(xbuf.at[slot], bm).astype(BF16)
        g = _dot(xb, wg_s[...]) + bg_ref[0]
        u = _dot(xb, wu_s[...]) + bu_ref[0]
        g = jnp.minimum(g, SWIGLU_LIMIT)
        u = jnp.clip(u, -SWIGLU_LIMIT, SWIGLU_LIMIT)
        act = g * jax.nn.sigmoid(SWIGLU_ALPHA * g) * (u + 1.0)
        _store_row_tiles(ybuf.at[slot], _dot(act.astype(BF16), wd_s[...]) + bd_ref[0], bm)

        @pl.when(i == drain)
        def _():
            gather_wait(1 - slot)
            scatter_wait(1 - slot)


def _experts(block_e, n_used, row_tok, row_dst, h1t, wg, bg, wu, bu, wd, bd):
    bm = BM_EXPERT
    n_blocks = row_tok.shape[0] // bm
    n_tok = h1t.shape[0] // ROW_TILE
    n_e, d, d_ff = wg.shape
    exp3 = lambda i, be, nu: (be[jnp.minimum(i, nu[0])], 0, 0)
    smem_block = lambda f: pl.BlockSpec((bm,), lambda i, be, nu: (f(i),), memory_space=pltpu.SMEM)
    grid_spec = pltpu.PrefetchScalarGridSpec(
        num_scalar_prefetch=2, grid=(n_blocks + 1,),
        in_specs=[smem_block(lambda i: 0),
                  smem_block(lambda i: jnp.minimum(i + 1, n_blocks - 1)),
                  smem_block(lambda i: jnp.clip(i - 1, 0, n_blocks - 1)),
                  pl.BlockSpec((1, d, d_ff), exp3), pl.BlockSpec((1, 1, d_ff), exp3),
                  pl.BlockSpec((1, d, d_ff), exp3), pl.BlockSpec((1, 1, d_ff), exp3),
                  pl.BlockSpec((1, d_ff, d), exp3), pl.BlockSpec((1, 1, d), exp3),
                  pl.BlockSpec(memory_space=pl.ANY)],
        out_specs=pl.BlockSpec(memory_space=pl.ANY),
        scratch_shapes=[pltpu.VMEM((2, bm * ROW_TILE, LANES), F32), pltpu.VMEM((2, bm * ROW_TILE, LANES), F32),
                        pltpu.VMEM((d, d_ff), BF16), pltpu.VMEM((d, d_ff), BF16), pltpu.VMEM((d_ff, d), BF16),
                        pltpu.SemaphoreType.DMA((2,)), pltpu.SemaphoreType.DMA((2,))])
    return pl.pallas_call(
        _experts_body, grid_spec=grid_spec,
        out_shape=jax.ShapeDtypeStruct(((TOP_K * n_tok + 2 * bm) * ROW_TILE, LANES), F32), name="experts",
        compiler_params=pltpu.CompilerParams(
            dimension_semantics=("arbitrary",), vmem_limit_bytes=56 * 1024 * 1024),
    )(block_e, n_used, row_tok, row_tok, row_dst, wg, bg.reshape(n_e, 1, d_ff), wu, bu.reshape(n_e, 1, d_ff),
      wd, bd.reshape(n_e, 1, d), h1t)


def _combine_body(y0_ref, y1_ref, y2_ref, y3_ref, h1t_ref, tw_ref, g2_ref, b2_ref, out_ref):
    tm = TM_COMBINE
    tw = tw_ref[...]
    ffn = tw[:, 0:1] * _load_row_tiles(y0_ref, tm)
    for kx, y_ref in enumerate((y1_ref, y2_ref, y3_ref), start=1):
        ffn = ffn + tw[:, kx:kx + 1] * _load_row_tiles(y_ref, tm)
    h1 = _load_row_tiles(h1t_ref, tm)
    out_ref[...] = _ln(DN_ALPHA * h1 + ffn, g2_ref[...], b2_ref[...])


def _combine(h1t, tw, g2, b2, ys):
    n_tok = h1t.shape[0] // ROW_TILE
    d = ROW_TILE * LANES
    tm = TM_COMBINE
    n_steps = n_tok // tm
    const = lambda i: (0, 0)
    y_spec = lambda kx: pl.BlockSpec((tm * ROW_TILE, LANES), lambda i: (kx * n_steps + i, 0))
    return pl.pallas_call(
        _combine_body, grid=(n_steps,),
        in_specs=[y_spec(0), y_spec(1), y_spec(2), y_spec(3),
                  pl.BlockSpec((tm * ROW_TILE, LANES), lambda i: (i, 0)),
                  pl.BlockSpec((tm, LANES), lambda i: (i, 0)),
                  pl.BlockSpec((1, d), const), pl.BlockSpec((1, d), const)],
        out_specs=pl.BlockSpec((tm, d), lambda i: (i, 0)),
        out_shape=jax.ShapeDtypeStruct((n_tok, d), F32),
        name="combine",
        compiler_params=pltpu.CompilerParams(
            dimension_semantics=("arbitrary",), vmem_limit_bytes=40 * 1024 * 1024),
    )(ys, ys, ys, ys, h1t, tw, g2, b2)


def _rope_tables(seq, gain):
    t = np.arange(seq)
    row = (t // GRID_W).astype(np.float32)
    col = (t % GRID_W).astype(np.float32)
    half = HEAD_DIM // 2
    quarter = half // 2
    inv = (ROPE_THETA ** (-np.arange(quarter, dtype=np.float32) * np.float32(2.0 / half))).astype(np.float32)
    ang_r = row[:, None] * inv[None, :]
    ang_c = col[:, None] * inv[None, :]
    zeros = np.zeros_like(ang_r)
    cos = np.concatenate([np.cos(ang_r), np.cos(ang_r), np.cos(ang_c), np.cos(ang_c)], -1)
    s_lo = np.concatenate([-np.sin(ang_r), zeros, -np.sin(ang_c), zeros], -1)
    s_hi = np.concatenate([zeros, np.sin(ang_r), zeros, np.sin(ang_c)], -1)
    g = gain.astype(F32)
    return (jnp.asarray(cos, F32) * g[None, :], jnp.asarray(s_lo, F32) * jnp.roll(g, -quarter)[None, :],
            jnp.asarray(s_hi, F32) * jnp.roll(g, quarter)[None, :])


def _routing(top_i, rank, counts, bm):
    n_tok = top_i.shape[0]
    n_slots = n_tok * TOP_K
    padded = (counts + bm - 1) // bm * bm
    pad_end = jnp.cumsum(padded)
    n_rows = n_slots + N_EXPERTS * bm
    n_blocks = n_rows // bm
    span = 2 * max(n_tok, bm)
    e_ids = jnp.arange(N_EXPERTS, dtype=jnp.int32)
    j_ids = jnp.arange(bm, dtype=jnp.int32)
    key_real = (top_i * span + rank).reshape(-1)
    pad_live = j_ids[None, :] < (padded - counts)[:, None]
    key_pad = jnp.where(pad_live, e_ids[:, None] * span + span // 2 + j_ids[None, :], N_EXPERTS * span).reshape(-1)
    keys = jnp.concatenate([key_real, key_pad]).astype(jnp.int32)
    vals = jnp.concatenate([jnp.arange(n_slots, dtype=jnp.int32), jnp.full((N_EXPERTS * bm,), -1, jnp.int32)])
    pad_start = pad_end - padded
    dest = (jnp.take(pad_start, top_i) + rank).reshape(-1)
    row_slot = jnp.full((n_rows,), -1, jnp.int32).at[dest].set(
        jnp.arange(n_slots, dtype=jnp.int32), unique_indices=True, mode="promise_in_bounds")
    is_slot = row_slot >= 0
    row_tok = jnp.where(is_slot, row_slot // TOP_K, 0).astype(jnp.int32)
    row_id = jnp.arange(n_rows, dtype=jnp.int32)
    row_dst = jnp.where(is_slot, (row_slot % TOP_K) * n_tok + row_slot // TOP_K,
                        n_slots + row_id % (2 * bm)).astype(jnp.int32)
    block_start = jnp.arange(n_blocks + 1, dtype=jnp.int32) * bm
    block_e = jnp.minimum(jnp.sum((pad_end[None, :] <= block_start[:, None]).astype(jnp.int32), axis=1),
                          N_EXPERTS - 1).astype(jnp.int32)
    n_used = (pad_end[-1] // bm).astype(jnp.int32).reshape(1)
    return row_tok, row_dst, block_e, n_used


def kernel(x, ln0_g, ln0_b, w_in, a_sink, b_q_norm, b_k_norm, w_branch_a, w_branch_b, w_out,
           ln1_g, ln1_b, w_router, b_router, w_gate, b_gate, w_up, b_up, w_down, b_down,
           ln2_g, ln2_b):
    bsz, seq, d = x.shape
    assert w_in.shape[0] == DEPTH == 1
    assert seq % TM_PROJ == 0 and seq % TQ_GRID == 0 and seq == (seq // GRID_W) * GRID_W
    n_tok = bsz * seq
    row2 = lambda a: a.reshape(1, -1)

    w = w_in[0]
    o_ka, o_va, o_qb, o_kb, o_vb, o_g = Q_W, Q_W + KV_W, Q_W + 2 * KV_W, 2 * Q_W + 2 * KV_W, \
        2 * Q_W + 3 * KV_W, 2 * Q_W + 4 * KV_W
    w_perm = jnp.concatenate([
        w[:, :Q_W][:, _PAIRED], w[:, o_ka:o_va],
        w[:, o_qb:o_kb][:, _PAIRED], w[:, o_kb:o_vb],
        w[:, o_va:o_qb], w[:, o_vb:o_g], w[:, o_g:]], axis=1).astype(BF16)
    head_id = np.arange(Q_W) // HEAD_DIM
    bd = jnp.asarray(head_id[:, None] == head_id[None, :], BF16)
    cq, s1q, s2q = (jnp.tile(t, (1, N_HEADS)) for t in _rope_tables(seq, b_q_norm[0]))
    ck, s1k, s2k = (jnp.tile(t, (1, N_KV)) for t in _rope_tables(seq, b_k_norm[0]))

    qa, ka, vat, qb, kb, vbt, sga, sgb = _in_proj(
        x, row2(ln0_g), row2(ln0_b), w_perm, bd, (cq, s1q, s2q, ck, s1k, s2k))
    oa = _win_attn(a_sink[0].astype(F32), qa, ka, vat)
    ob = _grid_attn(qb, kb, vbt)

    wr = jnp.zeros((d, LANES), BF16).at[:, :N_EXPERTS].set(w_router[0].astype(BF16))
    br = jnp.zeros((1, LANES), F32).at[0, :N_EXPERTS].set(b_router[0])
    h1t, ti, tw, rk, cnt = _post_attn(
        x, oa, ob, sga, sgb, row2(ln0_g), row2(ln0_b),
        w_branch_a[0][_PAIRED].astype(BF16), w_branch_b[0][_PAIRED].astype(BF16),
        w_out[0].astype(BF16), row2(ln1_g[0]), row2(ln1_b[0]), wr, br)
    h1t = h1t.reshape(n_tok * ROW_TILE, LANES)
    top_i = ti.reshape(n_tok, LANES)[:, :TOP_K]
    tw = tw.reshape(n_tok, LANES)

    rank = rk.reshape(n_tok, LANES)[:, :TOP_K]
    counts = cnt[0, :N_EXPERTS].astype(jnp.int32)
    row_tok, row_dst, block_e, n_used = _routing(top_i, rank, counts, BM_EXPERT)
    ys = _experts(block_e, n_used, row_tok, row_dst, h1t,
                  w_gate[0], b_gate[0], w_up[0], b_up[0], w_down[0], b_down[0])
    out = _combine(h1t, tw, row2(ln2_g[0]), row2(ln2_b[0]), ys)
    return out.reshape(bsz, seq, d)
```

```python
import functools

import jax
import jax.numpy as jnp
import numpy as np
from jax import lax
from jax.experimental import pallas as pl
from jax.experimental.pallas import tpu as pltpu

HEAD_DIM = 64
N_HEADS = 8
N_KV = 2
WINDOW = 128
BLOCK = 128
GRID_W = 64
ROPE_THETA = 10000.0
N_EXPERTS = 32
TOP_K = 4
SWIGLU_LIMIT = 7.0
SWIGLU_ALPHA = 1.702
LN_EPS = 1e-5
RMS_EPS = 1e-6
NEG_INF = -1e30
DEPTH = 1
DN_ALPHA = (2.0 * DEPTH) ** 0.25
ALIBI_SLOPES = tuple(2.0 ** (-8.0 * (h + 1) / N_HEADS) for h in range(N_HEADS))
QK_SCALE = HEAD_DIM ** -0.5
LOG2_E = 1.4426950408889634

LANES = 128
ROW_TILE = 8
Q_W = N_HEADS * HEAD_DIM
KV_W = N_KV * HEAD_DIM

TM_PROJ = 512
TQ_GRID = 256
KEY_CHUNK = 256
BM_EXPERT = 512
TM_DISPATCH = 512
TM_COMBINE = 256
COMBINE_ROW_CHUNK = 32

F32 = jnp.float32
BF16 = jnp.bfloat16

_PAIRED = np.array([(j if c == 0 else 4 + j) * HEAD_DIM + d
                    for j in range(4) for c in range(2) for d in range(HEAD_DIM)], np.int32)


def _ln(x, g, b):
    mu = jnp.mean(x, -1, keepdims=True)
    xc = x - mu
    var = jnp.mean(xc * xc, -1, keepdims=True)
    return xc * lax.rsqrt(var + LN_EPS) * g + b


def _dot(a, b):
    return jnp.dot(a, b, preferred_element_type=F32)


def _dot_nt(a, b):
    return lax.dot_general(a, b, (((1,), (1,)), ((), ())), preferred_element_type=F32)


def _load_row_tiles(ref, rows, first=0):
    return jnp.concatenate(
        [ref[pl.ds(first * ROW_TILE + c, rows, stride=ROW_TILE), :] for c in range(ROW_TILE)], axis=1)


def _store_row_tiles(ref, val, rows):
    for c in range(ROW_TILE):
        ref[pl.ds(c, rows, stride=ROW_TILE), :] = val[:, c * LANES:(c + 1) * LANES]


def _tile(ref, row):
    start = row * ROW_TILE
    if not isinstance(row, int):
        start = pl.multiple_of(start, ROW_TILE)
    return ref.at[pl.ds(start, ROW_TILE)]


def _in_proj_body(x_ref, g0_ref, b0_ref, w_ref, bd_ref, cq_ref, s1q_ref, s2q_ref,
                  ck_ref, s1k_ref, s2k_ref,
                  qa_ref, ka_ref, vat_ref, qb_ref, kb_ref, vbt_ref, sga_ref, sgb_ref):
    h = _ln(x_ref[0], g0_ref[...], b0_ref[...])
    hb = h.astype(BF16)

    def proj(lo, hi):
        return _dot(hb, w_ref[:, lo:hi])

    def norm_rope(t, width, c_ref, s1_ref, s2_ref):
        ss = _dot((t * t).astype(BF16), bd_ref[:width, :width])
        r = lax.rsqrt(ss * (1.0 / HEAD_DIM) + RMS_EPS)
        rot = (t * c_ref[...] + pltpu.roll(t, width - 16, 1) * s1_ref[...]
               + pltpu.roll(t, 16, 1) * s2_ref[...])
        return rot * r

    o = 0
    qa_ref[0] = (proj(o, o + Q_W) * QK_SCALE).astype(BF16)
    o += Q_W
    ka_ref[0] = proj(o, o + KV_W).astype(BF16)
    o += KV_W
    qb = norm_rope(proj(o, o + Q_W), Q_W, cq_ref, s1q_ref, s2q_ref)
    qb_ref[0] = (qb * (QK_SCALE * LOG2_E)).astype(BF16)
    o += Q_W
    kb_ref[0] = norm_rope(proj(o, o + KV_W), KV_W, ck_ref, s1k_ref, s2k_ref).astype(BF16)
    o += KV_W
    vat_ref[0] = proj(o, o + KV_W).T.astype(BF16)
    o += KV_W
    vbt_ref[0] = proj(o, o + KV_W).T.astype(BF16)
    o += KV_W
    d = sga_ref.shape[-1]
    sga_ref[0] = jax.nn.sigmoid(proj(o, o + d)).astype(BF16)
    o += d
    sgb_ref[0] = jax.nn.sigmoid(proj(o, o + d)).astype(BF16)


def _in_proj(x, g0, b0, w, bd, tabs):
    bsz, seq, d = x.shape
    tm = TM_PROJ
    cq, s1q, s2q, ck, s1k, s2k = tabs
    n_in = w.shape[1]
    const = lambda i, j: (0, 0)
    tok3 = lambda i, j: (j, i, 0)
    tab = lambda i, j: (i, 0)
    in_specs = [
        pl.BlockSpec((1, tm, d), tok3),
        pl.BlockSpec((1, d), const), pl.BlockSpec((1, d), const),
        pl.BlockSpec((d, n_in), const),
        pl.BlockSpec((Q_W, Q_W), const),
        pl.BlockSpec((tm, Q_W), tab), pl.BlockSpec((tm, Q_W), tab), pl.BlockSpec((tm, Q_W), tab),
        pl.BlockSpec((tm, KV_W), tab), pl.BlockSpec((tm, KV_W), tab), pl.BlockSpec((tm, KV_W), tab),
    ]
    tr3 = lambda i, j: (j, 0, i)
    out_specs = [
        pl.BlockSpec((1, tm, Q_W), tok3), pl.BlockSpec((1, tm, KV_W), tok3),
        pl.BlockSpec((1, KV_W, tm), tr3),
        pl.BlockSpec((1, tm, Q_W), tok3), pl.BlockSpec((1, tm, KV_W), tok3),
        pl.BlockSpec((1, KV_W, tm), tr3),
        pl.BlockSpec((1, tm, d), tok3), pl.BlockSpec((1, tm, d), tok3),
    ]
    out_shape = [
        jax.ShapeDtypeStruct((bsz, seq, Q_W), BF16), jax.ShapeDtypeStruct((bsz, seq, KV_W), BF16),
        jax.ShapeDtypeStruct((bsz, KV_W, seq), BF16),
        jax.ShapeDtypeStruct((bsz, seq, Q_W), BF16), jax.ShapeDtypeStruct((bsz, seq, KV_W), BF16),
        jax.ShapeDtypeStruct((bsz, KV_W, seq), BF16),
        jax.ShapeDtypeStruct((bsz, seq, d), BF16), jax.ShapeDtypeStruct((bsz, seq, d), BF16),
    ]
    return pl.pallas_call(
        _in_proj_body, grid=(seq // tm, bsz), in_specs=in_specs, out_specs=out_specs,
        out_shape=out_shape, name="in_proj",
        compiler_params=pltpu.CompilerParams(
            dimension_semantics=("arbitrary", "arbitrary"), vmem_limit_bytes=48 * 1024 * 1024),
    )(x, g0, b0, w, bd, cq, s1q, s2q, ck, s1k, s2k)


def _half_mask(rows, c):
    lane = lax.broadcasted_iota(jnp.int32, (rows, LANES), 1)
    return (lane >= HEAD_DIM) if c == 1 else (lane < HEAD_DIM)


def _win_attn_body(sink_ref, q_ref, k0_ref, k1_ref, k2_ref, v0_ref, v1_ref, v2_ref, o_ref, *, seq):
    n = pl.program_id(1)
    k = jnp.concatenate([k0_ref[0], k1_ref[0], k2_ref[0]], axis=0)
    vt = jnp.concatenate([v0_ref[0], v1_ref[0], v2_ref[0]], axis=1)
    kk = lax.broadcasted_iota(jnp.int32, (3 * BLOCK, BLOCK), 0)
    qq = lax.broadcasted_iota(jnp.int32, (3 * BLOCK, BLOCK), 1)
    dist_i = jnp.abs(kk - BLOCK - qq)
    k_pos = n * BLOCK - BLOCK + kk
    valid = (dist_i <= WINDOW) & (k_pos >= 0) & (k_pos < seq)
    dist = dist_i.astype(F32)
    slabs = [q_ref[0, :, j * LANES:(j + 1) * LANES] for j in range(4)]
    qm = jnp.concatenate(
        [jnp.where(_half_mask(BLOCK, c), s, jnp.zeros_like(s)) for c in range(2) for s in slabs], axis=0)
    st_all = _dot_nt(k, qm)
    ots = []
    for c in range(2):
        ps, dens = [], []
        for j in range(4):
            head = j + 4 * c
            col = (4 * c + j) * BLOCK
            st = st_all[:, col:col + BLOCK] + jnp.where(valid, -ALIBI_SLOPES[head] * dist, NEG_INF)
            sk = sink_ref[head]
            m = jnp.maximum(jnp.max(st, axis=0, keepdims=True), sk)
            p = jnp.exp(st - m)
            dens.append(jnp.sum(p, axis=0, keepdims=True) + jnp.exp(sk - m))
            ps.append(p.astype(BF16))
        ot = _dot(vt[c * HEAD_DIM:(c + 1) * HEAD_DIM, :], jnp.concatenate(ps, axis=1))
        ots.append(ot / jnp.concatenate(dens, axis=1))
    for j in range(4):
        pair = jnp.concatenate([ot[:, j * BLOCK:(j + 1) * BLOCK] for ot in ots], axis=0)
        o_ref[0, :, j * LANES:(j + 1) * LANES] = pair.T.astype(BF16)


def _win_attn(sink, qa, ka, vat):
    bsz, seq, _ = qa.shape
    nb = seq // BLOCK
    qmap = lambda b, n: (b, n, 0)
    kspec = lambda f: pl.BlockSpec((1, BLOCK, KV_W), lambda b, n: (b, f(n), 0))
    vspec = lambda f: pl.BlockSpec((1, KV_W, BLOCK), lambda b, n: (b, 0, f(n)))
    prev = lambda n: jnp.maximum(n - 1, 0)
    cur = lambda n: n
    nxt = lambda n: jnp.minimum(n + 1, nb - 1)
    return pl.pallas_call(
        functools.partial(_win_attn_body, seq=seq), grid=(bsz, nb),
        in_specs=[pl.BlockSpec(memory_space=pltpu.SMEM),
                  pl.BlockSpec((1, BLOCK, Q_W), qmap),
                  kspec(prev), kspec(cur), kspec(nxt), vspec(prev), vspec(cur), vspec(nxt)],
        out_specs=pl.BlockSpec((1, BLOCK, Q_W), qmap),
        out_shape=jax.ShapeDtypeStruct((bsz, seq, Q_W), BF16), name="win_attn",
        compiler_params=pltpu.CompilerParams(dimension_semantics=("arbitrary", "arbitrary")),
    )(sink, qa, ka, ka, ka, vat, vat, vat)


def _grid_attn_body(q_ref, k_ref, vt_ref, o_ref, s0_ref, s1_ref, p0_ref, p1_ref):
    tq = q_ref.shape[1]
    seq = k_ref.shape[1]
    kc = KEY_CHUNK
    n_chunks = seq // kc
    s_bufs, p_bufs = (s0_ref, s1_ref), (p0_ref, p1_ref)
    heads = [(j, c) for j in range(4) for c in range(2)]
    ones = jnp.ones((2 * ROW_TILE, seq), BF16)

    def masked_q(h):
        j, c = heads[h]
        slab = q_ref[0, :, j * LANES:(j + 1) * LANES]
        return jnp.where(_half_mask(tq, c), slab, jnp.zeros_like(slab))

    def score_chunk(h, qm, kb, m8):
        sc = _dot_nt(k_ref[0, kb * kc:(kb + 1) * kc, :], qm)
        s_bufs[h % 2][kb * kc:(kb + 1) * kc, :] = sc
        cm = jnp.max(sc.reshape(kc // ROW_TILE, ROW_TILE, tq), axis=0)
        return cm if m8 is None else jnp.maximum(m8, cm)

    def prob_chunk(h, kb, m):
        x = s_bufs[h % 2][kb * kc:(kb + 1) * kc, :] - m
        p_bufs[h % 2][kb * kc:(kb + 1) * kc, :] = jnp.exp2(x.astype(BF16))

    qm = masked_q(0)
    m8 = None
    for kb in range(n_chunks):
        m8 = score_chunk(0, qm, kb, m8)
    outs = []
    for h in range(len(heads)):
        m = jnp.max(m8, axis=0, keepdims=True)
        nxt = h + 1 < len(heads)
        if nxt:
            qm = masked_q(h + 1)
            m8 = None
        for kb in range(n_chunks):
            if nxt:
                m8 = score_chunk(h + 1, qm, kb, m8)
            prob_chunk(h, kb, m)
        j, c = heads[h]
        va = jnp.concatenate([vt_ref[0, c * HEAD_DIM:(c + 1) * HEAD_DIM, :], ones], axis=0)
        ot = _dot(va, p_bufs[h % 2][...])
        outs.append(ot[:HEAD_DIM] / ot[HEAD_DIM:HEAD_DIM + 1])
        if c == 1:
            o_ref[0, :, j * LANES:(j + 1) * LANES] = jnp.concatenate(outs, axis=0).T.astype(BF16)
            outs = []


def _grid_attn(qb, kb, vbt):
    bsz, seq, _ = qb.shape
    tq = TQ_GRID
    return pl.pallas_call(
        _grid_attn_body, grid=(bsz, seq // tq),
        in_specs=[pl.BlockSpec((1, tq, Q_W), lambda b, n: (b, n, 0)),
                  pl.BlockSpec((1, seq, KV_W), lambda b, n: (b, 0, 0)),
                  pl.BlockSpec((1, KV_W, seq), lambda b, n: (b, 0, 0))],
        out_specs=pl.BlockSpec((1, tq, Q_W), lambda b, n: (b, n, 0)),
        scratch_shapes=[pltpu.VMEM((seq, tq), F32), pltpu.VMEM((seq, tq), F32),
                        pltpu.VMEM((seq, tq), BF16), pltpu.VMEM((seq, tq), BF16)],
        out_shape=jax.ShapeDtypeStruct((bsz, seq, Q_W), BF16), name="grid_attn",
        compiler_params=pltpu.CompilerParams(
            dimension_semantics=("arbitrary", "arbitrary"), vmem_limit_bytes=40 * 1024 * 1024),
    )(qb, kb, vbt)


def _post_attn_body(x_ref, oa_ref, ob_ref, sga_ref, sgb_ref, g0_ref, b0_ref, wa_ref, wb_ref, wo_ref,
                    g1_ref, b1_ref, wr_ref, br_ref, h1t_ref, tr_ref, tw_ref, cnt_out_ref, cnt_ref):
    h0 = _ln(x_ref[0], g0_ref[...], b0_ref[...])
    out_a = _dot(oa_ref[0], wa_ref[...])
    out_b = _dot(ob_ref[0], wb_ref[...])
    merged = sga_ref[0].astype(F32) * out_a + sgb_ref[0].astype(F32) * out_b
    mix = _dot(merged.astype(BF16), wo_ref[...])
    h1 = _ln(DN_ALPHA * h0 + mix, g1_ref[...], b1_ref[...])
    tm = h1.shape[0]
    _store_row_tiles(h1t_ref.at[0], h1, tm)

    logits = _dot(h1.astype(BF16), wr_ref[...]) + br_ref[...]
    lane = lax.broadcasted_iota(jnp.int32, (tm, LANES), 1)
    cur = jnp.where(lane < N_EXPERTS, logits, -jnp.inf)
    vals, idxs = [], []
    for _ in range(TOP_K):
        mv = jnp.max(cur, axis=-1, keepdims=True)
        ix = jnp.min(jnp.where(cur == mv, lane, LANES), axis=-1, keepdims=True)
        vals.append(mv)
        idxs.append(ix)
        cur = jnp.where(lane == ix, -jnp.inf, cur)
    es = [jnp.exp(v - vals[0]) for v in vals]
    tot = es[0] + es[1] + es[2] + es[3]

    @pl.when((pl.program_id(0) == 0) & (pl.program_id(1) == 0))
    def _():
        cnt_ref[...] = jnp.zeros_like(cnt_ref)

    sel = jnp.zeros((tm, LANES), F32)
    for kx in range(TOP_K):
        sel = sel + (lane == idxs[kx]).astype(F32)
    r_i = lax.broadcasted_iota(jnp.int32, (tm, tm), 0)
    c_i = lax.broadcasted_iota(jnp.int32, (tm, tm), 1)
    tri = (c_i < r_i).astype(BF16)
    rank = _dot(tri, sel.astype(BF16)) + cnt_ref[...]
    cnt_ref[...] = cnt_ref[...] + jnp.sum(sel, axis=0, keepdims=True)
    cnt_out_ref[...] = cnt_ref[...]

    tr = jnp.zeros((tm, LANES), jnp.int32)
    tw = jnp.zeros((tm, LANES), F32)
    for kx in range(TOP_K):
        tr = jnp.where(lane == kx, idxs[kx], tr)
        tw = jnp.where(lane == kx, es[kx] / tot, tw)
        rk_k = jnp.sum(jnp.where(lane == idxs[kx], rank, 0.0), axis=-1, keepdims=True)
        tr = jnp.where(lane == TOP_K + kx, rk_k.astype(jnp.int32), tr)
    tr_ref[0] = tr[:, :2 * TOP_K]
    tw_ref[0] = tw


def _post_attn(x, oa, ob, sga, sgb, g0, b0, wa, wb, wo, g1, b1, wr, br):
    bsz, seq, d = x.shape
    tm = TM_PROJ
    tok3 = lambda b, i: (b, i, 0)
    const = lambda b, i: (0, 0)
    full = lambda a: pl.BlockSpec(a.shape, const)
    return pl.pallas_call(
        _post_attn_body, grid=(bsz, seq // tm),
        in_specs=[pl.BlockSpec((1, tm, d), tok3),
                  pl.BlockSpec((1, tm, Q_W), tok3), pl.BlockSpec((1, tm, Q_W), tok3),
                  pl.BlockSpec((1, tm, d), tok3), pl.BlockSpec((1, tm, d), tok3),
                  full(g0), full(b0), full(wa), full(wb), full(wo), full(g1), full(b1),
                  full(wr), full(br)],
        out_specs=[pl.BlockSpec((1, tm * ROW_TILE, LANES), tok3), pl.BlockSpec((1, tm, 2 * TOP_K), tok3),
                   pl.BlockSpec((1, tm, LANES), tok3), pl.BlockSpec((1, LANES), const)],
        out_shape=[jax.ShapeDtypeStruct((bsz, seq * ROW_TILE, LANES), F32),
                   jax.ShapeDtypeStruct((bsz, seq, 2 * TOP_K), jnp.int32),
                   jax.ShapeDtypeStruct((bsz, seq, LANES), F32),
                   jax.ShapeDtypeStruct((1, LANES), F32)],
        scratch_shapes=[pltpu.VMEM((1, LANES), F32)],
        name="post_attn",
        compiler_params=pltpu.CompilerParams(
            dimension_semantics=("arbitrary", "arbitrary"), vmem_limit_bytes=48 * 1024 * 1024),
    )(x, oa, ob, sga, sgb, g0, b0, wa, wb, wo, g1, b1, wr, br)


def _dispatch_body(dest_ref, pend_ref, padded_ref, nu_ref, h1t_ref, xs_hbm, zbuf, sem, zsem):
    tm = TM_DISPATCH
    zrows = BM_EXPERT * ROW_TILE
    n_blocks = xs_hbm.shape[0] // zrows

    @pl.when(pl.program_id(0) == 0)
    def _():
        zbuf[...] = jnp.zeros_like(zbuf)
        zero_wait = pltpu.make_async_copy(zbuf, xs_hbm.at[pl.ds(0, zrows)], zsem).wait
        for e in range(N_EXPERTS):
            @pl.when(padded_ref[e] > 0)
            def _():
                start = pl.multiple_of((pend_ref[e] - BM_EXPERT) * ROW_TILE, ROW_TILE)
                pltpu.make_async_copy(zbuf, xs_hbm.at[pl.ds(start, zrows)], zsem).start()
        for b in range(n_blocks - N_EXPERTS, n_blocks):
            @pl.when(b >= nu_ref[0])
            def _():
                pltpu.make_async_copy(zbuf, xs_hbm.at[pl.ds(b * zrows, zrows)], zsem).start()
        for e in range(N_EXPERTS):
            pl.when(padded_ref[e] > 0)(zero_wait)
        for b in range(n_blocks - N_EXPERTS, n_blocks):
            pl.when(b >= nu_ref[0])(zero_wait)

    def issue(t, carry):
        for kx in range(TOP_K):
            d = dest_ref[TOP_K * t + kx]
            pltpu.make_async_copy(_tile(h1t_ref, t), _tile(xs_hbm, d), sem).start(priority=kx % 2)
        return carry

    lax.fori_loop(0, tm, issue, 0)
    for _ in range(TOP_K):
        pltpu.make_async_copy(h1t_ref, xs_hbm.at[pl.ds(0, tm * ROW_TILE)], sem).wait()


def _dispatch(dest_flat, pad_end, padded, n_used, h1t, n_rows):
    n_tok = h1t.shape[0] // ROW_TILE
    tm = TM_DISPATCH
    return pl.pallas_call(
        _dispatch_body, grid=(n_tok // tm,),
        in_specs=[pl.BlockSpec((TOP_K * tm,), lambda i: (i,), memory_space=pltpu.SMEM),
                  pl.BlockSpec(memory_space=pltpu.SMEM), pl.BlockSpec(memory_space=pltpu.SMEM),
                  pl.BlockSpec(memory_space=pltpu.SMEM),
                  pl.BlockSpec((tm * ROW_TILE, LANES), lambda i: (i, 0))],
        out_specs=pl.BlockSpec(memory_space=pl.ANY),
        out_shape=jax.ShapeDtypeStruct((n_rows * ROW_TILE, LANES), F32),
        scratch_shapes=[pltpu.VMEM((BM_EXPERT * ROW_TILE, LANES), F32),
                        pltpu.SemaphoreType.DMA(()), pltpu.SemaphoreType.DMA(())],
        name="dispatch",
        compiler_params=pltpu.CompilerParams(dimension_semantics=("arbitrary",)),
    )(dest_flat, pad_end, padded, n_used, h1t)


def _experts_body(be_ref, nu_ref, nx_ref, xs_ref, bg_ref, bu_ref, bd_ref, wg_hbm, wu_hbm, wd_hbm, ys_ref,
                  stage, wg_s, wu_s, wd_s, wsem):
    i = pl.program_id(0)
    bm = BM_EXPERT
    used = i < nu_ref[0]
    prev = be_ref[jnp.maximum(i - 1, 0)]
    fresh = (i == 0) | (be_ref[i] != prev)

    def weight_copies(e):
        return [pltpu.make_async_copy(w_hbm.at[e], stage.at[n], wsem)
                for n, w_hbm in enumerate((wg_hbm, wu_hbm, wd_hbm))]

    @pl.when(i == 0)
    def _():
        for cp in weight_copies(be_ref[0]):
            cp.start()

    @pl.when(used & fresh)
    def _():
        for cp in weight_copies(be_ref[i]):
            cp.wait()
        wg_s[...] = stage[0].astype(BF16)
        wu_s[...] = stage[1].astype(BF16)
        wd_s[...] = stage[2].astype(BF16)

        @pl.when(nx_ref[i] >= 0)
        def _():
            for cp in weight_copies(nx_ref[i]):
                cp.start()

    @pl.when(used)
    def _():
        xb = _load_row_tiles(xs_ref, bm).astype(BF16)
        g = _dot(xb, wg_s[...]) + bg_ref[0]
        u = _dot(xb, wu_s[...]) + bu_ref[0]
        g = jnp.minimum(g, SWIGLU_LIMIT)
        u = jnp.clip(u, -SWIGLU_LIMIT, SWIGLU_LIMIT)
        act = g * jax.nn.sigmoid(SWIGLU_ALPHA * g) * (u + 1.0)
        _store_row_tiles(ys_ref, _dot(act.astype(BF16), wd_s[...]) + bd_ref[0], bm)

    @pl.when(jnp.logical_not(used))
    def _():
        ys_ref[...] = jnp.zeros_like(ys_ref)


def _experts(block_e, n_used, next_e, xs, wg, bg, wu, bu, wd, bd):
    n_rows = xs.shape[0] // ROW_TILE
    bm = BM_EXPERT
    n_e, d, d_ff = wg.shape
    assert d == d_ff and wd.shape == wg.shape
    row = lambda i, be, nu, nx: (jnp.minimum(i, nu[0] - 1), 0)
    exp3 = lambda i, be, nu, nx: (be[jnp.minimum(i, nu[0] - 1)], 0, 0)
    any_spec = pl.BlockSpec(memory_space=pl.ANY)
    grid_spec = pltpu.PrefetchScalarGridSpec(
        num_scalar_prefetch=3, grid=(n_rows // bm,),
        in_specs=[pl.BlockSpec((bm * ROW_TILE, LANES), row),
                  pl.BlockSpec((1, 1, d_ff), exp3), pl.BlockSpec((1, 1, d_ff), exp3),
                  pl.BlockSpec((1, 1, d), exp3), any_spec, any_spec, any_spec],
        out_specs=pl.BlockSpec((bm * ROW_TILE, LANES), lambda i, be, nu, nx: (i, 0)),
        scratch_shapes=[pltpu.VMEM((3, d, d_ff), F32),
                        pltpu.VMEM((d, d_ff), BF16), pltpu.VMEM((d, d_ff), BF16), pltpu.VMEM((d_ff, d), BF16),
                        pltpu.SemaphoreType.DMA(())])
    return pl.pallas_call(
        _experts_body, grid_spec=grid_spec,
        out_shape=jax.ShapeDtypeStruct((n_rows * ROW_TILE, LANES), F32), name="experts",
        compiler_params=pltpu.CompilerParams(
            dimension_semantics=("arbitrary",), vmem_limit_bytes=48 * 1024 * 1024),
    )(block_e, n_used, next_e, xs, bg.reshape(n_e, 1, d_ff), bu.reshape(n_e, 1, d_ff), bd.reshape(n_e, 1, d),
      wg, wu, wd)


def _combine_body(dest_ref, dest_next_ref, h1t_ref, tw_ref, g2_ref, b2_ref, ys_hbm, out_ref, buf, sems):
    tm = TM_COMBINE
    i = pl.program_id(0)
    slot = i % 2

    def start_row(d_ref, s, t):
        for kx in range(TOP_K):
            d = d_ref[TOP_K * t + kx]
            pltpu.make_async_copy(_tile(ys_hbm, d), _tile(buf.at[s, kx], t), sems.at[s]).start(priority=kx % 2)

    def wait_tile(s):
        for kx in range(TOP_K):
            pltpu.make_async_copy(ys_hbm.at[pl.ds(0, tm * ROW_TILE)], buf.at[s, kx], sems.at[s]).wait()

    @pl.when(i == 0)
    def _():
        lax.fori_loop(0, tm, lambda t, c: (start_row(dest_ref, 0, t), c)[1], 0)

    wait_tile(slot)
    rc = COMBINE_ROW_CHUNK
    for c in range(tm // rc):
        for t in range(c * rc, (c + 1) * rc):
            start_row(dest_next_ref, 1 - slot, t)
        tw = tw_ref[c * rc:(c + 1) * rc, :]
        ffn = tw[:, 0:1] * _load_row_tiles(buf.at[slot, 0], rc, c * rc)
        for kx in range(1, TOP_K):
            ffn = ffn + tw[:, kx:kx + 1] * _load_row_tiles(buf.at[slot, kx], rc, c * rc)
        h1 = _load_row_tiles(h1t_ref, rc, c * rc)
        out_ref[c * rc:(c + 1) * rc, :] = _ln(DN_ALPHA * h1 + ffn, g2_ref[...], b2_ref[...])

    @pl.when(i == pl.num_programs(0) - 1)
    def _():
        wait_tile(1 - slot)


def _combine(dest_flat, h1t, tw, g2, b2, ys):
    n_tok = h1t.shape[0] // ROW_TILE
    d = ROW_TILE * LANES
    tm = TM_COMBINE
    n_steps = n_tok // tm
    const = lambda i: (0, 0)
    return pl.pallas_call(
        _combine_body, grid=(n_steps,),
        in_specs=[pl.BlockSpec((TOP_K * tm,), lambda i: (i,), memory_space=pltpu.SMEM),
                  pl.BlockSpec((TOP_K * tm,), lambda i: (jnp.minimum(i + 1, n_steps - 1),),
                               memory_space=pltpu.SMEM),
                  pl.BlockSpec((tm * ROW_TILE, LANES), lambda i: (i, 0)),
                  pl.BlockSpec((tm, LANES), lambda i: (i, 0)),
                  pl.BlockSpec((1, d), const), pl.BlockSpec((1, d), const),
                  pl.BlockSpec(memory_space=pl.ANY)],
        out_specs=pl.BlockSpec((tm, d), lambda i: (i, 0)),
        out_shape=jax.ShapeDtypeStruct((n_tok, d), F32),
        scratch_shapes=[pltpu.VMEM((2, TOP_K, tm * ROW_TILE, LANES), F32), pltpu.SemaphoreType.DMA((2,))],
        name="combine",
        compiler_params=pltpu.CompilerParams(
            dimension_semantics=("arbitrary",), vmem_limit_bytes=40 * 1024 * 1024),
    )(dest_flat, dest_flat, h1t, tw, g2, b2, ys)


def _rope_tables(seq, gain):
    t = np.arange(seq)
    row = (t // GRID_W).astype(np.float32)
    col = (t % GRID_W).astype(np.float32)
    half = HEAD_DIM // 2
    quarter = half // 2
    inv = (ROPE_THETA ** (-np.arange(quarter, dtype=np.float32) * np.float32(2.0 / half))).astype(np.float32)
    ang_r = row[:, None] * inv[None, :]
    ang_c = col[:, None] * inv[None, :]
    zeros = np.zeros_like(ang_r)
    cos = np.concatenate([np.cos(ang_r), np.cos(ang_r), np.cos(ang_c), np.cos(ang_c)], -1)
    s_lo = np.concatenate([-np.sin(ang_r), zeros, -np.sin(ang_c), zeros], -1)
    s_hi = np.concatenate([zeros, np.sin(ang_r), zeros, np.sin(ang_c)], -1)
    g = gain.astype(F32)
    return (jnp.asarray(cos, F32) * g[None, :], jnp.asarray(s_lo, F32) * jnp.roll(g, -quarter)[None, :],
            jnp.asarray(s_hi, F32) * jnp.roll(g, quarter)[None, :])


def _routing(top_i, rank, counts, bm):
    n_tok = top_i.shape[0]
    padded = (counts + bm - 1) // bm * bm
    pad_end = jnp.cumsum(padded)
    pad_start = pad_end - padded
    base = jnp.sum(jnp.where(top_i[:, :, None] == jnp.arange(N_EXPERTS, dtype=jnp.int32)[None, None, :],
                             pad_start[None, None, :], 0), axis=-1)
    dest = (base + rank).astype(jnp.int32)
    n_rows = n_tok * TOP_K + N_EXPERTS * bm
    n_blocks = n_rows // bm
    block_start = jnp.arange(n_blocks, dtype=jnp.int32) * bm
    block_e = jnp.minimum(jnp.sum((pad_end[None, :] <= block_start[:, None]).astype(jnp.int32), axis=1),
                          N_EXPERTS - 1).astype(jnp.int32)
    n_used = (pad_end[-1] // bm).astype(jnp.int32).reshape(1)
    e_ids = jnp.arange(N_EXPERTS, dtype=jnp.int32)
    later = (e_ids[None, :] > e_ids[:, None]) & (counts[None, :] > 0)
    next_of = jnp.min(jnp.where(later, e_ids[None, :], N_EXPERTS), axis=1)
    next_of = jnp.where(next_of == N_EXPERTS, -1, next_of).astype(jnp.int32)
    next_e = jnp.sum(jnp.where(block_e[:, None] == e_ids[None, :], next_of[None, :], 0), axis=1).astype(jnp.int32)
    return (dest.reshape(-1), block_e, n_used, next_e, n_rows, pad_end.astype(jnp.int32),
            padded.astype(jnp.int32))


def kernel(x, ln0_g, ln0_b, w_in, a_sink, b_q_norm, b_k_norm, w_branch_a, w_branch_b, w_out,
           ln1_g, ln1_b, w_router, b_router, w_gate, b_gate, w_up, b_up, w_down, b_down,
           ln2_g, ln2_b):
    bsz, seq, d = x.shape
    assert w_in.shape[0] == DEPTH == 1
    assert seq % TM_PROJ == 0 and seq % TQ_GRID == 0 and seq == (seq // GRID_W) * GRID_W
    n_tok = bsz * seq
    row2 = lambda a: a.reshape(1, -1)

    w = w_in[0]
    o_ka, o_va, o_qb, o_kb, o_vb, o_g = Q_W, Q_W + KV_W, Q_W + 2 * KV_W, 2 * Q_W + 2 * KV_W, \
        2 * Q_W + 3 * KV_W, 2 * Q_W + 4 * KV_W
    w_perm = jnp.concatenate([
        w[:, :Q_W][:, _PAIRED], w[:, o_ka:o_va],
        w[:, o_qb:o_kb][:, _PAIRED], w[:, o_kb:o_vb],
        w[:, o_va:o_qb], w[:, o_vb:o_g], w[:, o_g:]], axis=1).astype(BF16)
    head_id = np.arange(Q_W) // HEAD_DIM
    bd = jnp.asarray(head_id[:, None] == head_id[None, :], BF16)
    cq, s1q, s2q = (jnp.tile(t, (1, N_HEADS)) for t in _rope_tables(seq, b_q_norm[0]))
    ck, s1k, s2k = (jnp.tile(t, (1, N_KV)) for t in _rope_tables(seq, b_k_norm[0]))

    qa, ka, vat, qb, kb, vbt, sga, sgb = _in_proj(
        x, row2(ln0_g), row2(ln0_b), w_perm, bd, (cq, s1q, s2q, ck, s1k, s2k))
    oa = _win_attn(a_sink[0].astype(F32), qa, ka, vat)
    ob = _grid_attn(qb, kb, vbt)

    wr = jnp.zeros((d, LANES), BF16).at[:, :N_EXPERTS].set(w_router[0].astype(BF16))
    br = jnp.zeros((1, LANES), F32).at[0, :N_EXPERTS].set(b_router[0])
    h1t, tr, tw, cnt = _post_attn(
        x, oa, ob, sga, sgb, row2(ln0_g), row2(ln0_b),
        w_branch_a[0][_PAIRED].astype(BF16), w_branch_b[0][_PAIRED].astype(BF16),
        w_out[0].astype(BF16), row2(ln1_g[0]), row2(ln1_b[0]), wr, br)
    h1t = h1t.reshape(n_tok * ROW_TILE, LANES)
    tr = tr.reshape(n_tok, 2 * TOP_K)
    top_i, rank = tr[:, :TOP_K], tr[:, TOP_K:]
    tw = tw.reshape(n_tok, LANES)

    counts = cnt[0, :N_EXPERTS].astype(jnp.int32)
    dest, block_e, n_used, next_e, n_rows, pad_end, padded = _routing(top_i, rank, counts, BM_EXPERT)
    xs = _dispatch(dest, pad_end, padded, n_used, h1t, n_rows)
    ys = _experts(block_e, n_used, next_e, xs, w_gate[0], b_gate[0], w_up[0], b_up[0], w_down[0], b_down[0])
    out = _combine(dest, h1t, tw, row2(ln2_g[0]), row2(ln2_b[0]), ys)
    return out.reshape(bsz, seq, d)
```

```python
import functools

import jax
import jax.numpy as jnp
import numpy as np
from jax import lax
from jax.experimental import pallas as pl
from jax.experimental.pallas import tpu as pltpu

HEAD_DIM = 64
N_HEADS = 8
N_KV = 2
WINDOW = 128
BLOCK = 128
GRID_W = 64
ROPE_THETA = 10000.0
N_EXPERTS = 32
TOP_K = 4
SWIGLU_LIMIT = 7.0
SWIGLU_ALPHA = 1.702
LN_EPS = 1e-5
RMS_EPS = 1e-6
NEG_INF = -1e30
DEPTH = 1
DN_ALPHA = (2.0 * DEPTH) ** 0.25
ALIBI_SLOPES = tuple(2.0 ** (-8.0 * (h + 1) / N_HEADS) for h in range(N_HEADS))
QK_SCALE = HEAD_DIM ** -0.5
LOG2_E = 1.4426950408889634

LANES = 128
ROW_TILE = 8
Q_W = N_HEADS * HEAD_DIM
KV_W = N_KV * HEAD_DIM

TM_PROJ = 512
TQ_GRID = 256
KEY_CHUNK = 256
BM_EXPERT = 512
TM_DISPATCH = 256
TM_COMBINE = 256
COMBINE_ROW_CHUNK = 32

F32 = jnp.float32
BF16 = jnp.bfloat16

_PAIRED = np.array([(j if c == 0 else 4 + j) * HEAD_DIM + d
                    for j in range(4) for c in range(2) for d in range(HEAD_DIM)], np.int32)


def _ln(x, g, b):
    mu = jnp.mean(x, -1, keepdims=True)
    xc = x - mu
    var = jnp.mean(xc * xc, -1, keepdims=True)
    return xc * lax.rsqrt(var + LN_EPS) * g + b


def _dot(a, b):
    return jnp.dot(a, b, preferred_element_type=F32)


def _dot_nt(a, b):
    return lax.dot_general(a, b, (((1,), (1,)), ((), ())), preferred_element_type=F32)


def _load_row_tiles(ref, rows, first=0):
    return jnp.concatenate(
        [ref[pl.ds(first * ROW_TILE + c, rows, stride=ROW_TILE), :] for c in range(ROW_TILE)], axis=1)


def _store_row_tiles(ref, val, rows):
    for c in range(ROW_TILE):
        ref[pl.ds(c, rows, stride=ROW_TILE), :] = val[:, c * LANES:(c + 1) * LANES]


def _tile(ref, row):
    start = row * ROW_TILE
    if not isinstance(row, int):
        start = pl.multiple_of(start, ROW_TILE)
    return ref.at[pl.ds(start, ROW_TILE)]


def _in_proj_body(x_ref, g0_ref, b0_ref, w_ref, bd_ref, gq_ref, gk_ref, c_ref, s1_ref, s2_ref,
                  qa_ref, ka_ref, vat_ref, qb_ref, kb_ref, vbt_ref, sga_ref, sgb_ref):
    h = _ln(x_ref[0], g0_ref[...], b0_ref[...])
    hb = h.astype(BF16)

    def proj(lo, hi):
        return _dot(hb, w_ref[:, lo:hi])

    def norm_rope(t, width, g_ref):
        ss = _dot((t * t).astype(BF16), bd_ref[:width, :width])
        r = lax.rsqrt(ss * (1.0 / HEAD_DIM) + RMS_EPS)
        reps = width // LANES
        tab = lambda ref: jnp.concatenate([ref[...]] * reps, axis=1)
        y = t * g_ref[...]
        rot = y * tab(c_ref) + pltpu.roll(y, width - 16, 1) * tab(s1_ref) + pltpu.roll(y, 16, 1) * tab(s2_ref)
        return rot * r

    q2 = proj(0, 2 * Q_W)
    qa_ref[0] = (q2[:, :Q_W] * (QK_SCALE * LOG2_E)).astype(BF16)
    qb = norm_rope(q2[:, Q_W:], Q_W, gq_ref)
    qb_ref[0] = (qb * (QK_SCALE * LOG2_E)).astype(BF16)
    o = 2 * Q_W
    k2 = proj(o, o + 2 * KV_W)
    ka_ref[0] = k2[:, :KV_W].astype(BF16)
    kb_ref[0] = norm_rope(k2[:, KV_W:], KV_W, gk_ref).astype(BF16)
    o += 2 * KV_W
    v2t = proj(o, o + 2 * KV_W).T
    vat_ref[0] = v2t[:KV_W].astype(BF16)
    vbt_ref[0] = v2t[KV_W:].astype(BF16)
    o += 2 * KV_W
    d = sga_ref.shape[-1]
    sga_ref[0] = jax.nn.sigmoid(proj(o, o + d)).astype(BF16)
    o += d
    sgb_ref[0] = jax.nn.sigmoid(proj(o, o + d)).astype(BF16)


def _in_proj(x, g0, b0, w, bd, gq, gk, tabs):
    bsz, seq, d = x.shape
    tm = TM_PROJ
    n_in = w.shape[1]
    const = lambda i, j: (0, 0)
    tok3 = lambda i, j: (j, i, 0)
    tab = lambda i, j: (i, 0)
    in_specs = [
        pl.BlockSpec((1, tm, d), tok3),
        pl.BlockSpec((1, d), const), pl.BlockSpec((1, d), const),
        pl.BlockSpec((d, n_in), const),
        pl.BlockSpec((Q_W, Q_W), const),
        pl.BlockSpec((1, Q_W), const), pl.BlockSpec((1, KV_W), const),
        pl.BlockSpec((tm, LANES), tab), pl.BlockSpec((tm, LANES), tab), pl.BlockSpec((tm, LANES), tab),
    ]
    tr3 = lambda i, j: (j, 0, i)
    out_specs = [
        pl.BlockSpec((1, tm, Q_W), tok3), pl.BlockSpec((1, tm, KV_W), tok3),
        pl.BlockSpec((1, KV_W, tm), tr3),
        pl.BlockSpec((1, tm, Q_W), tok3), pl.BlockSpec((1, tm, KV_W), tok3),
        pl.BlockSpec((1, KV_W, tm), tr3),
        pl.BlockSpec((1, tm, d), tok3), pl.BlockSpec((1, tm, d), tok3),
    ]
    out_shape = [
        jax.ShapeDtypeStruct((bsz, seq, Q_W), BF16), jax.ShapeDtypeStruct((bsz, seq, KV_W), BF16),
        jax.ShapeDtypeStruct((bsz, KV_W, seq), BF16),
        jax.ShapeDtypeStruct((bsz, seq, Q_W), BF16), jax.ShapeDtypeStruct((bsz, seq, KV_W), BF16),
        jax.ShapeDtypeStruct((bsz, KV_W, seq), BF16),
        jax.ShapeDtypeStruct((bsz, seq, d), BF16), jax.ShapeDtypeStruct((bsz, seq, d), BF16),
    ]
    return pl.pallas_call(
        _in_proj_body, grid=(seq // tm, bsz), in_specs=in_specs, out_specs=out_specs,
        out_shape=out_shape, name="in_proj",
        compiler_params=pltpu.CompilerParams(
            dimension_semantics=("arbitrary", "arbitrary"), vmem_limit_bytes=48 * 1024 * 1024),
    )(x, g0, b0, w, bd, gq, gk, *tabs)


def _half_mask(rows, c):
    lane = lax.broadcasted_iota(jnp.int32, (rows, LANES), 1)
    return (lane >= HEAD_DIM) if c == 1 else (lane < HEAD_DIM)


def _win_attn_body(sink_ref, q_ref, k0_ref, k1_ref, k2_ref, v0_ref, v1_ref, v2_ref, o_ref, *, seq):
    n = pl.program_id(1)
    k = jnp.concatenate([k0_ref[0], k1_ref[0], k2_ref[0]], axis=0)
    vt = jnp.concatenate([v0_ref[0], v1_ref[0], v2_ref[0]], axis=1)
    kk = lax.broadcasted_iota(jnp.int32, (3 * BLOCK, BLOCK), 0)
    qq = lax.broadcasted_iota(jnp.int32, (3 * BLOCK, BLOCK), 1)
    dist_i = jnp.abs(kk - BLOCK - qq)
    k_pos = n * BLOCK - BLOCK + kk
    valid = (dist_i <= WINDOW) & (k_pos >= 0) & (k_pos < seq)
    dist = dist_i.astype(F32)
    slabs = [q_ref[0, :, j * LANES:(j + 1) * LANES] for j in range(4)]
    qm = jnp.concatenate(
        [jnp.where(_half_mask(BLOCK, c), s, jnp.zeros_like(s)) for c in range(2) for s in slabs], axis=0)
    st_all = _dot_nt(k, qm)
    ones = jnp.ones((2 * ROW_TILE, 3 * BLOCK), BF16)
    ots = []
    for c in range(2):
        ps, sinks = [], []
        for j in range(4):
            head = j + 4 * c
            col = (4 * c + j) * BLOCK
            st = st_all[:, col:col + BLOCK] + jnp.where(valid, (-ALIBI_SLOPES[head] * LOG2_E) * dist, NEG_INF)
            sk = sink_ref[head] * LOG2_E
            m = jnp.maximum(jnp.max(st, axis=0, keepdims=True), sk)
            ps.append(jnp.exp2((st - m).astype(BF16)))
            sinks.append(jnp.exp2(sk - m))
        va = jnp.concatenate([vt[c * HEAD_DIM:(c + 1) * HEAD_DIM, :], ones], axis=0)
        ot = _dot(va, jnp.concatenate(ps, axis=1))
        ots.append(ot[:HEAD_DIM] / (ot[HEAD_DIM:HEAD_DIM + 1] + jnp.concatenate(sinks, axis=1)))
    for j in range(4):
        pair = jnp.concatenate([ot[:, j * BLOCK:(j + 1) * BLOCK] for ot in ots], axis=0)
        o_ref[0, :, j * LANES:(j + 1) * LANES] = pair.T.astype(BF16)


def _win_attn(sink, qa, ka, vat):
    bsz, seq, _ = qa.shape
    nb = seq // BLOCK
    qmap = lambda b, n: (b, n, 0)
    kspec = lambda f: pl.BlockSpec((1, BLOCK, KV_W), lambda b, n: (b, f(n), 0))
    vspec = lambda f: pl.BlockSpec((1, KV_W, BLOCK), lambda b, n: (b, 0, f(n)))
    prev = lambda n: jnp.maximum(n - 1, 0)
    cur = lambda n: n
    nxt = lambda n: jnp.minimum(n + 1, nb - 1)
    return pl.pallas_call(
        functools.partial(_win_attn_body, seq=seq), grid=(bsz, nb),
        in_specs=[pl.BlockSpec(memory_space=pltpu.SMEM),
                  pl.BlockSpec((1, BLOCK, Q_W), qmap),
                  kspec(prev), kspec(cur), kspec(nxt), vspec(prev), vspec(cur), vspec(nxt)],
        out_specs=pl.BlockSpec((1, BLOCK, Q_W), qmap),
        out_shape=jax.ShapeDtypeStruct((bsz, seq, Q_W), BF16), name="win_attn",
        compiler_params=pltpu.CompilerParams(dimension_semantics=("arbitrary", "arbitrary")),
    )(sink, qa, ka, ka, ka, vat, vat, vat)


def _grid_attn_body(q_ref, k_ref, vt_ref, o_ref, s0_ref, s1_ref, p0_ref, p1_ref):
    tq = q_ref.shape[1]
    seq = k_ref.shape[1]
    kc = KEY_CHUNK
    n_chunks = seq // kc
    s_bufs, p_bufs = (s0_ref, s1_ref), (p0_ref, p1_ref)
    heads = [(j, c) for j in range(4) for c in range(2)]
    ones = jnp.ones((2 * ROW_TILE, seq), BF16)

    def masked_q(h):
        j, c = heads[h]
        slab = q_ref[0, :, j * LANES:(j + 1) * LANES]
        return jnp.where(_half_mask(tq, c), slab, jnp.zeros_like(slab))

    def score_chunk(h, qm, kb, m8):
        sc = _dot_nt(k_ref[0, kb * kc:(kb + 1) * kc, :], qm)
        s_bufs[h % 2][kb * kc:(kb + 1) * kc, :] = sc
        cm = jnp.max(sc.reshape(kc // ROW_TILE, ROW_TILE, tq), axis=0)
        return cm if m8 is None else jnp.maximum(m8, cm)

    def prob_chunk(h, kb, m):
        x = s_bufs[h % 2][kb * kc:(kb + 1) * kc, :] - m
        p_bufs[h % 2][kb * kc:(kb + 1) * kc, :] = jnp.exp2(x.astype(BF16))

    qm = masked_q(0)
    m8 = None
    for kb in range(n_chunks):
        m8 = score_chunk(0, qm, kb, m8)
    outs = []
    for h in range(len(heads)):
        m = jnp.max(m8, axis=0, keepdims=True)
        nxt = h + 1 < len(heads)
        if nxt:
            qm = masked_q(h + 1)
            m8 = None
        for kb in range(n_chunks):
            if nxt:
                m8 = score_chunk(h + 1, qm, kb, m8)
            prob_chunk(h, kb, m)
        j, c = heads[h]
        va = jnp.concatenate([vt_ref[0, c * HEAD_DIM:(c + 1) * HEAD_DIM, :], ones], axis=0)
        ot = _dot(va, p_bufs[h % 2][...])
        outs.append(ot[:HEAD_DIM] / ot[HEAD_DIM:HEAD_DIM + 1])
        if c == 1:
            o_ref[0, :, j * LANES:(j + 1) * LANES] = jnp.concatenate(outs, axis=0).T.astype(BF16)
            outs = []


def _grid_attn(qb, kb, vbt):
    bsz, seq, _ = qb.shape
    tq = TQ_GRID
    return pl.pallas_call(
        _grid_attn_body, grid=(bsz, seq // tq),
        in_specs=[pl.BlockSpec((1, tq, Q_W), lambda b, n: (b, n, 0)),
                  pl.BlockSpec((1, seq, KV_W), lambda b, n: (b, 0, 0)),
                  pl.BlockSpec((1, KV_W, seq), lambda b, n: (b, 0, 0))],
        out_specs=pl.BlockSpec((1, tq, Q_W), lambda b, n: (b, n, 0)),
        scratch_shapes=[pltpu.VMEM((seq, tq), F32), pltpu.VMEM((seq, tq), F32),
                        pltpu.VMEM((seq, tq), BF16), pltpu.VMEM((seq, tq), BF16)],
        out_shape=jax.ShapeDtypeStruct((bsz, seq, Q_W), BF16), name="grid_attn",
        compiler_params=pltpu.CompilerParams(
            dimension_semantics=("arbitrary", "arbitrary"), vmem_limit_bytes=40 * 1024 * 1024),
    )(qb, kb, vbt)


def _post_attn_body(x_ref, oa_ref, ob_ref, sga_ref, sgb_ref, g0_ref, b0_ref, wa_ref, wb_ref, wo_ref,
                    g1_ref, b1_ref, wr_ref, br_ref, h1t_ref, tr_ref, tw_ref, cnt_out_ref, cnt_ref):
    h0 = _ln(x_ref[0], g0_ref[...], b0_ref[...])
    out_a = _dot(oa_ref[0], wa_ref[...])
    out_b = _dot(ob_ref[0], wb_ref[...])
    merged = sga_ref[0].astype(F32) * out_a + sgb_ref[0].astype(F32) * out_b
    mix = _dot(merged.astype(BF16), wo_ref[...])
    h1 = _ln(DN_ALPHA * h0 + mix, g1_ref[...], b1_ref[...])
    tm = h1.shape[0]
    _store_row_tiles(h1t_ref.at[0], h1, tm)

    logits = _dot(h1.astype(BF16), wr_ref[...]) + br_ref[...]
    lane = lax.broadcasted_iota(jnp.int32, (tm, LANES), 1)
    cur = jnp.where(lane < N_EXPERTS, logits, -jnp.inf)
    vals, idxs = [], []
    for _ in range(TOP_K):
        mv = jnp.max(cur, axis=-1, keepdims=True)
        ix = jnp.min(jnp.where(cur == mv, lane, LANES), axis=-1, keepdims=True)
        vals.append(mv)
        idxs.append(ix)
        cur = jnp.where(lane == ix, -jnp.inf, cur)
    es = [jnp.exp(v - vals[0]) for v in vals]
    tot = es[0] + es[1] + es[2] + es[3]

    @pl.when((pl.program_id(0) == 0) & (pl.program_id(1) == 0))
    def _():
        cnt_ref[...] = jnp.zeros_like(cnt_ref)

    sel = jnp.zeros((tm, LANES), F32)
    for kx in range(TOP_K):
        sel = sel + (lane == idxs[kx]).astype(F32)
    r_i = lax.broadcasted_iota(jnp.int32, (tm, tm), 0)
    c_i = lax.broadcasted_iota(jnp.int32, (tm, tm), 1)
    tri = (c_i < r_i).astype(BF16)
    rank = _dot(tri, sel.astype(BF16)) + cnt_ref[...]
    cnt_ref[...] = cnt_ref[...] + jnp.sum(sel, axis=0, keepdims=True)
    cnt_out_ref[...] = cnt_ref[...]

    tr = jnp.zeros((tm, LANES), jnp.int32)
    tw = jnp.zeros((tm, LANES), F32)
    for kx in range(TOP_K):
        tr = jnp.where(lane == kx, idxs[kx], tr)
        tw = jnp.where(lane == kx, es[kx] / tot, tw)
        rk_k = jnp.sum(jnp.where(lane == idxs[kx], rank, 0.0), axis=-1, keepdims=True)
        tr = jnp.where(lane == TOP_K + kx, rk_k.astype(jnp.int32), tr)
    tr_ref[0] = tr[:, :2 * TOP_K]
    tw_ref[0] = tw


def _post_attn(x, oa, ob, sga, sgb, g0, b0, wa, wb, wo, g1, b1, wr, br):
    bsz, seq, d = x.shape
    tm = TM_PROJ
    tok3 = lambda b, i: (b, i, 0)
    const = lambda b, i: (0, 0)
    full = lambda a: pl.BlockSpec(a.shape, const)
    return pl.pallas_call(
        _post_attn_body, grid=(bsz, seq // tm),
        in_specs=[pl.BlockSpec((1, tm, d), tok3),
                  pl.BlockSpec((1, tm, Q_W), tok3), pl.BlockSpec((1, tm, Q_W), tok3),
                  pl.BlockSpec((1, tm, d), tok3), pl.BlockSpec((1, tm, d), tok3),
                  full(g0), full(b0), full(wa), full(wb), full(wo), full(g1), full(b1),
                  full(wr), full(br)],
        out_specs=[pl.BlockSpec((1, tm * ROW_TILE, LANES), tok3), pl.BlockSpec((1, tm, 2 * TOP_K), tok3),
                   pl.BlockSpec((1, tm, LANES), tok3), pl.BlockSpec((1, LANES), const)],
        out_shape=[jax.ShapeDtypeStruct((bsz, seq * ROW_TILE, LANES), F32),
                   jax.ShapeDtypeStruct((bsz, seq, 2 * TOP_K), jnp.int32),
                   jax.ShapeDtypeStruct((bsz, seq, LANES), F32),
                   jax.ShapeDtypeStruct((1, LANES), F32)],
        scratch_shapes=[pltpu.VMEM((1, LANES), F32)],
        name="post_attn",
        compiler_params=pltpu.CompilerParams(
            dimension_semantics=("arbitrary", "arbitrary"), vmem_limit_bytes=48 * 1024 * 1024),
    )(x, oa, ob, sga, sgb, g0, b0, wa, wb, wo, g1, b1, wr, br)


def _dispatch_body(dest_ref, pend_ref, padded_ref, nu_ref, h1t_ref, xs_hbm, zbuf, sem, zsem):
    tm = TM_DISPATCH
    zrows = BM_EXPERT * ROW_TILE
    n_blocks = xs_hbm.shape[0] // zrows

    @pl.when(pl.program_id(0) == 0)
    def _():
        zbuf[...] = jnp.zeros_like(zbuf)
        zero_wait = pltpu.make_async_copy(zbuf, xs_hbm.at[pl.ds(0, zrows)], zsem).wait
        for e in range(N_EXPERTS):
            @pl.when(padded_ref[e] > 0)
            def _():
                start = pl.multiple_of((pend_ref[e] - BM_EXPERT) * ROW_TILE, ROW_TILE)
                pltpu.make_async_copy(zbuf, xs_hbm.at[pl.ds(start, zrows)], zsem).start()
        for b in range(n_blocks - N_EXPERTS, n_blocks):
            @pl.when(b >= nu_ref[0])
            def _():
                pltpu.make_async_copy(zbuf, xs_hbm.at[pl.ds(b * zrows, zrows)], zsem).start()
        for e in range(N_EXPERTS):
            pl.when(padded_ref[e] > 0)(zero_wait)
        for b in range(n_blocks - N_EXPERTS, n_blocks):
            pl.when(b >= nu_ref[0])(zero_wait)

    for t in range(tm):
        for kx in range(TOP_K):
            d = dest_ref[TOP_K * t + kx]
            pltpu.make_async_copy(_tile(h1t_ref, t), _tile(xs_hbm, d), sem).start(priority=kx % 2)
    for _ in range(TOP_K):
        pltpu.make_async_copy(h1t_ref, xs_hbm.at[pl.ds(0, tm * ROW_TILE)], sem).wait()


def _dispatch(dest_flat, pad_end, padded, n_used, h1t, n_rows):
    n_tok = h1t.shape[0] // ROW_TILE
    tm = TM_DISPATCH
    return pl.pallas_call(
        _dispatch_body, grid=(n_tok // tm,),
        in_specs=[pl.BlockSpec((TOP_K * tm,), lambda i: (i,), memory_space=pltpu.SMEM),
                  pl.BlockSpec(memory_space=pltpu.SMEM), pl.BlockSpec(memory_space=pltpu.SMEM),
                  pl.BlockSpec(memory_space=pltpu.SMEM),
                  pl.BlockSpec((tm * ROW_TILE, LANES), lambda i: (i, 0))],
        out_specs=pl.BlockSpec(memory_space=pl.ANY),
        out_shape=jax.ShapeDtypeStruct((n_rows * ROW_TILE, LANES), F32),
        scratch_shapes=[pltpu.VMEM((BM_EXPERT * ROW_TILE, LANES), F32),
                        pltpu.SemaphoreType.DMA(()), pltpu.SemaphoreType.DMA(())],
        name="dispatch",
        compiler_params=pltpu.CompilerParams(dimension_semantics=("arbitrary",)),
    )(dest_flat, pad_end, padded, n_used, h1t)


def _experts_body(be_ref, nu_ref, nx_ref, xs_ref, bg_ref, bu_ref, bd_ref, wg_hbm, wu_hbm, wd_hbm, ys_ref,
                  stage, wg_s, wu_s, wd_s, wsem):
    i = pl.program_id(0)
    bm = BM_EXPERT
    used = i < nu_ref[0]
    prev = be_ref[jnp.maximum(i - 1, 0)]
    fresh = (i == 0) | (be_ref[i] != prev)

    def weight_copies(e):
        return [pltpu.make_async_copy(w_hbm.at[e], stage.at[n], wsem)
                for n, w_hbm in enumerate((wg_hbm, wu_hbm, wd_hbm))]

    @pl.when(i == 0)
    def _():
        for cp in weight_copies(be_ref[0]):
            cp.start()

    @pl.when(used & fresh)
    def _():
        for cp in weight_copies(be_ref[i]):
            cp.wait()
        wg_s[...] = stage[0].astype(BF16)
        wu_s[...] = stage[1].astype(BF16)
        wd_s[...] = stage[2].astype(BF16)

        @pl.when(nx_ref[i] >= 0)
        def _():
            for cp in weight_copies(nx_ref[i]):
                cp.start()

    @pl.when(used)
    def _():
        xb = _load_row_tiles(xs_ref, bm).astype(BF16)
        g = _dot(xb, wg_s[...]) + bg_ref[0]
        u = _dot(xb, wu_s[...]) + bu_ref[0]
        g = jnp.minimum(g, SWIGLU_LIMIT)
        u = jnp.clip(u, -SWIGLU_LIMIT, SWIGLU_LIMIT)
        act = g * jax.nn.sigmoid(SWIGLU_ALPHA * g) * (u + 1.0)
        _store_row_tiles(ys_ref, _dot(act.astype(BF16), wd_s[...]) + bd_ref[0], bm)

    @pl.when(jnp.logical_not(used))
    def _():
        ys_ref[...] = jnp.zeros_like(ys_ref)


def _experts(block_e, n_used, next_e, xs, wg, bg, wu, bu, wd, bd):
    n_rows = xs.shape[0] // ROW_TILE
    bm = BM_EXPERT
    n_e, d, d_ff = wg.shape
    assert d == d_ff and wd.shape == wg.shape
    row = lambda i, be, nu, nx: (jnp.minimum(i, nu[0] - 1), 0)
    exp3 = lambda i, be, nu, nx: (be[jnp.minimum(i, nu[0] - 1)], 0, 0)
    any_spec = pl.BlockSpec(memory_space=pl.ANY)
    grid_spec = pltpu.PrefetchScalarGridSpec(
        num_scalar_prefetch=3, grid=(n_rows // bm,),
        in_specs=[pl.BlockSpec((bm * ROW_TILE, LANES), row),
                  pl.BlockSpec((1, 1, d_ff), exp3), pl.BlockSpec((1, 1, d_ff), exp3),
                  pl.BlockSpec((1, 1, d), exp3), any_spec, any_spec, any_spec],
        out_specs=pl.BlockSpec((bm * ROW_TILE, LANES), lambda i, be, nu, nx: (i, 0)),
        scratch_shapes=[pltpu.VMEM((3, d, d_ff), F32),
                        pltpu.VMEM((d, d_ff), BF16), pltpu.VMEM((d, d_ff), BF16), pltpu.VMEM((d_ff, d), BF16),
                        pltpu.SemaphoreType.DMA(())])
    return pl.pallas_call(
        _experts_body, grid_spec=grid_spec,
        out_shape=jax.ShapeDtypeStruct((n_rows * ROW_TILE, LANES), F32), name="experts",
        compiler_params=pltpu.CompilerParams(
            dimension_semantics=("arbitrary",), vmem_limit_bytes=48 * 1024 * 1024),
    )(block_e, n_used, next_e, xs, bg.reshape(n_e, 1, d_ff), bu.reshape(n_e, 1, d_ff), bd.reshape(n_e, 1, d),
      wg, wu, wd)


def _combine_body(dest_ref, dest_next_ref, h1t_ref, tw_ref, g2_ref, b2_ref, ys_hbm, out_ref, buf, sems):
    tm = TM_COMBINE
    i = pl.program_id(0)
    slot = i % 2

    def start_row(d_ref, s, t):
        for kx in range(TOP_K):
            d = d_ref[TOP_K * t + kx]
            pltpu.make_async_copy(_tile(ys_hbm, d), _tile(buf.at[s, kx], t), sems.at[s]).start(priority=kx % 2)

    def wait_tile(s):
        for kx in range(TOP_K):
            pltpu.make_async_copy(ys_hbm.at[pl.ds(0, tm * ROW_TILE)], buf.at[s, kx], sems.at[s]).wait()

    @pl.when(i == 0)
    def _():
        lax.fori_loop(0, tm, lambda t, c: (start_row(dest_ref, 0, t), c)[1], 0)

    wait_tile(slot)
    rc = COMBINE_ROW_CHUNK
    for c in range(tm // rc):
        for t in range(c * rc, (c + 1) * rc):
            start_row(dest_next_ref, 1 - slot, t)
        tw = tw_ref[c * rc:(c + 1) * rc, :]
        ffn = tw[:, 0:1] * _load_row_tiles(buf.at[slot, 0], rc, c * rc)
        for kx in range(1, TOP_K):
            ffn = ffn + tw[:, kx:kx + 1] * _load_row_tiles(buf.at[slot, kx], rc, c * rc)
        h1 = _load_row_tiles(h1t_ref, rc, c * rc)
        out_ref[c * rc:(c + 1) * rc, :] = _ln(DN_ALPHA * h1 + ffn, g2_ref[...], b2_ref[...])

    @pl.when(i == pl.num_programs(0) - 1)
    def _():
        wait_tile(1 - slot)


def _combine(dest_flat, h1t, tw, g2, b2, ys):
    n_tok = h1t.shape[0] // ROW_TILE
    d = ROW_TILE * LANES
    tm = TM_COMBINE
    n_steps = n_tok // tm
    const = lambda i: (0, 0)
    return pl.pallas_call(
        _combine_body, grid=(n_steps,),
        in_specs=[pl.BlockSpec((TOP_K * tm,), lambda i: (i,), memory_space=pltpu.SMEM),
                  pl.BlockSpec((TOP_K * tm,), lambda i: (jnp.minimum(i + 1, n_steps - 1),),
                               memory_space=pltpu.SMEM),
                  pl.BlockSpec((tm * ROW_TILE, LANES), lambda i: (i, 0)),
                  pl.BlockSpec((tm, LANES), lambda i: (i, 0)),
                  pl.BlockSpec((1, d), const), pl.BlockSpec((1, d), const),
                  pl.BlockSpec(memory_space=pl.ANY)],
        out_specs=pl.BlockSpec((tm, d), lambda i: (i, 0)),
        out_shape=jax.ShapeDtypeStruct((n_tok, d), F32),
        scratch_shapes=[pltpu.VMEM((2, TOP_K, tm * ROW_TILE, LANES), F32), pltpu.SemaphoreType.DMA((2,))],
        name="combine",
        compiler_params=pltpu.CompilerParams(
            dimension_semantics=("arbitrary",), vmem_limit_bytes=40 * 1024 * 1024),
    )(dest_flat, dest_flat, h1t, tw, g2, b2, ys)


def _rope_tables(seq):
    t = np.arange(seq)
    row = (t // GRID_W).astype(np.float32)
    col = (t % GRID_W).astype(np.float32)
    half = HEAD_DIM // 2
    quarter = half // 2
    inv = (ROPE_THETA ** (-np.arange(quarter, dtype=np.float32) * np.float32(2.0 / half))).astype(np.float32)
    ang_r = row[:, None] * inv[None, :]
    ang_c = col[:, None] * inv[None, :]
    zeros = np.zeros_like(ang_r)
    cos = np.concatenate([np.cos(ang_r), np.cos(ang_r), np.cos(ang_c), np.cos(ang_c)], -1)
    s_lo = np.concatenate([-np.sin(ang_r), zeros, -np.sin(ang_c), zeros], -1)
    s_hi = np.concatenate([zeros, np.sin(ang_r), zeros, np.sin(ang_c)], -1)
    return tuple(jnp.asarray(np.tile(a, (1, LANES // HEAD_DIM)), F32) for a in (cos, s_lo, s_hi))


def _routing(top_i, rank, counts, bm):
    n_tok = top_i.shape[0]
    padded = (counts + bm - 1) // bm * bm
    pad_end = jnp.cumsum(padded)
    pad_start = pad_end - padded
    base = jnp.sum(jnp.where(top_i[:, :, None] == jnp.arange(N_EXPERTS, dtype=jnp.int32)[None, None, :],
                             pad_start[None, None, :], 0), axis=-1)
    dest = (base + rank).astype(jnp.int32)
    n_rows = n_tok * TOP_K + N_EXPERTS * bm
    n_blocks = n_rows // bm
    block_start = jnp.arange(n_blocks, dtype=jnp.int32) * bm
    block_e = jnp.minimum(jnp.sum((pad_end[None, :] <= block_start[:, None]).astype(jnp.int32), axis=1),
                          N_EXPERTS - 1).astype(jnp.int32)
    n_used = (pad_end[-1] // bm).astype(jnp.int32).reshape(1)
    e_ids = jnp.arange(N_EXPERTS, dtype=jnp.int32)
    later = (e_ids[None, :] > e_ids[:, None]) & (counts[None, :] > 0)
    next_of = jnp.min(jnp.where(later, e_ids[None, :], N_EXPERTS), axis=1)
    next_of = jnp.where(next_of == N_EXPERTS, -1, next_of).astype(jnp.int32)
    next_e = jnp.sum(jnp.where(block_e[:, None] == e_ids[None, :], next_of[None, :], 0), axis=1).astype(jnp.int32)
    return (dest.reshape(-1), block_e, n_used, next_e, n_rows, pad_end.astype(jnp.int32),
            padded.astype(jnp.int32))


def kernel(x, ln0_g, ln0_b, w_in, a_sink, b_q_norm, b_k_norm, w_branch_a, w_branch_b, w_out,
           ln1_g, ln1_b, w_router, b_router, w_gate, b_gate, w_up, b_up, w_down, b_down,
           ln2_g, ln2_b):
    bsz, seq, d = x.shape
    assert w_in.shape[0] == DEPTH == 1
    assert seq % TM_PROJ == 0 and seq % TQ_GRID == 0 and seq == (seq // GRID_W) * GRID_W
    n_tok = bsz * seq
    row2 = lambda a: a.reshape(1, -1)

    o_ka, o_va, o_qb, o_kb, o_vb, o_g = Q_W, Q_W + KV_W, Q_W + 2 * KV_W, 2 * Q_W + 2 * KV_W, \
        2 * Q_W + 3 * KV_W, 2 * Q_W + 4 * KV_W
    col = np.arange(w_in.shape[-1])
    col_perm = np.concatenate([
        col[:Q_W][_PAIRED], col[o_qb:o_kb][_PAIRED], col[o_ka:o_va], col[o_kb:o_vb],
        col[o_va:o_qb], col[o_vb:o_g], col[o_g:]])
    w_perm = w_in[0].astype(BF16)[:, col_perm]
    head_id = np.arange(Q_W) // HEAD_DIM
    bd = jnp.asarray(head_id[:, None] == head_id[None, :], BF16)
    gq = jnp.tile(b_q_norm[0].astype(F32), N_HEADS).reshape(1, Q_W)
    gk = jnp.tile(b_k_norm[0].astype(F32), N_KV).reshape(1, KV_W)

    qa, ka, vat, qb, kb, vbt, sga, sgb = _in_proj(
        x, row2(ln0_g), row2(ln0_b), w_perm, bd, gq, gk, _rope_tables(seq))
    oa = _win_attn(a_sink[0].astype(F32), qa, ka, vat)
    ob = _grid_attn(qb, kb, vbt)

    wr = jnp.zeros((d, LANES), BF16).at[:, :N_EXPERTS].set(w_router[0].astype(BF16))
    br = jnp.zeros((1, LANES), F32).at[0, :N_EXPERTS].set(b_router[0])
    h1t, tr, tw, cnt = _post_attn(
        x, oa, ob, sga, sgb, row2(ln0_g), row2(ln0_b),
        w_branch_a[0][_PAIRED].astype(BF16), w_branch_b[0][_PAIRED].astype(BF16),
        w_out[0].astype(BF16), row2(ln1_g[0]), row2(ln1_b[0]), wr, br)
    h1t = h1t.reshape(n_tok * ROW_TILE, LANES)
    tr = tr.reshape(n_tok, 2 * TOP_K)
    top_i, rank = tr[:, :TOP_K], tr[:, TOP_K:]
    tw = tw.reshape(n_tok, LANES)

    counts = cnt[0, :N_EXPERTS].astype(jnp.int32)
    dest, block_e, n_used, next_e, n_rows, pad_end, padded = _routing(top_i, rank, counts, BM_EXPERT)
    xs = _dispatch(dest, pad_end, padded, n_used, h1t, n_rows)
    ys = _experts(block_e, n_used, next_e, xs, w_gate[0], b_gate[0], w_up[0], b_up[0], w_down[0], b_down[0])
    out = _combine(dest, h1t, tw, row2(ln2_g[0]), row2(ln2_b[0]), ys)
    return out.reshape(bsz, seq, d)
```

```python
import functools

import jax
import jax.numpy as jnp
import numpy as np
from jax import lax
from jax.experimental import pallas as pl
from jax.experimental.pallas import tpu as pltpu

HEAD_DIM = 64
N_HEADS = 8
N_KV = 2
WINDOW = 128
BLOCK = 128
GRID_W = 64
ROPE_THETA = 10000.0
N_EXPERTS = 32
TOP_K = 4
SWIGLU_LIMIT = 7.0
SWIGLU_ALPHA = 1.702
LN_EPS = 1e-5
RMS_EPS = 1e-6
NEG_INF = -1e30
DEPTH = 1
DN_ALPHA = (2.0 * DEPTH) ** 0.25
ALIBI_SLOPES = tuple(2.0 ** (-8.0 * (h + 1) / N_HEADS) for h in range(N_HEADS))
QK_SCALE = HEAD_DIM ** -0.5
LOG2_E = 1.4426950408889634

LANES = 128
ROW_TILE = 8
Q_W = N_HEADS * HEAD_DIM
KV_W = N_KV * HEAD_DIM

TM_PROJ = 512
TQ_GRID = 256
KEY_CHUNK = 256
BM_EXPERT = 512
TM_DISPATCH = 256
TM_COMBINE = 256
COMBINE_ROW_CHUNK = 32

F32 = jnp.float32
BF16 = jnp.bfloat16

_PAIRED = np.array([(j if c == 0 else 4 + j) * HEAD_DIM + d
                    for j in range(4) for c in range(2) for d in range(HEAD_DIM)], np.int32)


def _ln(x, g, b):
    mu = jnp.mean(x, -1, keepdims=True)
    xc = x - mu
    var = jnp.mean(xc * xc, -1, keepdims=True)
    return xc * lax.rsqrt(var + LN_EPS) * g + b


def _dot(a, b):
    return jnp.dot(a, b, preferred_element_type=F32)


def _dot_nt(a, b):
    return lax.dot_general(a, b, (((1,), (1,)), ((), ())), preferred_element_type=F32)


def _load_row_tiles(ref, rows, first=0):
    return jnp.concatenate(
        [ref[pl.ds(first * ROW_TILE + c, rows, stride=ROW_TILE), :] for c in range(ROW_TILE)], axis=1)


def _store_row_tiles(ref, val, rows):
    for c in range(ROW_TILE):
        ref[pl.ds(c, rows, stride=ROW_TILE), :] = val[:, c * LANES:(c + 1) * LANES]


def _tile(ref, row):
    start = row * ROW_TILE
    if not isinstance(row, int):
        start = pl.multiple_of(start, ROW_TILE)
    return ref.at[pl.ds(start, ROW_TILE)]


def _in_proj_body(x_ref, g0_ref, b0_ref, w_ref, bd_ref, gq_ref, gk_ref, c_ref, s1_ref, s2_ref,
                  qa_ref, ka_ref, vat_ref, qb_ref, kb_ref, vbt_ref, sga_ref, sgb_ref):
    h = _ln(x_ref[0], g0_ref[...], b0_ref[...])
    hb = h.astype(BF16)

    def proj(lo, hi):
        return _dot(hb, w_ref[:, lo:hi])

    def norm_rope(t, width, g_ref):
        ss = _dot((t * t).astype(BF16), bd_ref[:width, :width])
        r = lax.rsqrt(ss * (1.0 / HEAD_DIM) + RMS_EPS)
        reps = width // LANES
        tab = lambda ref: jnp.concatenate([ref[...]] * reps, axis=1)
        y = t * g_ref[...]
        rot = y * tab(c_ref) + pltpu.roll(y, width - 16, 1) * tab(s1_ref) + pltpu.roll(y, 16, 1) * tab(s2_ref)
        return rot * r

    q2 = proj(0, 2 * Q_W)
    qa_ref[0] = (q2[:, :Q_W] * (QK_SCALE * LOG2_E)).astype(BF16)
    qb = norm_rope(q2[:, Q_W:], Q_W, gq_ref)
    qb_ref[0] = (qb * (QK_SCALE * LOG2_E)).astype(BF16)
    o = 2 * Q_W
    k2 = proj(o, o + 2 * KV_W)
    ka_ref[0] = k2[:, :KV_W].astype(BF16)
    kb_ref[0] = norm_rope(k2[:, KV_W:], KV_W, gk_ref).astype(BF16)
    o += 2 * KV_W
    v2t = proj(o, o + 2 * KV_W).T
    vat_ref[0] = v2t[:KV_W].astype(BF16)
    vbt_ref[0] = v2t[KV_W:].astype(BF16)
    o += 2 * KV_W
    d = sga_ref.shape[-1]
    sga_ref[0] = jax.nn.sigmoid(proj(o, o + d)).astype(BF16)
    o += d
    sgb_ref[0] = jax.nn.sigmoid(proj(o, o + d)).astype(BF16)


def _in_proj(x, g0, b0, w, bd, gq, gk, tabs):
    bsz, seq, d = x.shape
    tm = TM_PROJ
    n_in = w.shape[1]
    const = lambda i, j: (0, 0)
    tok3 = lambda i, j: (j, i, 0)
    tab = lambda i, j: (i, 0)
    in_specs = [
        pl.BlockSpec((1, tm, d), tok3),
        pl.BlockSpec((1, d), const), pl.BlockSpec((1, d), const),
        pl.BlockSpec((d, n_in), const),
        pl.BlockSpec((Q_W, Q_W), const),
        pl.BlockSpec((1, Q_W), const), pl.BlockSpec((1, KV_W), const),
        pl.BlockSpec((tm, LANES), tab), pl.BlockSpec((tm, LANES), tab), pl.BlockSpec((tm, LANES), tab),
    ]
    tr3 = lambda i, j: (j, 0, i)
    out_specs = [
        pl.BlockSpec((1, tm, Q_W), tok3), pl.BlockSpec((1, tm, KV_W), tok3),
        pl.BlockSpec((1, KV_W, tm), tr3),
        pl.BlockSpec((1, tm, Q_W), tok3), pl.BlockSpec((1, tm, KV_W), tok3),
        pl.BlockSpec((1, KV_W, tm), tr3),
        pl.BlockSpec((1, tm, d), tok3), pl.BlockSpec((1, tm, d), tok3),
    ]
    out_shape = [
        jax.ShapeDtypeStruct((bsz, seq, Q_W), BF16), jax.ShapeDtypeStruct((bsz, seq, KV_W), BF16),
        jax.ShapeDtypeStruct((bsz, KV_W, seq), BF16),
        jax.ShapeDtypeStruct((bsz, seq, Q_W), BF16), jax.ShapeDtypeStruct((bsz, seq, KV_W), BF16),
        jax.ShapeDtypeStruct((bsz, KV_W, seq), BF16),
        jax.ShapeDtypeStruct((bsz, seq, d), BF16), jax.ShapeDtypeStruct((bsz, seq, d), BF16),
    ]
    return pl.pallas_call(
        _in_proj_body, grid=(seq // tm, bsz), in_specs=in_specs, out_specs=out_specs,
        out_shape=out_shape, name="in_proj",
        compiler_params=pltpu.CompilerParams(
            dimension_semantics=("arbitrary", "arbitrary"), vmem_limit_bytes=48 * 1024 * 1024),
    )(x, g0, b0, w, bd, gq, gk, *tabs)


def _half_mask(rows, c):
    lane = lax.broadcasted_iota(jnp.int32, (rows, LANES), 1)
    return (lane >= HEAD_DIM) if c == 1 else (lane < HEAD_DIM)


def _win_attn_body(sink_ref, q_ref, k0_ref, k1_ref, k2_ref, v0_ref, v1_ref, v2_ref, o_ref, *, seq):
    n = pl.program_id(1)
    k = jnp.concatenate([k0_ref[0], k1_ref[0], k2_ref[0]], axis=0)
    vt = jnp.concatenate([v0_ref[0], v1_ref[0], v2_ref[0]], axis=1)
    kk = lax.broadcasted_iota(jnp.int32, (3 * BLOCK, BLOCK), 0)
    qq = lax.broadcasted_iota(jnp.int32, (3 * BLOCK, BLOCK), 1)
    dist_i = jnp.abs(kk - BLOCK - qq)
    k_pos = n * BLOCK - BLOCK + kk
    valid = (dist_i <= WINDOW) & (k_pos >= 0) & (k_pos < seq)
    dist = dist_i.astype(F32)
    slabs = [q_ref[0, :, j * LANES:(j + 1) * LANES] for j in range(4)]
    qm = jnp.concatenate(
        [jnp.where(_half_mask(BLOCK, c), s, jnp.zeros_like(s)) for c in range(2) for s in slabs], axis=0)
    st_all = _dot_nt(k, qm)
    ones = jnp.ones((2 * ROW_TILE, 3 * BLOCK), BF16)
    ots = []
    for c in range(2):
        ps, sinks = [], []
        for j in range(4):
            head = j + 4 * c
            col = (4 * c + j) * BLOCK
            st = st_all[:, col:col + BLOCK] + jnp.where(valid, (-ALIBI_SLOPES[head] * LOG2_E) * dist, NEG_INF)
            sk = sink_ref[head] * LOG2_E
            m = jnp.maximum(jnp.max(st, axis=0, keepdims=True), sk)
            ps.append(jnp.exp2((st - m).astype(BF16)))
            sinks.append(jnp.exp2(sk - m))
        va = jnp.concatenate([vt[c * HEAD_DIM:(c + 1) * HEAD_DIM, :], ones], axis=0)
        ot = _dot(va, jnp.concatenate(ps, axis=1))
        ots.append(ot[:HEAD_DIM] / (ot[HEAD_DIM:HEAD_DIM + 1] + jnp.concatenate(sinks, axis=1)))
    for j in range(4):
        pair = jnp.concatenate([ot[:, j * BLOCK:(j + 1) * BLOCK] for ot in ots], axis=0)
        o_ref[0, :, j * LANES:(j + 1) * LANES] = pair.T.astype(BF16)


def _win_attn(sink, qa, ka, vat):
    bsz, seq, _ = qa.shape
    nb = seq // BLOCK
    qmap = lambda b, n: (b, n, 0)
    kspec = lambda f: pl.BlockSpec((1, BLOCK, KV_W), lambda b, n: (b, f(n), 0))
    vspec = lambda f: pl.BlockSpec((1, KV_W, BLOCK), lambda b, n: (b, 0, f(n)))
    prev = lambda n: jnp.maximum(n - 1, 0)
    cur = lambda n: n
    nxt = lambda n: jnp.minimum(n + 1, nb - 1)
    return pl.pallas_call(
        functools.partial(_win_attn_body, seq=seq), grid=(bsz, nb),
        in_specs=[pl.BlockSpec(memory_space=pltpu.SMEM),
                  pl.BlockSpec((1, BLOCK, Q_W), qmap),
                  kspec(prev), kspec(cur), kspec(nxt), vspec(prev), vspec(cur), vspec(nxt)],
        out_specs=pl.BlockSpec((1, BLOCK, Q_W), qmap),
        out_shape=jax.ShapeDtypeStruct((bsz, seq, Q_W), BF16), name="win_attn",
        compiler_params=pltpu.CompilerParams(dimension_semantics=("arbitrary", "arbitrary")),
    )(sink, qa, ka, ka, ka, vat, vat, vat)


def _grid_attn_body(q_ref, k_ref, vt_ref, o_ref, s0_ref, s1_ref, p0_ref, p1_ref):
    tq = q_ref.shape[1]
    seq = k_ref.shape[1]
    kc = KEY_CHUNK
    n_chunks = seq // kc
    s_bufs, p_bufs = (s0_ref, s1_ref), (p0_ref, p1_ref)
    heads = [(j, c) for j in range(4) for c in range(2)]
    ones = jnp.ones((2 * ROW_TILE, seq), BF16)

    def masked_q(h):
        j, c = heads[h]
        slab = q_ref[0, :, j * LANES:(j + 1) * LANES]
        return jnp.where(_half_mask(tq, c), slab, jnp.zeros_like(slab))

    def score_chunk(h, qm, kb, m8):
        sc = _dot_nt(k_ref[0, kb * kc:(kb + 1) * kc, :], qm)
        s_bufs[h % 2][kb * kc:(kb + 1) * kc, :] = sc
        cm = jnp.max(sc.reshape(kc // ROW_TILE, ROW_TILE, tq), axis=0)
        return cm if m8 is None else jnp.maximum(m8, cm)

    def prob_chunk(h, kb, m):
        x = s_bufs[h % 2][kb * kc:(kb + 1) * kc, :] - m
        p_bufs[h % 2][kb * kc:(kb + 1) * kc, :] = jnp.exp2(x.astype(BF16))

    qm = masked_q(0)
    m8 = None
    for kb in range(n_chunks):
        m8 = score_chunk(0, qm, kb, m8)
    outs = []
    for h in range(len(heads)):
        m = jnp.max(m8, axis=0, keepdims=True)
        nxt = h + 1 < len(heads)
        if nxt:
            qm = masked_q(h + 1)
            m8 = None
        for kb in range(n_chunks):
            if nxt:
                m8 = score_chunk(h + 1, qm, kb, m8)
            prob_chunk(h, kb, m)
        j, c = heads[h]
        va = jnp.concatenate([vt_ref[0, c * HEAD_DIM:(c + 1) * HEAD_DIM, :], ones], axis=0)
        ot = _dot(va, p_bufs[h % 2][...])
        outs.append(ot[:HEAD_DIM] / ot[HEAD_DIM:HEAD_DIM + 1])
        if c == 1:
            o_ref[0, :, j * LANES:(j + 1) * LANES] = jnp.concatenate(outs, axis=0).T.astype(BF16)
            outs = []


def _grid_attn(qb, kb, vbt):
    bsz, seq, _ = qb.shape
    tq = TQ_GRID
    return pl.pallas_call(
        _grid_attn_body, grid=(bsz, seq // tq),
        in_specs=[pl.BlockSpec((1, tq, Q_W), lambda b, n: (b, n, 0)),
                  pl.BlockSpec((1, seq, KV_W), lambda b, n: (b, 0, 0)),
                  pl.BlockSpec((1, KV_W, seq), lambda b, n: (b, 0, 0))],
        out_specs=pl.BlockSpec((1, tq, Q_W), lambda b, n: (b, n, 0)),
        scratch_shapes=[pltpu.VMEM((seq, tq), F32), pltpu.VMEM((seq, tq), F32),
                        pltpu.VMEM((seq, tq), BF16), pltpu.VMEM((seq, tq), BF16)],
        out_shape=jax.ShapeDtypeStruct((bsz, seq, Q_W), BF16), name="grid_attn",
        compiler_params=pltpu.CompilerParams(
            dimension_semantics=("arbitrary", "arbitrary"), vmem_limit_bytes=40 * 1024 * 1024),
    )(qb, kb, vbt)


def _post_attn_body(x_ref, oa_ref, ob_ref, sga_ref, sgb_ref, g0_ref, b0_ref, wa_ref, wb_ref, wo_ref,
                    g1_ref, b1_ref, wr_ref, br_ref, h1t_ref, tr_ref, tw_ref, cnt_out_ref, cnt_ref):
    h0 = _ln(x_ref[0], g0_ref[...], b0_ref[...])
    out_a = _dot(oa_ref[0], wa_ref[...])
    out_b = _dot(ob_ref[0], wb_ref[...])
    merged = sga_ref[0].astype(F32) * out_a + sgb_ref[0].astype(F32) * out_b
    mix = _dot(merged.astype(BF16), wo_ref[...])
    h1 = _ln(DN_ALPHA * h0 + mix, g1_ref[...], b1_ref[...])
    tm = h1.shape[0]
    _store_row_tiles(h1t_ref.at[0], h1, tm)

    logits = (_dot_nt(wr_ref[...], h1.astype(BF16)) + br_ref[...])[:N_EXPERTS]
    sub = lax.broadcasted_iota(jnp.int32, (N_EXPERTS, tm), 0)
    cur = logits
    vals, idxs = [], []
    for _ in range(TOP_K):
        mv = jnp.max(cur, axis=0, keepdims=True)
        ix = jnp.min(jnp.where(cur == mv, sub, N_EXPERTS), axis=0, keepdims=True)
        vals.append(mv)
        idxs.append(ix)
        cur = jnp.where(sub == ix, -jnp.inf, cur)
    es = [jnp.exp(v - vals[0]) for v in vals]
    tot = es[0] + es[1] + es[2] + es[3]

    @pl.when((pl.program_id(0) == 0) & (pl.program_id(1) == 0))
    def _():
        cnt_ref[...] = jnp.zeros_like(cnt_ref)

    sel = jnp.zeros((N_EXPERTS, tm), F32)
    for kx in range(TOP_K):
        sel = sel + (sub == idxs[kx]).astype(F32)
    r_i = lax.broadcasted_iota(jnp.int32, (tm, tm), 0)
    c_i = lax.broadcasted_iota(jnp.int32, (tm, tm), 1)
    tri = (r_i < c_i).astype(BF16)
    rank = _dot(sel.astype(BF16), tri) + cnt_ref[:, 0:1]
    cnt_ref[...] = cnt_ref[...] + jnp.sum(sel, axis=1, keepdims=True)
    cnt_out_ref[...] = cnt_ref[...]

    rks = [jnp.sum(jnp.where(sub == ix, rank, 0.0), axis=0, keepdims=True).astype(jnp.int32) for ix in idxs]
    tr_ref[...] = jnp.concatenate(idxs + rks, axis=0)
    tw_t = jnp.concatenate([e / tot for e in es] + [jnp.zeros((LANES - TOP_K, tm), F32)], axis=0)
    tw_ref[0] = tw_t.T


def _post_attn(x, oa, ob, sga, sgb, g0, b0, wa, wb, wo, g1, b1, wr, br):
    bsz, seq, d = x.shape
    tm = TM_PROJ
    tok3 = lambda b, i: (b, i, 0)
    const = lambda b, i: (0, 0)
    full = lambda a: pl.BlockSpec(a.shape, const)
    return pl.pallas_call(
        _post_attn_body, grid=(bsz, seq // tm),
        in_specs=[pl.BlockSpec((1, tm, d), tok3),
                  pl.BlockSpec((1, tm, Q_W), tok3), pl.BlockSpec((1, tm, Q_W), tok3),
                  pl.BlockSpec((1, tm, d), tok3), pl.BlockSpec((1, tm, d), tok3),
                  full(g0), full(b0), full(wa), full(wb), full(wo), full(g1), full(b1),
                  full(wr), full(br)],
        out_specs=[pl.BlockSpec((1, tm * ROW_TILE, LANES), tok3),
                   pl.BlockSpec((2 * TOP_K, tm), lambda b, i: (0, b * (seq // tm) + i)),
                   pl.BlockSpec((1, tm, LANES), tok3), pl.BlockSpec((N_EXPERTS, LANES), const)],
        out_shape=[jax.ShapeDtypeStruct((bsz, seq * ROW_TILE, LANES), F32),
                   jax.ShapeDtypeStruct((2 * TOP_K, bsz * seq), jnp.int32),
                   jax.ShapeDtypeStruct((bsz, seq, LANES), F32),
                   jax.ShapeDtypeStruct((N_EXPERTS, LANES), F32)],
        scratch_shapes=[pltpu.VMEM((N_EXPERTS, LANES), F32)],
        name="post_attn",
        compiler_params=pltpu.CompilerParams(
            dimension_semantics=("arbitrary", "arbitrary"), vmem_limit_bytes=48 * 1024 * 1024),
    )(x, oa, ob, sga, sgb, g0, b0, wa, wb, wo, g1, b1, wr, br)


def _dispatch_body(dest_ref, pend_ref, padded_ref, nu_ref, h1t_ref, xs_hbm, zbuf, sem, zsem):
    tm = TM_DISPATCH
    zrows = BM_EXPERT * ROW_TILE
    n_blocks = xs_hbm.shape[0] // zrows

    @pl.when(pl.program_id(0) == 0)
    def _():
        zbuf[...] = jnp.zeros_like(zbuf)
        zero_wait = pltpu.make_async_copy(zbuf, xs_hbm.at[pl.ds(0, zrows)], zsem).wait
        for e in range(N_EXPERTS):
            @pl.when(padded_ref[e] > 0)
            def _():
                start = pl.multiple_of((pend_ref[e] - BM_EXPERT) * ROW_TILE, ROW_TILE)
                pltpu.make_async_copy(zbuf, xs_hbm.at[pl.ds(start, zrows)], zsem).start()
        for b in range(n_blocks - N_EXPERTS, n_blocks):
            @pl.when(b >= nu_ref[0])
            def _():
                pltpu.make_async_copy(zbuf, xs_hbm.at[pl.ds(b * zrows, zrows)], zsem).start()
        for e in range(N_EXPERTS):
            pl.when(padded_ref[e] > 0)(zero_wait)
        for b in range(n_blocks - N_EXPERTS, n_blocks):
            pl.when(b >= nu_ref[0])(zero_wait)

    for t in range(tm):
        for kx in range(TOP_K):
            d = dest_ref[TOP_K * t + kx]
            pltpu.make_async_copy(_tile(h1t_ref, t), _tile(xs_hbm, d), sem).start(priority=kx % 2)
    for _ in range(TOP_K):
        pltpu.make_async_copy(h1t_ref, xs_hbm.at[pl.ds(0, tm * ROW_TILE)], sem).wait()


def _dispatch(dest_flat, pad_end, padded, n_used, h1t, n_rows):
    n_tok = h1t.shape[0] // ROW_TILE
    tm = TM_DISPATCH
    return pl.pallas_call(
        _dispatch_body, grid=(n_tok // tm,),
        in_specs=[pl.BlockSpec((TOP_K * tm,), lambda i: (i,), memory_space=pltpu.SMEM),
                  pl.BlockSpec(memory_space=pltpu.SMEM), pl.BlockSpec(memory_space=pltpu.SMEM),
                  pl.BlockSpec(memory_space=pltpu.SMEM),
                  pl.BlockSpec((tm * ROW_TILE, LANES), lambda i: (i, 0))],
        out_specs=pl.BlockSpec(memory_space=pl.ANY),
        out_shape=jax.ShapeDtypeStruct((n_rows * ROW_TILE, LANES), F32),
        scratch_shapes=[pltpu.VMEM((BM_EXPERT * ROW_TILE, LANES), F32),
                        pltpu.SemaphoreType.DMA(()), pltpu.SemaphoreType.DMA(())],
        name="dispatch",
        compiler_params=pltpu.CompilerParams(dimension_semantics=("arbitrary",)),
    )(dest_flat, pad_end, padded, n_used, h1t)


def _experts_body(be_ref, nu_ref, nx_ref, xs_ref, bg_ref, bu_ref, bd_ref, wg_hbm, wu_hbm, wd_hbm, ys_ref,
                  stage, wg_s, wu_s, wd_s, wsem):
    i = pl.program_id(0)
    bm = BM_EXPERT
    used = i < nu_ref[0]
    prev = be_ref[jnp.maximum(i - 1, 0)]
    fresh = (i == 0) | (be_ref[i] != prev)

    def weight_copies(e):
        return [pltpu.make_async_copy(w_hbm.at[e], stage.at[n], wsem)
                for n, w_hbm in enumerate((wg_hbm, wu_hbm, wd_hbm))]

    @pl.when(i == 0)
    def _():
        for cp in weight_copies(be_ref[0]):
            cp.start()

    @pl.when(used & fresh)
    def _():
        for cp in weight_copies(be_ref[i]):
            cp.wait()
        wg_s[...] = stage[0].astype(BF16)
        wu_s[...] = stage[1].astype(BF16)
        wd_s[...] = stage[2].astype(BF16)

        @pl.when(nx_ref[i] >= 0)
        def _():
            for cp in weight_copies(nx_ref[i]):
                cp.start()

    @pl.when(used)
    def _():
        xb = _load_row_tiles(xs_ref, bm).astype(BF16)
        g = _dot(xb, wg_s[...]) + bg_ref[0]
        u = _dot(xb, wu_s[...]) + bu_ref[0]
        g = jnp.minimum(g, SWIGLU_LIMIT)
        u = jnp.clip(u, -SWIGLU_LIMIT, SWIGLU_LIMIT)
        act = g * jax.nn.sigmoid(SWIGLU_ALPHA * g) * (u + 1.0)
        _store_row_tiles(ys_ref, _dot(act.astype(BF16), wd_s[...]) + bd_ref[0], bm)

    @pl.when(jnp.logical_not(used))
    def _():
        ys_ref[...] = jnp.zeros_like(ys_ref)


def _experts(block_e, n_used, next_e, xs, wg, bg, wu, bu, wd, bd):
    n_rows = xs.shape[0] // ROW_TILE
    bm = BM_EXPERT
    n_e, d, d_ff = wg.shape
    assert d == d_ff and wd.shape == wg.shape
    row = lambda i, be, nu, nx: (jnp.minimum(i, nu[0] - 1), 0)
    exp3 = lambda i, be, nu, nx: (be[jnp.minimum(i, nu[0] - 1)], 0, 0)
    any_spec = pl.BlockSpec(memory_space=pl.ANY)
    grid_spec = pltpu.PrefetchScalarGridSpec(
        num_scalar_prefetch=3, grid=(n_rows // bm,),
        in_specs=[pl.BlockSpec((bm * ROW_TILE, LANES), row),
                  pl.BlockSpec((1, 1, d_ff), exp3), pl.BlockSpec((1, 1, d_ff), exp3),
                  pl.BlockSpec((1, 1, d), exp3), any_spec, any_spec, any_spec],
        out_specs=pl.BlockSpec((bm * ROW_TILE, LANES), lambda i, be, nu, nx: (i, 0)),
        scratch_shapes=[pltpu.VMEM((3, d, d_ff), F32),
                        pltpu.VMEM((d, d_ff), BF16), pltpu.VMEM((d, d_ff), BF16), pltpu.VMEM((d_ff, d), BF16),
                        pltpu.SemaphoreType.DMA(())])
    return pl.pallas_call(
        _experts_body, grid_spec=grid_spec,
        out_shape=jax.ShapeDtypeStruct((n_rows * ROW_TILE, LANES), F32), name="experts",
        compiler_params=pltpu.CompilerParams(
            dimension_semantics=("arbitrary",), vmem_limit_bytes=48 * 1024 * 1024),
    )(block_e, n_used, next_e, xs, bg.reshape(n_e, 1, d_ff), bu.reshape(n_e, 1, d_ff), bd.reshape(n_e, 1, d),
      wg, wu, wd)


def _combine_body(dest_ref, dest_next_ref, h1t_ref, tw_ref, g2_ref, b2_ref, ys_hbm, out_ref, buf, sems):
    tm = TM_COMBINE
    i = pl.program_id(0)
    slot = i % 2

    def start_row(d_ref, s, t):
        for kx in range(TOP_K):
            d = d_ref[TOP_K * t + kx]
            pltpu.make_async_copy(_tile(ys_hbm, d), _tile(buf.at[s, kx], t), sems.at[s]).start(priority=kx % 2)

    def wait_tile(s):
        for kx in range(TOP_K):
            pltpu.make_async_copy(ys_hbm.at[pl.ds(0, tm * ROW_TILE)], buf.at[s, kx], sems.at[s]).wait()

    @pl.when(i == 0)
    def _():
        lax.fori_loop(0, tm, lambda t, c: (start_row(dest_ref, 0, t), c)[1], 0)

    wait_tile(slot)
    rc = COMBINE_ROW_CHUNK
    for c in range(tm // rc):
        for t in range(c * rc, (c + 1) * rc):
            start_row(dest_next_ref, 1 - slot, t)
        tw = tw_ref[c * rc:(c + 1) * rc, :]
        ffn = tw[:, 0:1] * _load_row_tiles(buf.at[slot, 0], rc, c * rc)
        for kx in range(1, TOP_K):
            ffn = ffn + tw[:, kx:kx + 1] * _load_row_tiles(buf.at[slot, kx], rc, c * rc)
        h1 = _load_row_tiles(h1t_ref, rc, c * rc)
        out_ref[c * rc:(c + 1) * rc, :] = _ln(DN_ALPHA * h1 + ffn, g2_ref[...], b2_ref[...])

    @pl.when(i == pl.num_programs(0) - 1)
    def _():
        wait_tile(1 - slot)


def _combine(dest_flat, h1t, tw, g2, b2, ys):
    n_tok = h1t.shape[0] // ROW_TILE
    d = ROW_TILE * LANES
    tm = TM_COMBINE
    n_steps = n_tok // tm
    const = lambda i: (0, 0)
    return pl.pallas_call(
        _combine_body, grid=(n_steps,),
        in_specs=[pl.BlockSpec((TOP_K * tm,), lambda i: (i,), memory_space=pltpu.SMEM),
                  pl.BlockSpec((TOP_K * tm,), lambda i: (jnp.minimum(i + 1, n_steps - 1),),
                               memory_space=pltpu.SMEM),
                  pl.BlockSpec((tm * ROW_TILE, LANES), lambda i: (i, 0)),
                  pl.BlockSpec((tm, LANES), lambda i: (i, 0)),
                  pl.BlockSpec((1, d), const), pl.BlockSpec((1, d), const),
                  pl.BlockSpec(memory_space=pl.ANY)],
        out_specs=pl.BlockSpec((tm, d), lambda i: (i, 0)),
        out_shape=jax.ShapeDtypeStruct((n_tok, d), F32),
        scratch_shapes=[pltpu.VMEM((2, TOP_K, tm * ROW_TILE, LANES), F32), pltpu.SemaphoreType.DMA((2,))],
        name="combine",
        compiler_params=pltpu.CompilerParams(
            dimension_semantics=("arbitrary",), vmem_limit_bytes=40 * 1024 * 1024),
    )(dest_flat, dest_flat, h1t, tw, g2, b2, ys)


def _rope_tables(seq):
    t = np.arange(seq)
    row = (t // GRID_W).astype(np.float32)
    col = (t % GRID_W).astype(np.float32)
    half = HEAD_DIM // 2
    quarter = half // 2
    inv = (ROPE_THETA ** (-np.arange(quarter, dtype=np.float32) * np.float32(2.0 / half))).astype(np.float32)
    ang_r = row[:, None] * inv[None, :]
    ang_c = col[:, None] * inv[None, :]
    zeros = np.zeros_like(ang_r)
    cos = np.concatenate([np.cos(ang_r), np.cos(ang_r), np.cos(ang_c), np.cos(ang_c)], -1)
    s_lo = np.concatenate([-np.sin(ang_r), zeros, -np.sin(ang_c), zeros], -1)
    s_hi = np.concatenate([zeros, np.sin(ang_r), zeros, np.sin(ang_c)], -1)
    return tuple(jnp.asarray(np.tile(a, (1, LANES // HEAD_DIM)), F32) for a in (cos, s_lo, s_hi))


def _routing(top_i, rank, counts, bm):
    n_tok = top_i.shape[0]
    padded = (counts + bm - 1) // bm * bm
    pad_end = jnp.cumsum(padded)
    pad_start = pad_end - padded
    base = jnp.sum(jnp.where(top_i[:, :, None] == jnp.arange(N_EXPERTS, dtype=jnp.int32)[None, None, :],
                             pad_start[None, None, :], 0), axis=-1)
    dest = (base + rank).astype(jnp.int32)
    n_rows = n_tok * TOP_K + N_EXPERTS * bm
    n_blocks = n_rows // bm
    block_start = jnp.arange(n_blocks, dtype=jnp.int32) * bm
    block_e = jnp.minimum(jnp.sum((pad_end[None, :] <= block_start[:, None]).astype(jnp.int32), axis=1),
                          N_EXPERTS - 1).astype(jnp.int32)
    n_used = (pad_end[-1] // bm).astype(jnp.int32).reshape(1)
    e_ids = jnp.arange(N_EXPERTS, dtype=jnp.int32)
    later = (e_ids[None, :] > e_ids[:, None]) & (counts[None, :] > 0)
    next_of = jnp.min(jnp.where(later, e_ids[None, :], N_EXPERTS), axis=1)
    next_of = jnp.where(next_of == N_EXPERTS, -1, next_of).astype(jnp.int32)
    next_e = jnp.sum(jnp.where(block_e[:, None] == e_ids[None, :], next_of[None, :], 0), axis=1).astype(jnp.int32)
    return (dest.reshape(-1), block_e, n_used, next_e, n_rows, pad_end.astype(jnp.int32),
            padded.astype(jnp.int32))


def kernel(x, ln0_g, ln0_b, w_in, a_sink, b_q_norm, b_k_norm, w_branch_a, w_branch_b, w_out,
           ln1_g, ln1_b, w_router, b_router, w_gate, b_gate, w_up, b_up, w_down, b_down,
           ln2_g, ln2_b):
    bsz, seq, d = x.shape
    assert w_in.shape[0] == DEPTH == 1
    assert seq % TM_PROJ == 0 and seq % TQ_GRID == 0 and seq == (seq // GRID_W) * GRID_W
    n_tok = bsz * seq
    row2 = lambda a: a.reshape(1, -1)

    o_ka, o_va, o_qb, o_kb, o_vb, o_g = Q_W, Q_W + KV_W, Q_W + 2 * KV_W, 2 * Q_W + 2 * KV_W, \
        2 * Q_W + 3 * KV_W, 2 * Q_W + 4 * KV_W
    col = np.arange(w_in.shape[-1])
    col_perm = np.concatenate([
        col[:Q_W][_PAIRED], col[o_qb:o_kb][_PAIRED], col[o_ka:o_va], col[o_kb:o_vb],
        col[o_va:o_qb], col[o_vb:o_g], col[o_g:]])
    w_perm = w_in[0].astype(BF16)[:, col_perm]
    head_id = np.arange(Q_W) // HEAD_DIM
    bd = jnp.asarray(head_id[:, None] == head_id[None, :], BF16)
    gq = jnp.tile(b_q_norm[0].astype(F32), N_HEADS).reshape(1, Q_W)
    gk = jnp.tile(b_k_norm[0].astype(F32), N_KV).reshape(1, KV_W)

    qa, ka, vat, qb, kb, vbt, sga, sgb = _in_proj(
        x, row2(ln0_g), row2(ln0_b), w_perm, bd, gq, gk, _rope_tables(seq))
    oa = _win_attn(a_sink[0].astype(F32), qa, ka, vat)
    ob = _grid_attn(qb, kb, vbt)

    wr = jnp.zeros((LANES, d), BF16).at[:N_EXPERTS].set(w_router[0].T.astype(BF16))
    br = jnp.zeros((LANES, 1), F32).at[:N_EXPERTS, 0].set(b_router[0])
    h1t, tr, tw, cnt = _post_attn(
        x, oa, ob, sga, sgb, row2(ln0_g), row2(ln0_b),
        w_branch_a[0][_PAIRED].astype(BF16), w_branch_b[0][_PAIRED].astype(BF16),
        w_out[0].astype(BF16), row2(ln1_g[0]), row2(ln1_b[0]), wr, br)
    h1t = h1t.reshape(n_tok * ROW_TILE, LANES)
    top_i, rank = tr[:TOP_K].T, tr[TOP_K:].T
    tw = tw.reshape(n_tok, LANES)

    counts = cnt[:, 0].astype(jnp.int32)
    dest, block_e, n_used, next_e, n_rows, pad_end, padded = _routing(top_i, rank, counts, BM_EXPERT)
    xs = _dispatch(dest, pad_end, padded, n_used, h1t, n_rows)
    ys = _experts(block_e, n_used, next_e, xs, w_gate[0], b_gate[0], w_up[0], b_up[0], w_down[0], b_down[0])
    out = _combine(dest, h1t, tw, row2(ln2_g[0]), row2(ln2_b[0]), ys)
    return out.reshape(bsz, seq, d)
```

```python
import functools

import jax
import jax.numpy as jnp
import numpy as np
from jax import lax
from jax.experimental import pallas as pl
from jax.experimental.pallas import tpu as pltpu

HEAD_DIM = 64
N_HEADS = 8
N_KV = 2
WINDOW = 128
BLOCK = 128
GRID_W = 64
ROPE_THETA = 10000.0
N_EXPERTS = 32
TOP_K = 4
SWIGLU_LIMIT = 7.0
SWIGLU_ALPHA = 1.702
LN_EPS = 1e-5
RMS_EPS = 1e-6
NEG_INF = -1e30
DEPTH = 1
DN_ALPHA = (2.0 * DEPTH) ** 0.25
ALIBI_SLOPES = tuple(2.0 ** (-8.0 * (h + 1) / N_HEADS) for h in range(N_HEADS))
QK_SCALE = HEAD_DIM ** -0.5
LOG2_E = 1.4426950408889634

LANES = 128
ROW_TILE = 8
Q_W = N_HEADS * HEAD_DIM
KV_W = N_KV * HEAD_DIM

TM_PROJ = 512
PROJ_ROW_GROUPS = 2
WIN_BLOCKS = 8
TQ_GRID = 256
KEY_CHUNK = 256
BM_EXPERT = 512
TM_DISPATCH = 256
TM_COMBINE = 256
COMBINE_ROW_CHUNK = 32

F32 = jnp.float32
BF16 = jnp.bfloat16

_PAIRED = np.array([(j if c == 0 else 4 + j) * HEAD_DIM + d
                    for j in range(4) for c in range(2) for d in range(HEAD_DIM)], np.int32)


def _ln(x, g, b):
    mu = jnp.mean(x, -1, keepdims=True)
    xc = x - mu
    var = jnp.mean(xc * xc, -1, keepdims=True)
    return xc * lax.rsqrt(var + LN_EPS) * g + b


def _dot(a, b):
    return jnp.dot(a, b, preferred_element_type=F32)


def _dot_nt(a, b):
    return lax.dot_general(a, b, (((1,), (1,)), ((), ())), preferred_element_type=F32)


def _load_row_tiles(ref, rows, first=0):
    return jnp.concatenate(
        [ref[pl.ds(first * ROW_TILE + c, rows, stride=ROW_TILE), :] for c in range(ROW_TILE)], axis=1)


def _store_row_tiles(ref, val, rows):
    for c in range(ROW_TILE):
        ref[pl.ds(c, rows, stride=ROW_TILE), :] = val[:, c * LANES:(c + 1) * LANES]


def _tile(ref, row):
    start = row * ROW_TILE
    if not isinstance(row, int):
        start = pl.multiple_of(start, ROW_TILE)
    return ref.at[pl.ds(start, ROW_TILE)]


def _in_proj_body(x_ref, g0_ref, b0_ref, w_ref, bd_ref, gq_ref, gk_ref, c_ref, s1_ref, s2_ref,
                  qa_ref, ka_ref, vat_ref, qb_ref, kb_ref, vbt_ref, sga_ref, sgb_ref):
    tm = x_ref.shape[1]
    d = sga_ref.shape[-1]
    o_k, o_v, o_g = 2 * Q_W, 2 * Q_W + 2 * KV_W, 2 * Q_W + 4 * KV_W
    rg = tm // PROJ_ROW_GROUPS
    for grp in range(PROJ_ROW_GROUPS):
        rows = slice(grp * rg, (grp + 1) * rg)
        hb = _ln(x_ref[0, rows, :], g0_ref[...], b0_ref[...]).astype(BF16)

        def proj(lo, hi):
            return _dot(hb, w_ref[:, lo:hi])

        def norm_rope(t, width, g_ref):
            ss = _dot((t * t).astype(BF16), bd_ref[:width, :width])
            r = lax.rsqrt(ss * (1.0 / HEAD_DIM) + RMS_EPS)
            reps = width // LANES
            tab = lambda ref: jnp.concatenate([ref[rows, :]] * reps, axis=1)
            y = t * g_ref[...]
            rot = (y * tab(c_ref) + pltpu.roll(y, width - 16, 1) * tab(s1_ref)
                   + pltpu.roll(y, 16, 1) * tab(s2_ref))
            return rot * r

        q2 = proj(0, o_k)
        k2 = proj(o_k, o_v)
        qa_ref[0, rows, :] = (q2[:, :Q_W] * (QK_SCALE * LOG2_E)).astype(BF16)
        ka_ref[0, rows, :] = k2[:, :KV_W].astype(BF16)
        qb = norm_rope(q2[:, Q_W:], Q_W, gq_ref)
        qb_ref[0, rows, :] = (qb * (QK_SCALE * LOG2_E)).astype(BF16)
        kb_ref[0, rows, :] = norm_rope(k2[:, KV_W:], KV_W, gk_ref).astype(BF16)
        v2t = proj(o_v, o_g).T
        vat_ref[0, :, rows] = v2t[:KV_W].astype(BF16)
        vbt_ref[0, :, rows] = v2t[KV_W:].astype(BF16)
        sga_ref[0, rows, :] = jax.nn.sigmoid(proj(o_g, o_g + d)).astype(BF16)
        sgb_ref[0, rows, :] = jax.nn.sigmoid(proj(o_g + d, o_g + 2 * d)).astype(BF16)


def _in_proj(x, g0, b0, w, bd, gq, gk, tabs):
    bsz, seq, d = x.shape
    tm = TM_PROJ
    n_in = w.shape[1]
    const = lambda i, j: (0, 0)
    tok3 = lambda i, j: (j, i, 0)
    tab = lambda i, j: (i, 0)
    in_specs = [
        pl.BlockSpec((1, tm, d), tok3),
        pl.BlockSpec((1, d), const), pl.BlockSpec((1, d), const),
        pl.BlockSpec((d, n_in), const),
        pl.BlockSpec((Q_W, Q_W), const),
        pl.BlockSpec((1, Q_W), const), pl.BlockSpec((1, KV_W), const),
        pl.BlockSpec((tm, LANES), tab), pl.BlockSpec((tm, LANES), tab), pl.BlockSpec((tm, LANES), tab),
    ]
    tr3 = lambda i, j: (j, 0, i)
    out_specs = [
        pl.BlockSpec((1, tm, Q_W), tok3), pl.BlockSpec((1, tm, KV_W), tok3),
        pl.BlockSpec((1, KV_W, tm), tr3),
        pl.BlockSpec((1, tm, Q_W), tok3), pl.BlockSpec((1, tm, KV_W), tok3),
        pl.BlockSpec((1, KV_W, tm), tr3),
        pl.BlockSpec((1, tm, d), tok3), pl.BlockSpec((1, tm, d), tok3),
    ]
    out_shape = [
        jax.ShapeDtypeStruct((bsz, seq, Q_W), BF16), jax.ShapeDtypeStruct((bsz, seq, KV_W), BF16),
        jax.ShapeDtypeStruct((bsz, KV_W, seq), BF16),
        jax.ShapeDtypeStruct((bsz, seq, Q_W), BF16), jax.ShapeDtypeStruct((bsz, seq, KV_W), BF16),
        jax.ShapeDtypeStruct((bsz, KV_W, seq), BF16),
        jax.ShapeDtypeStruct((bsz, seq, d), BF16), jax.ShapeDtypeStruct((bsz, seq, d), BF16),
    ]
    return pl.pallas_call(
        _in_proj_body, grid=(seq // tm, bsz), in_specs=in_specs, out_specs=out_specs,
        out_shape=out_shape, name="in_proj",
        compiler_params=pltpu.CompilerParams(
            dimension_semantics=("arbitrary", "arbitrary"), vmem_limit_bytes=48 * 1024 * 1024),
    )(x, g0, b0, w, bd, gq, gk, *tabs)


def _half_mask(rows, c):
    lane = lax.broadcasted_iota(jnp.int32, (rows, LANES), 1)
    return (lane >= HEAD_DIM) if c == 1 else (lane < HEAD_DIM)


def _win_attn_body(sink_ref, q_ref, *refs, seq):
    nk = WIN_BLOCKS + 2
    k_refs, v_refs, o_ref = refs[:nk], refs[nk:2 * nk], refs[2 * nk]
    kk = lax.broadcasted_iota(jnp.int32, (3 * BLOCK, BLOCK), 0)
    qq = lax.broadcasted_iota(jnp.int32, (3 * BLOCK, BLOCK), 1)
    dist_i = jnp.abs(kk - BLOCK - qq)
    dist = dist_i.astype(F32)
    ones = jnp.ones((2 * ROW_TILE, 3 * BLOCK), BF16)
    for blk in range(WIN_BLOCKS):
        n = pl.program_id(1) * WIN_BLOCKS + blk
        rows = slice(blk * BLOCK, (blk + 1) * BLOCK)
        k = jnp.concatenate([r[0] for r in k_refs[blk:blk + 3]], axis=0)
        vt = jnp.concatenate([r[0] for r in v_refs[blk:blk + 3]], axis=1)
        k_pos = n * BLOCK - BLOCK + kk
        valid = (dist_i <= WINDOW) & (k_pos >= 0) & (k_pos < seq)
        slabs = [q_ref[0, rows, j * LANES:(j + 1) * LANES] for j in range(4)]
        qm = jnp.concatenate(
            [jnp.where(_half_mask(BLOCK, c), s, jnp.zeros_like(s)) for c in range(2) for s in slabs], axis=0)
        st_all = _dot_nt(k, qm)
        ots = []
        for c in range(2):
            ps, sinks = [], []
            for j in range(4):
                head = j + 4 * c
                col = (4 * c + j) * BLOCK
                st = st_all[:, col:col + BLOCK] + jnp.where(valid, (-ALIBI_SLOPES[head] * LOG2_E) * dist, NEG_INF)
                sk = sink_ref[head] * LOG2_E
                m = jnp.maximum(jnp.max(st, axis=0, keepdims=True), sk)
                ps.append(jnp.exp2((st - m).astype(BF16)))
                sinks.append(jnp.exp2(sk - m))
            va = jnp.concatenate([vt[c * HEAD_DIM:(c + 1) * HEAD_DIM, :], ones], axis=0)
            ot = _dot(va, jnp.concatenate(ps, axis=1))
            ots.append(ot[:HEAD_DIM] / (ot[HEAD_DIM:HEAD_DIM + 1] + jnp.concatenate(sinks, axis=1)))
        for j in range(4):
            pair = jnp.concatenate([ot[:, j * BLOCK:(j + 1) * BLOCK] for ot in ots], axis=0)
            o_ref[0, rows, j * LANES:(j + 1) * LANES] = pair.T.astype(BF16)


def _win_attn(sink, qa, ka, vat):
    bsz, seq, _ = qa.shape
    nb = seq // BLOCK
    wb = WIN_BLOCKS
    qmap = lambda b, n: (b, n, 0)
    blk = lambda off: (lambda n: jnp.clip(n * wb + off, 0, nb - 1))
    kspec = lambda f: pl.BlockSpec((1, BLOCK, KV_W), lambda b, n: (b, f(n), 0))
    vspec = lambda f: pl.BlockSpec((1, KV_W, BLOCK), lambda b, n: (b, 0, f(n)))
    offs = range(-1, wb + 1)
    return pl.pallas_call(
        functools.partial(_win_attn_body, seq=seq), grid=(bsz, nb // wb),
        in_specs=[pl.BlockSpec(memory_space=pltpu.SMEM), pl.BlockSpec((1, wb * BLOCK, Q_W), qmap)]
        + [kspec(blk(o)) for o in offs] + [vspec(blk(o)) for o in offs],
        out_specs=pl.BlockSpec((1, wb * BLOCK, Q_W), qmap),
        out_shape=jax.ShapeDtypeStruct((bsz, seq, Q_W), BF16), name="win_attn",
        compiler_params=pltpu.CompilerParams(dimension_semantics=("arbitrary", "arbitrary")),
    )(sink, qa, *([ka] * (wb + 2)), *([vat] * (wb + 2)))


def _grid_attn_body(q_ref, k_ref, vt_ref, o_ref, s0_ref, s1_ref, p0_ref, p1_ref):
    tq = q_ref.shape[1]
    seq = k_ref.shape[1]
    kc = KEY_CHUNK
    n_chunks = seq // kc
    s_bufs, p_bufs = (s0_ref, s1_ref), (p0_ref, p1_ref)
    heads = [(j, c) for j in range(4) for c in range(2)]
    ones = jnp.ones((2 * ROW_TILE, seq), BF16)

    def masked_q(h):
        j, c = heads[h]
        slab = q_ref[0, :, j * LANES:(j + 1) * LANES]
        return jnp.where(_half_mask(tq, c), slab, jnp.zeros_like(slab))

    def score_chunk(h, qm, kb, m8):
        sc = _dot_nt(k_ref[0, kb * kc:(kb + 1) * kc, :], qm)
        s_bufs[h % 2][kb * kc:(kb + 1) * kc, :] = sc
        cm = jnp.max(sc.reshape(kc // ROW_TILE, ROW_TILE, tq), axis=0)
        return cm if m8 is None else jnp.maximum(m8, cm)

    def prob_chunk(h, kb, m):
        x = s_bufs[h % 2][kb * kc:(kb + 1) * kc, :] - m
        p_bufs[h % 2][kb * kc:(kb + 1) * kc, :] = jnp.exp2(x.astype(BF16))

    qm = masked_q(0)
    m8 = None
    for kb in range(n_chunks):
        m8 = score_chunk(0, qm, kb, m8)
    outs = []
    for h in range(len(heads)):
        m = jnp.max(m8, axis=0, keepdims=True)
        nxt = h + 1 < len(heads)
        if nxt:
            qm = masked_q(h + 1)
            m8 = None
        for kb in range(n_chunks):
            if nxt:
                m8 = score_chunk(h + 1, qm, kb, m8)
            prob_chunk(h, kb, m)
        j, c = heads[h]
        va = jnp.concatenate([vt_ref[0, c * HEAD_DIM:(c + 1) * HEAD_DIM, :], ones], axis=0)
        ot = _dot(va, p_bufs[h % 2][...])
        outs.append(ot[:HEAD_DIM] / ot[HEAD_DIM:HEAD_DIM + 1])
        if c == 1:
            o_ref[0, :, j * LANES:(j + 1) * LANES] = jnp.concatenate(outs, axis=0).T.astype(BF16)
            outs = []


def _grid_attn(qb, kb, vbt):
    bsz, seq, _ = qb.shape
    tq = TQ_GRID
    return pl.pallas_call(
        _grid_attn_body, grid=(bsz, seq // tq),
        in_specs=[pl.BlockSpec((1, tq, Q_W), lambda b, n: (b, n, 0)),
                  pl.BlockSpec((1, seq, KV_W), lambda b, n: (b, 0, 0)),
                  pl.BlockSpec((1, KV_W, seq), lambda b, n: (b, 0, 0))],
        out_specs=pl.BlockSpec((1, tq, Q_W), lambda b, n: (b, n, 0)),
        scratch_shapes=[pltpu.VMEM((seq, tq), F32), pltpu.VMEM((seq, tq), F32),
                        pltpu.VMEM((seq, tq), BF16), pltpu.VMEM((seq, tq), BF16)],
        out_shape=jax.ShapeDtypeStruct((bsz, seq, Q_W), BF16), name="grid_attn",
        compiler_params=pltpu.CompilerParams(
            dimension_semantics=("arbitrary", "arbitrary"), vmem_limit_bytes=40 * 1024 * 1024),
    )(qb, kb, vbt)


def _post_attn_body(x_ref, oa_ref, ob_ref, sga_ref, sgb_ref, g0_ref, b0_ref, wa_ref, wb_ref, wo_ref,
                    g1_ref, b1_ref, wr_ref, br_ref, h1t_ref, tr_ref, tw_ref, cnt_out_ref, cnt_ref):
    h0 = _ln(x_ref[0], g0_ref[...], b0_ref[...])
    out_a = _dot(oa_ref[0], wa_ref[...])
    out_b = _dot(ob_ref[0], wb_ref[...])
    merged = sga_ref[0].astype(F32) * out_a + sgb_ref[0].astype(F32) * out_b
    mix = _dot(merged.astype(BF16), wo_ref[...])
    h1 = _ln(DN_ALPHA * h0 + mix, g1_ref[...], b1_ref[...])
    tm = h1.shape[0]
    _store_row_tiles(h1t_ref.at[0], h1, tm)

    logits = (_dot_nt(wr_ref[...], h1.astype(BF16)) + br_ref[...])[:N_EXPERTS]
    sub = lax.broadcasted_iota(jnp.int32, (N_EXPERTS, tm), 0)
    cur = logits
    vals, idxs = [], []
    for _ in range(TOP_K):
        mv = jnp.max(cur, axis=0, keepdims=True)
        ix = jnp.min(jnp.where(cur == mv, sub, N_EXPERTS), axis=0, keepdims=True)
        vals.append(mv)
        idxs.append(ix)
        cur = jnp.where(sub == ix, -jnp.inf, cur)
    es = [jnp.exp(v - vals[0]) for v in vals]
    tot = es[0] + es[1] + es[2] + es[3]

    @pl.when((pl.program_id(0) == 0) & (pl.program_id(1) == 0))
    def _():
        cnt_ref[...] = jnp.zeros_like(cnt_ref)

    sel = jnp.zeros((N_EXPERTS, tm), F32)
    for kx in range(TOP_K):
        sel = sel + (sub == idxs[kx]).astype(F32)
    r_i = lax.broadcasted_iota(jnp.int32, (tm, tm), 0)
    c_i = lax.broadcasted_iota(jnp.int32, (tm, tm), 1)
    tri = (r_i < c_i).astype(BF16)
    rank = _dot(sel.astype(BF16), tri) + cnt_ref[:, 0:1]
    cnt_ref[...] = cnt_ref[...] + jnp.sum(sel, axis=1, keepdims=True)
    cnt_out_ref[...] = cnt_ref[...]

    rks = [jnp.sum(jnp.where(sub == ix, rank, 0.0), axis=0, keepdims=True).astype(jnp.int32) for ix in idxs]
    tr_ref[...] = jnp.concatenate(idxs + rks, axis=0)
    tw_t = jnp.concatenate([e / tot for e in es] + [jnp.zeros((LANES - TOP_K, tm), F32)], axis=0)
    tw_ref[0] = tw_t.T


def _post_attn(x, oa, ob, sga, sgb, g0, b0, wa, wb, wo, g1, b1, wr, br):
    bsz, seq, d = x.shape
    tm = TM_PROJ
    tok3 = lambda b, i: (b, i, 0)
    const = lambda b, i: (0, 0)
    full = lambda a: pl.BlockSpec(a.shape, const)
    return pl.pallas_call(
        _post_attn_body, grid=(bsz, seq // tm),
        in_specs=[pl.BlockSpec((1, tm, d), tok3),
                  pl.BlockSpec((1, tm, Q_W), tok3), pl.BlockSpec((1, tm, Q_W), tok3),
                  pl.BlockSpec((1, tm, d), tok3), pl.BlockSpec((1, tm, d), tok3),
                  full(g0), full(b0), full(wa), full(wb), full(wo), full(g1), full(b1),
                  full(wr), full(br)],
        out_specs=[pl.BlockSpec((1, tm * ROW_TILE, LANES), tok3),
                   pl.BlockSpec((2 * TOP_K, tm), lambda b, i: (0, b * (seq // tm) + i)),
                   pl.BlockSpec((1, tm, LANES), tok3), pl.BlockSpec((N_EXPERTS, LANES), const)],
        out_shape=[jax.ShapeDtypeStruct((bsz, seq * ROW_TILE, LANES), F32),
                   jax.ShapeDtypeStruct((2 * TOP_K, bsz * seq), jnp.int32),
                   jax.ShapeDtypeStruct((bsz, seq, LANES), F32),
                   jax.ShapeDtypeStruct((N_EXPERTS, LANES), F32)],
        scratch_shapes=[pltpu.VMEM((N_EXPERTS, LANES), F32)],
        name="post_attn",
        compiler_params=pltpu.CompilerParams(
            dimension_semantics=("arbitrary", "arbitrary"), vmem_limit_bytes=48 * 1024 * 1024),
    )(x, oa, ob, sga, sgb, g0, b0, wa, wb, wo, g1, b1, wr, br)


def _dispatch_body(dest_ref, pend_ref, padded_ref, nu_ref, h1t_ref, xs_hbm, zbuf, sem, zsem):
    tm = TM_DISPATCH
    zrows = BM_EXPERT * ROW_TILE
    n_blocks = xs_hbm.shape[0] // zrows

    @pl.when(pl.program_id(0) == 0)
    def _():
        zbuf[...] = jnp.zeros_like(zbuf)
        zero_wait = pltpu.make_async_copy(zbuf, xs_hbm.at[pl.ds(0, zrows)], zsem).wait
        for e in range(N_EXPERTS):
            @pl.when(padded_ref[e] > 0)
            def _():
                start = pl.multiple_of((pend_ref[e] - BM_EXPERT) * ROW_TILE, ROW_TILE)
                pltpu.make_async_copy(zbuf, xs_hbm.at[pl.ds(start, zrows)], zsem).start()
        for b in range(n_blocks - N_EXPERTS, n_blocks):
            @pl.when(b >= nu_ref[0])
            def _():
                pltpu.make_async_copy(zbuf, xs_hbm.at[pl.ds(b * zrows, zrows)], zsem).start()
        for e in range(N_EXPERTS):
            pl.when(padded_ref[e] > 0)(zero_wait)
        for b in range(n_blocks - N_EXPERTS, n_blocks):
            pl.when(b >= nu_ref[0])(zero_wait)

    for t in range(tm):
        for kx in range(TOP_K):
            d = dest_ref[TOP_K * t + kx]
            pltpu.make_async_copy(_tile(h1t_ref, t), _tile(xs_hbm, d), sem).start(priority=kx % 2)
    for _ in range(TOP_K):
        pltpu.make_async_copy(h1t_ref, xs_hbm.at[pl.ds(0, tm * ROW_TILE)], sem).wait()


def _dispatch(dest_flat, pad_end, padded, n_used, h1t, n_rows):
    n_tok = h1t.shape[0] // ROW_TILE
    tm = TM_DISPATCH
    return pl.pallas_call(
        _dispatch_body, grid=(n_tok // tm,),
        in_specs=[pl.BlockSpec((TOP_K * tm,), lambda i: (i,), memory_space=pltpu.SMEM),
                  pl.BlockSpec(memory_space=pltpu.SMEM), pl.BlockSpec(memory_space=pltpu.SMEM),
                  pl.BlockSpec(memory_space=pltpu.SMEM),
                  pl.BlockSpec((tm * ROW_TILE, LANES), lambda i: (i, 0))],
        out_specs=pl.BlockSpec(memory_space=pl.ANY),
        out_shape=jax.ShapeDtypeStruct((n_rows * ROW_TILE, LANES), F32),
        scratch_shapes=[pltpu.VMEM((BM_EXPERT * ROW_TILE, LANES), F32),
                        pltpu.SemaphoreType.DMA(()), pltpu.SemaphoreType.DMA(())],
        name="dispatch",
        compiler_params=pltpu.CompilerParams(dimension_semantics=("arbitrary",)),
    )(dest_flat, pad_end, padded, n_used, h1t)


def _experts_body(be_ref, nu_ref, nx_ref, xs_ref, bg_ref, bu_ref, bd_ref, wg_hbm, wu_hbm, wd_hbm, ys_ref,
                  stage, wg_s, wu_s, wd_s, wsem):
    i = pl.program_id(0)
    bm = BM_EXPERT
    used = i < nu_ref[0]
    prev = be_ref[jnp.maximum(i - 1, 0)]
    fresh = (i == 0) | (be_ref[i] != prev)

    def weight_copies(e):
        return [pltpu.make_async_copy(w_hbm.at[e], stage.at[n], wsem)
                for n, w_hbm in enumerate((wg_hbm, wu_hbm, wd_hbm))]

    @pl.when(i == 0)
    def _():
        for cp in weight_copies(be_ref[0]):
            cp.start()

    @pl.when(used & fresh)
    def _():
        for cp in weight_copies(be_ref[i]):
            cp.wait()
        wg_s[...] = stage[0].astype(BF16)
        wu_s[...] = stage[1].astype(BF16)
        wd_s[...] = stage[2].astype(BF16)

        @pl.when(nx_ref[i] >= 0)
        def _():
            for cp in weight_copies(nx_ref[i]):
                cp.start()

    @pl.when(used)
    def _():
        xb = _load_row_tiles(xs_ref, bm).astype(BF16)
        g = _dot(xb, wg_s[...]) + bg_ref[0]
        u = _dot(xb, wu_s[...]) + bu_ref[0]
        g = jnp.minimum(g, SWIGLU_LIMIT)
        u = jnp.clip(u, -SWIGLU_LIMIT, SWIGLU_LIMIT)
        act = g * jax.nn.sigmoid(SWIGLU_ALPHA * g) * (u + 1.0)
        _store_row_tiles(ys_ref, _dot(act.astype(BF16), wd_s[...]) + bd_ref[0], bm)

    @pl.when(jnp.logical_not(used))
    def _():
        ys_ref[...] = jnp.zeros_like(ys_ref)


def _experts(block_e, n_used, next_e, xs, wg, bg, wu, bu, wd, bd):
    n_rows = xs.shape[0] // ROW_TILE
    bm = BM_EXPERT
    n_e, d, d_ff = wg.shape
    assert d == d_ff and wd.shape == wg.shape
    row = lambda i, be, nu, nx: (jnp.minimum(i, nu[0] - 1), 0)
    exp3 = lambda i, be, nu, nx: (be[jnp.minimum(i, nu[0] - 1)], 0, 0)
    any_spec = pl.BlockSpec(memory_space=pl.ANY)
    grid_spec = pltpu.PrefetchScalarGridSpec(
        num_scalar_prefetch=3, grid=(n_rows // bm,),
        in_specs=[pl.BlockSpec((bm * ROW_TILE, LANES), row),
                  pl.BlockSpec((1, 1, d_ff), exp3), pl.BlockSpec((1, 1, d_ff), exp3),
                  pl.BlockSpec((1, 1, d), exp3), any_spec, any_spec, any_spec],
        out_specs=pl.BlockSpec((bm * ROW_TILE, LANES), lambda i, be, nu, nx: (i, 0)),
        scratch_shapes=[pltpu.VMEM((3, d, d_ff), F32),
                        pltpu.VMEM((d, d_ff), BF16), pltpu.VMEM((d, d_ff), BF16), pltpu.VMEM((d_ff, d), BF16),
                        pltpu.SemaphoreType.DMA(())])
    return pl.pallas_call(
        _experts_body, grid_spec=grid_spec,
        out_shape=jax.ShapeDtypeStruct((n_rows * ROW_TILE, LANES), F32), name="experts",
        compiler_params=pltpu.CompilerParams(
            dimension_semantics=("arbitrary",), vmem_limit_bytes=48 * 1024 * 1024),
    )(block_e, n_used, next_e, xs, bg.reshape(n_e, 1, d_ff), bu.reshape(n_e, 1, d_ff), bd.reshape(n_e, 1, d),
      wg, wu, wd)


def _combine_body(dest_ref, dest_next_ref, h1t_ref, tw_ref, g2_ref, b2_ref, ys_hbm, out_ref, buf, sems):
    tm = TM_COMBINE
    i = pl.program_id(0)
    slot = i % 2

    def start_row(d_ref, s, t):
        for kx in range(TOP_K):
            d = d_ref[TOP_K * t + kx]
            pltpu.make_async_copy(_tile(ys_hbm, d), _tile(buf.at[s, kx], t), sems.at[s]).start(priority=kx % 2)

    def wait_tile(s):
        for kx in range(TOP_K):
            pltpu.make_async_copy(ys_hbm.at[pl.ds(0, tm * ROW_TILE)], buf.at[s, kx], sems.at[s]).wait()

    @pl.when(i == 0)
    def _():
        lax.fori_loop(0, tm, lambda t, c: (start_row(dest_ref, 0, t), c)[1], 0)

    wait_tile(slot)
    rc = COMBINE_ROW_CHUNK
    for c in range(tm // rc):
        for t in range(c * rc, (c + 1) * rc):
            start_row(dest_next_ref, 1 - slot, t)
        tw = tw_ref[c * rc:(c + 1) * rc, :]
        ffn = tw[:, 0:1] * _load_row_tiles(buf.at[slot, 0], rc, c * rc)
        for kx in range(1, TOP_K):
            ffn = ffn + tw[:, kx:kx + 1] * _load_row_tiles(buf.at[slot, kx], rc, c * rc)
        h1 = _load_row_tiles(h1t_ref, rc, c * rc)
        out_ref[c * rc:(c + 1) * rc, :] = _ln(DN_ALPHA * h1 + ffn, g2_ref[...], b2_ref[...])

    @pl.when(i == pl.num_programs(0) - 1)
    def _():
        wait_tile(1 - slot)


def _combine(dest_flat, h1t, tw, g2, b2, ys):
    n_tok = h1t.shape[0] // ROW_TILE
    d = ROW_TILE * LANES
    tm = TM_COMBINE
    n_steps = n_tok // tm
    const = lambda i: (0, 0)
    return pl.pallas_call(
        _combine_body, grid=(n_steps,),
        in_specs=[pl.BlockSpec((TOP_K * tm,), lambda i: (i,), memory_space=pltpu.SMEM),
                  pl.BlockSpec((TOP_K * tm,), lambda i: (jnp.minimum(i + 1, n_steps - 1),),
                               memory_space=pltpu.SMEM),
                  pl.BlockSpec((tm * ROW_TILE, LANES), lambda i: (i, 0)),
                  pl.BlockSpec((tm, LANES), lambda i: (i, 0)),
                  pl.BlockSpec((1, d), const), pl.BlockSpec((1, d), const),
                  pl.BlockSpec(memory_space=pl.ANY)],
        out_specs=pl.BlockSpec((tm, d), lambda i: (i, 0)),
        out_shape=jax.ShapeDtypeStruct((n_tok, d), F32),
        scratch_shapes=[pltpu.VMEM((2, TOP_K, tm * ROW_TILE, LANES), F32), pltpu.SemaphoreType.DMA((2,))],
        name="combine",
        compiler_params=pltpu.CompilerParams(
            dimension_semantics=("arbitrary",), vmem_limit_bytes=40 * 1024 * 1024),
    )(dest_flat, dest_flat, h1t, tw, g2, b2, ys)


def _rope_tables(seq):
    t = np.arange(seq)
    row = (t // GRID_W).astype(np.float32)
    col = (t % GRID_W).astype(np.float32)
    half = HEAD_DIM // 2
    quarter = half // 2
    inv = (ROPE_THETA ** (-np.arange(quarter, dtype=np.float32) * np.float32(2.0 / half))).astype(np.float32)
    ang_r = row[:, None] * inv[None, :]
    ang_c = col[:, None] * inv[None, :]
    zeros = np.zeros_like(ang_r)
    cos = np.concatenate([np.cos(ang_r), np.cos(ang_r), np.cos(ang_c), np.cos(ang_c)], -1)
    s_lo = np.concatenate([-np.sin(ang_r), zeros, -np.sin(ang_c), zeros], -1)
    s_hi = np.concatenate([zeros, np.sin(ang_r), zeros, np.sin(ang_c)], -1)
    return tuple(jnp.asarray(np.tile(a, (1, LANES // HEAD_DIM)), F32) for a in (cos, s_lo, s_hi))


def _routing(top_i, rank, counts, bm):
    n_tok = top_i.shape[0]
    padded = (counts + bm - 1) // bm * bm
    pad_end = jnp.cumsum(padded)
    pad_start = pad_end - padded
    base = jnp.sum(jnp.where(top_i[:, :, None] == jnp.arange(N_EXPERTS, dtype=jnp.int32)[None, None, :],
                             pad_start[None, None, :], 0), axis=-1)
    dest = (base + rank).astype(jnp.int32)
    n_rows = n_tok * TOP_K + N_EXPERTS * bm
    n_blocks = n_rows // bm
    block_start = jnp.arange(n_blocks, dtype=jnp.int32) * bm
    block_e = jnp.minimum(jnp.sum((pad_end[None, :] <= block_start[:, None]).astype(jnp.int32), axis=1),
                          N_EXPERTS - 1).astype(jnp.int32)
    n_used = (pad_end[-1] // bm).astype(jnp.int32).reshape(1)
    e_ids = jnp.arange(N_EXPERTS, dtype=jnp.int32)
    later = (e_ids[None, :] > e_ids[:, None]) & (counts[None, :] > 0)
    next_of = jnp.min(jnp.where(later, e_ids[None, :], N_EXPERTS), axis=1)
    next_of = jnp.where(next_of == N_EXPERTS, -1, next_of).astype(jnp.int32)
    next_e = jnp.sum(jnp.where(block_e[:, None] == e_ids[None, :], next_of[None, :], 0), axis=1).astype(jnp.int32)
    return (dest.reshape(-1), block_e, n_used, next_e, n_rows, pad_end.astype(jnp.int32),
            padded.astype(jnp.int32))


def kernel(x, ln0_g, ln0_b, w_in, a_sink, b_q_norm, b_k_norm, w_branch_a, w_branch_b, w_out,
           ln1_g, ln1_b, w_router, b_router, w_gate, b_gate, w_up, b_up, w_down, b_down,
           ln2_g, ln2_b):
    bsz, seq, d = x.shape
    assert w_in.shape[0] == DEPTH == 1
    assert seq % TM_PROJ == 0 and seq % TQ_GRID == 0 and seq == (seq // GRID_W) * GRID_W
    n_tok = bsz * seq
    row2 = lambda a: a.reshape(1, -1)

    o_ka, o_va, o_qb, o_kb, o_vb, o_g = Q_W, Q_W + KV_W, Q_W + 2 * KV_W, 2 * Q_W + 2 * KV_W, \
        2 * Q_W + 3 * KV_W, 2 * Q_W + 4 * KV_W
    col = np.arange(w_in.shape[-1])
    col_perm = np.concatenate([
        col[:Q_W][_PAIRED], col[o_qb:o_kb][_PAIRED], col[o_ka:o_va], col[o_kb:o_vb],
        col[o_va:o_qb], col[o_vb:o_g], col[o_g:]])
    w_perm = w_in[0].astype(BF16)[:, col_perm]
    head_id = np.arange(Q_W) // HEAD_DIM
    bd = jnp.asarray(head_id[:, None] == head_id[None, :], BF16)
    gq = jnp.tile(b_q_norm[0].astype(F32), N_HEADS).reshape(1, Q_W)
    gk = jnp.tile(b_k_norm[0].astype(F32), N_KV).reshape(1, KV_W)

    qa, ka, vat, qb, kb, vbt, sga, sgb = _in_proj(
        x, row2(ln0_g), row2(ln0_b), w_perm, bd, gq, gk, _rope_tables(seq))
    oa = _win_attn(a_sink[0].astype(F32), qa, ka, vat)
    ob = _grid_attn(qb, kb, vbt)

    wr = jnp.zeros((LANES, d), BF16).at[:N_EXPERTS].set(w_router[0].T.astype(BF16))
    br = jnp.zeros((LANES, 1), F32).at[:N_EXPERTS, 0].set(b_router[0])
    h1t, tr, tw, cnt = _post_attn(
        x, oa, ob, sga, sgb, row2(ln0_g), row2(ln0_b),
        w_branch_a[0][_PAIRED].astype(BF16), w_branch_b[0][_PAIRED].astype(BF16),
        w_out[0].astype(BF16), row2(ln1_g[0]), row2(ln1_b[0]), wr, br)
    h1t = h1t.reshape(n_tok * ROW_TILE, LANES)
    top_i, rank = tr[:TOP_K].T, tr[TOP_K:].T
    tw = tw.reshape(n_tok, LANES)

    counts = cnt[:, 0].astype(jnp.int32)
    dest, block_e, n_used, next_e, n_rows, pad_end, padded = _routing(top_i, rank, counts, BM_EXPERT)
    xs = _dispatch(dest, pad_end, padded, n_used, h1t, n_rows)
    ys = _experts(block_e, n_used, next_e, xs, w_gate[0], b_gate[0], w_up[0], b_up[0], w_down[0], b_down[0])
    out = _combine(dest, h1t, tw, row2(ln2_g[0]), row2(ln2_b[0]), ys)
    return out.reshape(bsz, seq, d)
```

```python
import functools

import jax
import jax.numpy as jnp
import numpy as np
from jax import lax
from jax.experimental import pallas as pl
from jax.experimental.pallas import tpu as pltpu

HEAD_DIM = 64
N_HEADS = 8
N_KV = 2
WINDOW = 128
BLOCK = 128
GRID_W = 64
ROPE_THETA = 10000.0
N_EXPERTS = 32
TOP_K = 4
SWIGLU_LIMIT = 7.0
SWIGLU_ALPHA = 1.702
LN_EPS = 1e-5
RMS_EPS = 1e-6
NEG_INF = -1e30
DEPTH = 1
DN_ALPHA = (2.0 * DEPTH) ** 0.25
ALIBI_SLOPES = tuple(2.0 ** (-8.0 * (h + 1) / N_HEADS) for h in range(N_HEADS))
QK_SCALE = HEAD_DIM ** -0.5
LOG2_E = 1.4426950408889634

LANES = 128
ROW_TILE = 8
Q_W = N_HEADS * HEAD_DIM
KV_W = N_KV * HEAD_DIM

TM_PROJ = 512
PROJ_ROW_GROUPS = 2
WIN_BLOCKS = 8
TQ_GRID = 256
KEY_CHUNK = 256
BM_EXPERT = 512
TM_DISPATCH = 256
TM_COMBINE = 256
COMBINE_ROW_CHUNK = 32

F32 = jnp.float32
BF16 = jnp.bfloat16

_PAIRED = np.array([(j if c == 0 else 4 + j) * HEAD_DIM + d
                    for j in range(4) for c in range(2) for d in range(HEAD_DIM)], np.int32)


def _ln(x, g, b):
    mu = jnp.mean(x, -1, keepdims=True)
    xc = x - mu
    var = jnp.mean(xc * xc, -1, keepdims=True)
    return xc * lax.rsqrt(var + LN_EPS) * g + b


def _dot(a, b):
    return jnp.dot(a, b, preferred_element_type=F32)


def _dot_nt(a, b):
    return lax.dot_general(a, b, (((1,), (1,)), ((), ())), preferred_element_type=F32)


def _load_row_tiles(ref, rows, first=0):
    return jnp.concatenate(
        [ref[pl.ds(first * ROW_TILE + c, rows, stride=ROW_TILE), :] for c in range(ROW_TILE)], axis=1)


def _store_row_tiles(ref, val, rows):
    for c in range(ROW_TILE):
        ref[pl.ds(c, rows, stride=ROW_TILE), :] = val[:, c * LANES:(c + 1) * LANES]


def _tile(ref, row):
    start = row * ROW_TILE
    if not isinstance(row, int):
        start = pl.multiple_of(start, ROW_TILE)
    return ref.at[pl.ds(start, ROW_TILE)]


def _in_proj_body(x_ref, g0_ref, b0_ref, w_ref, bd_ref, gq_ref, gk_ref, c_ref, s1_ref, s2_ref,
                  qa_ref, ka_ref, vat_ref, qb_ref, kb_ref, vbt_ref, sga_ref, sgb_ref):
    tm = x_ref.shape[1]
    d = sga_ref.shape[-1]
    o_k, o_v, o_g = 2 * Q_W, 2 * Q_W + 2 * KV_W, 2 * Q_W + 4 * KV_W
    rg = tm // PROJ_ROW_GROUPS
    for grp in range(PROJ_ROW_GROUPS):
        rows = slice(grp * rg, (grp + 1) * rg)
        hb = _ln(x_ref[0, rows, :], g0_ref[...], b0_ref[...]).astype(BF16)

        def proj(lo, hi):
            return _dot(hb, w_ref[:, lo:hi])

        def norm_rope(t, width, g_ref):
            ss = _dot((t * t).astype(BF16), bd_ref[:width, :width])
            r = lax.rsqrt(ss * (1.0 / HEAD_DIM) + RMS_EPS)
            reps = width // LANES
            tab = lambda ref: jnp.concatenate([ref[rows, :]] * reps, axis=1)
            y = t * g_ref[...]
            rot = (y * tab(c_ref) + pltpu.roll(y, width - 16, 1) * tab(s1_ref)
                   + pltpu.roll(y, 16, 1) * tab(s2_ref))
            return rot * r

        q2 = proj(0, o_k)
        k2 = proj(o_k, o_v)
        qa_ref[0, rows, :] = (q2[:, :Q_W] * (QK_SCALE * LOG2_E)).astype(BF16)
        ka_ref[0, rows, :] = k2[:, :KV_W].astype(BF16)
        qb = norm_rope(q2[:, Q_W:], Q_W, gq_ref)
        qb_ref[0, rows, :] = (qb * (QK_SCALE * LOG2_E)).astype(BF16)
        kb_ref[0, rows, :] = norm_rope(k2[:, KV_W:], KV_W, gk_ref).astype(BF16)
        v2t = proj(o_v, o_g).T
        vat_ref[0, :, rows] = v2t[:KV_W].astype(BF16)
        vbt_ref[0, :, rows] = v2t[KV_W:].astype(BF16)
        sga_ref[0, rows, :] = jax.nn.sigmoid(proj(o_g, o_g + d)).astype(BF16)
        sgb_ref[0, rows, :] = jax.nn.sigmoid(proj(o_g + d, o_g + 2 * d)).astype(BF16)


def _in_proj(x, g0, b0, w, bd, gq, gk, tabs):
    bsz, seq, d = x.shape
    tm = TM_PROJ
    n_in = w.shape[1]
    const = lambda i, j: (0, 0)
    tok3 = lambda i, j: (j, i, 0)
    tab = lambda i, j: (i, 0)
    in_specs = [
        pl.BlockSpec((1, tm, d), tok3),
        pl.BlockSpec((1, d), const), pl.BlockSpec((1, d), const),
        pl.BlockSpec((d, n_in), const),
        pl.BlockSpec((Q_W, Q_W), const),
        pl.BlockSpec((1, Q_W), const), pl.BlockSpec((1, KV_W), const),
        pl.BlockSpec((tm, LANES), tab), pl.BlockSpec((tm, LANES), tab), pl.BlockSpec((tm, LANES), tab),
    ]
    tr3 = lambda i, j: (j, 0, i)
    out_specs = [
        pl.BlockSpec((1, tm, Q_W), tok3), pl.BlockSpec((1, tm, KV_W), tok3),
        pl.BlockSpec((1, KV_W, tm), tr3),
        pl.BlockSpec((1, tm, Q_W), tok3), pl.BlockSpec((1, tm, KV_W), tok3),
        pl.BlockSpec((1, KV_W, tm), tr3),
        pl.BlockSpec((1, tm, d), tok3), pl.BlockSpec((1, tm, d), tok3),
    ]
    out_shape = [
        jax.ShapeDtypeStruct((bsz, seq, Q_W), BF16), jax.ShapeDtypeStruct((bsz, seq, KV_W), BF16),
        jax.ShapeDtypeStruct((bsz, KV_W, seq), BF16),
        jax.ShapeDtypeStruct((bsz, seq, Q_W), BF16), jax.ShapeDtypeStruct((bsz, seq, KV_W), BF16),
        jax.ShapeDtypeStruct((bsz, KV_W, seq), BF16),
        jax.ShapeDtypeStruct((bsz, seq, d), BF16), jax.ShapeDtypeStruct((bsz, seq, d), BF16),
    ]
    return pl.pallas_call(
        _in_proj_body, grid=(seq // tm, bsz), in_specs=in_specs, out_specs=out_specs,
        out_shape=out_shape, name="in_proj",
        compiler_params=pltpu.CompilerParams(
            dimension_semantics=("arbitrary", "arbitrary"), vmem_limit_bytes=48 * 1024 * 1024),
    )(x, g0, b0, w, bd, gq, gk, *tabs)


def _half_mask(rows, c):
    lane = lax.broadcasted_iota(jnp.int32, (rows, LANES), 1)
    return (lane >= HEAD_DIM) if c == 1 else (lane < HEAD_DIM)


def _win_attn_body(sink_ref, q_ref, *refs, seq):
    nk = WIN_BLOCKS + 2
    k_refs, v_refs, o_ref = refs[:nk], refs[nk:2 * nk], refs[2 * nk]
    kk = lax.broadcasted_iota(jnp.int32, (3 * BLOCK, BLOCK), 0)
    qq = lax.broadcasted_iota(jnp.int32, (3 * BLOCK, BLOCK), 1)
    dist_i = jnp.abs(kk - BLOCK - qq)
    dist = dist_i.astype(F32)
    ones = jnp.ones((2 * ROW_TILE, 3 * BLOCK), BF16)
    for blk in range(WIN_BLOCKS):
        n = pl.program_id(1) * WIN_BLOCKS + blk
        rows = slice(blk * BLOCK, (blk + 1) * BLOCK)
        k = jnp.concatenate([r[0] for r in k_refs[blk:blk + 3]], axis=0)
        vt = jnp.concatenate([r[0] for r in v_refs[blk:blk + 3]], axis=1)
        k_pos = n * BLOCK - BLOCK + kk
        valid = (dist_i <= WINDOW) & (k_pos >= 0) & (k_pos < seq)
        slabs = [q_ref[0, rows, j * LANES:(j + 1) * LANES] for j in range(4)]
        qm = jnp.concatenate(
            [jnp.where(_half_mask(BLOCK, c), s, jnp.zeros_like(s)) for c in range(2) for s in slabs], axis=0)
        st_all = _dot_nt(k, qm)
        ots = []
        for c in range(2):
            ps, sinks = [], []
            for j in range(4):
                head = j + 4 * c
                col = (4 * c + j) * BLOCK
                st = st_all[:, col:col + BLOCK] + jnp.where(valid, (-ALIBI_SLOPES[head] * LOG2_E) * dist, NEG_INF)
                sk = sink_ref[head] * LOG2_E
                m = jnp.maximum(jnp.max(st, axis=0, keepdims=True), sk)
                ps.append(jnp.exp2((st - m).astype(BF16)))
                sinks.append(jnp.exp2(sk - m))
            va = jnp.concatenate([vt[c * HEAD_DIM:(c + 1) * HEAD_DIM, :], ones], axis=0)
            ot = _dot(va, jnp.concatenate(ps, axis=1))
            ots.append(ot[:HEAD_DIM] / (ot[HEAD_DIM:HEAD_DIM + 1] + jnp.concatenate(sinks, axis=1)))
        for j in range(4):
            pair = jnp.concatenate([ot[:, j * BLOCK:(j + 1) * BLOCK] for ot in ots], axis=0)
            o_ref[0, rows, j * LANES:(j + 1) * LANES] = pair.T.astype(BF16)


def _win_attn(sink, qa, ka, vat):
    bsz, seq, _ = qa.shape
    nb = seq // BLOCK
    wb = WIN_BLOCKS
    qmap = lambda b, n: (b, n, 0)
    blk = lambda off: (lambda n: jnp.clip(n * wb + off, 0, nb - 1))
    kspec = lambda f: pl.BlockSpec((1, BLOCK, KV_W), lambda b, n: (b, f(n), 0))
    vspec = lambda f: pl.BlockSpec((1, KV_W, BLOCK), lambda b, n: (b, 0, f(n)))
    offs = range(-1, wb + 1)
    return pl.pallas_call(
        functools.partial(_win_attn_body, seq=seq), grid=(bsz, nb // wb),
        in_specs=[pl.BlockSpec(memory_space=pltpu.SMEM), pl.BlockSpec((1, wb * BLOCK, Q_W), qmap)]
        + [kspec(blk(o)) for o in offs] + [vspec(blk(o)) for o in offs],
        out_specs=pl.BlockSpec((1, wb * BLOCK, Q_W), qmap),
        out_shape=jax.ShapeDtypeStruct((bsz, seq, Q_W), BF16), name="win_attn",
        compiler_params=pltpu.CompilerParams(dimension_semantics=("arbitrary", "arbitrary")),
    )(sink, qa, *([ka] * (wb + 2)), *([vat] * (wb + 2)))


def _grid_attn_body(q_ref, k_ref, vt_ref, o_ref, s0_ref, s1_ref, p0_ref, p1_ref):
    tq = q_ref.shape[1]
    seq = k_ref.shape[1]
    kc = KEY_CHUNK
    n_chunks = seq // kc
    s_bufs, p_bufs = (s0_ref, s1_ref), (p0_ref, p1_ref)
    heads = [(j, c) for j in range(4) for c in range(2)]
    ones = jnp.ones((2 * ROW_TILE, seq), BF16)

    def masked_q(h):
        j, c = heads[h]
        slab = q_ref[0, :, j * LANES:(j + 1) * LANES]
        return jnp.where(_half_mask(tq, c), slab, jnp.zeros_like(slab))

    def score_chunk(h, qm, kb, m8):
        sc = _dot_nt(k_ref[0, kb * kc:(kb + 1) * kc, :], qm)
        s_bufs[h % 2][kb * kc:(kb + 1) * kc, :] = sc
        cm = jnp.max(sc.reshape(kc // ROW_TILE, ROW_TILE, tq), axis=0)
        return cm if m8 is None else jnp.maximum(m8, cm)

    def prob_chunk(h, kb, m):
        x = s_bufs[h % 2][kb * kc:(kb + 1) * kc, :] - m
        p_bufs[h % 2][kb * kc:(kb + 1) * kc, :] = jnp.exp2(x.astype(BF16))

    qm = masked_q(0)
    m8 = None
    for kb in range(n_chunks):
        m8 = score_chunk(0, qm, kb, m8)
    outs = []
    for h in range(len(heads)):
        m = jnp.max(m8, axis=0, keepdims=True)
        nxt = h + 1 < len(heads)
        if nxt:
            qm = masked_q(h + 1)
            m8 = None
        for kb in range(n_chunks):
            if nxt:
                m8 = score_chunk(h + 1, qm, kb, m8)
            prob_chunk(h, kb, m)
        j, c = heads[h]
        va = jnp.concatenate([vt_ref[0, c * HEAD_DIM:(c + 1) * HEAD_DIM, :], ones], axis=0)
        ot = _dot(va, p_bufs[h % 2][...])
        outs.append(ot[:HEAD_DIM] / ot[HEAD_DIM:HEAD_DIM + 1])
        if c == 1:
            o_ref[0, :, j * LANES:(j + 1) * LANES] = jnp.concatenate(outs, axis=0).T.astype(BF16)
            outs = []


def _grid_attn(qb, kb, vbt):
    bsz, seq, _ = qb.shape
    tq = TQ_GRID
    return pl.pallas_call(
        _grid_attn_body, grid=(bsz, seq // tq),
        in_specs=[pl.BlockSpec((1, tq, Q_W), lambda b, n: (b, n, 0)),
                  pl.BlockSpec((1, seq, KV_W), lambda b, n: (b, 0, 0)),
                  pl.BlockSpec((1, KV_W, seq), lambda b, n: (b, 0, 0))],
        out_specs=pl.BlockSpec((1, tq, Q_W), lambda b, n: (b, n, 0)),
        scratch_shapes=[pltpu.VMEM((seq, tq), F32), pltpu.VMEM((seq, tq), F32),
                        pltpu.VMEM((seq, tq), BF16), pltpu.VMEM((seq, tq), BF16)],
        out_shape=jax.ShapeDtypeStruct((bsz, seq, Q_W), BF16), name="grid_attn",
        compiler_params=pltpu.CompilerParams(
            dimension_semantics=("arbitrary", "arbitrary"), vmem_limit_bytes=40 * 1024 * 1024),
    )(qb, kb, vbt)


def _post_attn_body(x_ref, oa_ref, ob_ref, sga_ref, sgb_ref, g0_ref, b0_ref, wa_ref, wb_ref, wo_ref,
                    g1_ref, b1_ref, wr_ref, br_ref, h1t_ref, tr_ref, tw_ref, cnt_out_ref, cnt_ref):
    h0 = _ln(x_ref[0], g0_ref[...], b0_ref[...])
    out_a = _dot(oa_ref[0], wa_ref[...])
    out_b = _dot(ob_ref[0], wb_ref[...])
    merged = sga_ref[0].astype(F32) * out_a + sgb_ref[0].astype(F32) * out_b
    mix = _dot(merged.astype(BF16), wo_ref[...])
    h1 = _ln(DN_ALPHA * h0 + mix, g1_ref[...], b1_ref[...])
    tm = h1.shape[0]
    _store_row_tiles(h1t_ref.at[0], h1, tm)

    logits = (_dot_nt(wr_ref[...], h1.astype(BF16)) + br_ref[...])[:N_EXPERTS]
    sub = lax.broadcasted_iota(jnp.int32, (N_EXPERTS, tm), 0)
    cur = logits
    vals, idxs = [], []
    for _ in range(TOP_K):
        mv = jnp.max(cur, axis=0, keepdims=True)
        ix = jnp.min(jnp.where(cur == mv, sub, N_EXPERTS), axis=0, keepdims=True)
        vals.append(mv)
        idxs.append(ix)
        cur = jnp.where(sub == ix, -jnp.inf, cur)
    es = [jnp.exp(v - vals[0]) for v in vals]
    tot = es[0] + es[1] + es[2] + es[3]

    @pl.when((pl.program_id(0) == 0) & (pl.program_id(1) == 0))
    def _():
        cnt_ref[...] = jnp.zeros_like(cnt_ref)

    sel = jnp.zeros((N_EXPERTS, tm), F32)
    for kx in range(TOP_K):
        sel = sel + (sub == idxs[kx]).astype(F32)
    r_i = lax.broadcasted_iota(jnp.int32, (tm, tm), 0)
    c_i = lax.broadcasted_iota(jnp.int32, (tm, tm), 1)
    tri = (r_i < c_i).astype(BF16)
    rank = _dot(sel.astype(BF16), tri) + cnt_ref[:, 0:1]
    cnt_ref[...] = cnt_ref[...] + jnp.sum(sel, axis=1, keepdims=True)
    cnt_out_ref[...] = cnt_ref[...]

    rks = [jnp.sum(jnp.where(sub == ix, rank, 0.0), axis=0, keepdims=True).astype(jnp.int32) for ix in idxs]
    tr_ref[...] = jnp.concatenate(idxs + rks, axis=0)
    tw_t = jnp.concatenate([e / tot for e in es] + [jnp.zeros((LANES - TOP_K, tm), F32)], axis=0)
    tw_ref[0] = tw_t.T


def _post_attn(x, oa, ob, sga, sgb, g0, b0, wa, wb, wo, g1, b1, wr, br):
    bsz, seq, d = x.shape
    tm = TM_PROJ
    tok3 = lambda b, i: (b, i, 0)
    const = lambda b, i: (0, 0)
    full = lambda a: pl.BlockSpec(a.shape, const)
    return pl.pallas_call(
        _post_attn_body, grid=(bsz, seq // tm),
        in_specs=[pl.BlockSpec((1, tm, d), tok3),
                  pl.BlockSpec((1, tm, Q_W), tok3), pl.BlockSpec((1, tm, Q_W), tok3),
                  pl.BlockSpec((1, tm, d), tok3), pl.BlockSpec((1, tm, d), tok3),
                  full(g0), full(b0), full(wa), full(wb), full(wo), full(g1), full(b1),
                  full(wr), full(br)],
        out_specs=[pl.BlockSpec((1, tm * ROW_TILE, LANES), tok3),
                   pl.BlockSpec((2 * TOP_K, tm), lambda b, i: (0, b * (seq // tm) + i)),
                   pl.BlockSpec((1, tm, LANES), tok3), pl.BlockSpec((N_EXPERTS, LANES), const)],
        out_shape=[jax.ShapeDtypeStruct((bsz, seq * ROW_TILE, LANES), F32),
                   jax.ShapeDtypeStruct((2 * TOP_K, bsz * seq), jnp.int32),
                   jax.ShapeDtypeStruct((bsz, seq, LANES), F32),
                   jax.ShapeDtypeStruct((N_EXPERTS, LANES), F32)],
        scratch_shapes=[pltpu.VMEM((N_EXPERTS, LANES), F32)],
        name="post_attn",
        compiler_params=pltpu.CompilerParams(
            dimension_semantics=("arbitrary", "arbitrary"), vmem_limit_bytes=48 * 1024 * 1024),
    )(x, oa, ob, sga, sgb, g0, b0, wa, wb, wo, g1, b1, wr, br)


def _dispatch_body(dest_ref, pend_ref, padded_ref, nu_ref, h1t_ref, h1t_hbm, xs_hbm, zbuf, sem, zsem):
    tm = TM_DISPATCH
    zrows = BM_EXPERT * ROW_TILE
    n_blocks = xs_hbm.shape[0] // zrows

    @pl.when(pl.program_id(0) == 0)
    def _():
        zbuf[...] = jnp.zeros_like(zbuf)
        zero_wait = pltpu.make_async_copy(zbuf, xs_hbm.at[pl.ds(0, zrows)], zsem).wait
        for e in range(N_EXPERTS):
            @pl.when(padded_ref[e] > 0)
            def _():
                start = pl.multiple_of((pend_ref[e] - BM_EXPERT) * ROW_TILE, ROW_TILE)
                pltpu.make_async_copy(zbuf, xs_hbm.at[pl.ds(start, zrows)], zsem).start()
        for b in range(n_blocks - N_EXPERTS, n_blocks):
            @pl.when(b >= nu_ref[0])
            def _():
                pltpu.make_async_copy(zbuf, xs_hbm.at[pl.ds(b * zrows, zrows)], zsem).start()
        for e in range(N_EXPERTS):
            pl.when(padded_ref[e] > 0)(zero_wait)
        for b in range(n_blocks - N_EXPERTS, n_blocks):
            pl.when(b >= nu_ref[0])(zero_wait)

    base = pl.program_id(0) * tm
    for t in range(tm):
        for kx in range(TOP_K):
            d = dest_ref[TOP_K * t + kx]
            if kx == TOP_K - 1:
                pltpu.make_async_copy(_tile(h1t_hbm, base + t), _tile(xs_hbm, d), sem).start()
            else:
                pltpu.make_async_copy(_tile(h1t_ref, t), _tile(xs_hbm, d), sem).start(priority=kx % 2)
    for _ in range(TOP_K):
        pltpu.make_async_copy(h1t_ref, xs_hbm.at[pl.ds(0, tm * ROW_TILE)], sem).wait()


def _dispatch(dest_flat, pad_end, padded, n_used, h1t, n_rows):
    n_tok = h1t.shape[0] // ROW_TILE
    tm = TM_DISPATCH
    return pl.pallas_call(
        _dispatch_body, grid=(n_tok // tm,),
        in_specs=[pl.BlockSpec((TOP_K * tm,), lambda i: (i,), memory_space=pltpu.SMEM),
                  pl.BlockSpec(memory_space=pltpu.SMEM), pl.BlockSpec(memory_space=pltpu.SMEM),
                  pl.BlockSpec(memory_space=pltpu.SMEM),
                  pl.BlockSpec((tm * ROW_TILE, LANES), lambda i: (i, 0)),
                  pl.BlockSpec(memory_space=pl.ANY)],
        out_specs=pl.BlockSpec(memory_space=pl.ANY),
        out_shape=jax.ShapeDtypeStruct((n_rows * ROW_TILE, LANES), F32),
        scratch_shapes=[pltpu.VMEM((BM_EXPERT * ROW_TILE, LANES), F32),
                        pltpu.SemaphoreType.DMA(()), pltpu.SemaphoreType.DMA(())],
        name="dispatch",
        compiler_params=pltpu.CompilerParams(dimension_semantics=("arbitrary",)),
    )(dest_flat, pad_end, padded, n_used, h1t, h1t)


def _experts_body(be_ref, nu_ref, nx_ref, xs_ref, bg_ref, bu_ref, bd_ref, wg_hbm, wu_hbm, wd_hbm, ys_ref,
                  stage, wg_s, wu_s, wd_s, wsem):
    i = pl.program_id(0)
    bm = BM_EXPERT
    used = i < nu_ref[0]
    prev = be_ref[jnp.maximum(i - 1, 0)]
    fresh = (i == 0) | (be_ref[i] != prev)

    def weight_copies(e):
        return [pltpu.make_async_copy(w_hbm.at[e], stage.at[n], wsem)
                for n, w_hbm in enumerate((wg_hbm, wu_hbm, wd_hbm))]

    @pl.when(i == 0)
    def _():
        for cp in weight_copies(be_ref[0]):
            cp.start()

    @pl.when(used & fresh)
    def _():
        for cp in weight_copies(be_ref[i]):
            cp.wait()
        wg_s[...] = stage[0].astype(BF16)
        wu_s[...] = stage[1].astype(BF16)
        wd_s[...] = stage[2].astype(BF16)

        @pl.when(nx_ref[i] >= 0)
        def _():
            for cp in weight_copies(nx_ref[i]):
                cp.start()

    @pl.when(used)
    def _():
        xb = _load_row_tiles(xs_ref, bm).astype(BF16)
        g = _dot(xb, wg_s[...]) + bg_ref[0]
        u = _dot(xb, wu_s[...]) + bu_ref[0]
        g = jnp.minimum(g, SWIGLU_LIMIT)
        u = jnp.clip(u, -SWIGLU_LIMIT, SWIGLU_LIMIT)
        act = g * jax.nn.sigmoid(SWIGLU_ALPHA * g) * (u + 1.0)
        _store_row_tiles(ys_ref, _dot(act.astype(BF16), wd_s[...]) + bd_ref[0], bm)

    @pl.when(jnp.logical_not(used))
    def _():
        ys_ref[...] = jnp.zeros_like(ys_ref)


def _experts(block_e, n_used, next_e, xs, wg, bg, wu, bu, wd, bd):
    n_rows = xs.shape[0] // ROW_TILE
    bm = BM_EXPERT
    n_e, d, d_ff = wg.shape
    assert d == d_ff and wd.shape == wg.shape
    row = lambda i, be, nu, nx: (jnp.minimum(i, nu[0] - 1), 0)
    exp3 = lambda i, be, nu, nx: (be[jnp.minimum(i, nu[0] - 1)], 0, 0)
    any_spec = pl.BlockSpec(memory_space=pl.ANY)
    grid_spec = pltpu.PrefetchScalarGridSpec(
        num_scalar_prefetch=3, grid=(n_rows // bm,),
        in_specs=[pl.BlockSpec((bm * ROW_TILE, LANES), row),
                  pl.BlockSpec((1, 1, d_ff), exp3), pl.BlockSpec((1, 1, d_ff), exp3),
                  pl.BlockSpec((1, 1, d), exp3), any_spec, any_spec, any_spec],
        out_specs=pl.BlockSpec((bm * ROW_TILE, LANES), lambda i, be, nu, nx: (i, 0)),
        scratch_shapes=[pltpu.VMEM((3, d, d_ff), F32),
                        pltpu.VMEM((d, d_ff), BF16), pltpu.VMEM((d, d_ff), BF16), pltpu.VMEM((d_ff, d), BF16),
                        pltpu.SemaphoreType.DMA(())])
    return pl.pallas_call(
        _experts_body, grid_spec=grid_spec,
        out_shape=jax.ShapeDtypeStruct((n_rows * ROW_TILE, LANES), F32), name="experts",
        compiler_params=pltpu.CompilerParams(
            dimension_semantics=("arbitrary",), vmem_limit_bytes=48 * 1024 * 1024),
    )(block_e, n_used, next_e, xs, bg.reshape(n_e, 1, d_ff), bu.reshape(n_e, 1, d_ff), bd.reshape(n_e, 1, d),
      wg, wu, wd)


def _combine_body(dest_ref, dest_next_ref, h1t_ref, tw_ref, g2_ref, b2_ref, ys_hbm, out_ref, buf, sems):
    tm = TM_COMBINE
    i = pl.program_id(0)
    slot = i % 2

    def start_row(d_ref, s, t):
        for kx in range(TOP_K):
            d = d_ref[TOP_K * t + kx]
            pltpu.make_async_copy(_tile(ys_hbm, d), _tile(buf.at[s, kx], t), sems.at[s]).start(priority=kx % 2)

    def wait_tile(s):
        for kx in range(TOP_K):
            pltpu.make_async_copy(ys_hbm.at[pl.ds(0, tm * ROW_TILE)], buf.at[s, kx], sems.at[s]).wait()

    @pl.when(i == 0)
    def _():
        lax.fori_loop(0, tm, lambda t, c: (start_row(dest_ref, 0, t), c)[1], 0)

    wait_tile(slot)
    rc = COMBINE_ROW_CHUNK
    for c in range(tm // rc):
        for t in range(c * rc, (c + 1) * rc):
            start_row(dest_next_ref, 1 - slot, t)
        tw = tw_ref[c * rc:(c + 1) * rc, :]
        ffn = tw[:, 0:1] * _load_row_tiles(buf.at[slot, 0], rc, c * rc)
        for kx in range(1, TOP_K):
            ffn = ffn + tw[:, kx:kx + 1] * _load_row_tiles(buf.at[slot, kx], rc, c * rc)
        h1 = _load_row_tiles(h1t_ref, rc, c * rc)
        out_ref[c * rc:(c + 1) * rc, :] = _ln(DN_ALPHA * h1 + ffn, g2_ref[...], b2_ref[...])

    @pl.when(i == pl.num_programs(0) - 1)
    def _():
        wait_tile(1 - slot)


def _combine(dest_flat, h1t, tw, g2, b2, ys):
    n_tok = h1t.shape[0] // ROW_TILE
    d = ROW_TILE * LANES
    tm = TM_COMBINE
    n_steps = n_tok // tm
    const = lambda i: (0, 0)
    return pl.pallas_call(
        _combine_body, grid=(n_steps,),
        in_specs=[pl.BlockSpec((TOP_K * tm,), lambda i: (i,), memory_space=pltpu.SMEM),
                  pl.BlockSpec((TOP_K * tm,), lambda i: (jnp.minimum(i + 1, n_steps - 1),),
                               memory_space=pltpu.SMEM),
                  pl.BlockSpec((tm * ROW_TILE, LANES), lambda i: (i, 0)),
                  pl.BlockSpec((tm, LANES), lambda i: (i, 0)),
                  pl.BlockSpec((1, d), const), pl.BlockSpec((1, d), const),
                  pl.BlockSpec(memory_space=pl.ANY)],
        out_specs=pl.BlockSpec((tm, d), lambda i: (i, 0)),
        out_shape=jax.ShapeDtypeStruct((n_tok, d), F32),
        scratch_shapes=[pltpu.VMEM((2, TOP_K, tm * ROW_TILE, LANES), F32), pltpu.SemaphoreType.DMA((2,))],
        name="combine",
        compiler_params=pltpu.CompilerParams(
            dimension_semantics=("arbitrary",), vmem_limit_bytes=40 * 1024 * 1024),
    )(dest_flat, dest_flat, h1t, tw, g2, b2, ys)


def _rope_tables(seq):
    t = np.arange(seq)
    row = (t // GRID_W).astype(np.float32)
    col = (t % GRID_W).astype(np.float32)
    half = HEAD_DIM // 2
    quarter = half // 2
    inv = (ROPE_THETA ** (-np.arange(quarter, dtype=np.float32) * np.float32(2.0 / half))).astype(np.float32)
    ang_r = row[:, None] * inv[None, :]
    ang_c = col[:, None] * inv[None, :]
    zeros = np.zeros_like(ang_r)
    cos = np.concatenate([np.cos(ang_r), np.cos(ang_r), np.cos(ang_c), np.cos(ang_c)], -1)
    s_lo = np.concatenate([-np.sin(ang_r), zeros, -np.sin(ang_c), zeros], -1)
    s_hi = np.concatenate([zeros, np.sin(ang_r), zeros, np.sin(ang_c)], -1)
    return tuple(jnp.asarray(np.tile(a, (1, LANES // HEAD_DIM)), F32) for a in (cos, s_lo, s_hi))


def _routing(top_i, rank, counts, bm):
    n_tok = top_i.shape[0]
    padded = (counts + bm - 1) // bm * bm
    pad_end = jnp.cumsum(padded)
    pad_start = pad_end - padded
    base = jnp.sum(jnp.where(top_i[:, :, None] == jnp.arange(N_EXPERTS, dtype=jnp.int32)[None, None, :],
                             pad_start[None, None, :], 0), axis=-1)
    dest = (base + rank).astype(jnp.int32)
    n_rows = n_tok * TOP_K + N_EXPERTS * bm
    n_blocks = n_rows // bm
    block_start = jnp.arange(n_blocks, dtype=jnp.int32) * bm
    block_e = jnp.minimum(jnp.sum((pad_end[None, :] <= block_start[:, None]).astype(jnp.int32), axis=1),
                          N_EXPERTS - 1).astype(jnp.int32)
    n_used = (pad_end[-1] // bm).astype(jnp.int32).reshape(1)
    e_ids = jnp.arange(N_EXPERTS, dtype=jnp.int32)
    later = (e_ids[None, :] > e_ids[:, None]) & (counts[None, :] > 0)
    next_of = jnp.min(jnp.where(later, e_ids[None, :], N_EXPERTS), axis=1)
    next_of = jnp.where(next_of == N_EXPERTS, -1, next_of).astype(jnp.int32)
    next_e = jnp.sum(jnp.where(block_e[:, None] == e_ids[None, :], next_of[None, :], 0), axis=1).astype(jnp.int32)
    return (dest.reshape(-1), block_e, n_used, next_e, n_rows, pad_end.astype(jnp.int32),
            padded.astype(jnp.int32))


def kernel(x, ln0_g, ln0_b, w_in, a_sink, b_q_norm, b_k_norm, w_branch_a, w_branch_b, w_out,
           ln1_g, ln1_b, w_router, b_router, w_gate, b_gate, w_up, b_up, w_down, b_down,
           ln2_g, ln2_b):
    bsz, seq, d = x.shape
    assert w_in.shape[0] == DEPTH == 1
    assert seq % TM_PROJ == 0 and seq % TQ_GRID == 0 and seq == (seq // GRID_W) * GRID_W
    n_tok = bsz * seq
    row2 = lambda a: a.reshape(1, -1)

    o_ka, o_va, o_qb, o_kb, o_vb, o_g = Q_W, Q_W + KV_W, Q_W + 2 * KV_W, 2 * Q_W + 2 * KV_W, \
        2 * Q_W + 3 * KV_W, 2 * Q_W + 4 * KV_W
    col = np.arange(w_in.shape[-1])
    col_perm = np.concatenate([
        col[:Q_W][_PAIRED], col[o_qb:o_kb][_PAIRED], col[o_ka:o_va], col[o_kb:o_vb],
        col[o_va:o_qb], col[o_vb:o_g], col[o_g:]])
    w_perm = w_in[0].astype(BF16)[:, col_perm]
    head_id = np.arange(Q_W) // HEAD_DIM
    bd = jnp.asarray(head_id[:, None] == head_id[None, :], BF16)
    gq = jnp.tile(b_q_norm[0].astype(F32), N_HEADS).reshape(1, Q_W)
    gk = jnp.tile(b_k_norm[0].astype(F32), N_KV).reshape(1, KV_W)

    qa, ka, vat, qb, kb, vbt, sga, sgb = _in_proj(
        x, row2(ln0_g), row2(ln0_b), w_perm, bd, gq, gk, _rope_tables(seq))
    oa = _win_attn(a_sink[0].astype(F32), qa, ka, vat)
    ob = _grid_attn(qb, kb, vbt)

    wr = jnp.zeros((LANES, d), BF16).at[:N_EXPERTS].set(w_router[0].T.astype(BF16))
    br = jnp.zeros((LANES, 1), F32).at[:N_EXPERTS, 0].set(b_router[0])
    h1t, tr, tw, cnt = _post_attn(
        x, oa, ob, sga, sgb, row2(ln0_g), row2(ln0_b),
        w_branch_a[0][_PAIRED].astype(BF16), w_branch_b[0][_PAIRED].astype(BF16),
        w_out[0].astype(BF16), row2(ln1_g[0]), row2(ln1_b[0]), wr, br)
    h1t = h1t.reshape(n_tok * ROW_TILE, LANES)
    top_i, rank = tr[:TOP_K].T, tr[TOP_K:].T
    tw = tw.reshape(n_tok, LANES)

    counts = cnt[:, 0].astype(jnp.int32)
    dest, block_e, n_used, next_e, n_rows, pad_end, padded = _routing(top_i, rank, counts, BM_EXPERT)
    xs = _dispatch(dest, pad_end, padded, n_used, h1t, n_rows)
    ys = _experts(block_e, n_used, next_e, xs, w_gate[0], b_gate[0], w_up[0], b_up[0], w_down[0], b_down[0])
    out = _combine(dest, h1t, tw, row2(ln2_g[0]), row2(ln2_b[0]), ys)
    return out.reshape(bsz, seq, d)
```

```python
import functools

import jax
import jax.numpy as jnp
import numpy as np
from jax import lax
from jax.experimental import pallas as pl
from jax.experimental.pallas import tpu as pltpu

HEAD_DIM = 64
N_HEADS = 8
N_KV = 2
WINDOW = 128
BLOCK = 128
GRID_W = 64
ROPE_THETA = 10000.0
N_EXPERTS = 32
TOP_K = 4
SWIGLU_LIMIT = 7.0
SWIGLU_ALPHA = 1.702
LN_EPS = 1e-5
RMS_EPS = 1e-6
NEG_INF = -1e30
DEPTH = 1
DN_ALPHA = (2.0 * DEPTH) ** 0.25
ALIBI_SLOPES = tuple(2.0 ** (-8.0 * (h + 1) / N_HEADS) for h in range(N_HEADS))
QK_SCALE = HEAD_DIM ** -0.5
LOG2_E = 1.4426950408889634

LANES = 128
ROW_TILE = 8
Q_W = N_HEADS * HEAD_DIM
KV_W = N_KV * HEAD_DIM

TM_PROJ = 512
PROJ_ROW_GROUPS = 2
WIN_BLOCKS = 8
TQ_GRID = 256
KEY_CHUNK = 256
BM_EXPERT = 512
TM_DISPATCH = 256
_RUN_PIECES = tuple(1 << s for s in range(TM_DISPATCH.bit_length() - 1, -1, -1))
TM_COMBINE = 256
COMBINE_ROW_CHUNK = 32

F32 = jnp.float32
BF16 = jnp.bfloat16

_PAIRED = np.array([(j if c == 0 else 4 + j) * HEAD_DIM + d
                    for j in range(4) for c in range(2) for d in range(HEAD_DIM)], np.int32)


def _ln(x, g, b):
    mu = jnp.mean(x, -1, keepdims=True)
    xc = x - mu
    var = jnp.mean(xc * xc, -1, keepdims=True)
    return xc * lax.rsqrt(var + LN_EPS) * g + b


def _dot(a, b):
    return jnp.dot(a, b, preferred_element_type=F32)


def _dot_nt(a, b):
    return lax.dot_general(a, b, (((1,), (1,)), ((), ())), preferred_element_type=F32)


def _load_row_tiles(ref, rows, first=0):
    return jnp.concatenate(
        [ref[pl.ds(first * ROW_TILE + c, rows, stride=ROW_TILE), :] for c in range(ROW_TILE)], axis=1)


def _store_row_tiles(ref, val, rows):
    for c in range(ROW_TILE):
        ref[pl.ds(c, rows, stride=ROW_TILE), :] = val[:, c * LANES:(c + 1) * LANES]


def _tile(ref, row):
    start = row * ROW_TILE
    if not isinstance(row, int):
        start = pl.multiple_of(start, ROW_TILE)
    return ref.at[pl.ds(start, ROW_TILE)]


def _in_proj_body(x_ref, g0_ref, b0_ref, w_ref, bd_ref, gq_ref, gk_ref, c_ref, s1_ref, s2_ref,
                  qa_ref, ka_ref, vat_ref, qb_ref, kb_ref, vbt_ref, sga_ref, sgb_ref):
    tm = x_ref.shape[1]
    d = sga_ref.shape[-1]
    o_k, o_v, o_g = 2 * Q_W, 2 * Q_W + 2 * KV_W, 2 * Q_W + 4 * KV_W
    rg = tm // PROJ_ROW_GROUPS
    for grp in range(PROJ_ROW_GROUPS):
        rows = slice(grp * rg, (grp + 1) * rg)
        hb = _ln(x_ref[0, rows, :], g0_ref[...], b0_ref[...]).astype(BF16)

        def proj(lo, hi):
            return _dot(hb, w_ref[:, lo:hi])

        def norm_rope(t, width, g_ref):
            ss = _dot((t * t).astype(BF16), bd_ref[:width, :width])
            r = lax.rsqrt(ss * (1.0 / HEAD_DIM) + RMS_EPS)
            reps = width // LANES
            tab = lambda ref: jnp.concatenate([ref[rows, :]] * reps, axis=1)
            y = t * g_ref[...]
            rot = (y * tab(c_ref) + pltpu.roll(y, width - 16, 1) * tab(s1_ref)
                   + pltpu.roll(y, 16, 1) * tab(s2_ref))
            return rot * r

        q2 = proj(0, o_k)
        k2 = proj(o_k, o_v)
        qa_ref[0, rows, :] = (q2[:, :Q_W] * (QK_SCALE * LOG2_E)).astype(BF16)
        ka_ref[0, rows, :] = k2[:, :KV_W].astype(BF16)
        qb = norm_rope(q2[:, Q_W:], Q_W, gq_ref)
        qb_ref[0, rows, :] = (qb * (QK_SCALE * LOG2_E)).astype(BF16)
        kb_ref[0, rows, :] = norm_rope(k2[:, KV_W:], KV_W, gk_ref).astype(BF16)
        v2t = proj(o_v, o_g).T
        vat_ref[0, :, rows] = v2t[:KV_W].astype(BF16)
        vbt_ref[0, :, rows] = v2t[KV_W:].astype(BF16)
        sga_ref[0, rows, :] = jax.nn.sigmoid(proj(o_g, o_g + d)).astype(BF16)
        sgb_ref[0, rows, :] = jax.nn.sigmoid(proj(o_g + d, o_g + 2 * d)).astype(BF16)


def _in_proj(x, g0, b0, w, bd, gq, gk, tabs):
    bsz, seq, d = x.shape
    tm = TM_PROJ
    n_in = w.shape[1]
    const = lambda i, j: (0, 0)
    tok3 = lambda i, j: (j, i, 0)
    tab = lambda i, j: (i, 0)
    in_specs = [
        pl.BlockSpec((1, tm, d), tok3),
        pl.BlockSpec((1, d), const), pl.BlockSpec((1, d), const),
        pl.BlockSpec((d, n_in), const),
        pl.BlockSpec((Q_W, Q_W), const),
        pl.BlockSpec((1, Q_W), const), pl.BlockSpec((1, KV_W), const),
        pl.BlockSpec((tm, LANES), tab), pl.BlockSpec((tm, LANES), tab), pl.BlockSpec((tm, LANES), tab),
    ]
    tr3 = lambda i, j: (j, 0, i)
    out_specs = [
        pl.BlockSpec((1, tm, Q_W), tok3), pl.BlockSpec((1, tm, KV_W), tok3),
        pl.BlockSpec((1, KV_W, tm), tr3),
        pl.BlockSpec((1, tm, Q_W), tok3), pl.BlockSpec((1, tm, KV_W), tok3),
        pl.BlockSpec((1, KV_W, tm), tr3),
        pl.BlockSpec((1, tm, d), tok3), pl.BlockSpec((1, tm, d), tok3),
    ]
    out_shape = [
        jax.ShapeDtypeStruct((bsz, seq, Q_W), BF16), jax.ShapeDtypeStruct((bsz, seq, KV_W), BF16),
        jax.ShapeDtypeStruct((bsz, KV_W, seq), BF16),
        jax.ShapeDtypeStruct((bsz, seq, Q_W), BF16), jax.ShapeDtypeStruct((bsz, seq, KV_W), BF16),
        jax.ShapeDtypeStruct((bsz, KV_W, seq), BF16),
        jax.ShapeDtypeStruct((bsz, seq, d), BF16), jax.ShapeDtypeStruct((bsz, seq, d), BF16),
    ]
    return pl.pallas_call(
        _in_proj_body, grid=(seq // tm, bsz), in_specs=in_specs, out_specs=out_specs,
        out_shape=out_shape, name="in_proj",
        compiler_params=pltpu.CompilerParams(
            dimension_semantics=("arbitrary", "arbitrary"), vmem_limit_bytes=48 * 1024 * 1024),
    )(x, g0, b0, w, bd, gq, gk, *tabs)


def _half_mask(rows, c):
    lane = lax.broadcasted_iota(jnp.int32, (rows, LANES), 1)
    return (lane >= HEAD_DIM) if c == 1 else (lane < HEAD_DIM)


def _win_attn_body(sink_ref, q_ref, *refs, seq):
    nk = WIN_BLOCKS + 2
    k_refs, v_refs, o_ref = refs[:nk], refs[nk:2 * nk], refs[2 * nk]
    kk = lax.broadcasted_iota(jnp.int32, (3 * BLOCK, BLOCK), 0)
    qq = lax.broadcasted_iota(jnp.int32, (3 * BLOCK, BLOCK), 1)
    dist_i = jnp.abs(kk - BLOCK - qq)
    dist = dist_i.astype(F32)
    ones = jnp.ones((2 * ROW_TILE, 3 * BLOCK), BF16)
    for blk in range(WIN_BLOCKS):
        n = pl.program_id(1) * WIN_BLOCKS + blk
        rows = slice(blk * BLOCK, (blk + 1) * BLOCK)
        k = jnp.concatenate([r[0] for r in k_refs[blk:blk + 3]], axis=0)
        vt = jnp.concatenate([r[0] for r in v_refs[blk:blk + 3]], axis=1)
        k_pos = n * BLOCK - BLOCK + kk
        valid = (dist_i <= WINDOW) & (k_pos >= 0) & (k_pos < seq)
        slabs = [q_ref[0, rows, j * LANES:(j + 1) * LANES] for j in range(4)]
        qm = jnp.concatenate(
            [jnp.where(_half_mask(BLOCK, c), s, jnp.zeros_like(s)) for c in range(2) for s in slabs], axis=0)
        st_all = _dot_nt(k, qm)
        ots = []
        for c in range(2):
            ps, sinks = [], []
            for j in range(4):
                head = j + 4 * c
                col = (4 * c + j) * BLOCK
                st = st_all[:, col:col + BLOCK] + jnp.where(valid, (-ALIBI_SLOPES[head] * LOG2_E) * dist, NEG_INF)
                sk = sink_ref[head] * LOG2_E
                m = jnp.maximum(jnp.max(st, axis=0, keepdims=True), sk)
                ps.append(jnp.exp2((st - m).astype(BF16)))
                sinks.append(jnp.exp2(sk - m))
            va = jnp.concatenate([vt[c * HEAD_DIM:(c + 1) * HEAD_DIM, :], ones], axis=0)
            ot = _dot(va, jnp.concatenate(ps, axis=1))
            ots.append(ot[:HEAD_DIM] / (ot[HEAD_DIM:HEAD_DIM + 1] + jnp.concatenate(sinks, axis=1)))
        for j in range(4):
            pair = jnp.concatenate([ot[:, j * BLOCK:(j + 1) * BLOCK] for ot in ots], axis=0)
            o_ref[0, rows, j * LANES:(j + 1) * LANES] = pair.T.astype(BF16)


def _win_attn(sink, qa, ka, vat):
    bsz, seq, _ = qa.shape
    nb = seq // BLOCK
    wb = WIN_BLOCKS
    qmap = lambda b, n: (b, n, 0)
    blk = lambda off: (lambda n: jnp.clip(n * wb + off, 0, nb - 1))
    kspec = lambda f: pl.BlockSpec((1, BLOCK, KV_W), lambda b, n: (b, f(n), 0))
    vspec = lambda f: pl.BlockSpec((1, KV_W, BLOCK), lambda b, n: (b, 0, f(n)))
    offs = range(-1, wb + 1)
    return pl.pallas_call(
        functools.partial(_win_attn_body, seq=seq), grid=(bsz, nb // wb),
        in_specs=[pl.BlockSpec(memory_space=pltpu.SMEM), pl.BlockSpec((1, wb * BLOCK, Q_W), qmap)]
        + [kspec(blk(o)) for o in offs] + [vspec(blk(o)) for o in offs],
        out_specs=pl.BlockSpec((1, wb * BLOCK, Q_W), qmap),
        out_shape=jax.ShapeDtypeStruct((bsz, seq, Q_W), BF16), name="win_attn",
        compiler_params=pltpu.CompilerParams(dimension_semantics=("arbitrary", "arbitrary")),
    )(sink, qa, *([ka] * (wb + 2)), *([vat] * (wb + 2)))


def _grid_attn_body(q_ref, k_ref, vt_ref, o_ref, s0_ref, s1_ref, p0_ref, p1_ref):
    tq = q_ref.shape[1]
    seq = k_ref.shape[1]
    kc = KEY_CHUNK
    n_chunks = seq // kc
    s_bufs, p_bufs = (s0_ref, s1_ref), (p0_ref, p1_ref)
    heads = [(j, c) for j in range(4) for c in range(2)]
    ones = jnp.ones((2 * ROW_TILE, seq), BF16)

    def masked_q(h):
        j, c = heads[h]
        slab = q_ref[0, :, j * LANES:(j + 1) * LANES]
        return jnp.where(_half_mask(tq, c), slab, jnp.zeros_like(slab))

    def score_chunk(h, qm, kb, m8):
        sc = _dot_nt(k_ref[0, kb * kc:(kb + 1) * kc, :], qm)
        s_bufs[h % 2][kb * kc:(kb + 1) * kc, :] = sc
        cm = jnp.max(sc.reshape(kc // ROW_TILE, ROW_TILE, tq), axis=0)
        return cm if m8 is None else jnp.maximum(m8, cm)

    def prob_chunk(h, kb, m):
        x = s_bufs[h % 2][kb * kc:(kb + 1) * kc, :] - m
        p_bufs[h % 2][kb * kc:(kb + 1) * kc, :] = jnp.exp2(x.astype(BF16))

    qm = masked_q(0)
    m8 = None
    for kb in range(n_chunks):
        m8 = score_chunk(0, qm, kb, m8)
    outs = []
    for h in range(len(heads)):
        m = jnp.max(m8, axis=0, keepdims=True)
        nxt = h + 1 < len(heads)
        if nxt:
            qm = masked_q(h + 1)
            m8 = None
        for kb in range(n_chunks):
            if nxt:
                m8 = score_chunk(h + 1, qm, kb, m8)
            prob_chunk(h, kb, m)
        j, c = heads[h]
        va = jnp.concatenate([vt_ref[0, c * HEAD_DIM:(c + 1) * HEAD_DIM, :], ones], axis=0)
        ot = _dot(va, p_bufs[h % 2][...])
        outs.append(ot[:HEAD_DIM] / ot[HEAD_DIM:HEAD_DIM + 1])
        if c == 1:
            o_ref[0, :, j * LANES:(j + 1) * LANES] = jnp.concatenate(outs, axis=0).T.astype(BF16)
            outs = []


def _grid_attn(qb, kb, vbt):
    bsz, seq, _ = qb.shape
    tq = TQ_GRID
    return pl.pallas_call(
        _grid_attn_body, grid=(bsz, seq // tq),
        in_specs=[pl.BlockSpec((1, tq, Q_W), lambda b, n: (b, n, 0)),
                  pl.BlockSpec((1, seq, KV_W), lambda b, n: (b, 0, 0)),
                  pl.BlockSpec((1, KV_W, seq), lambda b, n: (b, 0, 0))],
        out_specs=pl.BlockSpec((1, tq, Q_W), lambda b, n: (b, n, 0)),
        scratch_shapes=[pltpu.VMEM((seq, tq), F32), pltpu.VMEM((seq, tq), F32),
                        pltpu.VMEM((seq, tq), BF16), pltpu.VMEM((seq, tq), BF16)],
        out_shape=jax.ShapeDtypeStruct((bsz, seq, Q_W), BF16), name="grid_attn",
        compiler_params=pltpu.CompilerParams(
            dimension_semantics=("arbitrary", "arbitrary"), vmem_limit_bytes=40 * 1024 * 1024),
    )(qb, kb, vbt)


def _post_attn_body(x_ref, oa_ref, ob_ref, sga_ref, sgb_ref, g0_ref, b0_ref, wa_ref, wb_ref, wo_ref,
                    g1_ref, b1_ref, wr_ref, br_ref, h1t_ref, tr_ref, tw_ref, cnt_out_ref, cnt_ref):
    h0 = _ln(x_ref[0], g0_ref[...], b0_ref[...])
    out_a = _dot(oa_ref[0], wa_ref[...])
    out_b = _dot(ob_ref[0], wb_ref[...])
    merged = sga_ref[0].astype(F32) * out_a + sgb_ref[0].astype(F32) * out_b
    mix = _dot(merged.astype(BF16), wo_ref[...])
    h1 = _ln(DN_ALPHA * h0 + mix, g1_ref[...], b1_ref[...])
    tm = h1.shape[0]
    _store_row_tiles(h1t_ref.at[0], h1, tm)

    logits = (_dot_nt(wr_ref[...], h1.astype(BF16)) + br_ref[...])[:N_EXPERTS]
    sub = lax.broadcasted_iota(jnp.int32, (N_EXPERTS, tm), 0)
    cur = logits
    vals, idxs = [], []
    for _ in range(TOP_K):
        mv = jnp.max(cur, axis=0, keepdims=True)
        ix = jnp.min(jnp.where(cur == mv, sub, N_EXPERTS), axis=0, keepdims=True)
        vals.append(mv)
        idxs.append(ix)
        cur = jnp.where(sub == ix, -jnp.inf, cur)
    es = [jnp.exp(v - vals[0]) for v in vals]
    tot = es[0] + es[1] + es[2] + es[3]

    @pl.when((pl.program_id(0) == 0) & (pl.program_id(1) == 0))
    def _():
        cnt_ref[...] = jnp.zeros_like(cnt_ref)

    sel = jnp.zeros((N_EXPERTS, tm), F32)
    for kx in range(TOP_K):
        sel = sel + (sub == idxs[kx]).astype(F32)
    r_i = lax.broadcasted_iota(jnp.int32, (tm, tm), 0)
    c_i = lax.broadcasted_iota(jnp.int32, (tm, tm), 1)
    tri = (r_i < c_i).astype(BF16)
    rank = _dot(sel.astype(BF16), tri) + cnt_ref[:, 0:1]
    cnt_ref[...] = cnt_ref[...] + jnp.sum(sel, axis=1, keepdims=True)
    cnt_out_ref[...] = cnt_ref[...]

    rks = [jnp.sum(jnp.where(sub == ix, rank, 0.0), axis=0, keepdims=True).astype(jnp.int32) for ix in idxs]
    tr_ref[...] = jnp.concatenate(idxs + rks, axis=0)
    tw_t = jnp.concatenate([e / tot for e in es] + [jnp.zeros((LANES - TOP_K, tm), F32)], axis=0)
    tw_ref[0] = tw_t.T


def _post_attn(x, oa, ob, sga, sgb, g0, b0, wa, wb, wo, g1, b1, wr, br):
    bsz, seq, d = x.shape
    tm = TM_PROJ
    tok3 = lambda b, i: (b, i, 0)
    const = lambda b, i: (0, 0)
    full = lambda a: pl.BlockSpec(a.shape, const)
    return pl.pallas_call(
        _post_attn_body, grid=(bsz, seq // tm),
        in_specs=[pl.BlockSpec((1, tm, d), tok3),
                  pl.BlockSpec((1, tm, Q_W), tok3), pl.BlockSpec((1, tm, Q_W), tok3),
                  pl.BlockSpec((1, tm, d), tok3), pl.BlockSpec((1, tm, d), tok3),
                  full(g0), full(b0), full(wa), full(wb), full(wo), full(g1), full(b1),
                  full(wr), full(br)],
        out_specs=[pl.BlockSpec((1, tm * ROW_TILE, LANES), tok3),
                   pl.BlockSpec((2 * TOP_K, tm), lambda b, i: (0, b * (seq // tm) + i)),
                   pl.BlockSpec((1, tm, LANES), tok3), pl.BlockSpec((N_EXPERTS, LANES), const)],
        out_shape=[jax.ShapeDtypeStruct((bsz, seq * ROW_TILE, LANES), F32),
                   jax.ShapeDtypeStruct((2 * TOP_K, bsz * seq), jnp.int32),
                   jax.ShapeDtypeStruct((bsz, seq, LANES), F32),
                   jax.ShapeDtypeStruct((N_EXPERTS, LANES), F32)],
        scratch_shapes=[pltpu.VMEM((N_EXPERTS, LANES), F32)],
        name="post_attn",
        compiler_params=pltpu.CompilerParams(
            dimension_semantics=("arbitrary", "arbitrary"), vmem_limit_bytes=48 * 1024 * 1024),
    )(x, oa, ob, sga, sgb, g0, b0, wa, wb, wo, g1, b1, wr, br)


def _dispatch_body(loc_ref, rs_ref, rl_ref, so_ref, pend_ref, padded_ref, nu_ref, h1t_ref, xs_hbm,
                   stage, zbuf, sems, zsem):
    tm = TM_DISPATCH
    zrows = BM_EXPERT * ROW_TILE
    n_blocks = xs_hbm.shape[0] // zrows
    i = pl.program_id(0)
    slot = i % 2

    @pl.when(pl.program_id(0) == 0)
    def _():
        zbuf[...] = jnp.zeros_like(zbuf)
        zero_wait = pltpu.make_async_copy(zbuf, xs_hbm.at[pl.ds(0, zrows)], zsem).wait
        for e in range(N_EXPERTS):
            @pl.when(padded_ref[e] > 0)
            def _():
                start = pl.multiple_of((pend_ref[e] - BM_EXPERT) * ROW_TILE, ROW_TILE)
                pltpu.make_async_copy(zbuf, xs_hbm.at[pl.ds(start, zrows)], zsem).start()
        for b in range(n_blocks - N_EXPERTS, n_blocks):
            @pl.when(b >= nu_ref[0])
            def _():
                pltpu.make_async_copy(zbuf, xs_hbm.at[pl.ds(b * zrows, zrows)], zsem).start()
        for e in range(N_EXPERTS):
            pl.when(padded_ref[e] > 0)(zero_wait)
        for b in range(n_blocks - N_EXPERTS, n_blocks):
            pl.when(b >= nu_ref[0])(zero_wait)

    def tile_wait(s):
        for _ in range(TOP_K):
            pltpu.make_async_copy(h1t_ref, xs_hbm.at[pl.ds(0, tm * ROW_TILE)], sems.at[s]).wait()

    @pl.when(i >= 2)
    def _():
        tile_wait(slot)

    stg = stage.at[slot]

    def place(g, carry):
        for u in range(8):
            t = g * 8 + u
            row = h1t_ref[pl.ds(pl.multiple_of(t * ROW_TILE, ROW_TILE), ROW_TILE), :]
            for kx in range(TOP_K):
                at = loc_ref[TOP_K * t + kx]
                stg[pl.ds(pl.multiple_of(at * ROW_TILE, ROW_TILE), ROW_TILE), :] = row
        return carry

    lax.fori_loop(0, tm // 8, place, 0)

    def send(e, carry):
        n, src, dst = rl_ref[e], so_ref[e], rs_ref[e]
        for pi, b in enumerate(_RUN_PIECES):
            @pl.when((n & b) != 0)
            def _():
                done = n & (-2 * b)
                pltpu.make_async_copy(
                    stg.at[pl.ds(pl.multiple_of((src + done) * ROW_TILE, ROW_TILE), b * ROW_TILE)],
                    xs_hbm.at[pl.ds(pl.multiple_of((dst + done) * ROW_TILE, ROW_TILE), b * ROW_TILE)],
                    sems.at[slot]).start(priority=pi % 2)
        return carry

    lax.fori_loop(0, N_EXPERTS, send, 0)

    @pl.when(i == pl.num_programs(0) - 1)
    def _():
        tile_wait(slot)

        @pl.when(i >= 1)
        def _():
            tile_wait(1 - slot)


def _dispatch(loc_flat, run_start, run_len, stage_off, pad_end, padded, n_used, h1t, n_rows):
    n_tok = h1t.shape[0] // ROW_TILE
    tm = TM_DISPATCH
    per_tile = lambda n: pl.BlockSpec((n,), lambda i: (i,), memory_space=pltpu.SMEM)
    return pl.pallas_call(
        _dispatch_body, grid=(n_tok // tm,),
        in_specs=[per_tile(TOP_K * tm), per_tile(LANES), per_tile(LANES), per_tile(LANES),
                  pl.BlockSpec(memory_space=pltpu.SMEM), pl.BlockSpec(memory_space=pltpu.SMEM),
                  pl.BlockSpec(memory_space=pltpu.SMEM),
                  pl.BlockSpec((tm * ROW_TILE, LANES), lambda i: (i, 0))],
        out_specs=pl.BlockSpec(memory_space=pl.ANY),
        out_shape=jax.ShapeDtypeStruct((n_rows * ROW_TILE, LANES), F32),
        scratch_shapes=[pltpu.VMEM((2, TOP_K * tm * ROW_TILE, LANES), F32),
                        pltpu.VMEM((BM_EXPERT * ROW_TILE, LANES), F32),
                        pltpu.SemaphoreType.DMA((2,)), pltpu.SemaphoreType.DMA(())],
        name="dispatch",
        compiler_params=pltpu.CompilerParams(dimension_semantics=("arbitrary",)),
    )(loc_flat, run_start.reshape(-1), run_len.reshape(-1), stage_off.reshape(-1), pad_end, padded, n_used, h1t)


def _experts_body(be_ref, nu_ref, nx_ref, xs_ref, bg_ref, bu_ref, bd_ref, wg_hbm, wu_hbm, wd_hbm, ys_ref,
                  stage, wg_s, wu_s, wd_s, wsem):
    i = pl.program_id(0)
    bm = BM_EXPERT
    used = i < nu_ref[0]
    prev = be_ref[jnp.maximum(i - 1, 0)]
    fresh = (i == 0) | (be_ref[i] != prev)

    def weight_copies(e):
        return [pltpu.make_async_copy(w_hbm.at[e], stage.at[n], wsem)
                for n, w_hbm in enumerate((wg_hbm, wu_hbm, wd_hbm))]

    @pl.when(i == 0)
    def _():
        for cp in weight_copies(be_ref[0]):
            cp.start()

    @pl.when(used & fresh)
    def _():
        for cp in weight_copies(be_ref[i]):
            cp.wait()
        wg_s[...] = stage[0].astype(BF16)
        wu_s[...] = stage[1].astype(BF16)
        wd_s[...] = stage[2].astype(BF16)

        @pl.when(nx_ref[i] >= 0)
        def _():
            for cp in weight_copies(nx_ref[i]):
                cp.start()

    @pl.when(used)
    def _():
        xb = _load_row_tiles(xs_ref, bm).astype(BF16)
        g = _dot(xb, wg_s[...]) + bg_ref[0]
        u = _dot(xb, wu_s[...]) + bu_ref[0]
        g = jnp.minimum(g, SWIGLU_LIMIT)
        u = jnp.clip(u, -SWIGLU_LIMIT, SWIGLU_LIMIT)
        act = g * jax.nn.sigmoid(SWIGLU_ALPHA * g) * (u + 1.0)
        _store_row_tiles(ys_ref, _dot(act.astype(BF16), wd_s[...]) + bd_ref[0], bm)

    @pl.when(jnp.logical_not(used))
    def _():
        ys_ref[...] = jnp.zeros_like(ys_ref)


def _experts(block_e, n_used, next_e, xs, wg, bg, wu, bu, wd, bd):
    n_rows = xs.shape[0] // ROW_TILE
    bm = BM_EXPERT
    n_e, d, d_ff = wg.shape
    assert d == d_ff and wd.shape == wg.shape
    row = lambda i, be, nu, nx: (jnp.minimum(i, nu[0] - 1), 0)
    exp3 = lambda i, be, nu, nx: (be[jnp.minimum(i, nu[0] - 1)], 0, 0)
    any_spec = pl.BlockSpec(memory_space=pl.ANY)
    grid_spec = pltpu.PrefetchScalarGridSpec(
        num_scalar_prefetch=3, grid=(n_rows // bm,),
        in_specs=[pl.BlockSpec((bm * ROW_TILE, LANES), row),
                  pl.BlockSpec((1, 1, d_ff), exp3), pl.BlockSpec((1, 1, d_ff), exp3),
                  pl.BlockSpec((1, 1, d), exp3), any_spec, any_spec, any_spec],
        out_specs=pl.BlockSpec((bm * ROW_TILE, LANES), lambda i, be, nu, nx: (i, 0)),
        scratch_shapes=[pltpu.VMEM((3, d, d_ff), F32),
                        pltpu.VMEM((d, d_ff), BF16), pltpu.VMEM((d, d_ff), BF16), pltpu.VMEM((d_ff, d), BF16),
                        pltpu.SemaphoreType.DMA(())])
    return pl.pallas_call(
        _experts_body, grid_spec=grid_spec,
        out_shape=jax.ShapeDtypeStruct((n_rows * ROW_TILE, LANES), F32), name="experts",
        compiler_params=pltpu.CompilerParams(
            dimension_semantics=("arbitrary",), vmem_limit_bytes=48 * 1024 * 1024),
    )(block_e, n_used, next_e, xs, bg.reshape(n_e, 1, d_ff), bu.reshape(n_e, 1, d_ff), bd.reshape(n_e, 1, d),
      wg, wu, wd)


def _combine_body(dest_ref, dest_next_ref, h1t_ref, tw_ref, g2_ref, b2_ref, ys_hbm, out_ref, buf, sems):
    tm = TM_COMBINE
    i = pl.program_id(0)
    slot = i % 2

    def start_row(d_ref, s, t):
        for kx in range(TOP_K):
            d = d_ref[TOP_K * t + kx]
            pltpu.make_async_copy(_tile(ys_hbm, d), _tile(buf.at[s, kx], t), sems.at[s]).start(priority=kx % 2)

    def wait_tile(s):
        for kx in range(TOP_K):
            pltpu.make_async_copy(ys_hbm.at[pl.ds(0, tm * ROW_TILE)], buf.at[s, kx], sems.at[s]).wait()

    @pl.when(i == 0)
    def _():
        lax.fori_loop(0, tm, lambda t, c: (start_row(dest_ref, 0, t), c)[1], 0)

    wait_tile(slot)
    rc = COMBINE_ROW_CHUNK
    for c in range(tm // rc):
        for t in range(c * rc, (c + 1) * rc):
            start_row(dest_next_ref, 1 - slot, t)
        tw = tw_ref[c * rc:(c + 1) * rc, :]
        ffn = tw[:, 0:1] * _load_row_tiles(buf.at[slot, 0], rc, c * rc)
        for kx in range(1, TOP_K):
            ffn = ffn + tw[:, kx:kx + 1] * _load_row_tiles(buf.at[slot, kx], rc, c * rc)
        h1 = _load_row_tiles(h1t_ref, rc, c * rc)
        out_ref[c * rc:(c + 1) * rc, :] = _ln(DN_ALPHA * h1 + ffn, g2_ref[...], b2_ref[...])

    @pl.when(i == pl.num_programs(0) - 1)
    def _():
        wait_tile(1 - slot)


def _combine(dest_flat, h1t, tw, g2, b2, ys):
    n_tok = h1t.shape[0] // ROW_TILE
    d = ROW_TILE * LANES
    tm = TM_COMBINE
    n_steps = n_tok // tm
    const = lambda i: (0, 0)
    return pl.pallas_call(
        _combine_body, grid=(n_steps,),
        in_specs=[pl.BlockSpec((TOP_K * tm,), lambda i: (i,), memory_space=pltpu.SMEM),
                  pl.BlockSpec((TOP_K * tm,), lambda i: (jnp.minimum(i + 1, n_steps - 1),),
                               memory_space=pltpu.SMEM),
                  pl.BlockSpec((tm * ROW_TILE, LANES), lambda i: (i, 0)),
                  pl.BlockSpec((tm, LANES), lambda i: (i, 0)),
                  pl.BlockSpec((1, d), const), pl.BlockSpec((1, d), const),
                  pl.BlockSpec(memory_space=pl.ANY)],
        out_specs=pl.BlockSpec((tm, d), lambda i: (i, 0)),
        out_shape=jax.ShapeDtypeStruct((n_tok, d), F32),
        scratch_shapes=[pltpu.VMEM((2, TOP_K, tm * ROW_TILE, LANES), F32), pltpu.SemaphoreType.DMA((2,))],
        name="combine",
        compiler_params=pltpu.CompilerParams(
            dimension_semantics=("arbitrary",), vmem_limit_bytes=40 * 1024 * 1024),
    )(dest_flat, dest_flat, h1t, tw, g2, b2, ys)


def _rope_tables(seq):
    t = np.arange(seq)
    row = (t // GRID_W).astype(np.float32)
    col = (t % GRID_W).astype(np.float32)
    half = HEAD_DIM // 2
    quarter = half // 2
    inv = (ROPE_THETA ** (-np.arange(quarter, dtype=np.float32) * np.float32(2.0 / half))).astype(np.float32)
    ang_r = row[:, None] * inv[None, :]
    ang_c = col[:, None] * inv[None, :]
    zeros = np.zeros_like(ang_r)
    cos = np.concatenate([np.cos(ang_r), np.cos(ang_r), np.cos(ang_c), np.cos(ang_c)], -1)
    s_lo = np.concatenate([-np.sin(ang_r), zeros, -np.sin(ang_c), zeros], -1)
    s_hi = np.concatenate([zeros, np.sin(ang_r), zeros, np.sin(ang_c)], -1)
    return tuple(jnp.asarray(np.tile(a, (1, LANES // HEAD_DIM)), F32) for a in (cos, s_lo, s_hi))


def _routing(top_i, rank, counts, bm):
    n_tok = top_i.shape[0]
    padded = (counts + bm - 1) // bm * bm
    pad_end = jnp.cumsum(padded)
    pad_start = pad_end - padded
    onehot = top_i[:, :, None] == jnp.arange(N_EXPERTS, dtype=jnp.int32)[None, None, :]
    base = jnp.sum(jnp.where(onehot, pad_start[None, None, :], 0), axis=-1)
    dest = (base + rank).astype(jnp.int32)
    tmd = TM_DISPATCH
    n_tiles = n_tok // tmd
    tile_cnt = jnp.sum(onehot.reshape(n_tiles, tmd * TOP_K, N_EXPERTS).astype(jnp.int32), axis=1)
    run_start = pad_start[None, :] + jnp.cumsum(tile_cnt, axis=0) - tile_cnt
    stage_off = jnp.cumsum(tile_cnt, axis=1) - tile_cnt
    shift = jnp.repeat(run_start - stage_off, tmd, axis=0)
    loc = (dest - jnp.sum(jnp.where(onehot, shift[:, None, :], 0), axis=-1)).astype(jnp.int32)
    lane_pad = lambda a: jnp.pad(a.astype(jnp.int32), ((0, 0), (0, LANES - N_EXPERTS)))
    runs = (loc.reshape(-1), lane_pad(run_start), lane_pad(tile_cnt), lane_pad(stage_off))
    n_rows = n_tok * TOP_K + N_EXPERTS * bm
    n_blocks = n_rows // bm
    block_start = jnp.arange(n_blocks, dtype=jnp.int32) * bm
    block_e = jnp.minimum(jnp.sum((pad_end[None, :] <= block_start[:, None]).astype(jnp.int32), axis=1),
                          N_EXPERTS - 1).astype(jnp.int32)
    n_used = (pad_end[-1] // bm).astype(jnp.int32).reshape(1)
    e_ids = jnp.arange(N_EXPERTS, dtype=jnp.int32)
    later = (e_ids[None, :] > e_ids[:, None]) & (counts[None, :] > 0)
    next_of = jnp.min(jnp.where(later, e_ids[None, :], N_EXPERTS), axis=1)
    next_of = jnp.where(next_of == N_EXPERTS, -1, next_of).astype(jnp.int32)
    next_e = jnp.sum(jnp.where(block_e[:, None] == e_ids[None, :], next_of[None, :], 0), axis=1).astype(jnp.int32)
    return (dest.reshape(-1), runs, block_e, n_used, next_e, n_rows, pad_end.astype(jnp.int32),
            padded.astype(jnp.int32))


def kernel(x, ln0_g, ln0_b, w_in, a_sink, b_q_norm, b_k_norm, w_branch_a, w_branch_b, w_out,
           ln1_g, ln1_b, w_router, b_router, w_gate, b_gate, w_up, b_up, w_down, b_down,
           ln2_g, ln2_b):
    bsz, seq, d = x.shape
    assert w_in.shape[0] == DEPTH == 1
    assert seq % TM_PROJ == 0 and seq % TQ_GRID == 0 and seq == (seq // GRID_W) * GRID_W
    n_tok = bsz * seq
    row2 = lambda a: a.reshape(1, -1)

    o_ka, o_va, o_qb, o_kb, o_vb, o_g = Q_W, Q_W + KV_W, Q_W + 2 * KV_W, 2 * Q_W + 2 * KV_W, \
        2 * Q_W + 3 * KV_W, 2 * Q_W + 4 * KV_W
    col = np.arange(w_in.shape[-1])
    col_perm = np.concatenate([
        col[:Q_W][_PAIRED], col[o_qb:o_kb][_PAIRED], col[o_ka:o_va], col[o_kb:o_vb],
        col[o_va:o_qb], col[o_vb:o_g], col[o_g:]])
    w_perm = w_in[0].astype(BF16)[:, col_perm]
    head_id = np.arange(Q_W) // HEAD_DIM
    bd = jnp.asarray(head_id[:, None] == head_id[None, :], BF16)
    gq = jnp.tile(b_q_norm[0].astype(F32), N_HEADS).reshape(1, Q_W)
    gk = jnp.tile(b_k_norm[0].astype(F32), N_KV).reshape(1, KV_W)

    qa, ka, vat, qb, kb, vbt, sga, sgb = _in_proj(
        x, row2(ln0_g), row2(ln0_b), w_perm, bd, gq, gk, _rope_tables(seq))
    oa = _win_attn(a_sink[0].astype(F32), qa, ka, vat)
    ob = _grid_attn(qb, kb, vbt)

    wr = jnp.zeros((LANES, d), BF16).at[:N_EXPERTS].set(w_router[0].T.astype(BF16))
    br = jnp.zeros((LANES, 1), F32).at[:N_EXPERTS, 0].set(b_router[0])
    h1t, tr, tw, cnt = _post_attn(
        x, oa, ob, sga, sgb, row2(ln0_g), row2(ln0_b),
        w_branch_a[0][_PAIRED].astype(BF16), w_branch_b[0][_PAIRED].astype(BF16),
        w_out[0].astype(BF16), row2(ln1_g[0]), row2(ln1_b[0]), wr, br)
    h1t = h1t.reshape(n_tok * ROW_TILE, LANES)
    top_i, rank = tr[:TOP_K].T, tr[TOP_K:].T
    tw = tw.reshape(n_tok, LANES)

    counts = cnt[:, 0].astype(jnp.int32)
    dest, runs, block_e, n_used, next_e, n_rows, pad_end, padded = _routing(top_i, rank, counts, BM_EXPERT)
    xs = _dispatch(*runs, pad_end, padded, n_used, h1t, n_rows)
    ys = _experts(block_e, n_used, next_e, xs, w_gate[0], b_gate[0], w_up[0], b_up[0], w_down[0], b_down[0])
    out = _combine(dest, h1t, tw, row2(ln2_g[0]), row2(ln2_b[0]), ys)
    return out.reshape(bsz, seq, d)
```

```python
import functools

import jax
import jax.numpy as jnp
import numpy as np
from jax import lax
from jax.experimental import pallas as pl
from jax.experimental.pallas import tpu as pltpu

HEAD_DIM = 64
N_HEADS = 8
N_KV = 2
WINDOW = 128
BLOCK = 128
GRID_W = 64
ROPE_THETA = 10000.0
N_EXPERTS = 32
TOP_K = 4
SWIGLU_LIMIT = 7.0
SWIGLU_ALPHA = 1.702
LN_EPS = 1e-5
RMS_EPS = 1e-6
NEG_INF = -1e30
DEPTH = 1
DN_ALPHA = (2.0 * DEPTH) ** 0.25
ALIBI_SLOPES = tuple(2.0 ** (-8.0 * (h + 1) / N_HEADS) for h in range(N_HEADS))
QK_SCALE = HEAD_DIM ** -0.5
LOG2_E = 1.4426950408889634

LANES = 128
ROW_TILE = 8
Q_W = N_HEADS * HEAD_DIM
KV_W = N_KV * HEAD_DIM

TM_PROJ = 512
TM_IN_PROJ = 1024
PROJ_ROW_GROUPS = 4
WIN_BLOCKS = 8
TQ_GRID = 256
KEY_CHUNK = 256
BM_EXPERT = 512
TM_DISPATCH = 256
TM_COMBINE = 256
COMBINE_ROW_CHUNK = 32

F32 = jnp.float32
BF16 = jnp.bfloat16

_PAIRED = np.array([(j if c == 0 else 4 + j) * HEAD_DIM + d
                    for j in range(4) for c in range(2) for d in range(HEAD_DIM)], np.int32)


def _ln(x, g, b):
    mu = jnp.mean(x, -1, keepdims=True)
    xc = x - mu
    var = jnp.mean(xc * xc, -1, keepdims=True)
    return xc * lax.rsqrt(var + LN_EPS) * g + b


def _dot(a, b):
    return jnp.dot(a, b, preferred_element_type=F32)


def _dot_nt(a, b):
    return lax.dot_general(a, b, (((1,), (1,)), ((), ())), preferred_element_type=F32)


def _load_row_tiles(ref, rows, first=0):
    return jnp.concatenate(
        [ref[pl.ds(first * ROW_TILE + c, rows, stride=ROW_TILE), :] for c in range(ROW_TILE)], axis=1)


def _store_row_tiles(ref, val, rows):
    for c in range(ROW_TILE):
        ref[pl.ds(c, rows, stride=ROW_TILE), :] = val[:, c * LANES:(c + 1) * LANES]


def _tile(ref, row):
    start = row * ROW_TILE
    if not isinstance(row, int):
        start = pl.multiple_of(start, ROW_TILE)
    return ref.at[pl.ds(start, ROW_TILE)]


def _in_proj_body(x_ref, g0_ref, b0_ref, w_ref, bd_ref, gq_ref, gk_ref, c_ref, s1_ref, s2_ref,
                  qa_ref, ka_ref, vat_ref, qb_ref, kb_ref, vbt_ref, sga_ref, sgb_ref):
    tm = x_ref.shape[1]
    d = sga_ref.shape[-1]
    o_k, o_v, o_g = 2 * Q_W, 2 * Q_W + 2 * KV_W, 2 * Q_W + 4 * KV_W
    rg = tm // PROJ_ROW_GROUPS
    for grp in range(PROJ_ROW_GROUPS):
        rows = slice(grp * rg, (grp + 1) * rg)
        hb = _ln(x_ref[0, rows, :], g0_ref[...], b0_ref[...]).astype(BF16)

        def proj(lo, hi):
            return _dot(hb, w_ref[:, lo:hi])

        def norm_rope(t, width, g_ref):
            ss = _dot((t * t).astype(BF16), bd_ref[:width, :width])
            r = lax.rsqrt(ss * (1.0 / HEAD_DIM) + RMS_EPS)
            reps = width // LANES
            tab = lambda ref: jnp.concatenate([ref[rows, :]] * reps, axis=1)
            y = t * g_ref[...]
            rot = (y * tab(c_ref) + pltpu.roll(y, width - 16, 1) * tab(s1_ref)
                   + pltpu.roll(y, 16, 1) * tab(s2_ref))
            return rot * r

        q2 = proj(0, o_k)
        k2 = proj(o_k, o_v)
        qa_ref[0, rows, :] = (q2[:, :Q_W] * (QK_SCALE * LOG2_E)).astype(BF16)
        ka_ref[0, rows, :] = k2[:, :KV_W].astype(BF16)
        qb = norm_rope(q2[:, Q_W:], Q_W, gq_ref)
        qb_ref[0, rows, :] = (qb * (QK_SCALE * LOG2_E)).astype(BF16)
        kb_ref[0, rows, :] = norm_rope(k2[:, KV_W:], KV_W, gk_ref).astype(BF16)
        v2t = proj(o_v, o_g).T
        vat_ref[0, :, rows] = v2t[:KV_W].astype(BF16)
        vbt_ref[0, :, rows] = v2t[KV_W:].astype(BF16)
        sga_ref[0, rows, :] = jax.nn.sigmoid(proj(o_g, o_g + d)).astype(BF16)
        sgb_ref[0, rows, :] = jax.nn.sigmoid(proj(o_g + d, o_g + 2 * d)).astype(BF16)


def _in_proj(x, g0, b0, w, bd, gq, gk, tabs):
    bsz, seq, d = x.shape
    tm = TM_IN_PROJ
    n_in = w.shape[1]
    const = lambda i, j: (0, 0)
    tok3 = lambda i, j: (j, i, 0)
    tab = lambda i, j: (i, 0)
    in_specs = [
        pl.BlockSpec((1, tm, d), tok3),
        pl.BlockSpec((1, d), const), pl.BlockSpec((1, d), const),
        pl.BlockSpec((d, n_in), const),
        pl.BlockSpec((Q_W, Q_W), const),
        pl.BlockSpec((1, Q_W), const), pl.BlockSpec((1, KV_W), const),
        pl.BlockSpec((tm, LANES), tab), pl.BlockSpec((tm, LANES), tab), pl.BlockSpec((tm, LANES), tab),
    ]
    tr3 = lambda i, j: (j, 0, i)
    out_specs = [
        pl.BlockSpec((1, tm, Q_W), tok3), pl.BlockSpec((1, tm, KV_W), tok3),
        pl.BlockSpec((1, KV_W, tm), tr3),
        pl.BlockSpec((1, tm, Q_W), tok3), pl.BlockSpec((1, tm, KV_W), tok3),
        pl.BlockSpec((1, KV_W, tm), tr3),
        pl.BlockSpec((1, tm, d), tok3), pl.BlockSpec((1, tm, d), tok3),
    ]
    out_shape = [
        jax.ShapeDtypeStruct((bsz, seq, Q_W), BF16), jax.ShapeDtypeStruct((bsz, seq, KV_W), BF16),
        jax.ShapeDtypeStruct((bsz, KV_W, seq), BF16),
        jax.ShapeDtypeStruct((bsz, seq, Q_W), BF16), jax.ShapeDtypeStruct((bsz, seq, KV_W), BF16),
        jax.ShapeDtypeStruct((bsz, KV_W, seq), BF16),
        jax.ShapeDtypeStruct((bsz, seq, d), BF16), jax.ShapeDtypeStruct((bsz, seq, d), BF16),
    ]
    return pl.pallas_call(
        _in_proj_body, grid=(seq // tm, bsz), in_specs=in_specs, out_specs=out_specs,
        out_shape=out_shape, name="in_proj",
        compiler_params=pltpu.CompilerParams(
            dimension_semantics=("arbitrary", "arbitrary"), vmem_limit_bytes=48 * 1024 * 1024),
    )(x, g0, b0, w, bd, gq, gk, *tabs)


def _half_mask(rows, c):
    lane = lax.broadcasted_iota(jnp.int32, (rows, LANES), 1)
    return (lane >= HEAD_DIM) if c == 1 else (lane < HEAD_DIM)


def _win_attn_body(sink_ref, q_ref, *refs, seq):
    nk = WIN_BLOCKS + 2
    k_refs, v_refs, o_ref = refs[:nk], refs[nk:2 * nk], refs[2 * nk]
    kk = lax.broadcasted_iota(jnp.int32, (3 * BLOCK, BLOCK), 0)
    qq = lax.broadcasted_iota(jnp.int32, (3 * BLOCK, BLOCK), 1)
    dist_i = jnp.abs(kk - BLOCK - qq)
    dist = dist_i.astype(F32)
    ones = jnp.ones((2 * ROW_TILE, 3 * BLOCK), BF16)
    for blk in range(WIN_BLOCKS):
        n = pl.program_id(1) * WIN_BLOCKS + blk
        rows = slice(blk * BLOCK, (blk + 1) * BLOCK)
        k = jnp.concatenate([r[0] for r in k_refs[blk:blk + 3]], axis=0)
        vt = jnp.concatenate([r[0] for r in v_refs[blk:blk + 3]], axis=1)
        k_pos = n * BLOCK - BLOCK + kk
        valid = (dist_i <= WINDOW) & (k_pos >= 0) & (k_pos < seq)
        slabs = [q_ref[0, rows, j * LANES:(j + 1) * LANES] for j in range(4)]
        qm = jnp.concatenate(
            [jnp.where(_half_mask(BLOCK, c), s, jnp.zeros_like(s)) for c in range(2) for s in slabs], axis=0)
        st_all = _dot_nt(k, qm)
        ots = []
        for c in range(2):
            ps, sinks = [], []
            for j in range(4):
                head = j + 4 * c
                col = (4 * c + j) * BLOCK
                st = st_all[:, col:col + BLOCK] + jnp.where(valid, (-ALIBI_SLOPES[head] * LOG2_E) * dist, NEG_INF)
                sk = sink_ref[head] * LOG2_E
                m = jnp.maximum(jnp.max(st, axis=0, keepdims=True), sk)
                ps.append(jnp.exp2((st - m).astype(BF16)))
                sinks.append(jnp.exp2(sk - m))
            va = jnp.concatenate([vt[c * HEAD_DIM:(c + 1) * HEAD_DIM, :], ones], axis=0)
            ot = _dot(va, jnp.concatenate(ps, axis=1))
            ots.append(ot[:HEAD_DIM] / (ot[HEAD_DIM:HEAD_DIM + 1] + jnp.concatenate(sinks, axis=1)))
        for j in range(4):
            pair = jnp.concatenate([ot[:, j * BLOCK:(j + 1) * BLOCK] for ot in ots], axis=0)
            o_ref[0, rows, j * LANES:(j + 1) * LANES] = pair.T.astype(BF16)


def _win_attn(sink, qa, ka, vat):
    bsz, seq, _ = qa.shape
    nb = seq // BLOCK
    wb = WIN_BLOCKS
    qmap = lambda b, n: (b, n, 0)
    blk = lambda off: (lambda n: jnp.clip(n * wb + off, 0, nb - 1))
    kspec = lambda f: pl.BlockSpec((1, BLOCK, KV_W), lambda b, n: (b, f(n), 0))
    vspec = lambda f: pl.BlockSpec((1, KV_W, BLOCK), lambda b, n: (b, 0, f(n)))
    offs = range(-1, wb + 1)
    return pl.pallas_call(
        functools.partial(_win_attn_body, seq=seq), grid=(bsz, nb // wb),
        in_specs=[pl.BlockSpec(memory_space=pltpu.SMEM), pl.BlockSpec((1, wb * BLOCK, Q_W), qmap)]
        + [kspec(blk(o)) for o in offs] + [vspec(blk(o)) for o in offs],
        out_specs=pl.BlockSpec((1, wb * BLOCK, Q_W), qmap),
        out_shape=jax.ShapeDtypeStruct((bsz, seq, Q_W), BF16), name="win_attn",
        compiler_params=pltpu.CompilerParams(dimension_semantics=("arbitrary", "arbitrary")),
    )(sink, qa, *([ka] * (wb + 2)), *([vat] * (wb + 2)))


def _grid_attn_body(q_ref, k_ref, vt_ref, o_ref, s0_ref, s1_ref, p0_ref, p1_ref):
    tq = q_ref.shape[1]
    seq = k_ref.shape[1]
    kc = KEY_CHUNK
    n_chunks = seq // kc
    s_bufs, p_bufs = (s0_ref, s1_ref), (p0_ref, p1_ref)
    heads = [(j, c) for j in range(4) for c in range(2)]
    ones = jnp.ones((2 * ROW_TILE, seq), BF16)

    def masked_q(h):
        j, c = heads[h]
        slab = q_ref[0, :, j * LANES:(j + 1) * LANES]
        return jnp.where(_half_mask(tq, c), slab, jnp.zeros_like(slab))

    def score_chunk(h, qm, kb, m8):
        sc = _dot_nt(k_ref[0, kb * kc:(kb + 1) * kc, :], qm)
        s_bufs[h % 2][kb * kc:(kb + 1) * kc, :] = sc
        cm = jnp.max(sc.reshape(kc // ROW_TILE, ROW_TILE, tq), axis=0)
        return cm if m8 is None else jnp.maximum(m8, cm)

    def prob_chunk(h, kb, m):
        x = s_bufs[h % 2][kb * kc:(kb + 1) * kc, :] - m
        p_bufs[h % 2][kb * kc:(kb + 1) * kc, :] = jnp.exp2(x.astype(BF16))

    qm = masked_q(0)
    m8 = None
    for kb in range(n_chunks):
        m8 = score_chunk(0, qm, kb, m8)
    outs = []
    for h in range(len(heads)):
        m = jnp.max(m8, axis=0, keepdims=True)
        nxt = h + 1 < len(heads)
        if nxt:
            qm = masked_q(h + 1)
            m8 = None
        for kb in range(n_chunks):
            if nxt:
                m8 = score_chunk(h + 1, qm, kb, m8)
            prob_chunk(h, kb, m)
        j, c = heads[h]
        va = jnp.concatenate([vt_ref[0, c * HEAD_DIM:(c + 1) * HEAD_DIM, :], ones], axis=0)
        ot = _dot(va, p_bufs[h % 2][...])
        outs.append(ot[:HEAD_DIM] / ot[HEAD_DIM:HEAD_DIM + 1])
        if c == 1:
            o_ref[0, :, j * LANES:(j + 1) * LANES] = jnp.concatenate(outs, axis=0).T.astype(BF16)
            outs = []


def _grid_attn(qb, kb, vbt):
    bsz, seq, _ = qb.shape
    tq = TQ_GRID
    return pl.pallas_call(
        _grid_attn_body, grid=(bsz, seq // tq),
        in_specs=[pl.BlockSpec((1, tq, Q_W), lambda b, n: (b, n, 0)),
                  pl.BlockSpec((1, seq, KV_W), lambda b, n: (b, 0, 0)),
                  pl.BlockSpec((1, KV_W, seq), lambda b, n: (b, 0, 0))],
        out_specs=pl.BlockSpec((1, tq, Q_W), lambda b, n: (b, n, 0)),
        scratch_shapes=[pltpu.VMEM((seq, tq), F32), pltpu.VMEM((seq, tq), F32),
                        pltpu.VMEM((seq, tq), BF16), pltpu.VMEM((seq, tq), BF16)],
        out_shape=jax.ShapeDtypeStruct((bsz, seq, Q_W), BF16), name="grid_attn",
        compiler_params=pltpu.CompilerParams(
            dimension_semantics=("arbitrary", "arbitrary"), vmem_limit_bytes=40 * 1024 * 1024),
    )(qb, kb, vbt)


def _post_attn_body(x_ref, oa_ref, ob_ref, sga_ref, sgb_ref, g0_ref, b0_ref, wa_ref, wb_ref, wo_ref,
                    g1_ref, b1_ref, wr_ref, br_ref, h1t_ref, tr_ref, tw_ref, cnt_out_ref, cnt_ref):
    h0 = _ln(x_ref[0], g0_ref[...], b0_ref[...])
    out_a = _dot(oa_ref[0], wa_ref[...])
    out_b = _dot(ob_ref[0], wb_ref[...])
    merged = sga_ref[0].astype(F32) * out_a + sgb_ref[0].astype(F32) * out_b
    mix = _dot(merged.astype(BF16), wo_ref[...])
    h1 = _ln(DN_ALPHA * h0 + mix, g1_ref[...], b1_ref[...])
    tm = h1.shape[0]
    _store_row_tiles(h1t_ref.at[0], h1, tm)

    logits = (_dot_nt(wr_ref[...], h1.astype(BF16)) + br_ref[...])[:N_EXPERTS]
    sub = lax.broadcasted_iota(jnp.int32, (N_EXPERTS, tm), 0)
    cur = logits
    vals, idxs = [], []
    for _ in range(TOP_K):
        mv = jnp.max(cur, axis=0, keepdims=True)
        ix = jnp.min(jnp.where(cur == mv, sub, N_EXPERTS), axis=0, keepdims=True)
        vals.append(mv)
        idxs.append(ix)
        cur = jnp.where(sub == ix, -jnp.inf, cur)
    es = [jnp.exp(v - vals[0]) for v in vals]
    tot = es[0] + es[1] + es[2] + es[3]

    @pl.when((pl.program_id(0) == 0) & (pl.program_id(1) == 0))
    def _():
        cnt_ref[...] = jnp.zeros_like(cnt_ref)

    sel = jnp.zeros((N_EXPERTS, tm), F32)
    for kx in range(TOP_K):
        sel = sel + (sub == idxs[kx]).astype(F32)
    r_i = lax.broadcasted_iota(jnp.int32, (tm, tm), 0)
    c_i = lax.broadcasted_iota(jnp.int32, (tm, tm), 1)
    tri = (r_i < c_i).astype(BF16)
    rank = _dot(sel.astype(BF16), tri) + cnt_ref[:, 0:1]
    cnt_ref[...] = cnt_ref[...] + jnp.sum(sel, axis=1, keepdims=True)
    cnt_out_ref[...] = cnt_ref[...]

    rks = [jnp.sum(jnp.where(sub == ix, rank, 0.0), axis=0, keepdims=True).astype(jnp.int32) for ix in idxs]
    tr_ref[...] = jnp.concatenate(idxs + rks, axis=0)
    tw_t = jnp.concatenate([e / tot for e in es] + [jnp.zeros((LANES - TOP_K, tm), F32)], axis=0)
    tw_ref[0] = tw_t.T


def _post_attn(x, oa, ob, sga, sgb, g0, b0, wa, wb, wo, g1, b1, wr, br):
    bsz, seq, d = x.shape
    tm = TM_PROJ
    tok3 = lambda b, i: (b, i, 0)
    const = lambda b, i: (0, 0)
    full = lambda a: pl.BlockSpec(a.shape, const)
    return pl.pallas_call(
        _post_attn_body, grid=(bsz, seq // tm),
        in_specs=[pl.BlockSpec((1, tm, d), tok3),
                  pl.BlockSpec((1, tm, Q_W), tok3), pl.BlockSpec((1, tm, Q_W), tok3),
                  pl.BlockSpec((1, tm, d), tok3), pl.BlockSpec((1, tm, d), tok3),
                  full(g0), full(b0), full(wa), full(wb), full(wo), full(g1), full(b1),
                  full(wr), full(br)],
        out_specs=[pl.BlockSpec((1, tm * ROW_TILE, LANES), tok3),
                   pl.BlockSpec((2 * TOP_K, tm), lambda b, i: (0, b * (seq // tm) + i)),
                   pl.BlockSpec((1, tm, LANES), tok3), pl.BlockSpec((N_EXPERTS, LANES), const)],
        out_shape=[jax.ShapeDtypeStruct((bsz, seq * ROW_TILE, LANES), F32),
                   jax.ShapeDtypeStruct((2 * TOP_K, bsz * seq), jnp.int32),
                   jax.ShapeDtypeStruct((bsz, seq, LANES), F32),
                   jax.ShapeDtypeStruct((N_EXPERTS, LANES), F32)],
        scratch_shapes=[pltpu.VMEM((N_EXPERTS, LANES), F32)],
        name="post_attn",
        compiler_params=pltpu.CompilerParams(
            dimension_semantics=("arbitrary", "arbitrary"), vmem_limit_bytes=48 * 1024 * 1024),
    )(x, oa, ob, sga, sgb, g0, b0, wa, wb, wo, g1, b1, wr, br)


def _dispatch_body(dest_ref, pend_ref, padded_ref, nu_ref, h1t_ref, xs_hbm, zbuf, sem, zsem):
    tm = TM_DISPATCH
    zrows = BM_EXPERT * ROW_TILE
    n_blocks = xs_hbm.shape[0] // zrows

    @pl.when(pl.program_id(0) == 0)
    def _():
        zbuf[...] = jnp.zeros_like(zbuf)
        zero_wait = pltpu.make_async_copy(zbuf, xs_hbm.at[pl.ds(0, zrows)], zsem).wait
        for e in range(N_EXPERTS):
            @pl.when(padded_ref[e] > 0)
            def _():
                start = pl.multiple_of((pend_ref[e] - BM_EXPERT) * ROW_TILE, ROW_TILE)
                pltpu.make_async_copy(zbuf, xs_hbm.at[pl.ds(start, zrows)], zsem).start()
        for b in range(n_blocks - N_EXPERTS, n_blocks):
            @pl.when(b >= nu_ref[0])
            def _():
                pltpu.make_async_copy(zbuf, xs_hbm.at[pl.ds(b * zrows, zrows)], zsem).start()
        for e in range(N_EXPERTS):
            pl.when(padded_ref[e] > 0)(zero_wait)
        for b in range(n_blocks - N_EXPERTS, n_blocks):
            pl.when(b >= nu_ref[0])(zero_wait)

    for t in range(tm):
        for kx in range(TOP_K):
            d = dest_ref[TOP_K * t + kx]
            pltpu.make_async_copy(_tile(h1t_ref, t), _tile(xs_hbm, d), sem).start(priority=kx % 2)
    for _ in range(TOP_K):
        pltpu.make_async_copy(h1t_ref, xs_hbm.at[pl.ds(0, tm * ROW_TILE)], sem).wait()


def _dispatch(dest_flat, pad_end, padded, n_used, h1t, n_rows):
    n_tok = h1t.shape[0] // ROW_TILE
    tm = TM_DISPATCH
    return pl.pallas_call(
        _dispatch_body, grid=(n_tok // tm,),
        in_specs=[pl.BlockSpec((TOP_K * tm,), lambda i: (i,), memory_space=pltpu.SMEM),
                  pl.BlockSpec(memory_space=pltpu.SMEM), pl.BlockSpec(memory_space=pltpu.SMEM),
                  pl.BlockSpec(memory_space=pltpu.SMEM),
                  pl.BlockSpec((tm * ROW_TILE, LANES), lambda i: (i, 0))],
        out_specs=pl.BlockSpec(memory_space=pl.ANY),
        out_shape=jax.ShapeDtypeStruct((n_rows * ROW_TILE, LANES), F32),
        scratch_shapes=[pltpu.VMEM((BM_EXPERT * ROW_TILE, LANES), F32),
                        pltpu.SemaphoreType.DMA(()), pltpu.SemaphoreType.DMA(())],
        name="dispatch",
        compiler_params=pltpu.CompilerParams(dimension_semantics=("arbitrary",)),
    )(dest_flat, pad_end, padded, n_used, h1t)


def _experts_body(be_ref, nu_ref, nx_ref, nv_ref, xs_ref, bg_ref, bu_ref, bd_ref, wg_hbm, wu_hbm, wd_hbm, ys_ref,
                  stage, wg_s, wu_s, wd_s, wsem):
    i = pl.program_id(0)
    bm = BM_EXPERT
    used = i < nu_ref[0]
    prev = be_ref[jnp.maximum(i - 1, 0)]
    fresh = (i == 0) | (be_ref[i] != prev)

    def weight_copies(e):
        return [pltpu.make_async_copy(w_hbm.at[e], stage.at[n], wsem)
                for n, w_hbm in enumerate((wg_hbm, wu_hbm, wd_hbm))]

    @pl.when(i == 0)
    def _():
        for cp in weight_copies(be_ref[0]):
            cp.start()

    @pl.when(used & fresh)
    def _():
        for cp in weight_copies(be_ref[i]):
            cp.wait()
        wg_s[...] = stage[0].astype(BF16)
        wu_s[...] = stage[1].astype(BF16)
        wd_s[...] = stage[2].astype(BF16)

        @pl.when(nx_ref[i] >= 0)
        def _():
            for cp in weight_copies(nx_ref[i]):
                cp.start()

    def ffn(rows):
        xb = _load_row_tiles(xs_ref, rows).astype(BF16)
        g = _dot(xb, wg_s[...]) + bg_ref[0]
        u = _dot(xb, wu_s[...]) + bu_ref[0]
        g = jnp.minimum(g, SWIGLU_LIMIT)
        u = jnp.clip(u, -SWIGLU_LIMIT, SWIGLU_LIMIT)
        act = g * jax.nn.sigmoid(SWIGLU_ALPHA * g) * (u + 1.0)
        _store_row_tiles(ys_ref, _dot(act.astype(BF16), wd_s[...]) + bd_ref[0], rows)

    half = bm // 2
    full = used & (nv_ref[i] > half)

    @pl.when(full)
    def _():
        ffn(bm)

    @pl.when(used & jnp.logical_not(full))
    def _():
        ffn(half)
        ys_ref[pl.ds(half * ROW_TILE, half * ROW_TILE), :] = jnp.zeros((half * ROW_TILE, LANES), F32)

    @pl.when(jnp.logical_not(used))
    def _():
        ys_ref[...] = jnp.zeros_like(ys_ref)


def _experts(block_e, n_used, next_e, n_valid, xs, wg, bg, wu, bu, wd, bd):
    n_rows = xs.shape[0] // ROW_TILE
    bm = BM_EXPERT
    n_e, d, d_ff = wg.shape
    assert d == d_ff and wd.shape == wg.shape
    row = lambda i, be, nu, nx, nv: (jnp.minimum(i, nu[0] - 1), 0)
    exp3 = lambda i, be, nu, nx, nv: (be[jnp.minimum(i, nu[0] - 1)], 0, 0)
    any_spec = pl.BlockSpec(memory_space=pl.ANY)
    grid_spec = pltpu.PrefetchScalarGridSpec(
        num_scalar_prefetch=4, grid=(n_rows // bm,),
        in_specs=[pl.BlockSpec((bm * ROW_TILE, LANES), row),
                  pl.BlockSpec((1, 1, d_ff), exp3), pl.BlockSpec((1, 1, d_ff), exp3),
                  pl.BlockSpec((1, 1, d), exp3), any_spec, any_spec, any_spec],
        out_specs=pl.BlockSpec((bm * ROW_TILE, LANES), lambda i, be, nu, nx, nv: (i, 0)),
        scratch_shapes=[pltpu.VMEM((3, d, d_ff), F32),
                        pltpu.VMEM((d, d_ff), BF16), pltpu.VMEM((d, d_ff), BF16), pltpu.VMEM((d_ff, d), BF16),
                        pltpu.SemaphoreType.DMA(())])
    return pl.pallas_call(
        _experts_body, grid_spec=grid_spec,
        out_shape=jax.ShapeDtypeStruct((n_rows * ROW_TILE, LANES), F32), name="experts",
        compiler_params=pltpu.CompilerParams(
            dimension_semantics=("arbitrary",), vmem_limit_bytes=48 * 1024 * 1024),
    )(block_e, n_used, next_e, n_valid, xs, bg.reshape(n_e, 1, d_ff), bu.reshape(n_e, 1, d_ff), bd.reshape(n_e, 1, d),
      wg, wu, wd)


def _combine_body(dest_ref, dest_next_ref, h1t_ref, tw_ref, g2_ref, b2_ref, ys_hbm, out_ref, buf, sems):
    tm = TM_COMBINE
    i = pl.program_id(0)
    slot = i % 2

    def start_row(d_ref, s, t):
        for kx in range(TOP_K):
            d = d_ref[TOP_K * t + kx]
            pltpu.make_async_copy(_tile(ys_hbm, d), _tile(buf.at[s, kx], t), sems.at[s]).start(priority=kx % 2)

    def wait_tile(s):
        for kx in range(TOP_K):
            pltpu.make_async_copy(ys_hbm.at[pl.ds(0, tm * ROW_TILE)], buf.at[s, kx], sems.at[s]).wait()

    @pl.when(i == 0)
    def _():
        lax.fori_loop(0, tm, lambda t, c: (start_row(dest_ref, 0, t), c)[1], 0)

    wait_tile(slot)
    rc = COMBINE_ROW_CHUNK
    for c in range(tm // rc):
        for t in range(c * rc, (c + 1) * rc):
            start_row(dest_next_ref, 1 - slot, t)
        tw = tw_ref[c * rc:(c + 1) * rc, :]
        ffn = tw[:, 0:1] * _load_row_tiles(buf.at[slot, 0], rc, c * rc)
        for kx in range(1, TOP_K):
            ffn = ffn + tw[:, kx:kx + 1] * _load_row_tiles(buf.at[slot, kx], rc, c * rc)
        h1 = _load_row_tiles(h1t_ref, rc, c * rc)
        out_ref[c * rc:(c + 1) * rc, :] = _ln(DN_ALPHA * h1 + ffn, g2_ref[...], b2_ref[...])

    @pl.when(i == pl.num_programs(0) - 1)
    def _():
        wait_tile(1 - slot)


def _combine(dest_flat, h1t, tw, g2, b2, ys):
    n_tok = h1t.shape[0] // ROW_TILE
    d = ROW_TILE * LANES
    tm = TM_COMBINE
    n_steps = n_tok // tm
    const = lambda i: (0, 0)
    return pl.pallas_call(
        _combine_body, grid=(n_steps,),
        in_specs=[pl.BlockSpec((TOP_K * tm,), lambda i: (i,), memory_space=pltpu.SMEM),
                  pl.BlockSpec((TOP_K * tm,), lambda i: (jnp.minimum(i + 1, n_steps - 1),),
                               memory_space=pltpu.SMEM),
                  pl.BlockSpec((tm * ROW_TILE, LANES), lambda i: (i, 0)),
                  pl.BlockSpec((tm, LANES), lambda i: (i, 0)),
                  pl.BlockSpec((1, d), const), pl.BlockSpec((1, d), const),
                  pl.BlockSpec(memory_space=pl.ANY)],
        out_specs=pl.BlockSpec((tm, d), lambda i: (i, 0)),
        out_shape=jax.ShapeDtypeStruct((n_tok, d), F32),
        scratch_shapes=[pltpu.VMEM((2, TOP_K, tm * ROW_TILE, LANES), F32), pltpu.SemaphoreType.DMA((2,))],
        name="combine",
        compiler_params=pltpu.CompilerParams(
            dimension_semantics=("arbitrary",), vmem_limit_bytes=40 * 1024 * 1024),
    )(dest_flat, dest_flat, h1t, tw, g2, b2, ys)


def _rope_tables(seq):
    t = np.arange(seq)
    row = (t // GRID_W).astype(np.float32)
    col = (t % GRID_W).astype(np.float32)
    half = HEAD_DIM // 2
    quarter = half // 2
    inv = (ROPE_THETA ** (-np.arange(quarter, dtype=np.float32) * np.float32(2.0 / half))).astype(np.float32)
    ang_r = row[:, None] * inv[None, :]
    ang_c = col[:, None] * inv[None, :]
    zeros = np.zeros_like(ang_r)
    cos = np.concatenate([np.cos(ang_r), np.cos(ang_r), np.cos(ang_c), np.cos(ang_c)], -1)
    s_lo = np.concatenate([-np.sin(ang_r), zeros, -np.sin(ang_c), zeros], -1)
    s_hi = np.concatenate([zeros, np.sin(ang_r), zeros, np.sin(ang_c)], -1)
    return tuple(jnp.asarray(np.tile(a, (1, LANES // HEAD_DIM)), F32) for a in (cos, s_lo, s_hi))


def _routing(top_i, rank, counts, bm):
    n_tok = top_i.shape[0]
    padded = (counts + bm - 1) // bm * bm
    pad_end = jnp.cumsum(padded)
    pad_start = pad_end - padded
    base = jnp.sum(jnp.where(top_i[:, :, None] == jnp.arange(N_EXPERTS, dtype=jnp.int32)[None, None, :],
                             pad_start[None, None, :], 0), axis=-1)
    dest = (base + rank).astype(jnp.int32)
    n_rows = n_tok * TOP_K + N_EXPERTS * bm
    n_blocks = n_rows // bm
    block_start = jnp.arange(n_blocks, dtype=jnp.int32) * bm
    block_e = jnp.minimum(jnp.sum((pad_end[None, :] <= block_start[:, None]).astype(jnp.int32), axis=1),
                          N_EXPERTS - 1).astype(jnp.int32)
    n_used = (pad_end[-1] // bm).astype(jnp.int32).reshape(1)
    e_ids = jnp.arange(N_EXPERTS, dtype=jnp.int32)
    later = (e_ids[None, :] > e_ids[:, None]) & (counts[None, :] > 0)
    next_of = jnp.min(jnp.where(later, e_ids[None, :], N_EXPERTS), axis=1)
    next_of = jnp.where(next_of == N_EXPERTS, -1, next_of).astype(jnp.int32)
    next_e = jnp.sum(jnp.where(block_e[:, None] == e_ids[None, :], next_of[None, :], 0), axis=1).astype(jnp.int32)
    start_of = jnp.sum(jnp.where(block_e[:, None] == e_ids[None, :], pad_start[None, :], 0), axis=1)
    count_of = jnp.sum(jnp.where(block_e[:, None] == e_ids[None, :], counts[None, :], 0), axis=1)
    n_valid = jnp.clip(count_of - (block_start - start_of), 0, bm).astype(jnp.int32)
    return (dest.reshape(-1), block_e, n_used, next_e, n_valid, n_rows, pad_end.astype(jnp.int32),
            padded.astype(jnp.int32))


def kernel(x, ln0_g, ln0_b, w_in, a_sink, b_q_norm, b_k_norm, w_branch_a, w_branch_b, w_out,
           ln1_g, ln1_b, w_router, b_router, w_gate, b_gate, w_up, b_up, w_down, b_down,
           ln2_g, ln2_b):
    bsz, seq, d = x.shape
    assert w_in.shape[0] == DEPTH == 1
    assert seq % TM_PROJ == 0 and seq % TQ_GRID == 0 and seq == (seq // GRID_W) * GRID_W
    n_tok = bsz * seq
    row2 = lambda a: a.reshape(1, -1)

    o_ka, o_va, o_qb, o_kb, o_vb, o_g = Q_W, Q_W + KV_W, Q_W + 2 * KV_W, 2 * Q_W + 2 * KV_W, \
        2 * Q_W + 3 * KV_W, 2 * Q_W + 4 * KV_W
    col = np.arange(w_in.shape[-1])
    col_perm = np.concatenate([
        col[:Q_W][_PAIRED], col[o_qb:o_kb][_PAIRED], col[o_ka:o_va], col[o_kb:o_vb],
        col[o_va:o_qb], col[o_vb:o_g], col[o_g:]])
    w_perm = w_in[0].astype(BF16)[:, col_perm]
    head_id = np.arange(Q_W) // HEAD_DIM
    bd = jnp.asarray(head_id[:, None] == head_id[None, :], BF16)
    gq = jnp.tile(b_q_norm[0].astype(F32), N_HEADS).reshape(1, Q_W)
    gk = jnp.tile(b_k_norm[0].astype(F32), N_KV).reshape(1, KV_W)

    qa, ka, vat, qb, kb, vbt, sga, sgb = _in_proj(
        x, row2(ln0_g), row2(ln0_b), w_perm, bd, gq, gk, _rope_tables(seq))
    oa = _win_attn(a_sink[0].astype(F32), qa, ka, vat)
    ob = _grid_attn(qb, kb, vbt)

    wr = jnp.zeros((LANES, d), BF16).at[:N_EXPERTS].set(w_router[0].T.astype(BF16))
    br = jnp.zeros((LANES, 1), F32).at[:N_EXPERTS, 0].set(b_router[0])
    h1t, tr, tw, cnt = _post_attn(
        x, oa, ob, sga, sgb, row2(ln0_g), row2(ln0_b),
        w_branch_a[0][_PAIRED].astype(BF16), w_branch_b[0][_PAIRED].astype(BF16),
        w_out[0].astype(BF16), row2(ln1_g[0]), row2(ln1_b[0]), wr, br)
    h1t = h1t.reshape(n_tok * ROW_TILE, LANES)
    top_i, rank = tr[:TOP_K].T, tr[TOP_K:].T
    tw = tw.reshape(n_tok, LANES)

    counts = cnt[:, 0].astype(jnp.int32)
    dest, block_e, n_used, next_e, n_valid, n_rows, pad_end, padded = _routing(top_i, rank, counts, BM_EXPERT)
    xs = _dispatch(dest, pad_end, padded, n_used, h1t, n_rows)
    ys = _experts(block_e, n_used, next_e, n_valid, xs, w_gate[0], b_gate[0], w_up[0], b_up[0], w_down[0], b_down[0])
    out = _combine(dest, h1t, tw, row2(ln2_g[0]), row2(ln2_b[0]), ys)
    return out.reshape(bsz, seq, d)
```

```python
import functools

import jax
import jax.numpy as jnp
import numpy as np
from jax import lax
from jax.experimental import pallas as pl
from jax.experimental.pallas import tpu as pltpu

HEAD_DIM = 64
N_HEADS = 8
N_KV = 2
WINDOW = 128
BLOCK = 128
GRID_W = 64
ROPE_THETA = 10000.0
N_EXPERTS = 32
TOP_K = 4
SWIGLU_LIMIT = 7.0
SWIGLU_ALPHA = 1.702
LN_EPS = 1e-5
RMS_EPS = 1e-6
NEG_INF = -1e30
DEPTH = 1
DN_ALPHA = (2.0 * DEPTH) ** 0.25
ALIBI_SLOPES = tuple(2.0 ** (-8.0 * (h + 1) / N_HEADS) for h in range(N_HEADS))
QK_SCALE = HEAD_DIM ** -0.5
LOG2_E = 1.4426950408889634

LANES = 128
ROW_TILE = 8
Q_W = N_HEADS * HEAD_DIM
KV_W = N_KV * HEAD_DIM

TM_PROJ = 512
TM_IN_PROJ = 1024
PROJ_ROW_GROUPS = 4
WIN_BLOCKS = 8
TQ_GRID = 256
KEY_CHUNK = 256
BM_EXPERT = 512
EXPERT_TAIL_PARTS = 4
TM_DISPATCH = 256
TM_COMBINE = 256
COMBINE_ROW_CHUNK = 32

F32 = jnp.float32
BF16 = jnp.bfloat16

_PAIRED = np.array([(j if c == 0 else 4 + j) * HEAD_DIM + d
                    for j in range(4) for c in range(2) for d in range(HEAD_DIM)], np.int32)


def _ln(x, g, b):
    mu = jnp.mean(x, -1, keepdims=True)
    xc = x - mu
    var = jnp.mean(xc * xc, -1, keepdims=True)
    return xc * lax.rsqrt(var + LN_EPS) * g + b


def _dot(a, b):
    return jnp.dot(a, b, preferred_element_type=F32)


def _dot_nt(a, b):
    return lax.dot_general(a, b, (((1,), (1,)), ((), ())), preferred_element_type=F32)


def _load_row_tiles(ref, rows, first=0):
    return jnp.concatenate(
        [ref[pl.ds(first * ROW_TILE + c, rows, stride=ROW_TILE), :] for c in range(ROW_TILE)], axis=1)


def _store_row_tiles(ref, val, rows):
    for c in range(ROW_TILE):
        ref[pl.ds(c, rows, stride=ROW_TILE), :] = val[:, c * LANES:(c + 1) * LANES]


def _tile(ref, row):
    start = row * ROW_TILE
    if not isinstance(row, int):
        start = pl.multiple_of(start, ROW_TILE)
    return ref.at[pl.ds(start, ROW_TILE)]


def _in_proj_body(x_ref, g0_ref, b0_ref, w_ref, bd_ref, gq_ref, gk_ref, c_ref, s1_ref, s2_ref,
                  qa_ref, ka_ref, vat_ref, qb_ref, kb_ref, vbt_ref, sga_ref, sgb_ref):
    tm = x_ref.shape[1]
    d = sga_ref.shape[-1]
    o_k, o_v, o_g = 2 * Q_W, 2 * Q_W + 2 * KV_W, 2 * Q_W + 4 * KV_W
    rg = tm // PROJ_ROW_GROUPS
    for grp in range(PROJ_ROW_GROUPS):
        rows = slice(grp * rg, (grp + 1) * rg)
        hb = _ln(x_ref[0, rows, :], g0_ref[...], b0_ref[...]).astype(BF16)

        def proj(lo, hi):
            return _dot(hb, w_ref[:, lo:hi])

        def norm_rope(t, width, g_ref):
            ss = _dot((t * t).astype(BF16), bd_ref[:width, :width])
            r = lax.rsqrt(ss * (1.0 / HEAD_DIM) + RMS_EPS)
            reps = width // LANES
            tab = lambda ref: jnp.concatenate([ref[rows, :]] * reps, axis=1)
            y = t * g_ref[...]
            rot = (y * tab(c_ref) + pltpu.roll(y, width - 16, 1) * tab(s1_ref)
                   + pltpu.roll(y, 16, 1) * tab(s2_ref))
            return rot * r

        q2 = proj(0, o_k)
        k2 = proj(o_k, o_v)
        qa_ref[0, rows, :] = (q2[:, :Q_W] * (QK_SCALE * LOG2_E)).astype(BF16)
        ka_ref[0, rows, :] = k2[:, :KV_W].astype(BF16)
        qb = norm_rope(q2[:, Q_W:], Q_W, gq_ref)
        qb_ref[0, rows, :] = (qb * (QK_SCALE * LOG2_E)).astype(BF16)
        kb_ref[0, rows, :] = norm_rope(k2[:, KV_W:], KV_W, gk_ref).astype(BF16)
        v2t = proj(o_v, o_g).T
        vat_ref[0, :, rows] = v2t[:KV_W].astype(BF16)
        vbt_ref[0, :, rows] = v2t[KV_W:].astype(BF16)
        sga_ref[0, rows, :] = jax.nn.sigmoid(proj(o_g, o_g + d)).astype(BF16)
        sgb_ref[0, rows, :] = jax.nn.sigmoid(proj(o_g + d, o_g + 2 * d)).astype(BF16)


def _in_proj(x, g0, b0, w, bd, gq, gk, tabs):
    bsz, seq, d = x.shape
    tm = TM_IN_PROJ
    n_in = w.shape[1]
    const = lambda i, j: (0, 0)
    tok3 = lambda i, j: (j, i, 0)
    tab = lambda i, j: (i, 0)
    in_specs = [
        pl.BlockSpec((1, tm, d), tok3),
        pl.BlockSpec((1, d), const), pl.BlockSpec((1, d), const),
        pl.BlockSpec((d, n_in), const),
        pl.BlockSpec((Q_W, Q_W), const),
        pl.BlockSpec((1, Q_W), const), pl.BlockSpec((1, KV_W), const),
        pl.BlockSpec((tm, LANES), tab), pl.BlockSpec((tm, LANES), tab), pl.BlockSpec((tm, LANES), tab),
    ]
    tr3 = lambda i, j: (j, 0, i)
    out_specs = [
        pl.BlockSpec((1, tm, Q_W), tok3), pl.BlockSpec((1, tm, KV_W), tok3),
        pl.BlockSpec((1, KV_W, tm), tr3),
        pl.BlockSpec((1, tm, Q_W), tok3), pl.BlockSpec((1, tm, KV_W), tok3),
        pl.BlockSpec((1, KV_W, tm), tr3),
        pl.BlockSpec((1, tm, d), tok3), pl.BlockSpec((1, tm, d), tok3),
    ]
    out_shape = [
        jax.ShapeDtypeStruct((bsz, seq, Q_W), BF16), jax.ShapeDtypeStruct((bsz, seq, KV_W), BF16),
        jax.ShapeDtypeStruct((bsz, KV_W, seq), BF16),
        jax.ShapeDtypeStruct((bsz, seq, Q_W), BF16), jax.ShapeDtypeStruct((bsz, seq, KV_W), BF16),
        jax.ShapeDtypeStruct((bsz, KV_W, seq), BF16),
        jax.ShapeDtypeStruct((bsz, seq, d), BF16), jax.ShapeDtypeStruct((bsz, seq, d), BF16),
    ]
    return pl.pallas_call(
        _in_proj_body, grid=(seq // tm, bsz), in_specs=in_specs, out_specs=out_specs,
        out_shape=out_shape, name="in_proj",
        compiler_params=pltpu.CompilerParams(
            dimension_semantics=("arbitrary", "arbitrary"), vmem_limit_bytes=48 * 1024 * 1024),
    )(x, g0, b0, w, bd, gq, gk, *tabs)


def _half_mask(rows, c):
    lane = lax.broadcasted_iota(jnp.int32, (rows, LANES), 1)
    return (lane >= HEAD_DIM) if c == 1 else (lane < HEAD_DIM)


def _win_attn_body(sink_ref, q_ref, *refs, seq):
    nk = WIN_BLOCKS + 2
    k_refs, v_refs, o_ref = refs[:nk], refs[nk:2 * nk], refs[2 * nk]
    kk = lax.broadcasted_iota(jnp.int32, (3 * BLOCK, BLOCK), 0)
    qq = lax.broadcasted_iota(jnp.int32, (3 * BLOCK, BLOCK), 1)
    dist_i = jnp.abs(kk - BLOCK - qq)
    dist = dist_i.astype(F32)
    ones = jnp.ones((2 * ROW_TILE, 3 * BLOCK), BF16)
    for blk in range(WIN_BLOCKS):
        n = pl.program_id(1) * WIN_BLOCKS + blk
        rows = slice(blk * BLOCK, (blk + 1) * BLOCK)
        k = jnp.concatenate([r[0] for r in k_refs[blk:blk + 3]], axis=0)
        vt = jnp.concatenate([r[0] for r in v_refs[blk:blk + 3]], axis=1)
        k_pos = n * BLOCK - BLOCK + kk
        valid = (dist_i <= WINDOW) & (k_pos >= 0) & (k_pos < seq)
        slabs = [q_ref[0, rows, j * LANES:(j + 1) * LANES] for j in range(4)]
        qm = jnp.concatenate(
            [jnp.where(_half_mask(BLOCK, c), s, jnp.zeros_like(s)) for c in range(2) for s in slabs], axis=0)
        st_all = _dot_nt(k, qm)
        ots = []
        for c in range(2):
            ps, sinks = [], []
            for j in range(4):
                head = j + 4 * c
                col = (4 * c + j) * BLOCK
                st = st_all[:, col:col + BLOCK] + jnp.where(valid, (-ALIBI_SLOPES[head] * LOG2_E) * dist, NEG_INF)
                sk = sink_ref[head] * LOG2_E
                m = jnp.maximum(jnp.max(st, axis=0, keepdims=True), sk)
                ps.append(jnp.exp2((st - m).astype(BF16)))
                sinks.append(jnp.exp2(sk - m))
            va = jnp.concatenate([vt[c * HEAD_DIM:(c + 1) * HEAD_DIM, :], ones], axis=0)
            ot = _dot(va, jnp.concatenate(ps, axis=1))
            ots.append(ot[:HEAD_DIM] / (ot[HEAD_DIM:HEAD_DIM + 1] + jnp.concatenate(sinks, axis=1)))
        for j in range(4):
            pair = jnp.concatenate([ot[:, j * BLOCK:(j + 1) * BLOCK] for ot in ots], axis=0)
            o_ref[0, rows, j * LANES:(j + 1) * LANES] = pair.T.astype(BF16)


def _win_attn(sink, qa, ka, vat):
    bsz, seq, _ = qa.shape
    nb = seq // BLOCK
    wb = WIN_BLOCKS
    qmap = lambda b, n: (b, n, 0)
    blk = lambda off: (lambda n: jnp.clip(n * wb + off, 0, nb - 1))
    kspec = lambda f: pl.BlockSpec((1, BLOCK, KV_W), lambda b, n: (b, f(n), 0))
    vspec = lambda f: pl.BlockSpec((1, KV_W, BLOCK), lambda b, n: (b, 0, f(n)))
    offs = range(-1, wb + 1)
    return pl.pallas_call(
        functools.partial(_win_attn_body, seq=seq), grid=(bsz, nb // wb),
        in_specs=[pl.BlockSpec(memory_space=pltpu.SMEM), pl.BlockSpec((1, wb * BLOCK, Q_W), qmap)]
        + [kspec(blk(o)) for o in offs] + [vspec(blk(o)) for o in offs],
        out_specs=pl.BlockSpec((1, wb * BLOCK, Q_W), qmap),
        out_shape=jax.ShapeDtypeStruct((bsz, seq, Q_W), BF16), name="win_attn",
        compiler_params=pltpu.CompilerParams(dimension_semantics=("arbitrary", "arbitrary")),
    )(sink, qa, *([ka] * (wb + 2)), *([vat] * (wb + 2)))


def _grid_attn_body(q_ref, k_ref, vt_ref, o_ref, s0_ref, s1_ref, p0_ref, p1_ref):
    tq = q_ref.shape[1]
    seq = k_ref.shape[1]
    kc = KEY_CHUNK
    n_chunks = seq // kc
    s_bufs, p_bufs = (s0_ref, s1_ref), (p0_ref, p1_ref)
    heads = [(j, c) for j in range(4) for c in range(2)]
    ones = jnp.ones((2 * ROW_TILE, seq), BF16)

    def masked_q(h):
        j, c = heads[h]
        slab = q_ref[0, :, j * LANES:(j + 1) * LANES]
        return jnp.where(_half_mask(tq, c), slab, jnp.zeros_like(slab))

    def score_chunk(h, qm, kb, m8):
        sc = _dot_nt(k_ref[0, kb * kc:(kb + 1) * kc, :], qm)
        s_bufs[h % 2][kb * kc:(kb + 1) * kc, :] = sc
        cm = jnp.max(sc.reshape(kc // ROW_TILE, ROW_TILE, tq), axis=0)
        return cm if m8 is None else jnp.maximum(m8, cm)

    def prob_chunk(h, kb, m):
        x = s_bufs[h % 2][kb * kc:(kb + 1) * kc, :] - m
        p_bufs[h % 2][kb * kc:(kb + 1) * kc, :] = jnp.exp2(x.astype(BF16))

    qm = masked_q(0)
    m8 = None
    for kb in range(n_chunks):
        m8 = score_chunk(0, qm, kb, m8)
    outs = []
    for h in range(len(heads)):
        m = jnp.max(m8, axis=0, keepdims=True)
        nxt = h + 1 < len(heads)
        if nxt:
            qm = masked_q(h + 1)
            m8 = None
        for kb in range(n_chunks):
            if nxt:
                m8 = score_chunk(h + 1, qm, kb, m8)
            prob_chunk(h, kb, m)
        j, c = heads[h]
        va = jnp.concatenate([vt_ref[0, c * HEAD_DIM:(c + 1) * HEAD_DIM, :], ones], axis=0)
        ot = _dot(va, p_bufs[h % 2][...])
        outs.append(ot[:HEAD_DIM] / ot[HEAD_DIM:HEAD_DIM + 1])
        if c == 1:
            o_ref[0, :, j * LANES:(j + 1) * LANES] = jnp.concatenate(outs, axis=0).T.astype(BF16)
            outs = []


def _grid_attn(qb, kb, vbt):
    bsz, seq, _ = qb.shape
    tq = TQ_GRID
    return pl.pallas_call(
        _grid_attn_body, grid=(bsz, seq // tq),
        in_specs=[pl.BlockSpec((1, tq, Q_W), lambda b, n: (b, n, 0)),
                  pl.BlockSpec((1, seq, KV_W), lambda b, n: (b, 0, 0)),
                  pl.BlockSpec((1, KV_W, seq), lambda b, n: (b, 0, 0))],
        out_specs=pl.BlockSpec((1, tq, Q_W), lambda b, n: (b, n, 0)),
        scratch_shapes=[pltpu.VMEM((seq, tq), F32), pltpu.VMEM((seq, tq), F32),
                        pltpu.VMEM((seq, tq), BF16), pltpu.VMEM((seq, tq), BF16)],
        out_shape=jax.ShapeDtypeStruct((bsz, seq, Q_W), BF16), name="grid_attn",
        compiler_params=pltpu.CompilerParams(
            dimension_semantics=("arbitrary", "arbitrary"), vmem_limit_bytes=40 * 1024 * 1024),
    )(qb, kb, vbt)


def _post_attn_body(x_ref, oa_ref, ob_ref, sga_ref, sgb_ref, g0_ref, b0_ref, wa_ref, wb_ref, wo_ref,
                    g1_ref, b1_ref, wr_ref, br_ref, h1t_ref, tr_ref, tw_ref, cnt_out_ref, cnt_ref):
    h0 = _ln(x_ref[0], g0_ref[...], b0_ref[...])
    out_a = _dot(oa_ref[0], wa_ref[...])
    out_b = _dot(ob_ref[0], wb_ref[...])
    merged = sga_ref[0].astype(F32) * out_a + sgb_ref[0].astype(F32) * out_b
    mix = _dot(merged.astype(BF16), wo_ref[...])
    h1 = _ln(DN_ALPHA * h0 + mix, g1_ref[...], b1_ref[...])
    tm = h1.shape[0]
    _store_row_tiles(h1t_ref.at[0], h1, tm)

    logits = (_dot_nt(wr_ref[...], h1.astype(BF16)) + br_ref[...])[:N_EXPERTS]
    sub = lax.broadcasted_iota(jnp.int32, (N_EXPERTS, tm), 0)
    cur = logits
    vals, idxs = [], []
    for _ in range(TOP_K):
        mv = jnp.max(cur, axis=0, keepdims=True)
        ix = jnp.min(jnp.where(cur == mv, sub, N_EXPERTS), axis=0, keepdims=True)
        vals.append(mv)
        idxs.append(ix)
        cur = jnp.where(sub == ix, -jnp.inf, cur)
    es = [jnp.exp(v - vals[0]) for v in vals]
    tot = es[0] + es[1] + es[2] + es[3]

    @pl.when((pl.program_id(0) == 0) & (pl.program_id(1) == 0))
    def _():
        cnt_ref[...] = jnp.zeros_like(cnt_ref)

    sel = jnp.zeros((N_EXPERTS, tm), F32)
    for kx in range(TOP_K):
        sel = sel + (sub == idxs[kx]).astype(F32)
    r_i = lax.broadcasted_iota(jnp.int32, (tm, tm), 0)
    c_i = lax.broadcasted_iota(jnp.int32, (tm, tm), 1)
    tri = (r_i < c_i).astype(BF16)
    rank = _dot(sel.astype(BF16), tri) + cnt_ref[:, 0:1]
    cnt_ref[...] = cnt_ref[...] + jnp.sum(sel, axis=1, keepdims=True)
    cnt_out_ref[...] = cnt_ref[...]

    rks = [jnp.sum(jnp.where(sub == ix, rank, 0.0), axis=0, keepdims=True).astype(jnp.int32) for ix in idxs]
    tr_ref[...] = jnp.concatenate(idxs + rks, axis=0)
    tw_t = jnp.concatenate([e / tot for e in es] + [jnp.zeros((LANES - TOP_K, tm), F32)], axis=0)
    tw_ref[0] = tw_t.T


def _post_attn(x, oa, ob, sga, sgb, g0, b0, wa, wb, wo, g1, b1, wr, br):
    bsz, seq, d = x.shape
    tm = TM_PROJ
    tok3 = lambda b, i: (b, i, 0)
    const = lambda b, i: (0, 0)
    full = lambda a: pl.BlockSpec(a.shape, const)
    return pl.pallas_call(
        _post_attn_body, grid=(bsz, seq // tm),
        in_specs=[pl.BlockSpec((1, tm, d), tok3),
                  pl.BlockSpec((1, tm, Q_W), tok3), pl.BlockSpec((1, tm, Q_W), tok3),
                  pl.BlockSpec((1, tm, d), tok3), pl.BlockSpec((1, tm, d), tok3),
                  full(g0), full(b0), full(wa), full(wb), full(wo), full(g1), full(b1),
                  full(wr), full(br)],
        out_specs=[pl.BlockSpec((1, tm * ROW_TILE, LANES), tok3),
                   pl.BlockSpec((2 * TOP_K, tm), lambda b, i: (0, b * (seq // tm) + i)),
                   pl.BlockSpec((1, tm, LANES), tok3), pl.BlockSpec((N_EXPERTS, LANES), const)],
        out_shape=[jax.ShapeDtypeStruct((bsz, seq * ROW_TILE, LANES), F32),
                   jax.ShapeDtypeStruct((2 * TOP_K, bsz * seq), jnp.int32),
                   jax.ShapeDtypeStruct((bsz, seq, LANES), F32),
                   jax.ShapeDtypeStruct((N_EXPERTS, LANES), F32)],
        scratch_shapes=[pltpu.VMEM((N_EXPERTS, LANES), F32)],
        name="post_attn",
        compiler_params=pltpu.CompilerParams(
            dimension_semantics=("arbitrary", "arbitrary"), vmem_limit_bytes=48 * 1024 * 1024),
    )(x, oa, ob, sga, sgb, g0, b0, wa, wb, wo, g1, b1, wr, br)


def _dispatch_body(dest_ref, pend_ref, padded_ref, nu_ref, h1t_ref, xs_hbm, zbuf, sem, zsem):
    tm = TM_DISPATCH
    zrows = BM_EXPERT * ROW_TILE
    n_blocks = xs_hbm.shape[0] // zrows

    @pl.when(pl.program_id(0) == 0)
    def _():
        zbuf[...] = jnp.zeros_like(zbuf)
        zero_wait = pltpu.make_async_copy(zbuf, xs_hbm.at[pl.ds(0, zrows)], zsem).wait
        for e in range(N_EXPERTS):
            @pl.when(padded_ref[e] > 0)
            def _():
                start = pl.multiple_of((pend_ref[e] - BM_EXPERT) * ROW_TILE, ROW_TILE)
                pltpu.make_async_copy(zbuf, xs_hbm.at[pl.ds(start, zrows)], zsem).start()
        for b in range(n_blocks - N_EXPERTS, n_blocks):
            @pl.when(b >= nu_ref[0])
            def _():
                pltpu.make_async_copy(zbuf, xs_hbm.at[pl.ds(b * zrows, zrows)], zsem).start()
        for e in range(N_EXPERTS):
            pl.when(padded_ref[e] > 0)(zero_wait)
        for b in range(n_blocks - N_EXPERTS, n_blocks):
            pl.when(b >= nu_ref[0])(zero_wait)

    for t in range(tm):
        for kx in range(TOP_K):
            d = dest_ref[TOP_K * t + kx]
            pltpu.make_async_copy(_tile(h1t_ref, t), _tile(xs_hbm, d), sem).start(priority=kx % 2)
    for _ in range(TOP_K):
        pltpu.make_async_copy(h1t_ref, xs_hbm.at[pl.ds(0, tm * ROW_TILE)], sem).wait()


def _dispatch(dest_flat, pad_end, padded, n_used, h1t, n_rows):
    n_tok = h1t.shape[0] // ROW_TILE
    tm = TM_DISPATCH
    return pl.pallas_call(
        _dispatch_body, grid=(n_tok // tm,),
        in_specs=[pl.BlockSpec((TOP_K * tm,), lambda i: (i,), memory_space=pltpu.SMEM),
                  pl.BlockSpec(memory_space=pltpu.SMEM), pl.BlockSpec(memory_space=pltpu.SMEM),
                  pl.BlockSpec(memory_space=pltpu.SMEM),
                  pl.BlockSpec((tm * ROW_TILE, LANES), lambda i: (i, 0))],
        out_specs=pl.BlockSpec(memory_space=pl.ANY),
        out_shape=jax.ShapeDtypeStruct((n_rows * ROW_TILE, LANES), F32),
        scratch_shapes=[pltpu.VMEM((BM_EXPERT * ROW_TILE, LANES), F32),
                        pltpu.SemaphoreType.DMA(()), pltpu.SemaphoreType.DMA(())],
        name="dispatch",
        compiler_params=pltpu.CompilerParams(dimension_semantics=("arbitrary",)),
    )(dest_flat, pad_end, padded, n_used, h1t)


def _experts_body(be_ref, nu_ref, nx_ref, nv_ref, xs_ref, bg_ref, bu_ref, bd_ref, wg_hbm, wu_hbm, wd_hbm, ys_ref,
                  stage, wg_s, wu_s, wd_s, wsem):
    i = pl.program_id(0)
    bm = BM_EXPERT
    used = i < nu_ref[0]
    prev = be_ref[jnp.maximum(i - 1, 0)]
    fresh = (i == 0) | (be_ref[i] != prev)

    def weight_copies(e):
        return [pltpu.make_async_copy(w_hbm.at[e], stage.at[n], wsem)
                for n, w_hbm in enumerate((wg_hbm, wu_hbm, wd_hbm))]

    @pl.when(i == 0)
    def _():
        for cp in weight_copies(be_ref[0]):
            cp.start()

    @pl.when(used & fresh)
    def _():
        for cp in weight_copies(be_ref[i]):
            cp.wait()
        wg_s[...] = stage[0].astype(BF16)
        wu_s[...] = stage[1].astype(BF16)
        wd_s[...] = stage[2].astype(BF16)

        @pl.when(nx_ref[i] >= 0)
        def _():
            for cp in weight_copies(nx_ref[i]):
                cp.start()

    def ffn(rows):
        xb = _load_row_tiles(xs_ref, rows).astype(BF16)
        g = _dot(xb, wg_s[...]) + bg_ref[0]
        u = _dot(xb, wu_s[...]) + bu_ref[0]
        g = jnp.minimum(g, SWIGLU_LIMIT)
        u = jnp.clip(u, -SWIGLU_LIMIT, SWIGLU_LIMIT)
        act = g * jax.nn.sigmoid(SWIGLU_ALPHA * g) * (u + 1.0)
        _store_row_tiles(ys_ref, _dot(act.astype(BF16), wd_s[...]) + bd_ref[0], rows)

    part = bm // EXPERT_TAIL_PARTS
    nv = nv_ref[i]
    for q in range(1, EXPERT_TAIL_PARTS + 1):
        rows = q * part
        fits = used
        if q > 1:
            fits = fits & (nv > rows - part)
        if q < EXPERT_TAIL_PARTS:
            fits = fits & (nv <= rows)

        @pl.when(fits)
        def _(rows=rows):
            ffn(rows)
            if rows < bm:
                ys_ref[pl.ds(rows * ROW_TILE, (bm - rows) * ROW_TILE), :] = jnp.zeros(
                    ((bm - rows) * ROW_TILE, LANES), F32)

    @pl.when(jnp.logical_not(used))
    def _():
        ys_ref[...] = jnp.zeros_like(ys_ref)


def _experts(block_e, n_used, next_e, n_valid, xs, wg, bg, wu, bu, wd, bd):
    n_rows = xs.shape[0] // ROW_TILE
    bm = BM_EXPERT
    n_e, d, d_ff = wg.shape
    assert d == d_ff and wd.shape == wg.shape
    row = lambda i, be, nu, nx, nv: (jnp.minimum(i, nu[0] - 1), 0)
    exp3 = lambda i, be, nu, nx, nv: (be[jnp.minimum(i, nu[0] - 1)], 0, 0)
    any_spec = pl.BlockSpec(memory_space=pl.ANY)
    grid_spec = pltpu.PrefetchScalarGridSpec(
        num_scalar_prefetch=4, grid=(n_rows // bm,),
        in_specs=[pl.BlockSpec((bm * ROW_TILE, LANES), row),
                  pl.BlockSpec((1, 1, d_ff), exp3), pl.BlockSpec((1, 1, d_ff), exp3),
                  pl.BlockSpec((1, 1, d), exp3), any_spec, any_spec, any_spec],
        out_specs=pl.BlockSpec((bm * ROW_TILE, LANES), lambda i, be, nu, nx, nv: (i, 0)),
        scratch_shapes=[pltpu.VMEM((3, d, d_ff), F32),
                        pltpu.VMEM((d, d_ff), BF16), pltpu.VMEM((d, d_ff), BF16), pltpu.VMEM((d_ff, d), BF16),
                        pltpu.SemaphoreType.DMA(())])
    return pl.pallas_call(
        _experts_body, grid_spec=grid_spec,
        out_shape=jax.ShapeDtypeStruct((n_rows * ROW_TILE, LANES), F32), name="experts",
        compiler_params=pltpu.CompilerParams(
            dimension_semantics=("arbitrary",), vmem_limit_bytes=48 * 1024 * 1024),
    )(block_e, n_used, next_e, n_valid, xs, bg.reshape(n_e, 1, d_ff), bu.reshape(n_e, 1, d_ff), bd.reshape(n_e, 1, d),
      wg, wu, wd)


def _combine_body(dest_ref, dest_next_ref, h1t_ref, tw_ref, g2_ref, b2_ref, ys_hbm, out_ref, buf, sems):
    tm = TM_COMBINE
    i = pl.program_id(0)
    slot = i % 2

    def start_row(d_ref, s, t):
        for kx in range(TOP_K):
            d = d_ref[TOP_K * t + kx]
            pltpu.make_async_copy(_tile(ys_hbm, d), _tile(buf.at[s, kx], t), sems.at[s]).start(priority=kx % 2)

    def wait_tile(s):
        for kx in range(TOP_K):
            pltpu.make_async_copy(ys_hbm.at[pl.ds(0, tm * ROW_TILE)], buf.at[s, kx], sems.at[s]).wait()

    @pl.when(i == 0)
    def _():
        lax.fori_loop(0, tm, lambda t, c: (start_row(dest_ref, 0, t), c)[1], 0)

    wait_tile(slot)
    rc = COMBINE_ROW_CHUNK
    for c in range(tm // rc):
        for t in range(c * rc, (c + 1) * rc):
            start_row(dest_next_ref, 1 - slot, t)
        tw = tw_ref[c * rc:(c + 1) * rc, :]
        ffn = tw[:, 0:1] * _load_row_tiles(buf.at[slot, 0], rc, c * rc)
        for kx in range(1, TOP_K):
            ffn = ffn + tw[:, kx:kx + 1] * _load_row_tiles(buf.at[slot, kx], rc, c * rc)
        h1 = _load_row_tiles(h1t_ref, rc, c * rc)
        out_ref[c * rc:(c + 1) * rc, :] = _ln(DN_ALPHA * h1 + ffn, g2_ref[...], b2_ref[...])

    @pl.when(i == pl.num_programs(0) - 1)
    def _():
        wait_tile(1 - slot)


def _combine(dest_flat, h1t, tw, g2, b2, ys):
    n_tok = h1t.shape[0] // ROW_TILE
    d = ROW_TILE * LANES
    tm = TM_COMBINE
    n_steps = n_tok // tm
    const = lambda i: (0, 0)
    return pl.pallas_call(
        _combine_body, grid=(n_steps,),
        in_specs=[pl.BlockSpec((TOP_K * tm,), lambda i: (i,), memory_space=pltpu.SMEM),
                  pl.BlockSpec((TOP_K * tm,), lambda i: (jnp.minimum(i + 1, n_steps - 1),),
                               memory_space=pltpu.SMEM),
                  pl.BlockSpec((tm * ROW_TILE, LANES), lambda i: (i, 0)),
                  pl.BlockSpec((tm, LANES), lambda i: (i, 0)),
                  pl.BlockSpec((1, d), const), pl.BlockSpec((1, d), const),
                  pl.BlockSpec(memory_space=pl.ANY)],
        out_specs=pl.BlockSpec((tm, d), lambda i: (i, 0)),
        out_shape=jax.ShapeDtypeStruct((n_tok, d), F32),
        scratch_shapes=[pltpu.VMEM((2, TOP_K, tm * ROW_TILE, LANES), F32), pltpu.SemaphoreType.DMA((2,))],
        name="combine",
        compiler_params=pltpu.CompilerParams(
            dimension_semantics=("arbitrary",), vmem_limit_bytes=40 * 1024 * 1024),
    )(dest_flat, dest_flat, h1t, tw, g2, b2, ys)


def _rope_tables(seq):
    t = np.arange(seq)
    row = (t // GRID_W).astype(np.float32)
    col = (t % GRID_W).astype(np.float32)
    half = HEAD_DIM // 2
    quarter = half // 2
    inv = (ROPE_THETA ** (-np.arange(quarter, dtype=np.float32) * np.float32(2.0 / half))).astype(np.float32)
    ang_r = row[:, None] * inv[None, :]
    ang_c = col[:, None] * inv[None, :]
    zeros = np.zeros_like(ang_r)
    cos = np.concatenate([np.cos(ang_r), np.cos(ang_r), np.cos(ang_c), np.cos(ang_c)], -1)
    s_lo = np.concatenate([-np.sin(ang_r), zeros, -np.sin(ang_c), zeros], -1)
    s_hi = np.concatenate([zeros, np.sin(ang_r), zeros, np.sin(ang_c)], -1)
    return tuple(jnp.asarray(np.tile(a, (1, LANES // HEAD_DIM)), F32) for a in (cos, s_lo, s_hi))


def _routing(top_i, rank, counts, bm):
    n_tok = top_i.shape[0]
    padded = (counts + bm - 1) // bm * bm
    pad_end = jnp.cumsum(padded)
    pad_start = pad_end - padded
    base = jnp.sum(jnp.where(top_i[:, :, None] == jnp.arange(N_EXPERTS, dtype=jnp.int32)[None, None, :],
                             pad_start[None, None, :], 0), axis=-1)
    dest = (base + rank).astype(jnp.int32)
    n_rows = n_tok * TOP_K + N_EXPERTS * bm
    n_blocks = n_rows // bm
    block_start = jnp.arange(n_blocks, dtype=jnp.int32) * bm
    block_e = jnp.minimum(jnp.sum((pad_end[None, :] <= block_start[:, None]).astype(jnp.int32), axis=1),
                          N_EXPERTS - 1).astype(jnp.int32)
    n_used = (pad_end[-1] // bm).astype(jnp.int32).reshape(1)
    e_ids = jnp.arange(N_EXPERTS, dtype=jnp.int32)
    later = (e_ids[None, :] > e_ids[:, None]) & (counts[None, :] > 0)
    next_of = jnp.min(jnp.where(later, e_ids[None, :], N_EXPERTS), axis=1)
    next_of = jnp.where(next_of == N_EXPERTS, -1, next_of).astype(jnp.int32)
    next_e = jnp.sum(jnp.where(block_e[:, None] == e_ids[None, :], next_of[None, :], 0), axis=1).astype(jnp.int32)
    start_of = jnp.sum(jnp.where(block_e[:, None] == e_ids[None, :], pad_start[None, :], 0), axis=1)
    count_of = jnp.sum(jnp.where(block_e[:, None] == e_ids[None, :], counts[None, :], 0), axis=1)
    n_valid = jnp.clip(count_of - (block_start - start_of), 0, bm).astype(jnp.int32)
    return (dest.reshape(-1), block_e, n_used, next_e, n_valid, n_rows, pad_end.astype(jnp.int32),
            padded.astype(jnp.int32))


def kernel(x, ln0_g, ln0_b, w_in, a_sink, b_q_norm, b_k_norm, w_branch_a, w_branch_b, w_out,
           ln1_g, ln1_b, w_router, b_router, w_gate, b_gate, w_up, b_up, w_down, b_down,
           ln2_g, ln2_b):
    bsz, seq, d = x.shape
    assert w_in.shape[0] == DEPTH == 1
    assert seq % TM_PROJ == 0 and seq % TQ_GRID == 0 and seq == (seq // GRID_W) * GRID_W
    n_tok = bsz * seq
    row2 = lambda a: a.reshape(1, -1)

    o_ka, o_va, o_qb, o_kb, o_vb, o_g = Q_W, Q_W + KV_W, Q_W + 2 * KV_W, 2 * Q_W + 2 * KV_W, \
        2 * Q_W + 3 * KV_W, 2 * Q_W + 4 * KV_W
    col = np.arange(w_in.shape[-1])
    col_perm = np.concatenate([
        col[:Q_W][_PAIRED], col[o_qb:o_kb][_PAIRED], col[o_ka:o_va], col[o_kb:o_vb],
        col[o_va:o_qb], col[o_vb:o_g], col[o_g:]])
    w_perm = w_in[0].astype(BF16)[:, col_perm]
    head_id = np.arange(Q_W) // HEAD_DIM
    bd = jnp.asarray(head_id[:, None] == head_id[None, :], BF16)
    gq = jnp.tile(b_q_norm[0].astype(F32), N_HEADS).reshape(1, Q_W)
    gk = jnp.tile(b_k_norm[0].astype(F32), N_KV).reshape(1, KV_W)

    qa, ka, vat, qb, kb, vbt, sga, sgb = _in_proj(
        x, row2(ln0_g), row2(ln0_b), w_perm, bd, gq, gk, _rope_tables(seq))
    oa = _win_attn(a_sink[0].astype(F32), qa, ka, vat)
    ob = _grid_attn(qb, kb, vbt)

    wr = jnp.zeros((LANES, d), BF16).at[:N_EXPERTS].set(w_router[0].T.astype(BF16))
    br = jnp.zeros((LANES, 1), F32).at[:N_EXPERTS, 0].set(b_router[0])
    h1t, tr, tw, cnt = _post_attn(
        x, oa, ob, sga, sgb, row2(ln0_g), row2(ln0_b),
        w_branch_a[0][_PAIRED].astype(BF16), w_branch_b[0][_PAIRED].astype(BF16),
        w_out[0].astype(BF16), row2(ln1_g[0]), row2(ln1_b[0]), wr, br)
    h1t = h1t.reshape(n_tok * ROW_TILE, LANES)
    top_i, rank = tr[:TOP_K].T, tr[TOP_K:].T
    tw = tw.reshape(n_tok, LANES)

    counts = cnt[:, 0].astype(jnp.int32)
    dest, block_e, n_used, next_e, n_valid, n_rows, pad_end, padded = _routing(top_i, rank, counts, BM_EXPERT)
    xs = _dispatch(dest, pad_end, padded, n_used, h1t, n_rows)
    ys = _experts(block_e, n_used, next_e, n_valid, xs, w_gate[0], b_gate[0], w_up[0], b_up[0], w_down[0], b_down[0])
    out = _combine(dest, h1t, tw, row2(ln2_g[0]), row2(ln2_b[0]), ys)
    return out.reshape(bsz, seq, d)
```

```python
import functools

import jax
import jax.numpy as jnp
import numpy as np
from jax import lax
from jax.experimental import pallas as pl
from jax.experimental.pallas import tpu as pltpu

HEAD_DIM = 64
N_HEADS = 8
N_KV = 2
WINDOW = 128
BLOCK = 128
GRID_W = 64
ROPE_THETA = 10000.0
N_EXPERTS = 32
TOP_K = 4
SWIGLU_LIMIT = 7.0
SWIGLU_ALPHA = 1.702
LN_EPS = 1e-5
RMS_EPS = 1e-6
NEG_INF = -1e30
DEPTH = 1
DN_ALPHA = (2.0 * DEPTH) ** 0.25
ALIBI_SLOPES = tuple(2.0 ** (-8.0 * (h + 1) / N_HEADS) for h in range(N_HEADS))
QK_SCALE = HEAD_DIM ** -0.5
LOG2_E = 1.4426950408889634

LANES = 128
ROW_TILE = 8
Q_W = N_HEADS * HEAD_DIM
KV_W = N_KV * HEAD_DIM

TM_PROJ = 512
TM_IN_PROJ = 1024
PROJ_ROW_GROUPS = 4
WIN_BLOCKS = 8
TQ_GRID = 256
KEY_CHUNK = 256
BM_EXPERT = 512
TM_DISPATCH = 256
TM_COMBINE = 256
COMBINE_ROW_CHUNK = 32

F32 = jnp.float32
BF16 = jnp.bfloat16


def _ln(x, g, b):
    mu = jnp.mean(x, -1, keepdims=True)
    xc = x - mu
    var = jnp.mean(xc * xc, -1, keepdims=True)
    return xc * lax.rsqrt(var + LN_EPS) * g + b


def _dot(a, b):
    return jnp.dot(a, b, preferred_element_type=F32)


def _dot_nt(a, b):
    return lax.dot_general(a, b, (((1,), (1,)), ((), ())), preferred_element_type=F32)


def _load_row_tiles(ref, rows, first=0):
    return jnp.concatenate(
        [ref[pl.ds(first * ROW_TILE + c, rows, stride=ROW_TILE), :] for c in range(ROW_TILE)], axis=1)


def _store_row_tiles(ref, val, rows):
    for c in range(ROW_TILE):
        ref[pl.ds(c, rows, stride=ROW_TILE), :] = val[:, c * LANES:(c + 1) * LANES]


def _tile(ref, row):
    start = row * ROW_TILE
    if not isinstance(row, int):
        start = pl.multiple_of(start, ROW_TILE)
    return ref.at[pl.ds(start, ROW_TILE)]


def _in_proj_body(x_ref, g0_ref, b0_ref, w_ref, bd_ref, gq_ref, gk_ref, c_ref, s1_ref, s2_ref,
                  qa_ref, ka_ref, vat_ref, qb_ref, kb_ref, vbt_ref, sga_ref, sgb_ref):
    tm = x_ref.shape[1]
    d = sga_ref.shape[-1]
    o_kva, o_qb, o_kvb, o_g = Q_W, Q_W + 2 * KV_W, 2 * Q_W + 2 * KV_W, 2 * Q_W + 4 * KV_W
    both = lambda k: jnp.concatenate([k, pltpu.roll(k, HEAD_DIM, 1)], axis=1).astype(BF16)
    rg = tm // PROJ_ROW_GROUPS
    for grp in range(PROJ_ROW_GROUPS):
        rows = slice(grp * rg, (grp + 1) * rg)
        hb = _ln(x_ref[0, rows, :], g0_ref[...], b0_ref[...]).astype(BF16)

        def proj(lo, hi):
            return _dot(hb, w_ref[:, lo:hi])

        def norm_rope(t, width, g_ref):
            ss = _dot((t * t).astype(BF16), bd_ref[:width, :width])
            r = lax.rsqrt(ss * (1.0 / HEAD_DIM) + RMS_EPS)
            reps = width // LANES
            tab = lambda ref: jnp.concatenate([ref[rows, :]] * reps, axis=1)
            y = t * g_ref[...]
            rot = (y * tab(c_ref) + pltpu.roll(y, width - 16, 1) * tab(s1_ref)
                   + pltpu.roll(y, 16, 1) * tab(s2_ref))
            return rot * r

        qa_ref[0, rows, :] = (proj(0, o_kva) * (QK_SCALE * LOG2_E)).astype(BF16)
        kva = proj(o_kva, o_qb)
        ka_ref[0, rows, :] = both(kva[:, :KV_W])
        vat_ref[0, :, rows] = kva[:, KV_W:].T.astype(BF16)
        qb = norm_rope(proj(o_qb, o_kvb), Q_W, gq_ref)
        qb_ref[0, rows, :] = (qb * (QK_SCALE * LOG2_E)).astype(BF16)
        kvb = proj(o_kvb, o_g)
        kb_ref[0, rows, :] = both(norm_rope(kvb[:, :KV_W], KV_W, gk_ref))
        vbt_ref[0, :, rows] = kvb[:, KV_W:].T.astype(BF16)
        sga_ref[0, rows, :] = jax.nn.sigmoid(proj(o_g, o_g + d)).astype(BF16)
        sgb_ref[0, rows, :] = jax.nn.sigmoid(proj(o_g + d, o_g + 2 * d)).astype(BF16)


def _in_proj(x, g0, b0, w, bd, gq, gk, tabs):
    bsz, seq, d = x.shape
    tm = TM_IN_PROJ
    n_in = w.shape[1]
    const = lambda i, j: (0, 0)
    tok3 = lambda i, j: (j, i, 0)
    tab = lambda i, j: (i, 0)
    in_specs = [
        pl.BlockSpec((1, tm, d), tok3),
        pl.BlockSpec((1, d), const), pl.BlockSpec((1, d), const),
        pl.BlockSpec((d, n_in), const),
        pl.BlockSpec((Q_W, Q_W), const),
        pl.BlockSpec((1, Q_W), const), pl.BlockSpec((1, KV_W), const),
        pl.BlockSpec((tm, LANES), tab), pl.BlockSpec((tm, LANES), tab), pl.BlockSpec((tm, LANES), tab),
    ]
    tr3 = lambda i, j: (j, 0, i)
    out_specs = [
        pl.BlockSpec((1, tm, Q_W), tok3), pl.BlockSpec((1, tm, 2 * KV_W), tok3),
        pl.BlockSpec((1, KV_W, tm), tr3),
        pl.BlockSpec((1, tm, Q_W), tok3), pl.BlockSpec((1, tm, 2 * KV_W), tok3),
        pl.BlockSpec((1, KV_W, tm), tr3),
        pl.BlockSpec((1, tm, d), tok3), pl.BlockSpec((1, tm, d), tok3),
    ]
    out_shape = [
        jax.ShapeDtypeStruct((bsz, seq, Q_W), BF16), jax.ShapeDtypeStruct((bsz, seq, 2 * KV_W), BF16),
        jax.ShapeDtypeStruct((bsz, KV_W, seq), BF16),
        jax.ShapeDtypeStruct((bsz, seq, Q_W), BF16), jax.ShapeDtypeStruct((bsz, seq, 2 * KV_W), BF16),
        jax.ShapeDtypeStruct((bsz, KV_W, seq), BF16),
        jax.ShapeDtypeStruct((bsz, seq, d), BF16), jax.ShapeDtypeStruct((bsz, seq, d), BF16),
    ]
    return pl.pallas_call(
        _in_proj_body, grid=(seq // tm, bsz), in_specs=in_specs, out_specs=out_specs,
        out_shape=out_shape, name="in_proj",
        compiler_params=pltpu.CompilerParams(
            dimension_semantics=("arbitrary", "arbitrary"), vmem_limit_bytes=48 * 1024 * 1024),
    )(x, g0, b0, w, bd, gq, gk, *tabs)


def _half_mask(rows, c):
    lane = lax.broadcasted_iota(jnp.int32, (rows, LANES), 1)
    return (lane >= HEAD_DIM) if c == 1 else (lane < HEAD_DIM)


def _win_attn_body(sink_ref, q_ref, *refs, seq):
    nk = WIN_BLOCKS + 2
    k_refs, v_refs, o_ref = refs[:nk], refs[nk:2 * nk], refs[2 * nk]
    kk = lax.broadcasted_iota(jnp.int32, (3 * BLOCK, BLOCK), 0)
    qq = lax.broadcasted_iota(jnp.int32, (3 * BLOCK, BLOCK), 1)
    dist_i = jnp.abs(kk - BLOCK - qq)
    dist = dist_i.astype(F32)
    ones = jnp.ones((2 * ROW_TILE, 3 * BLOCK), BF16)
    for blk in range(WIN_BLOCKS):
        n = pl.program_id(1) * WIN_BLOCKS + blk
        rows = slice(blk * BLOCK, (blk + 1) * BLOCK)
        k2 = jnp.concatenate([r[0] for r in k_refs[blk:blk + 3]], axis=0)
        vt = jnp.concatenate([r[0] for r in v_refs[blk:blk + 3]], axis=1)
        k_pos = n * BLOCK - BLOCK + kk
        valid = (dist_i <= WINDOW) & (k_pos >= 0) & (k_pos < seq)
        slabs = [q_ref[0, rows, j * LANES:(j + 1) * LANES] for j in range(4)]
        variant = [[h for h in range(N_HEADS) if (h // 4 == h % 2) == straight] for straight in (True, False)]
        scores = {}
        for v, heads in enumerate(variant):
            qm = jnp.concatenate([jnp.where(_half_mask(BLOCK, h % 2), slabs[h // 2], jnp.zeros_like(slabs[0]))
                                  for h in heads], axis=0)
            st_v = _dot_nt(k2[:, v * LANES:(v + 1) * LANES], qm)
            for col, h in enumerate(heads):
                scores[h] = st_v[:, col * BLOCK:(col + 1) * BLOCK]
        outs = {}
        for c in range(N_KV):
            heads = range(c * (N_HEADS // N_KV), (c + 1) * (N_HEADS // N_KV))
            ps, sinks = [], []
            for h in heads:
                st = scores[h] + jnp.where(valid, (-ALIBI_SLOPES[h] * LOG2_E) * dist, NEG_INF)
                sk = sink_ref[h] * LOG2_E
                m = jnp.maximum(jnp.max(st, axis=0, keepdims=True), sk)
                ps.append(jnp.exp2((st - m).astype(BF16)))
                sinks.append(jnp.exp2(sk - m))
            va = jnp.concatenate([vt[c * HEAD_DIM:(c + 1) * HEAD_DIM, :], ones], axis=0)
            ot = _dot(va, jnp.concatenate(ps, axis=1))
            ot = ot[:HEAD_DIM] / (ot[HEAD_DIM:HEAD_DIM + 1] + jnp.concatenate(sinks, axis=1))
            for col, h in enumerate(heads):
                outs[h] = ot[:, col * BLOCK:(col + 1) * BLOCK]
        for j in range(4):
            pair = jnp.concatenate([outs[2 * j], outs[2 * j + 1]], axis=0)
            o_ref[0, rows, j * LANES:(j + 1) * LANES] = pair.T.astype(BF16)


def _win_attn(sink, qa, ka, vat):
    bsz, seq, _ = qa.shape
    nb = seq // BLOCK
    wb = WIN_BLOCKS
    qmap = lambda b, n: (b, n, 0)
    blk = lambda off: (lambda n: jnp.clip(n * wb + off, 0, nb - 1))
    kspec = lambda f: pl.BlockSpec((1, BLOCK, 2 * KV_W), lambda b, n: (b, f(n), 0))
    vspec = lambda f: pl.BlockSpec((1, KV_W, BLOCK), lambda b, n: (b, 0, f(n)))
    offs = range(-1, wb + 1)
    return pl.pallas_call(
        functools.partial(_win_attn_body, seq=seq), grid=(bsz, nb // wb),
        in_specs=[pl.BlockSpec(memory_space=pltpu.SMEM), pl.BlockSpec((1, wb * BLOCK, Q_W), qmap)]
        + [kspec(blk(o)) for o in offs] + [vspec(blk(o)) for o in offs],
        out_specs=pl.BlockSpec((1, wb * BLOCK, Q_W), qmap),
        out_shape=jax.ShapeDtypeStruct((bsz, seq, Q_W), BF16), name="win_attn",
        compiler_params=pltpu.CompilerParams(dimension_semantics=("arbitrary", "arbitrary")),
    )(sink, qa, *([ka] * (wb + 2)), *([vat] * (wb + 2)))


def _grid_attn_body(q_ref, k_ref, vt_ref, o_ref, s0_ref, s1_ref, p0_ref, p1_ref):
    tq = q_ref.shape[1]
    seq = k_ref.shape[1]
    kc = KEY_CHUNK
    n_chunks = seq // kc
    s_bufs, p_bufs = (s0_ref, s1_ref), (p0_ref, p1_ref)
    n_heads = N_HEADS
    ones = jnp.ones((2 * ROW_TILE, seq), BF16)

    def masked_q(h):
        slab = q_ref[0, :, (h // 2) * LANES:(h // 2 + 1) * LANES]
        return jnp.where(_half_mask(tq, h % 2), slab, jnp.zeros_like(slab))

    def score_chunk(h, qm, kb, m8):
        v = 0 if h // 4 == h % 2 else 1
        sc = _dot_nt(k_ref[0, kb * kc:(kb + 1) * kc, v * LANES:(v + 1) * LANES], qm)
        s_bufs[h % 2][kb * kc:(kb + 1) * kc, :] = sc
        cm = jnp.max(sc.reshape(kc // ROW_TILE, ROW_TILE, tq), axis=0)
        return cm if m8 is None else jnp.maximum(m8, cm)

    def prob_chunk(h, kb, m):
        x = s_bufs[h % 2][kb * kc:(kb + 1) * kc, :] - m
        p_bufs[h % 2][kb * kc:(kb + 1) * kc, :] = jnp.exp2(x.astype(BF16))

    qm = masked_q(0)
    m8 = None
    for kb in range(n_chunks):
        m8 = score_chunk(0, qm, kb, m8)
    outs = []
    for h in range(n_heads):
        m = jnp.max(m8, axis=0, keepdims=True)
        nxt = h + 1 < n_heads
        if nxt:
            qm = masked_q(h + 1)
            m8 = None
        for kb in range(n_chunks):
            if nxt:
                m8 = score_chunk(h + 1, qm, kb, m8)
            prob_chunk(h, kb, m)
        c = h // (n_heads // N_KV)
        va = jnp.concatenate([vt_ref[0, c * HEAD_DIM:(c + 1) * HEAD_DIM, :], ones], axis=0)
        ot = _dot(va, p_bufs[h % 2][...])
        outs.append(ot[:HEAD_DIM] / ot[HEAD_DIM:HEAD_DIM + 1])
        if h % 2 == 1:
            o_ref[0, :, (h // 2) * LANES:(h // 2 + 1) * LANES] = jnp.concatenate(outs, axis=0).T.astype(BF16)
            outs = []


def _grid_attn(qb, kb, vbt):
    bsz, seq, _ = qb.shape
    tq = TQ_GRID
    return pl.pallas_call(
        _grid_attn_body, grid=(bsz, seq // tq),
        in_specs=[pl.BlockSpec((1, tq, Q_W), lambda b, n: (b, n, 0)),
                  pl.BlockSpec((1, seq, 2 * KV_W), lambda b, n: (b, 0, 0)),
                  pl.BlockSpec((1, KV_W, seq), lambda b, n: (b, 0, 0))],
        out_specs=pl.BlockSpec((1, tq, Q_W), lambda b, n: (b, n, 0)),
        scratch_shapes=[pltpu.VMEM((seq, tq), F32), pltpu.VMEM((seq, tq), F32),
                        pltpu.VMEM((seq, tq), BF16), pltpu.VMEM((seq, tq), BF16)],
        out_shape=jax.ShapeDtypeStruct((bsz, seq, Q_W), BF16), name="grid_attn",
        compiler_params=pltpu.CompilerParams(
            dimension_semantics=("arbitrary", "arbitrary"), vmem_limit_bytes=40 * 1024 * 1024),
    )(qb, kb, vbt)


def _post_attn_body(x_ref, oa_ref, ob_ref, sga_ref, sgb_ref, g0_ref, b0_ref, wa_ref, wb_ref, wo_ref,
                    g1_ref, b1_ref, wr_ref, br_ref, h1t_ref, tr_ref, tw_ref, cnt_out_ref, cnt_ref):
    h0 = _ln(x_ref[0], g0_ref[...], b0_ref[...])
    out_a = _dot(oa_ref[0], wa_ref[...])
    out_b = _dot(ob_ref[0], wb_ref[...])
    merged = sga_ref[0].astype(F32) * out_a + sgb_ref[0].astype(F32) * out_b
    mix = _dot(merged.astype(BF16), wo_ref[...])
    h1 = _ln(DN_ALPHA * h0 + mix, g1_ref[...], b1_ref[...])
    tm = h1.shape[0]
    _store_row_tiles(h1t_ref.at[0], h1, tm)

    logits = (_dot_nt(wr_ref[...], h1.astype(BF16)) + br_ref[...])[:N_EXPERTS]
    sub = lax.broadcasted_iota(jnp.int32, (N_EXPERTS, tm), 0)
    cur = logits
    vals, idxs = [], []
    for _ in range(TOP_K):
        mv = jnp.max(cur, axis=0, keepdims=True)
        ix = jnp.min(jnp.where(cur == mv, sub, N_EXPERTS), axis=0, keepdims=True)
        vals.append(mv)
        idxs.append(ix)
        cur = jnp.where(sub == ix, -jnp.inf, cur)
    es = [jnp.exp(v - vals[0]) for v in vals]
    tot = es[0] + es[1] + es[2] + es[3]

    @pl.when((pl.program_id(0) == 0) & (pl.program_id(1) == 0))
    def _():
        cnt_ref[...] = jnp.zeros_like(cnt_ref)

    sel = jnp.zeros((N_EXPERTS, tm), F32)
    for kx in range(TOP_K):
        sel = sel + (sub == idxs[kx]).astype(F32)
    r_i = lax.broadcasted_iota(jnp.int32, (tm, tm), 0)
    c_i = lax.broadcasted_iota(jnp.int32, (tm, tm), 1)
    tri = (r_i < c_i).astype(BF16)
    rank = _dot(sel.astype(BF16), tri) + cnt_ref[:, 0:1]
    cnt_ref[...] = cnt_ref[...] + jnp.sum(sel, axis=1, keepdims=True)
    cnt_out_ref[...] = cnt_ref[...]

    rks = [jnp.sum(jnp.where(sub == ix, rank, 0.0), axis=0, keepdims=True).astype(jnp.int32) for ix in idxs]
    tr_ref[...] = jnp.concatenate(idxs + rks, axis=0)
    tw_t = jnp.concatenate([e / tot for e in es] + [jnp.zeros((LANES - TOP_K, tm), F32)], axis=0)
    tw_ref[0] = tw_t.T


def _post_attn(x, oa, ob, sga, sgb, g0, b0, wa, wb, wo, g1, b1, wr, br):
    bsz, seq, d = x.shape
    tm = TM_PROJ
    tok3 = lambda b, i: (b, i, 0)
    const = lambda b, i: (0, 0)
    full = lambda a: pl.BlockSpec(a.shape, const)
    return pl.pallas_call(
        _post_attn_body, grid=(bsz, seq // tm),
        in_specs=[pl.BlockSpec((1, tm, d), tok3),
                  pl.BlockSpec((1, tm, Q_W), tok3), pl.BlockSpec((1, tm, Q_W), tok3),
                  pl.BlockSpec((1, tm, d), tok3), pl.BlockSpec((1, tm, d), tok3),
                  full(g0), full(b0), full(wa), full(wb), full(wo), full(g1), full(b1),
                  full(wr), full(br)],
        out_specs=[pl.BlockSpec((1, tm * ROW_TILE, LANES), tok3),
                   pl.BlockSpec((2 * TOP_K, tm), lambda b, i: (0, b * (seq // tm) + i)),
                   pl.BlockSpec((1, tm, LANES), tok3), pl.BlockSpec((N_EXPERTS, LANES), const)],
        out_shape=[jax.ShapeDtypeStruct((bsz, seq * ROW_TILE, LANES), F32),
                   jax.ShapeDtypeStruct((2 * TOP_K, bsz * seq), jnp.int32),
                   jax.ShapeDtypeStruct((bsz, seq, LANES), F32),
                   jax.ShapeDtypeStruct((N_EXPERTS, LANES), F32)],
        scratch_shapes=[pltpu.VMEM((N_EXPERTS, LANES), F32)],
        name="post_attn",
        compiler_params=pltpu.CompilerParams(
            dimension_semantics=("arbitrary", "arbitrary"), vmem_limit_bytes=48 * 1024 * 1024),
    )(x, oa, ob, sga, sgb, g0, b0, wa, wb, wo, g1, b1, wr, br)


def _dispatch_body(dest_ref, pend_ref, padded_ref, nu_ref, h1t_ref, xs_hbm, zbuf, sem, zsem):
    tm = TM_DISPATCH
    zrows = BM_EXPERT * ROW_TILE
    n_blocks = xs_hbm.shape[0] // zrows

    @pl.when(pl.program_id(0) == 0)
    def _():
        zbuf[...] = jnp.zeros_like(zbuf)
        zero_wait = pltpu.make_async_copy(zbuf, xs_hbm.at[pl.ds(0, zrows)], zsem).wait
        for e in range(N_EXPERTS):
            @pl.when(padded_ref[e] > 0)
            def _():
                start = pl.multiple_of((pend_ref[e] - BM_EXPERT) * ROW_TILE, ROW_TILE)
                pltpu.make_async_copy(zbuf, xs_hbm.at[pl.ds(start, zrows)], zsem).start()
        for b in range(n_blocks - N_EXPERTS, n_blocks):
            @pl.when(b >= nu_ref[0])
            def _():
                pltpu.make_async_copy(zbuf, xs_hbm.at[pl.ds(b * zrows, zrows)], zsem).start()
        for e in range(N_EXPERTS):
            pl.when(padded_ref[e] > 0)(zero_wait)
        for b in range(n_blocks - N_EXPERTS, n_blocks):
            pl.when(b >= nu_ref[0])(zero_wait)

    for t in range(tm):
        for kx in range(TOP_K):
            d = dest_ref[TOP_K * t + kx]
            pltpu.make_async_copy(_tile(h1t_ref, t), _tile(xs_hbm, d), sem).start(priority=kx % 2)
    for _ in range(TOP_K):
        pltpu.make_async_copy(h1t_ref, xs_hbm.at[pl.ds(0, tm * ROW_TILE)], sem).wait()


def _dispatch(dest_flat, pad_end, padded, n_used, h1t, n_rows):
    n_tok = h1t.shape[0] // ROW_TILE
    tm = TM_DISPATCH
    return pl.pallas_call(
        _dispatch_body, grid=(n_tok // tm,),
        in_specs=[pl.BlockSpec((TOP_K * tm,), lambda i: (i,), memory_space=pltpu.SMEM),
                  pl.BlockSpec(memory_space=pltpu.SMEM), pl.BlockSpec(memory_space=pltpu.SMEM),
                  pl.BlockSpec(memory_space=pltpu.SMEM),
                  pl.BlockSpec((tm * ROW_TILE, LANES), lambda i: (i, 0))],
        out_specs=pl.BlockSpec(memory_space=pl.ANY),
        out_shape=jax.ShapeDtypeStruct((n_rows * ROW_TILE, LANES), F32),
        scratch_shapes=[pltpu.VMEM((BM_EXPERT * ROW_TILE, LANES), F32),
                        pltpu.SemaphoreType.DMA(()), pltpu.SemaphoreType.DMA(())],
        name="dispatch",
        compiler_params=pltpu.CompilerParams(dimension_semantics=("arbitrary",)),
    )(dest_flat, pad_end, padded, n_used, h1t)


def _experts_body(be_ref, nu_ref, nx_ref, nv_ref, xs_ref, bg_ref, bu_ref, bd_ref, wg_hbm, wu_hbm, wd_hbm, ys_ref,
                  stage, wg_s, wu_s, wd_s, wsem):
    i = pl.program_id(0)
    bm = BM_EXPERT
    used = i < nu_ref[0]
    prev = be_ref[jnp.maximum(i - 1, 0)]
    fresh = (i == 0) | (be_ref[i] != prev)

    def weight_copies(e):
        return [pltpu.make_async_copy(w_hbm.at[e], stage.at[n], wsem)
                for n, w_hbm in enumerate((wg_hbm, wu_hbm, wd_hbm))]

    @pl.when(i == 0)
    def _():
        for cp in weight_copies(be_ref[0]):
            cp.start()

    @pl.when(used & fresh)
    def _():
        for cp in weight_copies(be_ref[i]):
            cp.wait()
        wg_s[...] = stage[0].astype(BF16)
        wu_s[...] = stage[1].astype(BF16)
        wd_s[...] = stage[2].astype(BF16)

        @pl.when(nx_ref[i] >= 0)
        def _():
            for cp in weight_copies(nx_ref[i]):
                cp.start()

    def ffn(rows):
        xb = _load_row_tiles(xs_ref, rows).astype(BF16)
        g = _dot(xb, wg_s[...]) + bg_ref[0]
        u = _dot(xb, wu_s[...]) + bu_ref[0]
        g = jnp.minimum(g, SWIGLU_LIMIT)
        u = jnp.clip(u, -SWIGLU_LIMIT, SWIGLU_LIMIT)
        act = g * jax.nn.sigmoid(SWIGLU_ALPHA * g) * (u + 1.0)
        _store_row_tiles(ys_ref, _dot(act.astype(BF16), wd_s[...]) + bd_ref[0], rows)

    half = bm // 2
    full = used & (nv_ref[i] > half)

    @pl.when(full)
    def _():
        ffn(bm)

    @pl.when(used & jnp.logical_not(full))
    def _():
        ffn(half)
        ys_ref[pl.ds(half * ROW_TILE, half * ROW_TILE), :] = jnp.zeros((half * ROW_TILE, LANES), F32)

    @pl.when(jnp.logical_not(used))
    def _():
        ys_ref[...] = jnp.zeros_like(ys_ref)


def _experts(block_e, n_used, next_e, n_valid, xs, wg, bg, wu, bu, wd, bd):
    n_rows = xs.shape[0] // ROW_TILE
    bm = BM_EXPERT
    n_e, d, d_ff = wg.shape
    assert d == d_ff and wd.shape == wg.shape
    row = lambda i, be, nu, nx, nv: (jnp.minimum(i, nu[0] - 1), 0)
    exp3 = lambda i, be, nu, nx, nv: (be[jnp.minimum(i, nu[0] - 1)], 0, 0)
    any_spec = pl.BlockSpec(memory_space=pl.ANY)
    grid_spec = pltpu.PrefetchScalarGridSpec(
        num_scalar_prefetch=4, grid=(n_rows // bm,),
        in_specs=[pl.BlockSpec((bm * ROW_TILE, LANES), row),
                  pl.BlockSpec((1, 1, d_ff), exp3), pl.BlockSpec((1, 1, d_ff), exp3),
                  pl.BlockSpec((1, 1, d), exp3), any_spec, any_spec, any_spec],
        out_specs=pl.BlockSpec((bm * ROW_TILE, LANES), lambda i, be, nu, nx, nv: (i, 0)),
        scratch_shapes=[pltpu.VMEM((3, d, d_ff), F32),
                        pltpu.VMEM((d, d_ff), BF16), pltpu.VMEM((d, d_ff), BF16), pltpu.VMEM((d_ff, d), BF16),
                        pltpu.SemaphoreType.DMA(())])
    return pl.pallas_call(
        _experts_body, grid_spec=grid_spec,
        out_shape=jax.ShapeDtypeStruct((n_rows * ROW_TILE, LANES), F32), name="experts",
        compiler_params=pltpu.CompilerParams(
            dimension_semantics=("arbitrary",), vmem_limit_bytes=48 * 1024 * 1024),
    )(block_e, n_used, next_e, n_valid, xs, bg.reshape(n_e, 1, d_ff), bu.reshape(n_e, 1, d_ff), bd.reshape(n_e, 1, d),
      wg, wu, wd)


def _combine_body(dest_ref, dest_next_ref, h1t_ref, tw_ref, g2_ref, b2_ref, ys_hbm, out_ref, buf, sems):
    tm = TM_COMBINE
    i = pl.program_id(0)
    slot = i % 2

    def start_row(d_ref, s, t):
        for kx in range(TOP_K):
            d = d_ref[TOP_K * t + kx]
            pltpu.make_async_copy(_tile(ys_hbm, d), _tile(buf.at[s, kx], t), sems.at[s]).start(priority=kx % 2)

    def wait_tile(s):
        for kx in range(TOP_K):
            pltpu.make_async_copy(ys_hbm.at[pl.ds(0, tm * ROW_TILE)], buf.at[s, kx], sems.at[s]).wait()

    @pl.when(i == 0)
    def _():
        lax.fori_loop(0, tm, lambda t, c: (start_row(dest_ref, 0, t), c)[1], 0)

    wait_tile(slot)
    rc = COMBINE_ROW_CHUNK
    for c in range(tm // rc):
        for t in range(c * rc, (c + 1) * rc):
            start_row(dest_next_ref, 1 - slot, t)
        tw = tw_ref[c * rc:(c + 1) * rc, :]
        ffn = tw[:, 0:1] * _load_row_tiles(buf.at[slot, 0], rc, c * rc)
        for kx in range(1, TOP_K):
            ffn = ffn + tw[:, kx:kx + 1] * _load_row_tiles(buf.at[slot, kx], rc, c * rc)
        h1 = _load_row_tiles(h1t_ref, rc, c * rc)
        out_ref[c * rc:(c + 1) * rc, :] = _ln(DN_ALPHA * h1 + ffn, g2_ref[...], b2_ref[...])

    @pl.when(i == pl.num_programs(0) - 1)
    def _():
        wait_tile(1 - slot)


def _combine(dest_flat, h1t, tw, g2, b2, ys):
    n_tok = h1t.shape[0] // ROW_TILE
    d = ROW_TILE * LANES
    tm = TM_COMBINE
    n_steps = n_tok // tm
    const = lambda i: (0, 0)
    return pl.pallas_call(
        _combine_body, grid=(n_steps,),
        in_specs=[pl.BlockSpec((TOP_K * tm,), lambda i: (i,), memory_space=pltpu.SMEM),
                  pl.BlockSpec((TOP_K * tm,), lambda i: (jnp.minimum(i + 1, n_steps - 1),),
                               memory_space=pltpu.SMEM),
                  pl.BlockSpec((tm * ROW_TILE, LANES), lambda i: (i, 0)),
                  pl.BlockSpec((tm, LANES), lambda i: (i, 0)),
                  pl.BlockSpec((1, d), const), pl.BlockSpec((1, d), const),
                  pl.BlockSpec(memory_space=pl.ANY)],
        out_specs=pl.BlockSpec((tm, d), lambda i: (i, 0)),
        out_shape=jax.ShapeDtypeStruct((n_tok, d), F32),
        scratch_shapes=[pltpu.VMEM((2, TOP_K, tm * ROW_TILE, LANES), F32), pltpu.SemaphoreType.DMA((2,))],
        name="combine",
        compiler_params=pltpu.CompilerParams(
            dimension_semantics=("arbitrary",), vmem_limit_bytes=40 * 1024 * 1024),
    )(dest_flat, dest_flat, h1t, tw, g2, b2, ys)


def _rope_tables(seq):
    t = np.arange(seq)
    row = (t // GRID_W).astype(np.float32)
    col = (t % GRID_W).astype(np.float32)
    half = HEAD_DIM // 2
    quarter = half // 2
    inv = (ROPE_THETA ** (-np.arange(quarter, dtype=np.float32) * np.float32(2.0 / half))).astype(np.float32)
    ang_r = row[:, None] * inv[None, :]
    ang_c = col[:, None] * inv[None, :]
    zeros = np.zeros_like(ang_r)
    cos = np.concatenate([np.cos(ang_r), np.cos(ang_r), np.cos(ang_c), np.cos(ang_c)], -1)
    s_lo = np.concatenate([-np.sin(ang_r), zeros, -np.sin(ang_c), zeros], -1)
    s_hi = np.concatenate([zeros, np.sin(ang_r), zeros, np.sin(ang_c)], -1)
    return tuple(jnp.asarray(np.tile(a, (1, LANES // HEAD_DIM)), F32) for a in (cos, s_lo, s_hi))


def _routing(top_i, rank, counts, bm):
    n_tok = top_i.shape[0]
    padded = (counts + bm - 1) // bm * bm
    pad_end = jnp.cumsum(padded)
    pad_start = pad_end - padded
    base = jnp.sum(jnp.where(top_i[:, :, None] == jnp.arange(N_EXPERTS, dtype=jnp.int32)[None, None, :],
                             pad_start[None, None, :], 0), axis=-1)
    dest = (base + rank).astype(jnp.int32)
    n_rows = n_tok * TOP_K + N_EXPERTS * bm
    n_blocks = n_rows // bm
    block_start = jnp.arange(n_blocks, dtype=jnp.int32) * bm
    block_e = jnp.minimum(jnp.sum((pad_end[None, :] <= block_start[:, None]).astype(jnp.int32), axis=1),
                          N_EXPERTS - 1).astype(jnp.int32)
    n_used = (pad_end[-1] // bm).astype(jnp.int32).reshape(1)
    e_ids = jnp.arange(N_EXPERTS, dtype=jnp.int32)
    later = (e_ids[None, :] > e_ids[:, None]) & (counts[None, :] > 0)
    next_of = jnp.min(jnp.where(later, e_ids[None, :], N_EXPERTS), axis=1)
    next_of = jnp.where(next_of == N_EXPERTS, -1, next_of).astype(jnp.int32)
    next_e = jnp.sum(jnp.where(block_e[:, None] == e_ids[None, :], next_of[None, :], 0), axis=1).astype(jnp.int32)
    start_of = jnp.sum(jnp.where(block_e[:, None] == e_ids[None, :], pad_start[None, :], 0), axis=1)
    count_of = jnp.sum(jnp.where(block_e[:, None] == e_ids[None, :], counts[None, :], 0), axis=1)
    n_valid = jnp.clip(count_of - (block_start - start_of), 0, bm).astype(jnp.int32)
    return (dest.reshape(-1), block_e, n_used, next_e, n_valid, n_rows, pad_end.astype(jnp.int32),
            padded.astype(jnp.int32))


def kernel(x, ln0_g, ln0_b, w_in, a_sink, b_q_norm, b_k_norm, w_branch_a, w_branch_b, w_out,
           ln1_g, ln1_b, w_router, b_router, w_gate, b_gate, w_up, b_up, w_down, b_down,
           ln2_g, ln2_b):
    bsz, seq, d = x.shape
    assert w_in.shape[0] == DEPTH == 1
    assert seq % TM_PROJ == 0 and seq % TQ_GRID == 0 and seq == (seq // GRID_W) * GRID_W
    n_tok = bsz * seq
    row2 = lambda a: a.reshape(1, -1)

    w_perm = w_in[0].astype(BF16)
    head_id = np.arange(Q_W) // HEAD_DIM
    bd = jnp.asarray(head_id[:, None] == head_id[None, :], BF16)
    gq = jnp.tile(b_q_norm[0].astype(F32), N_HEADS).reshape(1, Q_W)
    gk = jnp.tile(b_k_norm[0].astype(F32), N_KV).reshape(1, KV_W)

    qa, ka, vat, qb, kb, vbt, sga, sgb = _in_proj(
        x, row2(ln0_g), row2(ln0_b), w_perm, bd, gq, gk, _rope_tables(seq))
    oa = _win_attn(a_sink[0].astype(F32), qa, ka, vat)
    ob = _grid_attn(qb, kb, vbt)

    wr = jnp.zeros((LANES, d), BF16).at[:N_EXPERTS].set(w_router[0].T.astype(BF16))
    br = jnp.zeros((LANES, 1), F32).at[:N_EXPERTS, 0].set(b_router[0])
    h1t, tr, tw, cnt = _post_attn(
        x, oa, ob, sga, sgb, row2(ln0_g), row2(ln0_b),
        w_branch_a[0].astype(BF16), w_branch_b[0].astype(BF16),
        w_out[0].astype(BF16), row2(ln1_g[0]), row2(ln1_b[0]), wr, br)
    h1t = h1t.reshape(n_tok * ROW_TILE, LANES)
    top_i, rank = tr[:TOP_K].T, tr[TOP_K:].T
    tw = tw.reshape(n_tok, LANES)

    counts = cnt[:, 0].astype(jnp.int32)
    dest, block_e, n_used, next_e, n_valid, n_rows, pad_end, padded = _routing(top_i, rank, counts, BM_EXPERT)
    xs = _dispatch(dest, pad_end, padded, n_used, h1t, n_rows)
    ys = _experts(block_e, n_used, next_e, n_valid, xs, w_gate[0], b_gate[0], w_up[0], b_up[0], w_down[0], b_down[0])
    out = _combine(dest, h1t, tw, row2(ln2_g[0]), row2(ln2_b[0]), ys)
    return out.reshape(bsz, seq, d)
```

```python
import functools

import jax
import jax.numpy as jnp
import numpy as np
from jax import lax
from jax.experimental import pallas as pl
from jax.experimental.pallas import tpu as pltpu

HEAD_DIM = 64
N_HEADS = 8
N_KV = 2
WINDOW = 128
BLOCK = 128
GRID_W = 64
ROPE_THETA = 10000.0
N_EXPERTS = 32
TOP_K = 4
SWIGLU_LIMIT = 7.0
SWIGLU_ALPHA = 1.702
LN_EPS = 1e-5
RMS_EPS = 1e-6
NEG_INF = -1e30
DEPTH = 1
DN_ALPHA = (2.0 * DEPTH) ** 0.25
ALIBI_SLOPES = tuple(2.0 ** (-8.0 * (h + 1) / N_HEADS) for h in range(N_HEADS))
QK_SCALE = HEAD_DIM ** -0.5
LOG2_E = 1.4426950408889634

LANES = 128
ROW_TILE = 8
Q_W = N_HEADS * HEAD_DIM
KV_W = N_KV * HEAD_DIM

TM_PROJ = 512
TM_IN_PROJ = 1024
PROJ_ROW_GROUPS = 2
WIN_BLOCKS = 8
TQ_GRID = 256
KEY_CHUNK = 256
BM_EXPERT = 512
TM_DISPATCH = 256
TM_COMBINE = 256
COMBINE_ROW_CHUNK = 32

F32 = jnp.float32
BF16 = jnp.bfloat16


def _ln(x, g, b):
    mu = jnp.mean(x, -1, keepdims=True)
    xc = x - mu
    var = jnp.mean(xc * xc, -1, keepdims=True)
    return xc * lax.rsqrt(var + LN_EPS) * g + b


def _dot(a, b):
    return jnp.dot(a, b, preferred_element_type=F32)


def _dot_nt(a, b):
    return lax.dot_general(a, b, (((1,), (1,)), ((), ())), preferred_element_type=F32)


def _load_row_tiles(ref, rows, first=0):
    return jnp.concatenate(
        [ref[pl.ds(first * ROW_TILE + c, rows, stride=ROW_TILE), :] for c in range(ROW_TILE)], axis=1)


def _store_row_tiles(ref, val, rows):
    for c in range(ROW_TILE):
        ref[pl.ds(c, rows, stride=ROW_TILE), :] = val[:, c * LANES:(c + 1) * LANES]


def _tile(ref, row):
    start = row * ROW_TILE
    if not isinstance(row, int):
        start = pl.multiple_of(start, ROW_TILE)
    return ref.at[pl.ds(start, ROW_TILE)]


def _in_proj_body(x_ref, g0_ref, b0_ref, w_ref, bd_ref, gq_ref, gk_ref, c_ref, s1_ref, s2_ref,
                  qa_ref, ka_ref, vat_ref, qb_ref, kb_ref, vbt_ref, sga_ref, sgb_ref):
    tm = x_ref.shape[1]
    d = sga_ref.shape[-1]
    o_kva, o_qb, o_kvb, o_g = Q_W, Q_W + 2 * KV_W, 2 * Q_W + 2 * KV_W, 2 * Q_W + 4 * KV_W
    both = lambda k: jnp.concatenate([k, pltpu.roll(k, HEAD_DIM, 1)], axis=1).astype(BF16)
    rg = tm // PROJ_ROW_GROUPS
    for grp in range(PROJ_ROW_GROUPS):
        rows = slice(grp * rg, (grp + 1) * rg)
        hb = _ln(x_ref[0, rows, :], g0_ref[...], b0_ref[...]).astype(BF16)

        def proj(lo, hi):
            return _dot(hb, w_ref[:, lo:hi])

        def norm_rope(t, width, g_ref):
            ss = _dot((t * t).astype(BF16), bd_ref[:width, :width])
            r = lax.rsqrt(ss * (1.0 / HEAD_DIM) + RMS_EPS)
            reps = width // LANES
            tab = lambda ref: jnp.concatenate([ref[rows, :]] * reps, axis=1)
            y = t * g_ref[...]
            rot = (y * tab(c_ref) + pltpu.roll(y, width - 16, 1) * tab(s1_ref)
                   + pltpu.roll(y, 16, 1) * tab(s2_ref))
            return rot * r

        qa_ref[0, rows, :] = (proj(0, o_kva) * (QK_SCALE * LOG2_E)).astype(BF16)
        kva = proj(o_kva, o_qb)
        ka_ref[0, rows, :] = both(kva[:, :KV_W])
        vat_ref[0, :, rows] = kva[:, KV_W:].T.astype(BF16)
        qb = norm_rope(proj(o_qb, o_kvb), Q_W, gq_ref)
        qb_ref[0, rows, :] = (qb * (QK_SCALE * LOG2_E)).astype(BF16)
        kvb = proj(o_kvb, o_g)
        kb_ref[0, rows, :] = both(norm_rope(kvb[:, :KV_W], KV_W, gk_ref))
        vbt_ref[0, :, rows] = kvb[:, KV_W:].T.astype(BF16)
        sga_ref[0, rows, :] = jax.nn.sigmoid(proj(o_g, o_g + d)).astype(BF16)
        sgb_ref[0, rows, :] = jax.nn.sigmoid(proj(o_g + d, o_g + 2 * d)).astype(BF16)


def _in_proj(x, g0, b0, w, bd, gq, gk, tabs):
    bsz, seq, d = x.shape
    tm = TM_IN_PROJ
    n_in = w.shape[1]
    const = lambda i, j: (0, 0)
    tok3 = lambda i, j: (j, i, 0)
    tab = lambda i, j: (i, 0)
    in_specs = [
        pl.BlockSpec((1, tm, d), tok3),
        pl.BlockSpec((1, d), const), pl.BlockSpec((1, d), const),
        pl.BlockSpec((d, n_in), const),
        pl.BlockSpec((Q_W, Q_W), const),
        pl.BlockSpec((1, Q_W), const), pl.BlockSpec((1, KV_W), const),
        pl.BlockSpec((tm, LANES), tab), pl.BlockSpec((tm, LANES), tab), pl.BlockSpec((tm, LANES), tab),
    ]
    tr3 = lambda i, j: (j, 0, i)
    out_specs = [
        pl.BlockSpec((1, tm, Q_W), tok3), pl.BlockSpec((1, tm, 2 * KV_W), tok3),
        pl.BlockSpec((1, KV_W, tm), tr3),
        pl.BlockSpec((1, tm, Q_W), tok3), pl.BlockSpec((1, tm, 2 * KV_W), tok3),
        pl.BlockSpec((1, KV_W, tm), tr3),
        pl.BlockSpec((1, tm, d), tok3), pl.BlockSpec((1, tm, d), tok3),
    ]
    out_shape = [
        jax.ShapeDtypeStruct((bsz, seq, Q_W), BF16), jax.ShapeDtypeStruct((bsz, seq, 2 * KV_W), BF16),
        jax.ShapeDtypeStruct((bsz, KV_W, seq), BF16),
        jax.ShapeDtypeStruct((bsz, seq, Q_W), BF16), jax.ShapeDtypeStruct((bsz, seq, 2 * KV_W), BF16),
        jax.ShapeDtypeStruct((bsz, KV_W, seq), BF16),
        jax.ShapeDtypeStruct((bsz, seq, d), BF16), jax.ShapeDtypeStruct((bsz, seq, d), BF16),
    ]
    return pl.pallas_call(
        _in_proj_body, grid=(seq // tm, bsz), in_specs=in_specs, out_specs=out_specs,
        out_shape=out_shape, name="in_proj",
        compiler_params=pltpu.CompilerParams(
            dimension_semantics=("arbitrary", "arbitrary"), vmem_limit_bytes=48 * 1024 * 1024),
    )(x, g0, b0, w, bd, gq, gk, *tabs)


def _half_mask(rows, c):
    lane = lax.broadcasted_iota(jnp.int32, (rows, LANES), 1)
    return (lane >= HEAD_DIM) if c == 1 else (lane < HEAD_DIM)


def _win_attn_body(sink_ref, q_ref, *refs, seq):
    nk = WIN_BLOCKS + 2
    k_refs, v_refs, o_ref = refs[:nk], refs[nk:2 * nk], refs[2 * nk]
    kk = lax.broadcasted_iota(jnp.int32, (3 * BLOCK, BLOCK), 0)
    qq = lax.broadcasted_iota(jnp.int32, (3 * BLOCK, BLOCK), 1)
    dist_i = jnp.abs(kk - BLOCK - qq)
    dist = dist_i.astype(F32)
    ones = jnp.ones((2 * ROW_TILE, 3 * BLOCK), BF16)
    for blk in range(WIN_BLOCKS):
        n = pl.program_id(1) * WIN_BLOCKS + blk
        rows = slice(blk * BLOCK, (blk + 1) * BLOCK)
        k2 = jnp.concatenate([r[0] for r in k_refs[blk:blk + 3]], axis=0)
        vt = jnp.concatenate([r[0] for r in v_refs[blk:blk + 3]], axis=1)
        k_pos = n * BLOCK - BLOCK + kk
        valid = (dist_i <= WINDOW) & (k_pos >= 0) & (k_pos < seq)
        slabs = [q_ref[0, rows, j * LANES:(j + 1) * LANES] for j in range(4)]
        variant = [[h for h in range(N_HEADS) if (h // 4 == h % 2) == straight] for straight in (True, False)]
        scores = {}
        for v, heads in enumerate(variant):
            qm = jnp.concatenate([jnp.where(_half_mask(BLOCK, h % 2), slabs[h // 2], jnp.zeros_like(slabs[0]))
                                  for h in heads], axis=0)
            st_v = _dot_nt(k2[:, v * LANES:(v + 1) * LANES], qm)
            for col, h in enumerate(heads):
                scores[h] = st_v[:, col * BLOCK:(col + 1) * BLOCK]
        outs = {}
        for c in range(N_KV):
            heads = range(c * (N_HEADS // N_KV), (c + 1) * (N_HEADS // N_KV))
            ps, sinks = [], []
            for h in heads:
                st = scores[h] + jnp.where(valid, (-ALIBI_SLOPES[h] * LOG2_E) * dist, NEG_INF)
                sk = sink_ref[h] * LOG2_E
                m = jnp.maximum(jnp.max(st, axis=0, keepdims=True), sk)
                ps.append(jnp.exp2((st - m).astype(BF16)))
                sinks.append(jnp.exp2(sk - m))
            va = jnp.concatenate([vt[c * HEAD_DIM:(c + 1) * HEAD_DIM, :], ones], axis=0)
            ot = _dot(va, jnp.concatenate(ps, axis=1))
            ot = ot[:HEAD_DIM] / (ot[HEAD_DIM:HEAD_DIM + 1] + jnp.concatenate(sinks, axis=1))
            for col, h in enumerate(heads):
                outs[h] = ot[:, col * BLOCK:(col + 1) * BLOCK]
        for j in range(4):
            pair = jnp.concatenate([outs[2 * j], outs[2 * j + 1]], axis=0)
            o_ref[0, rows, j * LANES:(j + 1) * LANES] = pair.T.astype(BF16)


def _win_attn(sink, qa, ka, vat):
    bsz, seq, _ = qa.shape
    nb = seq // BLOCK
    wb = WIN_BLOCKS
    qmap = lambda b, n: (b, n, 0)
    blk = lambda off: (lambda n: jnp.clip(n * wb + off, 0, nb - 1))
    kspec = lambda f: pl.BlockSpec((1, BLOCK, 2 * KV_W), lambda b, n: (b, f(n), 0))
    vspec = lambda f: pl.BlockSpec((1, KV_W, BLOCK), lambda b, n: (b, 0, f(n)))
    offs = range(-1, wb + 1)
    return pl.pallas_call(
        functools.partial(_win_attn_body, seq=seq), grid=(bsz, nb // wb),
        in_specs=[pl.BlockSpec(memory_space=pltpu.SMEM), pl.BlockSpec((1, wb * BLOCK, Q_W), qmap)]
        + [kspec(blk(o)) for o in offs] + [vspec(blk(o)) for o in offs],
        out_specs=pl.BlockSpec((1, wb * BLOCK, Q_W), qmap),
        out_shape=jax.ShapeDtypeStruct((bsz, seq, Q_W), BF16), name="win_attn",
        compiler_params=pltpu.CompilerParams(dimension_semantics=("arbitrary", "arbitrary")),
    )(sink, qa, *([ka] * (wb + 2)), *([vat] * (wb + 2)))


def _grid_attn_body(q_ref, k_ref, vt_ref, o_ref, s0_ref, s1_ref, p0_ref, p1_ref):
    tq = q_ref.shape[1]
    seq = k_ref.shape[1]
    kc = KEY_CHUNK
    n_chunks = seq // kc
    s_bufs, p_bufs = (s0_ref, s1_ref), (p0_ref, p1_ref)
    n_heads = N_HEADS
    ones = jnp.ones((2 * ROW_TILE, seq), BF16)

    def masked_q(h):
        slab = q_ref[0, :, (h // 2) * LANES:(h // 2 + 1) * LANES]
        return jnp.where(_half_mask(tq, h % 2), slab, jnp.zeros_like(slab))

    def score_chunk(h, qm, kb, m8):
        v = 0 if h // 4 == h % 2 else 1
        sc = _dot_nt(k_ref[0, kb * kc:(kb + 1) * kc, v * LANES:(v + 1) * LANES], qm)
        s_bufs[h % 2][kb * kc:(kb + 1) * kc, :] = sc
        cm = jnp.max(sc.reshape(kc // ROW_TILE, ROW_TILE, tq), axis=0)
        return cm if m8 is None else jnp.maximum(m8, cm)

    def prob_chunk(h, kb, m):
        x = s_bufs[h % 2][kb * kc:(kb + 1) * kc, :] - m
        p_bufs[h % 2][kb * kc:(kb + 1) * kc, :] = jnp.exp2(x.astype(BF16))

    qm = masked_q(0)
    m8 = None
    for kb in range(n_chunks):
        m8 = score_chunk(0, qm, kb, m8)
    outs = []
    for h in range(n_heads):
        m = jnp.max(m8, axis=0, keepdims=True)
        nxt = h + 1 < n_heads
        if nxt:
            qm = masked_q(h + 1)
            m8 = None
        for kb in range(n_chunks):
            if nxt:
                m8 = score_chunk(h + 1, qm, kb, m8)
            prob_chunk(h, kb, m)
        c = h // (n_heads // N_KV)
        va = jnp.concatenate([vt_ref[0, c * HEAD_DIM:(c + 1) * HEAD_DIM, :], ones], axis=0)
        ot = _dot(va, p_bufs[h % 2][...])
        outs.append(ot[:HEAD_DIM] / ot[HEAD_DIM:HEAD_DIM + 1])
        if h % 2 == 1:
            o_ref[0, :, (h // 2) * LANES:(h // 2 + 1) * LANES] = jnp.concatenate(outs, axis=0).T.astype(BF16)
            outs = []


def _grid_attn(qb, kb, vbt):
    bsz, seq, _ = qb.shape
    tq = TQ_GRID
    return pl.pallas_call(
        _grid_attn_body, grid=(bsz, seq // tq),
        in_specs=[pl.BlockSpec((1, tq, Q_W), lambda b, n: (b, n, 0)),
                  pl.BlockSpec((1, seq, 2 * KV_W), lambda b, n: (b, 0, 0)),
                  pl.BlockSpec((1, KV_W, seq), lambda b, n: (b, 0, 0))],
        out_specs=pl.BlockSpec((1, tq, Q_W), lambda b, n: (b, n, 0)),
        scratch_shapes=[pltpu.VMEM((seq, tq), F32), pltpu.VMEM((seq, tq), F32),
                        pltpu.VMEM((seq, tq), BF16), pltpu.VMEM((seq, tq), BF16)],
        out_shape=jax.ShapeDtypeStruct((bsz, seq, Q_W), BF16), name="grid_attn",
        compiler_params=pltpu.CompilerParams(
            dimension_semantics=("arbitrary", "arbitrary"), vmem_limit_bytes=40 * 1024 * 1024),
    )(qb, kb, vbt)


def _post_attn_body(x_ref, oa_ref, ob_ref, sga_ref, sgb_ref, g0_ref, b0_ref, wa_ref, wb_ref, wo_ref,
                    g1_ref, b1_ref, wr_ref, br_ref, h1t_ref, tr_ref, tw_ref, cnt_out_ref, cnt_ref):
    h0 = _ln(x_ref[0], g0_ref[...], b0_ref[...])
    out_a = _dot(oa_ref[0], wa_ref[...])
    out_b = _dot(ob_ref[0], wb_ref[...])
    merged = sga_ref[0].astype(F32) * out_a + sgb_ref[0].astype(F32) * out_b
    mix = _dot(merged.astype(BF16), wo_ref[...])
    h1 = _ln(DN_ALPHA * h0 + mix, g1_ref[...], b1_ref[...])
    tm = h1.shape[0]
    _store_row_tiles(h1t_ref.at[0], h1, tm)

    logits = (_dot_nt(wr_ref[...], h1.astype(BF16)) + br_ref[...])[:N_EXPERTS]
    sub = lax.broadcasted_iota(jnp.int32, (N_EXPERTS, tm), 0)
    cur = logits
    vals, idxs = [], []
    for _ in range(TOP_K):
        mv = jnp.max(cur, axis=0, keepdims=True)
        ix = jnp.min(jnp.where(cur == mv, sub, N_EXPERTS), axis=0, keepdims=True)
        vals.append(mv)
        idxs.append(ix)
        cur = jnp.where(sub == ix, -jnp.inf, cur)
    es = [jnp.exp(v - vals[0]) for v in vals]
    tot = es[0] + es[1] + es[2] + es[3]

    @pl.when((pl.program_id(0) == 0) & (pl.program_id(1) == 0))
    def _():
        cnt_ref[...] = jnp.zeros_like(cnt_ref)

    sel = jnp.zeros((N_EXPERTS, tm), F32)
    for kx in range(TOP_K):
        sel = sel + (sub == idxs[kx]).astype(F32)
    r_i = lax.broadcasted_iota(jnp.int32, (tm, tm), 0)
    c_i = lax.broadcasted_iota(jnp.int32, (tm, tm), 1)
    tri = (r_i < c_i).astype(BF16)
    rank = _dot(sel.astype(BF16), tri) + cnt_ref[:, 0:1]
    cnt_ref[...] = cnt_ref[...] + jnp.sum(sel, axis=1, keepdims=True)
    cnt_out_ref[...] = cnt_ref[...]

    rks = [jnp.sum(jnp.where(sub == ix, rank, 0.0), axis=0, keepdims=True).astype(jnp.int32) for ix in idxs]
    tr_ref[...] = jnp.concatenate(idxs + rks, axis=0)
    tw_t = jnp.concatenate([e / tot for e in es] + [jnp.zeros((LANES - TOP_K, tm), F32)], axis=0)
    tw_ref[0] = tw_t.T


def _post_attn(x, oa, ob, sga, sgb, g0, b0, wa, wb, wo, g1, b1, wr, br):
    bsz, seq, d = x.shape
    tm = TM_PROJ
    tok3 = lambda b, i: (b, i, 0)
    const = lambda b, i: (0, 0)
    full = lambda a: pl.BlockSpec(a.shape, const)
    return pl.pallas_call(
        _post_attn_body, grid=(bsz, seq // tm),
        in_specs=[pl.BlockSpec((1, tm, d), tok3),
                  pl.BlockSpec((1, tm, Q_W), tok3), pl.BlockSpec((1, tm, Q_W), tok3),
                  pl.BlockSpec((1, tm, d), tok3), pl.BlockSpec((1, tm, d), tok3),
                  full(g0), full(b0), full(wa), full(wb), full(wo), full(g1), full(b1),
                  full(wr), full(br)],
        out_specs=[pl.BlockSpec((1, tm * ROW_TILE, LANES), tok3),
                   pl.BlockSpec((2 * TOP_K, tm), lambda b, i: (0, b * (seq // tm) + i)),
                   pl.BlockSpec((1, tm, LANES), tok3), pl.BlockSpec((N_EXPERTS, LANES), const)],
        out_shape=[jax.ShapeDtypeStruct((bsz, seq * ROW_TILE, LANES), F32),
                   jax.ShapeDtypeStruct((2 * TOP_K, bsz * seq), jnp.int32),
                   jax.ShapeDtypeStruct((bsz, seq, LANES), F32),
                   jax.ShapeDtypeStruct((N_EXPERTS, LANES), F32)],
        scratch_shapes=[pltpu.VMEM((N_EXPERTS, LANES), F32)],
        name="post_attn",
        compiler_params=pltpu.CompilerParams(
            dimension_semantics=("arbitrary", "arbitrary"), vmem_limit_bytes=48 * 1024 * 1024),
    )(x, oa, ob, sga, sgb, g0, b0, wa, wb, wo, g1, b1, wr, br)


def _dispatch_body(dest_ref, pend_ref, padded_ref, nu_ref, h1t_ref, xs_hbm, zbuf, sem, zsem):
    tm = TM_DISPATCH
    zrows = BM_EXPERT * ROW_TILE
    n_blocks = xs_hbm.shape[0] // zrows

    @pl.when(pl.program_id(0) == 0)
    def _():
        zbuf[...] = jnp.zeros_like(zbuf)
        zero_wait = pltpu.make_async_copy(zbuf, xs_hbm.at[pl.ds(0, zrows)], zsem).wait
        for e in range(N_EXPERTS):
            @pl.when(padded_ref[e] > 0)
            def _():
                start = pl.multiple_of((pend_ref[e] - BM_EXPERT) * ROW_TILE, ROW_TILE)
                pltpu.make_async_copy(zbuf, xs_hbm.at[pl.ds(start, zrows)], zsem).start()
        for b in range(n_blocks - N_EXPERTS, n_blocks):
            @pl.when(b >= nu_ref[0])
            def _():
                pltpu.make_async_copy(zbuf, xs_hbm.at[pl.ds(b * zrows, zrows)], zsem).start()
        for e in range(N_EXPERTS):
            pl.when(padded_ref[e] > 0)(zero_wait)
        for b in range(n_blocks - N_EXPERTS, n_blocks):
            pl.when(b >= nu_ref[0])(zero_wait)

    for t in range(tm):
        for kx in range(TOP_K):
            d = dest_ref[kx, t]
            pltpu.make_async_copy(_tile(h1t_ref, t), _tile(xs_hbm, d), sem).start(priority=kx % 2)
    for _ in range(TOP_K):
        pltpu.make_async_copy(h1t_ref, xs_hbm.at[pl.ds(0, tm * ROW_TILE)], sem).wait()


def _dispatch(dest_km, pad_end, padded, n_used, h1t, n_rows):
    n_tok = h1t.shape[0] // ROW_TILE
    tm = TM_DISPATCH
    return pl.pallas_call(
        _dispatch_body, grid=(n_tok // tm,),
        in_specs=[pl.BlockSpec((TOP_K, tm), lambda i: (0, i), memory_space=pltpu.SMEM),
                  pl.BlockSpec(memory_space=pltpu.SMEM), pl.BlockSpec(memory_space=pltpu.SMEM),
                  pl.BlockSpec(memory_space=pltpu.SMEM),
                  pl.BlockSpec((tm * ROW_TILE, LANES), lambda i: (i, 0))],
        out_specs=pl.BlockSpec(memory_space=pl.ANY),
        out_shape=jax.ShapeDtypeStruct((n_rows * ROW_TILE, LANES), F32),
        scratch_shapes=[pltpu.VMEM((BM_EXPERT * ROW_TILE, LANES), F32),
                        pltpu.SemaphoreType.DMA(()), pltpu.SemaphoreType.DMA(())],
        name="dispatch",
        compiler_params=pltpu.CompilerParams(dimension_semantics=("arbitrary",)),
    )(dest_km, pad_end, padded, n_used, h1t)


def _experts_body(be_ref, nu_ref, nx_ref, nv_ref, xs_ref, bg_ref, bu_ref, bd_ref, wg_hbm, wu_hbm, wd_hbm, ys_ref,
                  stage, wg_s, wu_s, wd_s, wsem):
    i = pl.program_id(0)
    bm = BM_EXPERT
    used = i < nu_ref[0]
    prev = be_ref[jnp.maximum(i - 1, 0)]
    fresh = (i == 0) | (be_ref[i] != prev)

    def weight_copies(e):
        return [pltpu.make_async_copy(w_hbm.at[e], stage.at[n], wsem)
                for n, w_hbm in enumerate((wg_hbm, wu_hbm, wd_hbm))]

    @pl.when(i == 0)
    def _():
        for cp in weight_copies(be_ref[0]):
            cp.start()

    @pl.when(used & fresh)
    def _():
        for cp in weight_copies(be_ref[i]):
            cp.wait()
        wg_s[...] = stage[0].astype(BF16)
        wu_s[...] = stage[1].astype(BF16)
        wd_s[...] = stage[2].astype(BF16)

        @pl.when(nx_ref[i] >= 0)
        def _():
            for cp in weight_copies(nx_ref[i]):
                cp.start()

    def ffn(rows):
        xb = _load_row_tiles(xs_ref, rows).astype(BF16)
        g = _dot(xb, wg_s[...]) + bg_ref[0]
        u = _dot(xb, wu_s[...]) + bu_ref[0]
        g = jnp.minimum(g, SWIGLU_LIMIT)
        u = jnp.clip(u, -SWIGLU_LIMIT, SWIGLU_LIMIT)
        act = g * jax.nn.sigmoid(SWIGLU_ALPHA * g) * (u + 1.0)
        _store_row_tiles(ys_ref, _dot(act.astype(BF16), wd_s[...]) + bd_ref[0], rows)

    half = bm // 2
    full = used & (nv_ref[i] > half)

    @pl.when(full)
    def _():
        ffn(bm)

    @pl.when(used & jnp.logical_not(full))
    def _():
        ffn(half)
        ys_ref[pl.ds(half * ROW_TILE, half * ROW_TILE), :] = jnp.zeros((half * ROW_TILE, LANES), F32)

    @pl.when(jnp.logical_not(used))
    def _():
        ys_ref[...] = jnp.zeros_like(ys_ref)


def _experts(block_e, n_used, next_e, n_valid, xs, wg, bg, wu, bu, wd, bd):
    n_rows = xs.shape[0] // ROW_TILE
    bm = BM_EXPERT
    n_e, d, d_ff = wg.shape
    assert d == d_ff and wd.shape == wg.shape
    row = lambda i, be, nu, nx, nv: (jnp.minimum(i, nu[0] - 1), 0)
    exp3 = lambda i, be, nu, nx, nv: (be[jnp.minimum(i, nu[0] - 1)], 0, 0)
    any_spec = pl.BlockSpec(memory_space=pl.ANY)
    grid_spec = pltpu.PrefetchScalarGridSpec(
        num_scalar_prefetch=4, grid=(n_rows // bm,),
        in_specs=[pl.BlockSpec((bm * ROW_TILE, LANES), row),
                  pl.BlockSpec((1, 1, d_ff), exp3), pl.BlockSpec((1, 1, d_ff), exp3),
                  pl.BlockSpec((1, 1, d), exp3), any_spec, any_spec, any_spec],
        out_specs=pl.BlockSpec((bm * ROW_TILE, LANES), lambda i, be, nu, nx, nv: (i, 0)),
        scratch_shapes=[pltpu.VMEM((3, d, d_ff), F32),
                        pltpu.VMEM((d, d_ff), BF16), pltpu.VMEM((d, d_ff), BF16), pltpu.VMEM((d_ff, d), BF16),
                        pltpu.SemaphoreType.DMA(())])
    return pl.pallas_call(
        _experts_body, grid_spec=grid_spec,
        out_shape=jax.ShapeDtypeStruct((n_rows * ROW_TILE, LANES), F32), name="experts",
        compiler_params=pltpu.CompilerParams(
            dimension_semantics=("arbitrary",), vmem_limit_bytes=48 * 1024 * 1024),
    )(block_e, n_used, next_e, n_valid, xs, bg.reshape(n_e, 1, d_ff), bu.reshape(n_e, 1, d_ff), bd.reshape(n_e, 1, d),
      wg, wu, wd)


def _combine_body(dest_ref, dest_next_ref, h1t_ref, tw_ref, g2_ref, b2_ref, ys_hbm, out_ref, buf, sems):
    tm = TM_COMBINE
    i = pl.program_id(0)
    slot = i % 2

    def start_row(d_ref, s, t):
        for kx in range(TOP_K):
            d = d_ref[kx, t]
            pltpu.make_async_copy(_tile(ys_hbm, d), _tile(buf.at[s, kx], t), sems.at[s]).start(priority=kx % 2)

    def wait_tile(s):
        for kx in range(TOP_K):
            pltpu.make_async_copy(ys_hbm.at[pl.ds(0, tm * ROW_TILE)], buf.at[s, kx], sems.at[s]).wait()

    @pl.when(i == 0)
    def _():
        lax.fori_loop(0, tm, lambda t, c: (start_row(dest_ref, 0, t), c)[1], 0)

    wait_tile(slot)
    rc = COMBINE_ROW_CHUNK
    for c in range(tm // rc):
        for t in range(c * rc, (c + 1) * rc):
            start_row(dest_next_ref, 1 - slot, t)
        tw = tw_ref[c * rc:(c + 1) * rc, :]
        ffn = tw[:, 0:1] * _load_row_tiles(buf.at[slot, 0], rc, c * rc)
        for kx in range(1, TOP_K):
            ffn = ffn + tw[:, kx:kx + 1] * _load_row_tiles(buf.at[slot, kx], rc, c * rc)
        h1 = _load_row_tiles(h1t_ref, rc, c * rc)
        out_ref[c * rc:(c + 1) * rc, :] = _ln(DN_ALPHA * h1 + ffn, g2_ref[...], b2_ref[...])

    @pl.when(i == pl.num_programs(0) - 1)
    def _():
        wait_tile(1 - slot)


def _combine(dest_km, h1t, tw, g2, b2, ys):
    n_tok = h1t.shape[0] // ROW_TILE
    d = ROW_TILE * LANES
    tm = TM_COMBINE
    n_steps = n_tok // tm
    const = lambda i: (0, 0)
    return pl.pallas_call(
        _combine_body, grid=(n_steps,),
        in_specs=[pl.BlockSpec((TOP_K, tm), lambda i: (0, i), memory_space=pltpu.SMEM),
                  pl.BlockSpec((TOP_K, tm), lambda i: (0, jnp.minimum(i + 1, n_steps - 1)),
                               memory_space=pltpu.SMEM),
                  pl.BlockSpec((tm * ROW_TILE, LANES), lambda i: (i, 0)),
                  pl.BlockSpec((tm, LANES), lambda i: (i, 0)),
                  pl.BlockSpec((1, d), const), pl.BlockSpec((1, d), const),
                  pl.BlockSpec(memory_space=pl.ANY)],
        out_specs=pl.BlockSpec((tm, d), lambda i: (i, 0)),
        out_shape=jax.ShapeDtypeStruct((n_tok, d), F32),
        scratch_shapes=[pltpu.VMEM((2, TOP_K, tm * ROW_TILE, LANES), F32), pltpu.SemaphoreType.DMA((2,))],
        name="combine",
        compiler_params=pltpu.CompilerParams(
            dimension_semantics=("arbitrary",), vmem_limit_bytes=40 * 1024 * 1024),
    )(dest_km, dest_km, h1t, tw, g2, b2, ys)


def _rope_tables(seq):
    t = np.arange(seq)
    row = (t // GRID_W).astype(np.float32)
    col = (t % GRID_W).astype(np.float32)
    half = HEAD_DIM // 2
    quarter = half // 2
    inv = (ROPE_THETA ** (-np.arange(quarter, dtype=np.float32) * np.float32(2.0 / half))).astype(np.float32)
    ang_r = row[:, None] * inv[None, :]
    ang_c = col[:, None] * inv[None, :]
    zeros = np.zeros_like(ang_r)
    cos = np.concatenate([np.cos(ang_r), np.cos(ang_r), np.cos(ang_c), np.cos(ang_c)], -1)
    s_lo = np.concatenate([-np.sin(ang_r), zeros, -np.sin(ang_c), zeros], -1)
    s_hi = np.concatenate([zeros, np.sin(ang_r), zeros, np.sin(ang_c)], -1)
    return tuple(jnp.asarray(np.tile(a, (1, LANES // HEAD_DIM)), F32) for a in (cos, s_lo, s_hi))


def _routing(top_i, rank, counts, bm):
    n_tok = top_i.shape[1]
    padded = (counts + bm - 1) // bm * bm
    pad_end = jnp.cumsum(padded)
    pad_start = pad_end - padded
    base = jnp.sum(jnp.where(top_i[:, :, None] == jnp.arange(N_EXPERTS, dtype=jnp.int32)[None, None, :],
                             pad_start[None, None, :], 0), axis=-1)
    dest = (base + rank).astype(jnp.int32)
    n_rows = n_tok * TOP_K + N_EXPERTS * bm
    n_blocks = n_rows // bm
    block_start = jnp.arange(n_blocks, dtype=jnp.int32) * bm
    block_e = jnp.minimum(jnp.sum((pad_end[None, :] <= block_start[:, None]).astype(jnp.int32), axis=1),
                          N_EXPERTS - 1).astype(jnp.int32)
    n_used = (pad_end[-1] // bm).astype(jnp.int32).reshape(1)
    e_ids = jnp.arange(N_EXPERTS, dtype=jnp.int32)
    later = (e_ids[None, :] > e_ids[:, None]) & (counts[None, :] > 0)
    next_of = jnp.min(jnp.where(later, e_ids[None, :], N_EXPERTS), axis=1)
    next_of = jnp.where(next_of == N_EXPERTS, -1, next_of).astype(jnp.int32)
    next_e = jnp.sum(jnp.where(block_e[:, None] == e_ids[None, :], next_of[None, :], 0), axis=1).astype(jnp.int32)
    start_of = jnp.sum(jnp.where(block_e[:, None] == e_ids[None, :], pad_start[None, :], 0), axis=1)
    count_of = jnp.sum(jnp.where(block_e[:, None] == e_ids[None, :], counts[None, :], 0), axis=1)
    n_valid = jnp.clip(count_of - (block_start - start_of), 0, bm).astype(jnp.int32)
    return (dest, block_e, n_used, next_e, n_valid, n_rows, pad_end.astype(jnp.int32),
            padded.astype(jnp.int32))


def kernel(x, ln0_g, ln0_b, w_in, a_sink, b_q_norm, b_k_norm, w_branch_a, w_branch_b, w_out,
           ln1_g, ln1_b, w_router, b_router, w_gate, b_gate, w_up, b_up, w_down, b_down,
           ln2_g, ln2_b):
    bsz, seq, d = x.shape
    assert w_in.shape[0] == DEPTH == 1
    assert seq % TM_PROJ == 0 and seq % TQ_GRID == 0 and seq == (seq // GRID_W) * GRID_W
    n_tok = bsz * seq
    row2 = lambda a: a.reshape(1, -1)

    w_perm = w_in[0].astype(BF16)
    head_id = np.arange(Q_W) // HEAD_DIM
    bd = jnp.asarray(head_id[:, None] == head_id[None, :], BF16)
    gq = jnp.tile(b_q_norm[0].astype(F32), N_HEADS).reshape(1, Q_W)
    gk = jnp.tile(b_k_norm[0].astype(F32), N_KV).reshape(1, KV_W)

    qa, ka, vat, qb, kb, vbt, sga, sgb = _in_proj(
        x, row2(ln0_g), row2(ln0_b), w_perm, bd, gq, gk, _rope_tables(seq))
    oa = _win_attn(a_sink[0].astype(F32), qa, ka, vat)
    ob = _grid_attn(qb, kb, vbt)

    wr = jnp.zeros((LANES, d), BF16).at[:N_EXPERTS].set(w_router[0].T.astype(BF16))
    br = jnp.zeros((LANES, 1), F32).at[:N_EXPERTS, 0].set(b_router[0])
    h1t, tr, tw, cnt = _post_attn(
        x, oa, ob, sga, sgb, row2(ln0_g), row2(ln0_b),
        w_branch_a[0].astype(BF16), w_branch_b[0].astype(BF16),
        w_out[0].astype(BF16), row2(ln1_g[0]), row2(ln1_b[0]), wr, br)
    h1t = h1t.reshape(n_tok * ROW_TILE, LANES)
    top_i, rank = tr[:TOP_K], tr[TOP_K:]
    tw = tw.reshape(n_tok, LANES)

    counts = cnt[:, 0].astype(jnp.int32)
    dest, block_e, n_used, next_e, n_valid, n_rows, pad_end, padded = _routing(top_i, rank, counts, BM_EXPERT)
    xs = _dispatch(dest, pad_end, padded, n_used, h1t, n_rows)
    ys = _experts(block_e, n_used, next_e, n_valid, xs, w_gate[0], b_gate[0], w_up[0], b_up[0], w_down[0], b_down[0])
    out = _combine(dest, h1t, tw, row2(ln2_g[0]), row2(ln2_b[0]), ys)
    return out.reshape(bsz, seq, d)
```

```python
import functools

import jax
import jax.numpy as jnp
import numpy as np
from jax import lax
from jax.experimental import pallas as pl
from jax.experimental.pallas import tpu as pltpu

HEAD_DIM = 64
N_HEADS = 8
N_KV = 2
WINDOW = 128
BLOCK = 128
GRID_W = 64
ROPE_THETA = 10000.0
N_EXPERTS = 32
TOP_K = 4
SWIGLU_LIMIT = 7.0
SWIGLU_ALPHA = 1.702
LN_EPS = 1e-5
RMS_EPS = 1e-6
NEG_INF = -1e30
DEPTH = 1
DN_ALPHA = (2.0 * DEPTH) ** 0.25
ALIBI_SLOPES = tuple(2.0 ** (-8.0 * (h + 1) / N_HEADS) for h in range(N_HEADS))
QK_SCALE = HEAD_DIM ** -0.5
LOG2_E = 1.4426950408889634

LANES = 128
ROW_TILE = 8
Q_W = N_HEADS * HEAD_DIM
KV_W = N_KV * HEAD_DIM

TM_PROJ = 512
TM_IN_PROJ = 1024
PROJ_ROW_GROUPS = 2
WIN_BLOCKS = 8
TQ_GRID = 256
KEY_CHUNK = 256
BM_EXPERT = 512
TM_DISPATCH = 512
TM_COMBINE = 512
COMBINE_ROW_CHUNK = 32

F32 = jnp.float32
BF16 = jnp.bfloat16


def _ln(x, g, b):
    mu = jnp.mean(x, -1, keepdims=True)
    xc = x - mu
    var = jnp.mean(xc * xc, -1, keepdims=True)
    return xc * lax.rsqrt(var + LN_EPS) * g + b


def _dot(a, b):
    return jnp.dot(a, b, preferred_element_type=F32)


def _dot_nt(a, b):
    return lax.dot_general(a, b, (((1,), (1,)), ((), ())), preferred_element_type=F32)


def _load_row_tiles(ref, rows, first=0):
    return jnp.concatenate(
        [ref[pl.ds(first * ROW_TILE + c, rows, stride=ROW_TILE), :] for c in range(ROW_TILE)], axis=1)


def _store_row_tiles(ref, val, rows):
    for c in range(ROW_TILE):
        ref[pl.ds(c, rows, stride=ROW_TILE), :] = val[:, c * LANES:(c + 1) * LANES]


def _tile(ref, row):
    start = row * ROW_TILE
    if not isinstance(row, int):
        start = pl.multiple_of(start, ROW_TILE)
    return ref.at[pl.ds(start, ROW_TILE)]


def _in_proj_body(x_ref, g0_ref, b0_ref, w_ref, bd_ref, gq_ref, gk_ref, c_ref, s1_ref, s2_ref,
                  qa_ref, ka_ref, vat_ref, qb_ref, kb_ref, vbt_ref, sga_ref, sgb_ref):
    tm = x_ref.shape[1]
    d = sga_ref.shape[-1]
    o_kva, o_qb, o_kvb, o_g = Q_W, Q_W + 2 * KV_W, 2 * Q_W + 2 * KV_W, 2 * Q_W + 4 * KV_W
    both = lambda k: jnp.concatenate([k, pltpu.roll(k, HEAD_DIM, 1)], axis=1).astype(BF16)
    rg = tm // PROJ_ROW_GROUPS
    for grp in range(PROJ_ROW_GROUPS):
        rows = slice(grp * rg, (grp + 1) * rg)
        hb = _ln(x_ref[0, rows, :], g0_ref[...], b0_ref[...]).astype(BF16)

        def proj(lo, hi):
            return _dot(hb, w_ref[:, lo:hi])

        def norm_rope(t, width, g_ref):
            ss = _dot((t * t).astype(BF16), bd_ref[:width, :width])
            r = lax.rsqrt(ss * (1.0 / HEAD_DIM) + RMS_EPS)
            reps = width // LANES
            tab = lambda ref: jnp.concatenate([ref[rows, :]] * reps, axis=1)
            y = t * g_ref[...]
            rot = (y * tab(c_ref) + pltpu.roll(y, width - 16, 1) * tab(s1_ref)
                   + pltpu.roll(y, 16, 1) * tab(s2_ref))
            return rot * r

        qa_ref[0, rows, :] = (proj(0, o_kva) * (QK_SCALE * LOG2_E)).astype(BF16)
        kva = proj(o_kva, o_qb)
        ka_ref[0, rows, :] = both(kva[:, :KV_W])
        vat_ref[0, :, rows] = kva[:, KV_W:].T.astype(BF16)
        qb = norm_rope(proj(o_qb, o_kvb), Q_W, gq_ref)
        qb_ref[0, rows, :] = (qb * (QK_SCALE * LOG2_E)).astype(BF16)
        kvb = proj(o_kvb, o_g)
        kb_ref[0, rows, :] = both(norm_rope(kvb[:, :KV_W], KV_W, gk_ref))
        vbt_ref[0, :, rows] = kvb[:, KV_W:].T.astype(BF16)
        sga_ref[0, rows, :] = jax.nn.sigmoid(proj(o_g, o_g + d)).astype(BF16)
        sgb_ref[0, rows, :] = jax.nn.sigmoid(proj(o_g + d, o_g + 2 * d)).astype(BF16)


def _in_proj(x, g0, b0, w, bd, gq, gk, tabs):
    bsz, seq, d = x.shape
    tm = TM_IN_PROJ
    n_in = w.shape[1]
    const = lambda i, j: (0, 0)
    tok3 = lambda i, j: (j, i, 0)
    tab = lambda i, j: (i, 0)
    in_specs = [
        pl.BlockSpec((1, tm, d), tok3),
        pl.BlockSpec((1, d), const), pl.BlockSpec((1, d), const),
        pl.BlockSpec((d, n_in), const),
        pl.BlockSpec((Q_W, Q_W), const),
        pl.BlockSpec((1, Q_W), const), pl.BlockSpec((1, KV_W), const),
        pl.BlockSpec((tm, LANES), tab), pl.BlockSpec((tm, LANES), tab), pl.BlockSpec((tm, LANES), tab),
    ]
    tr3 = lambda i, j: (j, 0, i)
    out_specs = [
        pl.BlockSpec((1, tm, Q_W), tok3), pl.BlockSpec((1, tm, 2 * KV_W), tok3),
        pl.BlockSpec((1, KV_W, tm), tr3),
        pl.BlockSpec((1, tm, Q_W), tok3), pl.BlockSpec((1, tm, 2 * KV_W), tok3),
        pl.BlockSpec((1, KV_W, tm), tr3),
        pl.BlockSpec((1, tm, d), tok3), pl.BlockSpec((1, tm, d), tok3),
    ]
    out_shape = [
        jax.ShapeDtypeStruct((bsz, seq, Q_W), BF16), jax.ShapeDtypeStruct((bsz, seq, 2 * KV_W), BF16),
        jax.ShapeDtypeStruct((bsz, KV_W, seq), BF16),
        jax.ShapeDtypeStruct((bsz, seq, Q_W), BF16), jax.ShapeDtypeStruct((bsz, seq, 2 * KV_W), BF16),
        jax.ShapeDtypeStruct((bsz, KV_W, seq), BF16),
        jax.ShapeDtypeStruct((bsz, seq, d), BF16), jax.ShapeDtypeStruct((bsz, seq, d), BF16),
    ]
    return pl.pallas_call(
        _in_proj_body, grid=(seq // tm, bsz), in_specs=in_specs, out_specs=out_specs,
        out_shape=out_shape, name="in_proj",
        compiler_params=pltpu.CompilerParams(
            dimension_semantics=("arbitrary", "arbitrary"), vmem_limit_bytes=48 * 1024 * 1024),
    )(x, g0, b0, w, bd, gq, gk, *tabs)


def _half_mask(rows, c):
    lane = lax.broadcasted_iota(jnp.int32, (rows, LANES), 1)
    return (lane >= HEAD_DIM) if c == 1 else (lane < HEAD_DIM)


def _win_attn_body(sink_ref, q_ref, *refs, seq):
    nk = WIN_BLOCKS + 2
    k_refs, v_refs, o_ref = refs[:nk], refs[nk:2 * nk], refs[2 * nk]
    kk = lax.broadcasted_iota(jnp.int32, (3 * BLOCK, BLOCK), 0)
    qq = lax.broadcasted_iota(jnp.int32, (3 * BLOCK, BLOCK), 1)
    dist_i = jnp.abs(kk - BLOCK - qq)
    dist = dist_i.astype(F32)
    ones = jnp.ones((2 * ROW_TILE, 3 * BLOCK), BF16)
    for blk in range(WIN_BLOCKS):
        n = pl.program_id(1) * WIN_BLOCKS + blk
        rows = slice(blk * BLOCK, (blk + 1) * BLOCK)
        k2 = jnp.concatenate([r[0] for r in k_refs[blk:blk + 3]], axis=0)
        vt = jnp.concatenate([r[0] for r in v_refs[blk:blk + 3]], axis=1)
        k_pos = n * BLOCK - BLOCK + kk
        valid = (dist_i <= WINDOW) & (k_pos >= 0) & (k_pos < seq)
        slabs = [q_ref[0, rows, j * LANES:(j + 1) * LANES] for j in range(4)]
        variant = [[h for h in range(N_HEADS) if (h // 4 == h % 2) == straight] for straight in (True, False)]
        scores = {}
        for v, heads in enumerate(variant):
            qm = jnp.concatenate([jnp.where(_half_mask(BLOCK, h % 2), slabs[h // 2], jnp.zeros_like(slabs[0]))
                                  for h in heads], axis=0)
            st_v = _dot_nt(k2[:, v * LANES:(v + 1) * LANES], qm)
            for col, h in enumerate(heads):
                scores[h] = st_v[:, col * BLOCK:(col + 1) * BLOCK]
        outs = {}
        for c in range(N_KV):
            heads = range(c * (N_HEADS // N_KV), (c + 1) * (N_HEADS // N_KV))
            ps, sinks = [], []
            for h in heads:
                st = scores[h] + jnp.where(valid, (-ALIBI_SLOPES[h] * LOG2_E) * dist, NEG_INF)
                sk = sink_ref[h] * LOG2_E
                m = jnp.maximum(jnp.max(st, axis=0, keepdims=True), sk)
                ps.append(jnp.exp2((st - m).astype(BF16)))
                sinks.append(jnp.exp2(sk - m))
            va = jnp.concatenate([vt[c * HEAD_DIM:(c + 1) * HEAD_DIM, :], ones], axis=0)
            ot = _dot(va, jnp.concatenate(ps, axis=1))
            ot = ot[:HEAD_DIM] / (ot[HEAD_DIM:HEAD_DIM + 1] + jnp.concatenate(sinks, axis=1))
            for col, h in enumerate(heads):
                outs[h] = ot[:, col * BLOCK:(col + 1) * BLOCK]
        for j in range(4):
            pair = jnp.concatenate([outs[2 * j], outs[2 * j + 1]], axis=0)
            o_ref[0, rows, j * LANES:(j + 1) * LANES] = pair.T.astype(BF16)


def _win_attn(sink, qa, ka, vat):
    bsz, seq, _ = qa.shape
    nb = seq // BLOCK
    wb = WIN_BLOCKS
    qmap = lambda b, n: (b, n, 0)
    blk = lambda off: (lambda n: jnp.clip(n * wb + off, 0, nb - 1))
    kspec = lambda f: pl.BlockSpec((1, BLOCK, 2 * KV_W), lambda b, n: (b, f(n), 0))
    vspec = lambda f: pl.BlockSpec((1, KV_W, BLOCK), lambda b, n: (b, 0, f(n)))
    offs = range(-1, wb + 1)
    return pl.pallas_call(
        functools.partial(_win_attn_body, seq=seq), grid=(bsz, nb // wb),
        in_specs=[pl.BlockSpec(memory_space=pltpu.SMEM), pl.BlockSpec((1, wb * BLOCK, Q_W), qmap)]
        + [kspec(blk(o)) for o in offs] + [vspec(blk(o)) for o in offs],
        out_specs=pl.BlockSpec((1, wb * BLOCK, Q_W), qmap),
        out_shape=jax.ShapeDtypeStruct((bsz, seq, Q_W), BF16), name="win_attn",
        compiler_params=pltpu.CompilerParams(dimension_semantics=("arbitrary", "arbitrary")),
    )(sink, qa, *([ka] * (wb + 2)), *([vat] * (wb + 2)))


def _grid_attn_body(q_ref, k_ref, vt_ref, o_ref, s0_ref, s1_ref, p0_ref, p1_ref):
    tq = q_ref.shape[1]
    seq = k_ref.shape[1]
    kc = KEY_CHUNK
    n_chunks = seq // kc
    s_bufs, p_bufs = (s0_ref, s1_ref), (p0_ref, p1_ref)
    n_heads = N_HEADS
    ones = jnp.ones((2 * ROW_TILE, seq), BF16)

    def masked_q(h):
        slab = q_ref[0, :, (h // 2) * LANES:(h // 2 + 1) * LANES]
        return jnp.where(_half_mask(tq, h % 2), slab, jnp.zeros_like(slab))

    def score_chunk(h, qm, kb, m8):
        v = 0 if h // 4 == h % 2 else 1
        sc = _dot_nt(k_ref[0, kb * kc:(kb + 1) * kc, v * LANES:(v + 1) * LANES], qm)
        s_bufs[h % 2][kb * kc:(kb + 1) * kc, :] = sc
        cm = jnp.max(sc.reshape(kc // ROW_TILE, ROW_TILE, tq), axis=0)
        return cm if m8 is None else jnp.maximum(m8, cm)

    def prob_chunk(h, kb, m):
        x = s_bufs[h % 2][kb * kc:(kb + 1) * kc, :] - m
        p_bufs[h % 2][kb * kc:(kb + 1) * kc, :] = jnp.exp2(x.astype(BF16))

    qm = masked_q(0)
    m8 = None
    for kb in range(n_chunks):
        m8 = score_chunk(0, qm, kb, m8)
    outs = []
    for h in range(n_heads):
        m = jnp.max(m8, axis=0, keepdims=True)
        nxt = h + 1 < n_heads
        if nxt:
            qm = masked_q(h + 1)
            m8 = None
        for kb in range(n_chunks):
            if nxt:
                m8 = score_chunk(h + 1, qm, kb, m8)
            prob_chunk(h, kb, m)
        c = h // (n_heads // N_KV)
        va = jnp.concatenate([vt_ref[0, c * HEAD_DIM:(c + 1) * HEAD_DIM, :], ones], axis=0)
        ot = _dot(va, p_bufs[h % 2][...])
        outs.append(ot[:HEAD_DIM] / ot[HEAD_DIM:HEAD_DIM + 1])
        if h % 2 == 1:
            o_ref[0, :, (h // 2) * LANES:(h // 2 + 1) * LANES] = jnp.concatenate(outs, axis=0).T.astype(BF16)
            outs = []


def _grid_attn(qb, kb, vbt):
    bsz, seq, _ = qb.shape
    tq = TQ_GRID
    return pl.pallas_call(
        _grid_attn_body, grid=(bsz, seq // tq),
        in_specs=[pl.BlockSpec((1, tq, Q_W), lambda b, n: (b, n, 0)),
                  pl.BlockSpec((1, seq, 2 * KV_W), lambda b, n: (b, 0, 0)),
                  pl.BlockSpec((1, KV_W, seq), lambda b, n: (b, 0, 0))],
        out_specs=pl.BlockSpec((1, tq, Q_W), lambda b, n: (b, n, 0)),
        scratch_shapes=[pltpu.VMEM((seq, tq), F32), pltpu.VMEM((seq, tq), F32),
                        pltpu.VMEM((seq, tq), BF16), pltpu.VMEM((seq, tq), BF16)],
        out_shape=jax.ShapeDtypeStruct((bsz, seq, Q_W), BF16), name="grid_attn",
        compiler_params=pltpu.CompilerParams(
            dimension_semantics=("arbitrary", "arbitrary"), vmem_limit_bytes=40 * 1024 * 1024),
    )(qb, kb, vbt)


def _post_attn_body(x_ref, oa_ref, ob_ref, sga_ref, sgb_ref, g0_ref, b0_ref, wa_ref, wb_ref, wo_ref,
                    g1_ref, b1_ref, wr_ref, br_ref, h1t_ref, tr_ref, tw_ref, cnt_out_ref, cnt_ref):
    h0 = _ln(x_ref[0], g0_ref[...], b0_ref[...])
    out_a = _dot(oa_ref[0], wa_ref[...])
    out_b = _dot(ob_ref[0], wb_ref[...])
    merged = sga_ref[0].astype(F32) * out_a + sgb_ref[0].astype(F32) * out_b
    mix = _dot(merged.astype(BF16), wo_ref[...])
    h1 = _ln(DN_ALPHA * h0 + mix, g1_ref[...], b1_ref[...])
    tm = h1.shape[0]
    _store_row_tiles(h1t_ref.at[0], h1, tm)

    logits = (_dot_nt(wr_ref[...], h1.astype(BF16)) + br_ref[...])[:N_EXPERTS]
    sub = lax.broadcasted_iota(jnp.int32, (N_EXPERTS, tm), 0)
    cur = logits
    vals, idxs = [], []
    for _ in range(TOP_K):
        mv = jnp.max(cur, axis=0, keepdims=True)
        ix = jnp.min(jnp.where(cur == mv, sub, N_EXPERTS), axis=0, keepdims=True)
        vals.append(mv)
        idxs.append(ix)
        cur = jnp.where(sub == ix, -jnp.inf, cur)
    es = [jnp.exp(v - vals[0]) for v in vals]
    tot = es[0] + es[1] + es[2] + es[3]

    @pl.when((pl.program_id(0) == 0) & (pl.program_id(1) == 0))
    def _():
        cnt_ref[...] = jnp.zeros_like(cnt_ref)

    sel = jnp.zeros((N_EXPERTS, tm), F32)
    for kx in range(TOP_K):
        sel = sel + (sub == idxs[kx]).astype(F32)
    r_i = lax.broadcasted_iota(jnp.int32, (tm, tm), 0)
    c_i = lax.broadcasted_iota(jnp.int32, (tm, tm), 1)
    tri = (r_i < c_i).astype(BF16)
    rank = _dot(sel.astype(BF16), tri) + cnt_ref[:, 0:1]
    cnt_ref[...] = cnt_ref[...] + jnp.sum(sel, axis=1, keepdims=True)
    cnt_out_ref[...] = cnt_ref[...]

    rks = [jnp.sum(jnp.where(sub == ix, rank, 0.0), axis=0, keepdims=True).astype(jnp.int32) for ix in idxs]
    tr_ref[...] = jnp.concatenate(idxs + rks, axis=0)
    tw_t = jnp.concatenate([e / tot for e in es] + [jnp.zeros((LANES - TOP_K, tm), F32)], axis=0)
    tw_ref[0] = tw_t.T


def _post_attn(x, oa, ob, sga, sgb, g0, b0, wa, wb, wo, g1, b1, wr, br):
    bsz, seq, d = x.shape
    tm = TM_PROJ
    tok3 = lambda b, i: (b, i, 0)
    const = lambda b, i: (0, 0)
    full = lambda a: pl.BlockSpec(a.shape, const)
    return pl.pallas_call(
        _post_attn_body, grid=(bsz, seq // tm),
        in_specs=[pl.BlockSpec((1, tm, d), tok3),
                  pl.BlockSpec((1, tm, Q_W), tok3), pl.BlockSpec((1, tm, Q_W), tok3),
                  pl.BlockSpec((1, tm, d), tok3), pl.BlockSpec((1, tm, d), tok3),
                  full(g0), full(b0), full(wa), full(wb), full(wo), full(g1), full(b1),
                  full(wr), full(br)],
        out_specs=[pl.BlockSpec((1, tm * ROW_TILE, LANES), tok3),
                   pl.BlockSpec((2 * TOP_K, tm), lambda b, i: (0, b * (seq // tm) + i)),
                   pl.BlockSpec((1, tm, LANES), tok3), pl.BlockSpec((N_EXPERTS, LANES), const)],
        out_shape=[jax.ShapeDtypeStruct((bsz, seq * ROW_TILE, LANES), F32),
                   jax.ShapeDtypeStruct((2 * TOP_K, bsz * seq), jnp.int32),
                   jax.ShapeDtypeStruct((bsz, seq, LANES), F32),
                   jax.ShapeDtypeStruct((N_EXPERTS, LANES), F32)],
        scratch_shapes=[pltpu.VMEM((N_EXPERTS, LANES), F32)],
        name="post_attn",
        compiler_params=pltpu.CompilerParams(
            dimension_semantics=("arbitrary", "arbitrary"), vmem_limit_bytes=48 * 1024 * 1024),
    )(x, oa, ob, sga, sgb, g0, b0, wa, wb, wo, g1, b1, wr, br)


def _dispatch_body(dest_ref, pend_ref, padded_ref, nu_ref, h1t_ref, xs_hbm, zbuf, sem, zsem):
    tm = TM_DISPATCH
    zrows = BM_EXPERT * ROW_TILE
    n_blocks = xs_hbm.shape[0] // zrows

    @pl.when(pl.program_id(0) == 0)
    def _():
        zbuf[...] = jnp.zeros_like(zbuf)
        zero_wait = pltpu.make_async_copy(zbuf, xs_hbm.at[pl.ds(0, zrows)], zsem).wait
        for e in range(N_EXPERTS):
            @pl.when(padded_ref[e] > 0)
            def _():
                start = pl.multiple_of((pend_ref[e] - BM_EXPERT) * ROW_TILE, ROW_TILE)
                pltpu.make_async_copy(zbuf, xs_hbm.at[pl.ds(start, zrows)], zsem).start()
        for b in range(n_blocks - N_EXPERTS, n_blocks):
            @pl.when(b >= nu_ref[0])
            def _():
                pltpu.make_async_copy(zbuf, xs_hbm.at[pl.ds(b * zrows, zrows)], zsem).start()
        for e in range(N_EXPERTS):
            pl.when(padded_ref[e] > 0)(zero_wait)
        for b in range(n_blocks - N_EXPERTS, n_blocks):
            pl.when(b >= nu_ref[0])(zero_wait)

    for t in range(tm):
        for kx in range(TOP_K):
            d = dest_ref[kx, t]
            pltpu.make_async_copy(_tile(h1t_ref, t), _tile(xs_hbm, d), sem).start(priority=kx % 2)
    for _ in range(TOP_K):
        pltpu.make_async_copy(h1t_ref, xs_hbm.at[pl.ds(0, tm * ROW_TILE)], sem).wait()


def _dispatch(dest_km, pad_end, padded, n_used, h1t, n_rows):
    n_tok = h1t.shape[0] // ROW_TILE
    tm = TM_DISPATCH
    return pl.pallas_call(
        _dispatch_body, grid=(n_tok // tm,),
        in_specs=[pl.BlockSpec((TOP_K, tm), lambda i: (0, i), memory_space=pltpu.SMEM),
                  pl.BlockSpec(memory_space=pltpu.SMEM), pl.BlockSpec(memory_space=pltpu.SMEM),
                  pl.BlockSpec(memory_space=pltpu.SMEM),
                  pl.BlockSpec((tm * ROW_TILE, LANES), lambda i: (i, 0))],
        out_specs=pl.BlockSpec(memory_space=pl.ANY),
        out_shape=jax.ShapeDtypeStruct((n_rows * ROW_TILE, LANES), F32),
        scratch_shapes=[pltpu.VMEM((BM_EXPERT * ROW_TILE, LANES), F32),
                        pltpu.SemaphoreType.DMA(()), pltpu.SemaphoreType.DMA(())],
        name="dispatch",
        compiler_params=pltpu.CompilerParams(dimension_semantics=("arbitrary",)),
    )(dest_km, pad_end, padded, n_used, h1t)


def _experts_body(be_ref, nu_ref, nx_ref, nv_ref, xs_ref, bg_ref, bu_ref, bd_ref, wg_hbm, wu_hbm, wd_hbm, ys_ref,
                  stage, wg_s, wu_s, wd_s, wsem):
    i = pl.program_id(0)
    bm = BM_EXPERT
    used = i < nu_ref[0]
    prev = be_ref[jnp.maximum(i - 1, 0)]
    fresh = (i == 0) | (be_ref[i] != prev)

    def weight_copies(e):
        return [pltpu.make_async_copy(w_hbm.at[e], stage.at[n], wsem)
                for n, w_hbm in enumerate((wg_hbm, wu_hbm, wd_hbm))]

    @pl.when(i == 0)
    def _():
        for cp in weight_copies(be_ref[0]):
            cp.start()

    @pl.when(used & fresh)
    def _():
        for cp in weight_copies(be_ref[i]):
            cp.wait()
        wg_s[...] = stage[0].astype(BF16)
        wu_s[...] = stage[1].astype(BF16)
        wd_s[...] = stage[2].astype(BF16)

        @pl.when(nx_ref[i] >= 0)
        def _():
            for cp in weight_copies(nx_ref[i]):
                cp.start()

    def ffn(rows):
        xb = _load_row_tiles(xs_ref, rows).astype(BF16)
        g = _dot(xb, wg_s[...]) + bg_ref[0]
        u = _dot(xb, wu_s[...]) + bu_ref[0]
        g = jnp.minimum(g, SWIGLU_LIMIT)
        u = jnp.clip(u, -SWIGLU_LIMIT, SWIGLU_LIMIT)
        act = g * jax.nn.sigmoid(SWIGLU_ALPHA * g) * (u + 1.0)
        _store_row_tiles(ys_ref, _dot(act.astype(BF16), wd_s[...]) + bd_ref[0], rows)

    half = bm // 2
    full = used & (nv_ref[i] > half)

    @pl.when(full)
    def _():
        ffn(bm)

    @pl.when(used & jnp.logical_not(full))
    def _():
        ffn(half)
        ys_ref[pl.ds(half * ROW_TILE, half * ROW_TILE), :] = jnp.zeros((half * ROW_TILE, LANES), F32)

    @pl.when(jnp.logical_not(used))
    def _():
        ys_ref[...] = jnp.zeros_like(ys_ref)


def _experts(block_e, n_used, next_e, n_valid, xs, wg, bg, wu, bu, wd, bd):
    n_rows = xs.shape[0] // ROW_TILE
    bm = BM_EXPERT
    n_e, d, d_ff = wg.shape
    assert d == d_ff and wd.shape == wg.shape
    row = lambda i, be, nu, nx, nv: (jnp.minimum(i, nu[0] - 1), 0)
    exp3 = lambda i, be, nu, nx, nv: (be[jnp.minimum(i, nu[0] - 1)], 0, 0)
    any_spec = pl.BlockSpec(memory_space=pl.ANY)
    grid_spec = pltpu.PrefetchScalarGridSpec(
        num_scalar_prefetch=4, grid=(n_rows // bm,),
        in_specs=[pl.BlockSpec((bm * ROW_TILE, LANES), row),
                  pl.BlockSpec((1, 1, d_ff), exp3), pl.BlockSpec((1, 1, d_ff), exp3),
                  pl.BlockSpec((1, 1, d), exp3), any_spec, any_spec, any_spec],
        out_specs=pl.BlockSpec((bm * ROW_TILE, LANES), lambda i, be, nu, nx, nv: (i, 0)),
        scratch_shapes=[pltpu.VMEM((3, d, d_ff), F32),
                        pltpu.VMEM((d, d_ff), BF16), pltpu.VMEM((d, d_ff), BF16), pltpu.VMEM((d_ff, d), BF16),
                        pltpu.SemaphoreType.DMA(())])
    return pl.pallas_call(
        _experts_body, grid_spec=grid_spec,
        out_shape=jax.ShapeDtypeStruct((n_rows * ROW_TILE, LANES), F32), name="experts",
        compiler_params=pltpu.CompilerParams(
            dimension_semantics=("arbitrary",), vmem_limit_bytes=48 * 1024 * 1024),
    )(block_e, n_used, next_e, n_valid, xs, bg.reshape(n_e, 1, d_ff), bu.reshape(n_e, 1, d_ff), bd.reshape(n_e, 1, d),
      wg, wu, wd)


def _combine_body(dest_ref, dest_next_ref, h1t_ref, tw_ref, g2_ref, b2_ref, ys_hbm, out_ref, buf, sems):
    tm = TM_COMBINE
    i = pl.program_id(0)
    slot = i % 2

    def start_row(d_ref, s, t):
        for kx in range(TOP_K):
            d = d_ref[kx, t]
            pltpu.make_async_copy(_tile(ys_hbm, d), _tile(buf.at[s, kx], t), sems.at[s]).start(priority=kx % 2)

    def wait_tile(s):
        for kx in range(TOP_K):
            pltpu.make_async_copy(ys_hbm.at[pl.ds(0, tm * ROW_TILE)], buf.at[s, kx], sems.at[s]).wait()

    @pl.when(i == 0)
    def _():
        lax.fori_loop(0, tm, lambda t, c: (start_row(dest_ref, 0, t), c)[1], 0)

    wait_tile(slot)
    rc = COMBINE_ROW_CHUNK
    for c in range(tm // rc):
        for t in range(c * rc, (c + 1) * rc):
            start_row(dest_next_ref, 1 - slot, t)
        tw = tw_ref[c * rc:(c + 1) * rc, :]
        ffn = tw[:, 0:1] * _load_row_tiles(buf.at[slot, 0], rc, c * rc)
        for kx in range(1, TOP_K):
            ffn = ffn + tw[:, kx:kx + 1] * _load_row_tiles(buf.at[slot, kx], rc, c * rc)
        h1 = _load_row_tiles(h1t_ref, rc, c * rc)
        out_ref[c * rc:(c + 1) * rc, :] = _ln(DN_ALPHA * h1 + ffn, g2_ref[...], b2_ref[...])

    @pl.when(i == pl.num_programs(0) - 1)
    def _():
        wait_tile(1 - slot)


def _combine(dest_km, h1t, tw, g2, b2, ys):
    n_tok = h1t.shape[0] // ROW_TILE
    d = ROW_TILE * LANES
    tm = TM_COMBINE
    n_steps = n_tok // tm
    const = lambda i: (0, 0)
    return pl.pallas_call(
        _combine_body, grid=(n_steps,),
        in_specs=[pl.BlockSpec((TOP_K, tm), lambda i: (0, i), memory_space=pltpu.SMEM),
                  pl.BlockSpec((TOP_K, tm), lambda i: (0, jnp.minimum(i + 1, n_steps - 1)),
                               memory_space=pltpu.SMEM),
                  pl.BlockSpec((tm * ROW_TILE, LANES), lambda i: (i, 0)),
                  pl.BlockSpec((tm, LANES), lambda i: (i, 0)),
                  pl.BlockSpec((1, d), const), pl.BlockSpec((1, d), const),
                  pl.BlockSpec(memory_space=pl.ANY)],
        out_specs=pl.BlockSpec((tm, d), lambda i: (i, 0)),
        out_shape=jax.ShapeDtypeStruct((n_tok, d), F32),
        scratch_shapes=[pltpu.VMEM((2, TOP_K, tm * ROW_TILE, LANES), F32), pltpu.SemaphoreType.DMA((2,))],
        name="combine",
        compiler_params=pltpu.CompilerParams(
            dimension_semantics=("arbitrary",), vmem_limit_bytes=40 * 1024 * 1024),
    )(dest_km, dest_km, h1t, tw, g2, b2, ys)


def _rope_tables(seq):
    t = np.arange(seq)
    row = (t // GRID_W).astype(np.float32)
    col = (t % GRID_W).astype(np.float32)
    half = HEAD_DIM // 2
    quarter = half // 2
    inv = (ROPE_THETA ** (-np.arange(quarter, dtype=np.float32) * np.float32(2.0 / half))).astype(np.float32)
    ang_r = row[:, None] * inv[None, :]
    ang_c = col[:, None] * inv[None, :]
    zeros = np.zeros_like(ang_r)
    cos = np.concatenate([np.cos(ang_r), np.cos(ang_r), np.cos(ang_c), np.cos(ang_c)], -1)
    s_lo = np.concatenate([-np.sin(ang_r), zeros, -np.sin(ang_c), zeros], -1)
    s_hi = np.concatenate([zeros, np.sin(ang_r), zeros, np.sin(ang_c)], -1)
    return tuple(jnp.asarray(np.tile(a, (1, LANES // HEAD_DIM)), F32) for a in (cos, s_lo, s_hi))


def _routing(top_i, rank, counts, bm):
    n_tok = top_i.shape[1]
    padded = (counts + bm - 1) // bm * bm
    pad_end = jnp.cumsum(padded)
    pad_start = pad_end - padded
    base = jnp.sum(jnp.where(top_i[:, :, None] == jnp.arange(N_EXPERTS, dtype=jnp.int32)[None, None, :],
                             pad_start[None, None, :], 0), axis=-1)
    dest = (base + rank).astype(jnp.int32)
    n_rows = n_tok * TOP_K + N_EXPERTS * bm
    n_blocks = n_rows // bm
    block_start = jnp.arange(n_blocks, dtype=jnp.int32) * bm
    block_e = jnp.minimum(jnp.sum((pad_end[None, :] <= block_start[:, None]).astype(jnp.int32), axis=1),
                          N_EXPERTS - 1).astype(jnp.int32)
    n_used = (pad_end[-1] // bm).astype(jnp.int32).reshape(1)
    e_ids = jnp.arange(N_EXPERTS, dtype=jnp.int32)
    later = (e_ids[None, :] > e_ids[:, None]) & (counts[None, :] > 0)
    next_of = jnp.min(jnp.where(later, e_ids[None, :], N_EXPERTS), axis=1)
    next_of = jnp.where(next_of == N_EXPERTS, -1, next_of).astype(jnp.int32)
    next_e = jnp.sum(jnp.where(block_e[:, None] == e_ids[None, :], next_of[None, :], 0), axis=1).astype(jnp.int32)
    start_of = jnp.sum(jnp.where(block_e[:, None] == e_ids[None, :], pad_start[None, :], 0), axis=1)
    count_of = jnp.sum(jnp.where(block_e[:, None] == e_ids[None, :], counts[None, :], 0), axis=1)
    n_valid = jnp.clip(count_of - (block_start - start_of), 0, bm).astype(jnp.int32)
    return (dest, block_e, n_used, next_e, n_valid, n_rows, pad_end.astype(jnp.int32),
            padded.astype(jnp.int32))


def kernel(x, ln0_g, ln0_b, w_in, a_sink, b_q_norm, b_k_norm, w_branch_a, w_branch_b, w_out,
           ln1_g, ln1_b, w_router, b_router, w_gate, b_gate, w_up, b_up, w_down, b_down,
           ln2_g, ln2_b):
    bsz, seq, d = x.shape
    assert w_in.shape[0] == DEPTH == 1
    assert seq % TM_PROJ == 0 and seq % TQ_GRID == 0 and seq == (seq // GRID_W) * GRID_W
    n_tok = bsz * seq
    row2 = lambda a: a.reshape(1, -1)

    w_perm = w_in[0].astype(BF16)
    head_id = np.arange(Q_W) // HEAD_DIM
    bd = jnp.asarray(head_id[:, None] == head_id[None, :], BF16)
    gq = jnp.tile(b_q_norm[0].astype(F32), N_HEADS).reshape(1, Q_W)
    gk = jnp.tile(b_k_norm[0].astype(F32), N_KV).reshape(1, KV_W)

    qa, ka, vat, qb, kb, vbt, sga, sgb = _in_proj(
        x, row2(ln0_g), row2(ln0_b), w_perm, bd, gq, gk, _rope_tables(seq))
    oa = _win_attn(a_sink[0].astype(F32), qa, ka, vat)
    ob = _grid_attn(qb, kb, vbt)

    wr = jnp.zeros((LANES, d), BF16).at[:N_EXPERTS].set(w_router[0].T.astype(BF16))
    br = jnp.zeros((LANES, 1), F32).at[:N_EXPERTS, 0].set(b_router[0])
    h1t, tr, tw, cnt = _post_attn(
        x, oa, ob, sga, sgb, row2(ln0_g), row2(ln0_b),
        w_branch_a[0].astype(BF16), w_branch_b[0].astype(BF16),
        w_out[0].astype(BF16), row2(ln1_g[0]), row2(ln1_b[0]), wr, br)
    h1t = h1t.reshape(n_tok * ROW_TILE, LANES)
    top_i, rank = tr[:TOP_K], tr[TOP_K:]
    tw = tw.reshape(n_tok, LANES)

    counts = cnt[:, 0].astype(jnp.int32)
    dest, block_e, n_used, next_e, n_valid, n_rows, pad_end, padded = _routing(top_i, rank, counts, BM_EXPERT)
    xs = _dispatch(dest, pad_end, padded, n_used, h1t, n_rows)
    ys = _experts(block_e, n_used, next_e, n_valid, xs, w_gate[0], b_gate[0], w_up[0], b_up[0], w_down[0], b_down[0])
    out = _combine(dest, h1t, tw, row2(ln2_g[0]), row2(ln2_b[0]), ys)
    return out.reshape(bsz, seq, d)
```

```python
import functools

import jax
import jax.numpy as jnp
import numpy as np
from jax import lax
from jax.experimental import pallas as pl
from jax.experimental.pallas import tpu as pltpu

HEAD_DIM = 64
N_HEADS = 8
N_KV = 2
WINDOW = 128
BLOCK = 128
GRID_W = 64
ROPE_THETA = 10000.0
N_EXPERTS = 32
TOP_K = 4
SWIGLU_LIMIT = 7.0
SWIGLU_ALPHA = 1.702
LN_EPS = 1e-5
RMS_EPS = 1e-6
NEG_INF = -1e30
DEPTH = 1
DN_ALPHA = (2.0 * DEPTH) ** 0.25
ALIBI_SLOPES = tuple(2.0 ** (-8.0 * (h + 1) / N_HEADS) for h in range(N_HEADS))
QK_SCALE = HEAD_DIM ** -0.5
LOG2_E = 1.4426950408889634

LANES = 128
ROW_TILE = 8
Q_W = N_HEADS * HEAD_DIM
KV_W = N_KV * HEAD_DIM

TM_PROJ = 512
TM_IN_PROJ = 1024
PROJ_ROW_GROUPS = 2
WIN_BLOCKS = 8
TQ_GRID = 256
KEY_CHUNK = 256
BM_EXPERT = 512
TM_DISPATCH = 1024
TM_COMBINE = 512
COMBINE_ROW_CHUNK = 32

F32 = jnp.float32
BF16 = jnp.bfloat16


def _ln(x, g, b):
    mu = jnp.mean(x, -1, keepdims=True)
    xc = x - mu
    var = jnp.mean(xc * xc, -1, keepdims=True)
    return xc * lax.rsqrt(var + LN_EPS) * g + b


def _dot(a, b):
    return jnp.dot(a, b, preferred_element_type=F32)


def _dot_nt(a, b):
    return lax.dot_general(a, b, (((1,), (1,)), ((), ())), preferred_element_type=F32)


def _load_row_tiles(ref, rows, first=0):
    return jnp.concatenate(
        [ref[pl.ds(first * ROW_TILE + c, rows, stride=ROW_TILE), :] for c in range(ROW_TILE)], axis=1)


def _store_row_tiles(ref, val, rows):
    for c in range(ROW_TILE):
        ref[pl.ds(c, rows, stride=ROW_TILE), :] = val[:, c * LANES:(c + 1) * LANES]


def _tile(ref, row):
    start = row * ROW_TILE
    if not isinstance(row, int):
        start = pl.multiple_of(start, ROW_TILE)
    return ref.at[pl.ds(start, ROW_TILE)]


def _in_proj_body(x_ref, g0_ref, b0_ref, w_ref, bd_ref, gq_ref, gk_ref, c_ref, s1_ref, s2_ref,
                  qa_ref, ka_ref, vat_ref, qb_ref, kb_ref, vbt_ref, sga_ref, sgb_ref):
    tm = x_ref.shape[1]
    d = sga_ref.shape[-1]
    o_kva, o_qb, o_kvb, o_g = Q_W, Q_W + 2 * KV_W, 2 * Q_W + 2 * KV_W, 2 * Q_W + 4 * KV_W
    both = lambda k: jnp.concatenate([k, pltpu.roll(k, HEAD_DIM, 1)], axis=1).astype(BF16)
    rg = tm // PROJ_ROW_GROUPS
    for grp in range(PROJ_ROW_GROUPS):
        rows = slice(grp * rg, (grp + 1) * rg)
        hb = _ln(x_ref[0, rows, :], g0_ref[...], b0_ref[...]).astype(BF16)

        def proj(lo, hi):
            return _dot(hb, w_ref[:, lo:hi])

        def norm_rope(t, width, g_ref):
            ss = _dot((t * t).astype(BF16), bd_ref[:width, :width])
            r = lax.rsqrt(ss * (1.0 / HEAD_DIM) + RMS_EPS)
            reps = width // LANES
            tab = lambda ref: jnp.concatenate([ref[rows, :]] * reps, axis=1)
            y = t * g_ref[...]
            rot = (y * tab(c_ref) + pltpu.roll(y, width - 16, 1) * tab(s1_ref)
                   + pltpu.roll(y, 16, 1) * tab(s2_ref))
            return rot * r

        qa_ref[0, rows, :] = (proj(0, o_kva) * (QK_SCALE * LOG2_E)).astype(BF16)
        kva = proj(o_kva, o_qb)
        ka_ref[0, rows, :] = both(kva[:, :KV_W])
        vat_ref[0, :, rows] = kva[:, KV_W:].T.astype(BF16)
        qb = norm_rope(proj(o_qb, o_kvb), Q_W, gq_ref)
        qb_ref[0, rows, :] = (qb * (QK_SCALE * LOG2_E)).astype(BF16)
        kvb = proj(o_kvb, o_g)
        kb_ref[0, rows, :] = both(norm_rope(kvb[:, :KV_W], KV_W, gk_ref))
        vbt_ref[0, :, rows] = kvb[:, KV_W:].T.astype(BF16)
        sga_ref[0, rows, :] = jax.nn.sigmoid(proj(o_g, o_g + d)).astype(BF16)
        sgb_ref[0, rows, :] = jax.nn.sigmoid(proj(o_g + d, o_g + 2 * d)).astype(BF16)


def _in_proj(x, g0, b0, w, bd, gq, gk, tabs):
    bsz, seq, d = x.shape
    tm = TM_IN_PROJ
    n_in = w.shape[1]
    const = lambda i, j: (0, 0)
    tok3 = lambda i, j: (j, i, 0)
    tab = lambda i, j: (i, 0)
    in_specs = [
        pl.BlockSpec((1, tm, d), tok3),
        pl.BlockSpec((1, d), const), pl.BlockSpec((1, d), const),
        pl.BlockSpec((d, n_in), const),
        pl.BlockSpec((Q_W, Q_W), const),
        pl.BlockSpec((1, Q_W), const), pl.BlockSpec((1, KV_W), const),
        pl.BlockSpec((tm, LANES), tab), pl.BlockSpec((tm, LANES), tab), pl.BlockSpec((tm, LANES), tab),
    ]
    tr3 = lambda i, j: (j, 0, i)
    out_specs = [
        pl.BlockSpec((1, tm, Q_W), tok3), pl.BlockSpec((1, tm, 2 * KV_W), tok3),
        pl.BlockSpec((1, KV_W, tm), tr3),
        pl.BlockSpec((1, tm, Q_W), tok3), pl.BlockSpec((1, tm, 2 * KV_W), tok3),
        pl.BlockSpec((1, KV_W, tm), tr3),
        pl.BlockSpec((1, tm, d), tok3), pl.BlockSpec((1, tm, d), tok3),
    ]
    out_shape = [
        jax.ShapeDtypeStruct((bsz, seq, Q_W), BF16), jax.ShapeDtypeStruct((bsz, seq, 2 * KV_W), BF16),
        jax.ShapeDtypeStruct((bsz, KV_W, seq), BF16),
        jax.ShapeDtypeStruct((bsz, seq, Q_W), BF16), jax.ShapeDtypeStruct((bsz, seq, 2 * KV_W), BF16),
        jax.ShapeDtypeStruct((bsz, KV_W, seq), BF16),
        jax.ShapeDtypeStruct((bsz, seq, d), BF16), jax.ShapeDtypeStruct((bsz, seq, d), BF16),
    ]
    return pl.pallas_call(
        _in_proj_body, grid=(seq // tm, bsz), in_specs=in_specs, out_specs=out_specs,
        out_shape=out_shape, name="in_proj",
        compiler_params=pltpu.CompilerParams(
            dimension_semantics=("arbitrary", "arbitrary"), vmem_limit_bytes=48 * 1024 * 1024),
    )(x, g0, b0, w, bd, gq, gk, *tabs)


def _half_mask(rows, c):
    lane = lax.broadcasted_iota(jnp.int32, (rows, LANES), 1)
    return (lane >= HEAD_DIM) if c == 1 else (lane < HEAD_DIM)


def _win_attn_body(sink_ref, q_ref, *refs, seq):
    nk = WIN_BLOCKS + 2
    k_refs, v_refs, o_ref = refs[:nk], refs[nk:2 * nk], refs[2 * nk]
    kk = lax.broadcasted_iota(jnp.int32, (3 * BLOCK, BLOCK), 0)
    qq = lax.broadcasted_iota(jnp.int32, (3 * BLOCK, BLOCK), 1)
    dist_i = jnp.abs(kk - BLOCK - qq)
    dist = dist_i.astype(F32)
    ones = jnp.ones((2 * ROW_TILE, 3 * BLOCK), BF16)
    for blk in range(WIN_BLOCKS):
        n = pl.program_id(1) * WIN_BLOCKS + blk
        rows = slice(blk * BLOCK, (blk + 1) * BLOCK)
        k2 = jnp.concatenate([r[0] for r in k_refs[blk:blk + 3]], axis=0)
        vt = jnp.concatenate([r[0] for r in v_refs[blk:blk + 3]], axis=1)
        k_pos = n * BLOCK - BLOCK + kk
        valid = (dist_i <= WINDOW) & (k_pos >= 0) & (k_pos < seq)
        slabs = [q_ref[0, rows, j * LANES:(j + 1) * LANES] for j in range(4)]
        variant = [[h for h in range(N_HEADS) if (h // 4 == h % 2) == straight] for straight in (True, False)]
        scores = {}
        for v, heads in enumerate(variant):
            qm = jnp.concatenate([jnp.where(_half_mask(BLOCK, h % 2), slabs[h // 2], jnp.zeros_like(slabs[0]))
                                  for h in heads], axis=0)
            st_v = _dot_nt(k2[:, v * LANES:(v + 1) * LANES], qm)
            for col, h in enumerate(heads):
                scores[h] = st_v[:, col * BLOCK:(col + 1) * BLOCK]
        outs = {}
        for c in range(N_KV):
            heads = range(c * (N_HEADS // N_KV), (c + 1) * (N_HEADS // N_KV))
            ps, sinks = [], []
            for h in heads:
                st = scores[h] + jnp.where(valid, (-ALIBI_SLOPES[h] * LOG2_E) * dist, NEG_INF)
                sk = sink_ref[h] * LOG2_E
                m = jnp.maximum(jnp.max(st, axis=0, keepdims=True), sk)
                ps.append(jnp.exp2((st - m).astype(BF16)))
                sinks.append(jnp.exp2(sk - m))
            va = jnp.concatenate([vt[c * HEAD_DIM:(c + 1) * HEAD_DIM, :], ones], axis=0)
            ot = _dot(va, jnp.concatenate(ps, axis=1))
            ot = ot[:HEAD_DIM] / (ot[HEAD_DIM:HEAD_DIM + 1] + jnp.concatenate(sinks, axis=1))
            for col, h in enumerate(heads):
                outs[h] = ot[:, col * BLOCK:(col + 1) * BLOCK]
        for j in range(4):
            pair = jnp.concatenate([outs[2 * j], outs[2 * j + 1]], axis=0)
            o_ref[0, rows, j * LANES:(j + 1) * LANES] = pair.T.astype(BF16)


def _win_attn(sink, qa, ka, vat):
    bsz, seq, _ = qa.shape
    nb = seq // BLOCK
    wb = WIN_BLOCKS
    qmap = lambda b, n: (b, n, 0)
    blk = lambda off: (lambda n: jnp.clip(n * wb + off, 0, nb - 1))
    kspec = lambda f: pl.BlockSpec((1, BLOCK, 2 * KV_W), lambda b, n: (b, f(n), 0))
    vspec = lambda f: pl.BlockSpec((1, KV_W, BLOCK), lambda b, n: (b, 0, f(n)))
    offs = range(-1, wb + 1)
    return pl.pallas_call(
        functools.partial(_win_attn_body, seq=seq), grid=(bsz, nb // wb),
        in_specs=[pl.BlockSpec(memory_space=pltpu.SMEM), pl.BlockSpec((1, wb * BLOCK, Q_W), qmap)]
        + [kspec(blk(o)) for o in offs] + [vspec(blk(o)) for o in offs],
        out_specs=pl.BlockSpec((1, wb * BLOCK, Q_W), qmap),
        out_shape=jax.ShapeDtypeStruct((bsz, seq, Q_W), BF16), name="win_attn",
        compiler_params=pltpu.CompilerParams(dimension_semantics=("arbitrary", "arbitrary")),
    )(sink, qa, *([ka] * (wb + 2)), *([vat] * (wb + 2)))


def _grid_attn_body(q_ref, k_ref, vt_ref, o_ref, s0_ref, s1_ref, p0_ref, p1_ref):
    tq = q_ref.shape[1]
    seq = k_ref.shape[1]
    kc = KEY_CHUNK
    n_chunks = seq // kc
    s_bufs, p_bufs = (s0_ref, s1_ref), (p0_ref, p1_ref)
    n_heads = N_HEADS
    ones = jnp.ones((2 * ROW_TILE, seq), BF16)

    def masked_q(h):
        slab = q_ref[0, :, (h // 2) * LANES:(h // 2 + 1) * LANES]
        return jnp.where(_half_mask(tq, h % 2), slab, jnp.zeros_like(slab))

    def score_chunk(h, qm, kb, m8):
        v = 0 if h // 4 == h % 2 else 1
        sc = _dot_nt(k_ref[0, kb * kc:(kb + 1) * kc, v * LANES:(v + 1) * LANES], qm)
        s_bufs[h % 2][kb * kc:(kb + 1) * kc, :] = sc
        cm = jnp.max(sc.reshape(kc // ROW_TILE, ROW_TILE, tq), axis=0)
        return cm if m8 is None else jnp.maximum(m8, cm)

    def prob_chunk(h, kb, m):
        x = s_bufs[h % 2][kb * kc:(kb + 1) * kc, :] - m
        p_bufs[h % 2][kb * kc:(kb + 1) * kc, :] = jnp.exp2(x.astype(BF16))

    qm = masked_q(0)
    m8 = None
    for kb in range(n_chunks):
        m8 = score_chunk(0, qm, kb, m8)
    outs = []
    for h in range(n_heads):
        m = jnp.max(m8, axis=0, keepdims=True)
        nxt = h + 1 < n_heads
        if nxt:
            qm = masked_q(h + 1)
            m8 = None
        for kb in range(n_chunks):
            if nxt:
                m8 = score_chunk(h + 1, qm, kb, m8)
            prob_chunk(h, kb, m)
        c = h // (n_heads // N_KV)
        va = jnp.concatenate([vt_ref[0, c * HEAD_DIM:(c + 1) * HEAD_DIM, :], ones], axis=0)
        ot = _dot(va, p_bufs[h % 2][...])
        outs.append(ot[:HEAD_DIM] / ot[HEAD_DIM:HEAD_DIM + 1])
        if h % 2 == 1:
            o_ref[0, :, (h // 2) * LANES:(h // 2 + 1) * LANES] = jnp.concatenate(outs, axis=0).T.astype(BF16)
            outs = []


def _grid_attn(qb, kb, vbt):
    bsz, seq, _ = qb.shape
    tq = TQ_GRID
    return pl.pallas_call(
        _grid_attn_body, grid=(bsz, seq // tq),
        in_specs=[pl.BlockSpec((1, tq, Q_W), lambda b, n: (b, n, 0)),
                  pl.BlockSpec((1, seq, 2 * KV_W), lambda b, n: (b, 0, 0)),
                  pl.BlockSpec((1, KV_W, seq), lambda b, n: (b, 0, 0))],
        out_specs=pl.BlockSpec((1, tq, Q_W), lambda b, n: (b, n, 0)),
        scratch_shapes=[pltpu.VMEM((seq, tq), F32), pltpu.VMEM((seq, tq), F32),
                        pltpu.VMEM((seq, tq), BF16), pltpu.VMEM((seq, tq), BF16)],
        out_shape=jax.ShapeDtypeStruct((bsz, seq, Q_W), BF16), name="grid_attn",
        compiler_params=pltpu.CompilerParams(
            dimension_semantics=("arbitrary", "arbitrary"), vmem_limit_bytes=40 * 1024 * 1024),
    )(qb, kb, vbt)


def _post_attn_body(x_ref, oa_ref, ob_ref, sga_ref, sgb_ref, g0_ref, b0_ref, wa_ref, wb_ref, wo_ref,
                    g1_ref, b1_ref, wr_ref, br_ref, h1t_ref, tr_ref, tw_ref, cnt_out_ref, cnt_ref):
    h0 = _ln(x_ref[0], g0_ref[...], b0_ref[...])
    out_a = _dot(oa_ref[0], wa_ref[...])
    out_b = _dot(ob_ref[0], wb_ref[...])
    merged = sga_ref[0].astype(F32) * out_a + sgb_ref[0].astype(F32) * out_b
    mix = _dot(merged.astype(BF16), wo_ref[...])
    h1 = _ln(DN_ALPHA * h0 + mix, g1_ref[...], b1_ref[...])
    tm = h1.shape[0]
    _store_row_tiles(h1t_ref.at[0], h1, tm)

    logits = (_dot_nt(wr_ref[...], h1.astype(BF16)) + br_ref[...])[:N_EXPERTS]
    sub = lax.broadcasted_iota(jnp.int32, (N_EXPERTS, tm), 0)
    cur = logits
    vals, idxs = [], []
    for _ in range(TOP_K):
        mv = jnp.max(cur, axis=0, keepdims=True)
        ix = jnp.min(jnp.where(cur == mv, sub, N_EXPERTS), axis=0, keepdims=True)
        vals.append(mv)
        idxs.append(ix)
        cur = jnp.where(sub == ix, -jnp.inf, cur)
    es = [jnp.exp(v - vals[0]) for v in vals]
    tot = es[0] + es[1] + es[2] + es[3]

    @pl.when((pl.program_id(0) == 0) & (pl.program_id(1) == 0))
    def _():
        cnt_ref[...] = jnp.zeros_like(cnt_ref)

    sel = jnp.zeros((N_EXPERTS, tm), F32)
    for kx in range(TOP_K):
        sel = sel + (sub == idxs[kx]).astype(F32)
    r_i = lax.broadcasted_iota(jnp.int32, (tm, tm), 0)
    c_i = lax.broadcasted_iota(jnp.int32, (tm, tm), 1)
    tri = (r_i < c_i).astype(BF16)
    rank = _dot(sel.astype(BF16), tri) + cnt_ref[:, 0:1]
    cnt_ref[...] = cnt_ref[...] + jnp.sum(sel, axis=1, keepdims=True)
    cnt_out_ref[...] = cnt_ref[...]

    rks = [jnp.sum(jnp.where(sub == ix, rank, 0.0), axis=0, keepdims=True).astype(jnp.int32) for ix in idxs]
    tr_ref[...] = jnp.concatenate(idxs + rks, axis=0)
    tw_t = jnp.concatenate([e / tot for e in es] + [jnp.zeros((LANES - TOP_K, tm), F32)], axis=0)
    tw_ref[0] = tw_t.T


def _post_attn(x, oa, ob, sga, sgb, g0, b0, wa, wb, wo, g1, b1, wr, br):
    bsz, seq, d = x.shape
    tm = TM_PROJ
    tok3 = lambda b, i: (b, i, 0)
    const = lambda b, i: (0, 0)
    full = lambda a: pl.BlockSpec(a.shape, const)
    return pl.pallas_call(
        _post_attn_body, grid=(bsz, seq // tm),
        in_specs=[pl.BlockSpec((1, tm, d), tok3),
                  pl.BlockSpec((1, tm, Q_W), tok3), pl.BlockSpec((1, tm, Q_W), tok3),
                  pl.BlockSpec((1, tm, d), tok3), pl.BlockSpec((1, tm, d), tok3),
                  full(g0), full(b0), full(wa), full(wb), full(wo), full(g1), full(b1),
                  full(wr), full(br)],
        out_specs=[pl.BlockSpec((1, tm * ROW_TILE, LANES), tok3),
                   pl.BlockSpec((2 * TOP_K, tm), lambda b, i: (0, b * (seq // tm) + i)),
                   pl.BlockSpec((1, tm, LANES), tok3), pl.BlockSpec((N_EXPERTS, LANES), const)],
        out_shape=[jax.ShapeDtypeStruct((bsz, seq * ROW_TILE, LANES), F32),
                   jax.ShapeDtypeStruct((2 * TOP_K, bsz * seq), jnp.int32),
                   jax.ShapeDtypeStruct((bsz, seq, LANES), F32),
                   jax.ShapeDtypeStruct((N_EXPERTS, LANES), F32)],
        scratch_shapes=[pltpu.VMEM((N_EXPERTS, LANES), F32)],
        name="post_attn",
        compiler_params=pltpu.CompilerParams(
            dimension_semantics=("arbitrary", "arbitrary"), vmem_limit_bytes=48 * 1024 * 1024),
    )(x, oa, ob, sga, sgb, g0, b0, wa, wb, wo, g1, b1, wr, br)


def _dispatch_body(dest_ref, pend_ref, padded_ref, nu_ref, h1t_ref, xs_hbm, zbuf, sem, zsem):
    tm = TM_DISPATCH
    zrows = BM_EXPERT * ROW_TILE
    n_blocks = xs_hbm.shape[0] // zrows

    @pl.when(pl.program_id(0) == 0)
    def _():
        zbuf[...] = jnp.zeros_like(zbuf)
        zero_wait = pltpu.make_async_copy(zbuf, xs_hbm.at[pl.ds(0, zrows)], zsem).wait
        for e in range(N_EXPERTS):
            @pl.when(padded_ref[e] > 0)
            def _():
                start = pl.multiple_of((pend_ref[e] - BM_EXPERT) * ROW_TILE, ROW_TILE)
                pltpu.make_async_copy(zbuf, xs_hbm.at[pl.ds(start, zrows)], zsem).start()
        for b in range(n_blocks - N_EXPERTS, n_blocks):
            @pl.when(b >= nu_ref[0])
            def _():
                pltpu.make_async_copy(zbuf, xs_hbm.at[pl.ds(b * zrows, zrows)], zsem).start()
        for e in range(N_EXPERTS):
            pl.when(padded_ref[e] > 0)(zero_wait)
        for b in range(n_blocks - N_EXPERTS, n_blocks):
            pl.when(b >= nu_ref[0])(zero_wait)

    for t in range(tm):
        for kx in range(TOP_K):
            d = dest_ref[kx, t]
            pltpu.make_async_copy(_tile(h1t_ref, t), _tile(xs_hbm, d), sem).start(priority=kx % 2)
    for _ in range(TOP_K):
        pltpu.make_async_copy(h1t_ref, xs_hbm.at[pl.ds(0, tm * ROW_TILE)], sem).wait()


def _dispatch(dest_km, pad_end, padded, n_used, h1t, n_rows):
    n_tok = h1t.shape[0] // ROW_TILE
    tm = TM_DISPATCH
    return pl.pallas_call(
        _dispatch_body, grid=(n_tok // tm,),
        in_specs=[pl.BlockSpec((TOP_K, tm), lambda i: (0, i), memory_space=pltpu.SMEM),
                  pl.BlockSpec(memory_space=pltpu.SMEM), pl.BlockSpec(memory_space=pltpu.SMEM),
                  pl.BlockSpec(memory_space=pltpu.SMEM),
                  pl.BlockSpec((tm * ROW_TILE, LANES), lambda i: (i, 0))],
        out_specs=pl.BlockSpec(memory_space=pl.ANY),
        out_shape=jax.ShapeDtypeStruct((n_rows * ROW_TILE, LANES), F32),
        scratch_shapes=[pltpu.VMEM((BM_EXPERT * ROW_TILE, LANES), F32),
                        pltpu.SemaphoreType.DMA(()), pltpu.SemaphoreType.DMA(())],
        name="dispatch",
        compiler_params=pltpu.CompilerParams(dimension_semantics=("arbitrary",)),
    )(dest_km, pad_end, padded, n_used, h1t)


def _experts_body(be_ref, nu_ref, nx_ref, nv_ref, xs_ref, bg_ref, bu_ref, bd_ref, wg_hbm, wu_hbm, wd_hbm, ys_ref,
                  stage, wg_s, wu_s, wd_s, wsem):
    i = pl.program_id(0)
    bm = BM_EXPERT
    used = i < nu_ref[0]
    prev = be_ref[jnp.maximum(i - 1, 0)]
    fresh = (i == 0) | (be_ref[i] != prev)

    def weight_copies(e):
        return [pltpu.make_async_copy(w_hbm.at[e], stage.at[n], wsem)
                for n, w_hbm in enumerate((wg_hbm, wu_hbm, wd_hbm))]

    @pl.when(i == 0)
    def _():
        for cp in weight_copies(be_ref[0]):
            cp.start()

    @pl.when(used & fresh)
    def _():
        for cp in weight_copies(be_ref[i]):
            cp.wait()
        wg_s[...] = stage[0].astype(BF16)
        wu_s[...] = stage[1].astype(BF16)
        wd_s[...] = stage[2].astype(BF16)

        @pl.when(nx_ref[i] >= 0)
        def _():
            for cp in weight_copies(nx_ref[i]):
                cp.start()

    def ffn(rows):
        xb = _load_row_tiles(xs_ref, rows).astype(BF16)
        g = _dot(xb, wg_s[...]) + bg_ref[0]
        u = _dot(xb, wu_s[...]) + bu_ref[0]
        g = jnp.minimum(g, SWIGLU_LIMIT)
        u = jnp.clip(u, -SWIGLU_LIMIT, SWIGLU_LIMIT)
        act = g * jax.nn.sigmoid(SWIGLU_ALPHA * g) * (u + 1.0)
        _store_row_tiles(ys_ref, _dot(act.astype(BF16), wd_s[...]) + bd_ref[0], rows)

    half = bm // 2
    full = used & (nv_ref[i] > half)

    @pl.when(full)
    def _():
        ffn(bm)

    @pl.when(used & jnp.logical_not(full))
    def _():
        ffn(half)
        ys_ref[pl.ds(half * ROW_TILE, half * ROW_TILE), :] = jnp.zeros((half * ROW_TILE, LANES), F32)

    @pl.when(jnp.logical_not(used))
    def _():
        ys_ref[...] = jnp.zeros_like(ys_ref)


def _experts(block_e, n_used, next_e, n_valid, xs, wg, bg, wu, bu, wd, bd):
    n_rows = xs.shape[0] // ROW_TILE
    bm = BM_EXPERT
    n_e, d, d_ff = wg.shape
    assert d == d_ff and wd.shape == wg.shape
    row = lambda i, be, nu, nx, nv: (jnp.minimum(i, nu[0] - 1), 0)
    exp3 = lambda i, be, nu, nx, nv: (be[jnp.minimum(i, nu[0] - 1)], 0, 0)
    any_spec = pl.BlockSpec(memory_space=pl.ANY)
    grid_spec = pltpu.PrefetchScalarGridSpec(
        num_scalar_prefetch=4, grid=(n_rows // bm,),
        in_specs=[pl.BlockSpec((bm * ROW_TILE, LANES), row),
                  pl.BlockSpec((1, 1, d_ff), exp3), pl.BlockSpec((1, 1, d_ff), exp3),
                  pl.BlockSpec((1, 1, d), exp3), any_spec, any_spec, any_spec],
        out_specs=pl.BlockSpec((bm * ROW_TILE, LANES), lambda i, be, nu, nx, nv: (i, 0)),
        scratch_shapes=[pltpu.VMEM((3, d, d_ff), F32),
                        pltpu.VMEM((d, d_ff), BF16), pltpu.VMEM((d, d_ff), BF16), pltpu.VMEM((d_ff, d), BF16),
                        pltpu.SemaphoreType.DMA(())])
    return pl.pallas_call(
        _experts_body, grid_spec=grid_spec,
        out_shape=jax.ShapeDtypeStruct((n_rows * ROW_TILE, LANES), F32), name="experts",
        compiler_params=pltpu.CompilerParams(
            dimension_semantics=("arbitrary",), vmem_limit_bytes=48 * 1024 * 1024),
    )(block_e, n_used, next_e, n_valid, xs, bg.reshape(n_e, 1, d_ff), bu.reshape(n_e, 1, d_ff), bd.reshape(n_e, 1, d),
      wg, wu, wd)


def _combine_body(dest_ref, dest_next_ref, h1t_ref, tw_ref, g2_ref, b2_ref, ys_hbm, out_ref, buf, sems):
    tm = TM_COMBINE
    i = pl.program_id(0)
    slot = i % 2

    def start_row(d_ref, s, t):
        for kx in range(TOP_K):
            d = d_ref[kx, t]
            pltpu.make_async_copy(_tile(ys_hbm, d), _tile(buf.at[s, kx], t), sems.at[s]).start(priority=kx % 2)

    def wait_tile(s):
        for kx in range(TOP_K):
            pltpu.make_async_copy(ys_hbm.at[pl.ds(0, tm * ROW_TILE)], buf.at[s, kx], sems.at[s]).wait()

    @pl.when(i == 0)
    def _():
        lax.fori_loop(0, tm, lambda t, c: (start_row(dest_ref, 0, t), c)[1], 0)

    wait_tile(slot)
    rc = COMBINE_ROW_CHUNK
    for c in range(tm // rc):
        for t in range(c * rc, (c + 1) * rc):
            start_row(dest_next_ref, 1 - slot, t)
        tw = tw_ref[c * rc:(c + 1) * rc, :]
        ffn = tw[:, 0:1] * _load_row_tiles(buf.at[slot, 0], rc, c * rc)
        for kx in range(1, TOP_K):
            ffn = ffn + tw[:, kx:kx + 1] * _load_row_tiles(buf.at[slot, kx], rc, c * rc)
        h1 = _load_row_tiles(h1t_ref, rc, c * rc)
        out_ref[c * rc:(c + 1) * rc, :] = _ln(DN_ALPHA * h1 + ffn, g2_ref[...], b2_ref[...])

    @pl.when(i == pl.num_programs(0) - 1)
    def _():
        wait_tile(1 - slot)


def _combine(dest_km, h1t, tw, g2, b2, ys):
    n_tok = h1t.shape[0] // ROW_TILE
    d = ROW_TILE * LANES
    tm = TM_COMBINE
    n_steps = n_tok // tm
    const = lambda i: (0, 0)
    return pl.pallas_call(
        _combine_body, grid=(n_steps,),
        in_specs=[pl.BlockSpec((TOP_K, tm), lambda i: (0, i), memory_space=pltpu.SMEM),
                  pl.BlockSpec((TOP_K, tm), lambda i: (0, jnp.minimum(i + 1, n_steps - 1)),
                               memory_space=pltpu.SMEM),
                  pl.BlockSpec((tm * ROW_TILE, LANES), lambda i: (i, 0)),
                  pl.BlockSpec((tm, LANES), lambda i: (i, 0)),
                  pl.BlockSpec((1, d), const), pl.BlockSpec((1, d), const),
                  pl.BlockSpec(memory_space=pl.ANY)],
        out_specs=pl.BlockSpec((tm, d), lambda i: (i, 0)),
        out_shape=jax.ShapeDtypeStruct((n_tok, d), F32),
        scratch_shapes=[pltpu.VMEM((2, TOP_K, tm * ROW_TILE, LANES), F32), pltpu.SemaphoreType.DMA((2,))],
        name="combine",
        compiler_params=pltpu.CompilerParams(
            dimension_semantics=("arbitrary",), vmem_limit_bytes=40 * 1024 * 1024),
    )(dest_km, dest_km, h1t, tw, g2, b2, ys)


def _rope_tables(seq):
    t = np.arange(seq)
    row = (t // GRID_W).astype(np.float32)
    col = (t % GRID_W).astype(np.float32)
    half = HEAD_DIM // 2
    quarter = half // 2
    inv = (ROPE_THETA ** (-np.arange(quarter, dtype=np.float32) * np.float32(2.0 / half))).astype(np.float32)
    ang_r = row[:, None] * inv[None, :]
    ang_c = col[:, None] * inv[None, :]
    zeros = np.zeros_like(ang_r)
    cos = np.concatenate([np.cos(ang_r), np.cos(ang_r), np.cos(ang_c), np.cos(ang_c)], -1)
    s_lo = np.concatenate([-np.sin(ang_r), zeros, -np.sin(ang_c), zeros], -1)
    s_hi = np.concatenate([zeros, np.sin(ang_r), zeros, np.sin(ang_c)], -1)
    return tuple(jnp.asarray(np.tile(a, (1, LANES // HEAD_DIM)), F32) for a in (cos, s_lo, s_hi))


def _routing(top_i, rank, counts, bm):
    n_tok = top_i.shape[1]
    padded = (counts + bm - 1) // bm * bm
    pad_end = jnp.cumsum(padded)
    pad_start = pad_end - padded
    base = jnp.sum(jnp.where(top_i[:, :, None] == jnp.arange(N_EXPERTS, dtype=jnp.int32)[None, None, :],
                             pad_start[None, None, :], 0), axis=-1)
    dest = (base + rank).astype(jnp.int32)
    n_rows = n_tok * TOP_K + N_EXPERTS * bm
    n_blocks = n_rows // bm
    block_start = jnp.arange(n_blocks, dtype=jnp.int32) * bm
    block_e = jnp.minimum(jnp.sum((pad_end[None, :] <= block_start[:, None]).astype(jnp.int32), axis=1),
                          N_EXPERTS - 1).astype(jnp.int32)
    n_used = (pad_end[-1] // bm).astype(jnp.int32).reshape(1)
    e_ids = jnp.arange(N_EXPERTS, dtype=jnp.int32)
    later = (e_ids[None, :] > e_ids[:, None]) & (counts[None, :] > 0)
    next_of = jnp.min(jnp.where(later, e_ids[None, :], N_EXPERTS), axis=1)
    next_of = jnp.where(next_of == N_EXPERTS, -1, next_of).astype(jnp.int32)
    next_e = jnp.sum(jnp.where(block_e[:, None] == e_ids[None, :], next_of[None, :], 0), axis=1).astype(jnp.int32)
    start_of = jnp.sum(jnp.where(block_e[:, None] == e_ids[None, :], pad_start[None, :], 0), axis=1)
    count_of = jnp.sum(jnp.where(block_e[:, None] == e_ids[None, :], counts[None, :], 0), axis=1)
    n_valid = jnp.clip(count_of - (block_start - start_of), 0, bm).astype(jnp.int32)
    return (dest, block_e, n_used, next_e, n_valid, n_rows, pad_end.astype(jnp.int32),
            padded.astype(jnp.int32))


def kernel(x, ln0_g, ln0_b, w_in, a_sink, b_q_norm, b_k_norm, w_branch_a, w_branch_b, w_out,
           ln1_g, ln1_b, w_router, b_router, w_gate, b_gate, w_up, b_up, w_down, b_down,
           ln2_g, ln2_b):
    bsz, seq, d = x.shape
    assert w_in.shape[0] == DEPTH == 1
    assert seq % TM_PROJ == 0 and seq % TQ_GRID == 0 and seq == (seq // GRID_W) * GRID_W
    n_tok = bsz * seq
    row2 = lambda a: a.reshape(1, -1)

    w_perm = w_in[0].astype(BF16)
    head_id = np.arange(Q_W) // HEAD_DIM
    bd = jnp.asarray(head_id[:, None] == head_id[None, :], BF16)
    gq = jnp.tile(b_q_norm[0].astype(F32), N_HEADS).reshape(1, Q_W)
    gk = jnp.tile(b_k_norm[0].astype(F32), N_KV).reshape(1, KV_W)

    qa, ka, vat, qb, kb, vbt, sga, sgb = _in_proj(
        x, row2(ln0_g), row2(ln0_b), w_perm, bd, gq, gk, _rope_tables(seq))
    oa = _win_attn(a_sink[0].astype(F32), qa, ka, vat)
    ob = _grid_attn(qb, kb, vbt)

    wr = jnp.zeros((LANES, d), BF16).at[:N_EXPERTS].set(w_router[0].T.astype(BF16))
    br = jnp.zeros((LANES, 1), F32).at[:N_EXPERTS, 0].set(b_router[0])
    h1t, tr, tw, cnt = _post_attn(
        x, oa, ob, sga, sgb, row2(ln0_g), row2(ln0_b),
        w_branch_a[0].astype(BF16), w_branch_b[0].astype(BF16),
        w_out[0].astype(BF16), row2(ln1_g[0]), row2(ln1_b[0]), wr, br)
    h1t = h1t.reshape(n_tok * ROW_TILE, LANES)
    top_i, rank = tr[:TOP_K], tr[TOP_K:]
    tw = tw.reshape(n_tok, LANES)

    counts = cnt[:, 0].astype(jnp.int32)
    dest, block_e, n_used, next_e, n_valid, n_rows, pad_end, padded = _routing(top_i, rank, counts, BM_EXPERT)
    xs = _dispatch(dest, pad_end, padded, n_used, h1t, n_rows)
    ys = _experts(block_e, n_used, next_e, n_valid, xs, w_gate[0], b_gate[0], w_up[0], b_up[0], w_down[0], b_down[0])
    out = _combine(dest, h1t, tw, row2(ln2_g[0]), row2(ln2_b[0]), ys)
    return out.reshape(bsz, seq, d)
```

```python
import functools

import jax
import jax.numpy as jnp
import numpy as np
from jax import lax
from jax.experimental import pallas as pl
from jax.experimental.pallas import tpu as pltpu

HEAD_DIM = 64
N_HEADS = 8
N_KV = 2
WINDOW = 128
BLOCK = 128
GRID_W = 64
ROPE_THETA = 10000.0
N_EXPERTS = 32
TOP_K = 4
SWIGLU_LIMIT = 7.0
SWIGLU_ALPHA = 1.702
LN_EPS = 1e-5
RMS_EPS = 1e-6
NEG_INF = -1e30
DEPTH = 1
DN_ALPHA = (2.0 * DEPTH) ** 0.25
ALIBI_SLOPES = tuple(2.0 ** (-8.0 * (h + 1) / N_HEADS) for h in range(N_HEADS))
QK_SCALE = HEAD_DIM ** -0.5
LOG2_E = 1.4426950408889634

LANES = 128
ROW_TILE = 8
Q_W = N_HEADS * HEAD_DIM
KV_W = N_KV * HEAD_DIM

TM_PROJ = 1024
TM_IN_PROJ = 1024
PROJ_ROW_GROUPS = 2
WIN_BLOCKS = 8
TQ_GRID = 256
KEY_CHUNK = 256
BM_EXPERT = 512
TM_DISPATCH = 512
TM_COMBINE = 512
COMBINE_ROW_CHUNK = 32

F32 = jnp.float32
BF16 = jnp.bfloat16


def _ln(x, g, b):
    mu = jnp.mean(x, -1, keepdims=True)
    xc = x - mu
    var = jnp.mean(xc * xc, -1, keepdims=True)
    return xc * lax.rsqrt(var + LN_EPS) * g + b


def _dot(a, b):
    return jnp.dot(a, b, preferred_element_type=F32)


def _dot_nt(a, b):
    return lax.dot_general(a, b, (((1,), (1,)), ((), ())), preferred_element_type=F32)


def _load_row_tiles(ref, rows, first=0):
    return jnp.concatenate(
        [ref[pl.ds(first * ROW_TILE + c, rows, stride=ROW_TILE), :] for c in range(ROW_TILE)], axis=1)


def _store_row_tiles(ref, val, rows):
    for c in range(ROW_TILE):
        ref[pl.ds(c, rows, stride=ROW_TILE), :] = val[:, c * LANES:(c + 1) * LANES]


def _tile(ref, row):
    start = row * ROW_TILE
    if not isinstance(row, int):
        start = pl.multiple_of(start, ROW_TILE)
    return ref.at[pl.ds(start, ROW_TILE)]


def _in_proj_body(x_ref, g0_ref, b0_ref, w_ref, bd_ref, gq_ref, gk_ref, c_ref, s1_ref, s2_ref,
                  qa_ref, ka_ref, vat_ref, qb_ref, kb_ref, vbt_ref, sga_ref, sgb_ref):
    tm = x_ref.shape[1]
    d = sga_ref.shape[-1]
    o_kva, o_qb, o_kvb, o_g = Q_W, Q_W + 2 * KV_W, 2 * Q_W + 2 * KV_W, 2 * Q_W + 4 * KV_W
    both = lambda k: jnp.concatenate([k, pltpu.roll(k, HEAD_DIM, 1)], axis=1).astype(BF16)
    rg = tm // PROJ_ROW_GROUPS
    for grp in range(PROJ_ROW_GROUPS):
        rows = slice(grp * rg, (grp + 1) * rg)
        hb = _ln(x_ref[0, rows, :], g0_ref[...], b0_ref[...]).astype(BF16)

        def proj(lo, hi):
            return _dot(hb, w_ref[:, lo:hi])

        def norm_rope(t, width, g_ref):
            ss = _dot((t * t).astype(BF16), bd_ref[:width, :width])
            r = lax.rsqrt(ss * (1.0 / HEAD_DIM) + RMS_EPS)
            reps = width // LANES
            tab = lambda ref: jnp.concatenate([ref[rows, :]] * reps, axis=1)
            y = t * g_ref[...]
            rot = (y * tab(c_ref) + pltpu.roll(y, width - 16, 1) * tab(s1_ref)
                   + pltpu.roll(y, 16, 1) * tab(s2_ref))
            return rot * r

        qa_ref[0, rows, :] = (proj(0, o_kva) * (QK_SCALE * LOG2_E)).astype(BF16)
        kva = proj(o_kva, o_qb)
        ka_ref[0, rows, :] = both(kva[:, :KV_W])
        vat_ref[0, :, rows] = kva[:, KV_W:].T.astype(BF16)
        qb = norm_rope(proj(o_qb, o_kvb), Q_W, gq_ref)
        qb_ref[0, rows, :] = (qb * (QK_SCALE * LOG2_E)).astype(BF16)
        kvb = proj(o_kvb, o_g)
        kb_ref[0, rows, :] = both(norm_rope(kvb[:, :KV_W], KV_W, gk_ref))
        vbt_ref[0, :, rows] = kvb[:, KV_W:].T.astype(BF16)
        sga_ref[0, rows, :] = jax.nn.sigmoid(proj(o_g, o_g + d)).astype(BF16)
        sgb_ref[0, rows, :] = jax.nn.sigmoid(proj(o_g + d, o_g + 2 * d)).astype(BF16)


def _in_proj(x, g0, b0, w, bd, gq, gk, tabs):
    bsz, seq, d = x.shape
    tm = TM_IN_PROJ
    n_in = w.shape[1]
    const = lambda i, j: (0, 0)
    tok3 = lambda i, j: (j, i, 0)
    tab = lambda i, j: (i, 0)
    in_specs = [
        pl.BlockSpec((1, tm, d), tok3),
        pl.BlockSpec((1, d), const), pl.BlockSpec((1, d), const),
        pl.BlockSpec((d, n_in), const),
        pl.BlockSpec((Q_W, Q_W), const),
        pl.BlockSpec((1, Q_W), const), pl.BlockSpec((1, KV_W), const),
        pl.BlockSpec((tm, LANES), tab), pl.BlockSpec((tm, LANES), tab), pl.BlockSpec((tm, LANES), tab),
    ]
    tr3 = lambda i, j: (j, 0, i)
    out_specs = [
        pl.BlockSpec((1, tm, Q_W), tok3), pl.BlockSpec((1, tm, 2 * KV_W), tok3),
        pl.BlockSpec((1, KV_W, tm), tr3),
        pl.BlockSpec((1, tm, Q_W), tok3), pl.BlockSpec((1, tm, 2 * KV_W), tok3),
        pl.BlockSpec((1, KV_W, tm), tr3),
        pl.BlockSpec((1, tm, d), tok3), pl.BlockSpec((1, tm, d), tok3),
    ]
    out_shape = [
        jax.ShapeDtypeStruct((bsz, seq, Q_W), BF16), jax.ShapeDtypeStruct((bsz, seq, 2 * KV_W), BF16),
        jax.ShapeDtypeStruct((bsz, KV_W, seq), BF16),
        jax.ShapeDtypeStruct((bsz, seq, Q_W), BF16), jax.ShapeDtypeStruct((bsz, seq, 2 * KV_W), BF16),
        jax.ShapeDtypeStruct((bsz, KV_W, seq), BF16),
        jax.ShapeDtypeStruct((bsz, seq, d), BF16), jax.ShapeDtypeStruct((bsz, seq, d), BF16),
    ]
    return pl.pallas_call(
        _in_proj_body, grid=(seq // tm, bsz), in_specs=in_specs, out_specs=out_specs,
        out_shape=out_shape, name="in_proj",
        compiler_params=pltpu.CompilerParams(
            dimension_semantics=("arbitrary", "arbitrary"), vmem_limit_bytes=48 * 1024 * 1024),
    )(x, g0, b0, w, bd, gq, gk, *tabs)


def _half_mask(rows, c):
    lane = lax.broadcasted_iota(jnp.int32, (rows, LANES), 1)
    return (lane >= HEAD_DIM) if c == 1 else (lane < HEAD_DIM)


def _win_attn_body(sink_ref, q_ref, *refs, seq):
    nk = WIN_BLOCKS + 2
    k_refs, v_refs, o_ref = refs[:nk], refs[nk:2 * nk], refs[2 * nk]
    kk = lax.broadcasted_iota(jnp.int32, (3 * BLOCK, BLOCK), 0)
    qq = lax.broadcasted_iota(jnp.int32, (3 * BLOCK, BLOCK), 1)
    dist_i = jnp.abs(kk - BLOCK - qq)
    dist = dist_i.astype(F32)
    ones = jnp.ones((2 * ROW_TILE, 3 * BLOCK), BF16)
    for blk in range(WIN_BLOCKS):
        n = pl.program_id(1) * WIN_BLOCKS + blk
        rows = slice(blk * BLOCK, (blk + 1) * BLOCK)
        k2 = jnp.concatenate([r[0] for r in k_refs[blk:blk + 3]], axis=0)
        vt = jnp.concatenate([r[0] for r in v_refs[blk:blk + 3]], axis=1)
        k_pos = n * BLOCK - BLOCK + kk
        valid = (dist_i <= WINDOW) & (k_pos >= 0) & (k_pos < seq)
        slabs = [q_ref[0, rows, j * LANES:(j + 1) * LANES] for j in range(4)]
        variant = [[h for h in range(N_HEADS) if (h // 4 == h % 2) == straight] for straight in (True, False)]
        scores = {}
        for v, heads in enumerate(variant):
            qm = jnp.concatenate([jnp.where(_half_mask(BLOCK, h % 2), slabs[h // 2], jnp.zeros_like(slabs[0]))
                                  for h in heads], axis=0)
            st_v = _dot_nt(k2[:, v * LANES:(v + 1) * LANES], qm)
            for col, h in enumerate(heads):
                scores[h] = st_v[:, col * BLOCK:(col + 1) * BLOCK]
        outs = {}
        for c in range(N_KV):
            heads = range(c * (N_HEADS // N_KV), (c + 1) * (N_HEADS // N_KV))
            ps, sinks = [], []
            for h in heads:
                st = scores[h] + jnp.where(valid, (-ALIBI_SLOPES[h] * LOG2_E) * dist, NEG_INF)
                sk = sink_ref[h] * LOG2_E
                m = jnp.maximum(jnp.max(st, axis=0, keepdims=True), sk)
                ps.append(jnp.exp2((st - m).astype(BF16)))
                sinks.append(jnp.exp2(sk - m))
            va = jnp.concatenate([vt[c * HEAD_DIM:(c + 1) * HEAD_DIM, :], ones], axis=0)
            ot = _dot(va, jnp.concatenate(ps, axis=1))
            ot = ot[:HEAD_DIM] / (ot[HEAD_DIM:HEAD_DIM + 1] + jnp.concatenate(sinks, axis=1))
            for col, h in enumerate(heads):
                outs[h] = ot[:, col * BLOCK:(col + 1) * BLOCK]
        for j in range(4):
            pair = jnp.concatenate([outs[2 * j], outs[2 * j + 1]], axis=0)
            o_ref[0, rows, j * LANES:(j + 1) * LANES] = pair.T.astype(BF16)


def _win_attn(sink, qa, ka, vat):
    bsz, seq, _ = qa.shape
    nb = seq // BLOCK
    wb = WIN_BLOCKS
    qmap = lambda b, n: (b, n, 0)
    blk = lambda off: (lambda n: jnp.clip(n * wb + off, 0, nb - 1))
    kspec = lambda f: pl.BlockSpec((1, BLOCK, 2 * KV_W), lambda b, n: (b, f(n), 0))
    vspec = lambda f: pl.BlockSpec((1, KV_W, BLOCK), lambda b, n: (b, 0, f(n)))
    offs = range(-1, wb + 1)
    return pl.pallas_call(
        functools.partial(_win_attn_body, seq=seq), grid=(bsz, nb // wb),
        in_specs=[pl.BlockSpec(memory_space=pltpu.SMEM), pl.BlockSpec((1, wb * BLOCK, Q_W), qmap)]
        + [kspec(blk(o)) for o in offs] + [vspec(blk(o)) for o in offs],
        out_specs=pl.BlockSpec((1, wb * BLOCK, Q_W), qmap),
        out_shape=jax.ShapeDtypeStruct((bsz, seq, Q_W), BF16), name="win_attn",
        compiler_params=pltpu.CompilerParams(dimension_semantics=("arbitrary", "arbitrary")),
    )(sink, qa, *([ka] * (wb + 2)), *([vat] * (wb + 2)))


def _grid_attn_body(q_ref, k_ref, vt_ref, o_ref, s0_ref, s1_ref, p0_ref, p1_ref):
    tq = q_ref.shape[1]
    seq = k_ref.shape[1]
    kc = KEY_CHUNK
    n_chunks = seq // kc
    s_bufs, p_bufs = (s0_ref, s1_ref), (p0_ref, p1_ref)
    n_heads = N_HEADS
    ones = jnp.ones((2 * ROW_TILE, seq), BF16)

    def masked_q(h):
        slab = q_ref[0, :, (h // 2) * LANES:(h // 2 + 1) * LANES]
        return jnp.where(_half_mask(tq, h % 2), slab, jnp.zeros_like(slab))

    def score_chunk(h, qm, kb, m8):
        v = 0 if h // 4 == h % 2 else 1
        sc = _dot_nt(k_ref[0, kb * kc:(kb + 1) * kc, v * LANES:(v + 1) * LANES], qm)
        s_bufs[h % 2][kb * kc:(kb + 1) * kc, :] = sc
        cm = jnp.max(sc.reshape(kc // ROW_TILE, ROW_TILE, tq), axis=0)
        return cm if m8 is None else jnp.maximum(m8, cm)

    def prob_chunk(h, kb, m):
        x = s_bufs[h % 2][kb * kc:(kb + 1) * kc, :] - m
        p_bufs[h % 2][kb * kc:(kb + 1) * kc, :] = jnp.exp2(x.astype(BF16))

    qm = masked_q(0)
    m8 = None
    for kb in range(n_chunks):
        m8 = score_chunk(0, qm, kb, m8)
    outs = []
    for h in range(n_heads):
        m = jnp.max(m8, axis=0, keepdims=True)
        nxt = h + 1 < n_heads
        if nxt:
            qm = masked_q(h + 1)
            m8 = None
        for kb in range(n_chunks):
            if nxt:
                m8 = score_chunk(h + 1, qm, kb, m8)
            prob_chunk(h, kb, m)
        c = h // (n_heads // N_KV)
        va = jnp.concatenate([vt_ref[0, c * HEAD_DIM:(c + 1) * HEAD_DIM, :], ones], axis=0)
        ot = _dot(va, p_bufs[h % 2][...])
        outs.append(ot[:HEAD_DIM] / ot[HEAD_DIM:HEAD_DIM + 1])
        if h % 2 == 1:
            o_ref[0, :, (h // 2) * LANES:(h // 2 + 1) * LANES] = jnp.concatenate(outs, axis=0).T.astype(BF16)
            outs = []


def _grid_attn(qb, kb, vbt):
    bsz, seq, _ = qb.shape
    tq = TQ_GRID
    return pl.pallas_call(
        _grid_attn_body, grid=(bsz, seq // tq),
        in_specs=[pl.BlockSpec((1, tq, Q_W), lambda b, n: (b, n, 0)),
                  pl.BlockSpec((1, seq, 2 * KV_W), lambda b, n: (b, 0, 0)),
                  pl.BlockSpec((1, KV_W, seq), lambda b, n: (b, 0, 0))],
        out_specs=pl.BlockSpec((1, tq, Q_W), lambda b, n: (b, n, 0)),
        scratch_shapes=[pltpu.VMEM((seq, tq), F32), pltpu.VMEM((seq, tq), F32),
                        pltpu.VMEM((seq, tq), BF16), pltpu.VMEM((seq, tq), BF16)],
        out_shape=jax.ShapeDtypeStruct((bsz, seq, Q_W), BF16), name="grid_attn",
        compiler_params=pltpu.CompilerParams(
            dimension_semantics=("arbitrary", "arbitrary"), vmem_limit_bytes=40 * 1024 * 1024),
    )(qb, kb, vbt)


def _post_attn_body(x_ref, oa_ref, ob_ref, sga_ref, sgb_ref, g0_ref, b0_ref, wa_ref, wb_ref, wo_ref,
                    g1_ref, b1_ref, wr_ref, br_ref, h1t_ref, tr_ref, tw_ref, cnt_out_ref, cnt_ref):
    h0 = _ln(x_ref[0], g0_ref[...], b0_ref[...])
    out_a = _dot(oa_ref[0], wa_ref[...])
    out_b = _dot(ob_ref[0], wb_ref[...])
    merged = sga_ref[0].astype(F32) * out_a + sgb_ref[0].astype(F32) * out_b
    mix = _dot(merged.astype(BF16), wo_ref[...])
    h1 = _ln(DN_ALPHA * h0 + mix, g1_ref[...], b1_ref[...])
    tm = h1.shape[0]
    _store_row_tiles(h1t_ref.at[0], h1, tm)

    logits = (_dot_nt(wr_ref[...], h1.astype(BF16)) + br_ref[...])[:N_EXPERTS]
    sub = lax.broadcasted_iota(jnp.int32, (N_EXPERTS, tm), 0)
    cur = logits
    vals, idxs = [], []
    for _ in range(TOP_K):
        mv = jnp.max(cur, axis=0, keepdims=True)
        ix = jnp.min(jnp.where(cur == mv, sub, N_EXPERTS), axis=0, keepdims=True)
        vals.append(mv)
        idxs.append(ix)
        cur = jnp.where(sub == ix, -jnp.inf, cur)
    es = [jnp.exp(v - vals[0]) for v in vals]
    tot = es[0] + es[1] + es[2] + es[3]

    @pl.when((pl.program_id(0) == 0) & (pl.program_id(1) == 0))
    def _():
        cnt_ref[...] = jnp.zeros_like(cnt_ref)

    sel = jnp.zeros((N_EXPERTS, tm), F32)
    for kx in range(TOP_K):
        sel = sel + (sub == idxs[kx]).astype(F32)
    r_i = lax.broadcasted_iota(jnp.int32, (tm, tm), 0)
    c_i = lax.broadcasted_iota(jnp.int32, (tm, tm), 1)
    tri = (r_i < c_i).astype(BF16)
    rank = _dot(sel.astype(BF16), tri) + cnt_ref[:, 0:1]
    cnt_ref[...] = cnt_ref[...] + jnp.sum(sel, axis=1, keepdims=True)
    cnt_out_ref[...] = cnt_ref[...]

    rks = [jnp.sum(jnp.where(sub == ix, rank, 0.0), axis=0, keepdims=True).astype(jnp.int32) for ix in idxs]
    tr_ref[...] = jnp.concatenate(idxs + rks, axis=0)
    tw_t = jnp.concatenate([e / tot for e in es] + [jnp.zeros((LANES - TOP_K, tm), F32)], axis=0)
    tw_ref[0] = tw_t.T


def _post_attn(x, oa, ob, sga, sgb, g0, b0, wa, wb, wo, g1, b1, wr, br):
    bsz, seq, d = x.shape
    tm = TM_PROJ
    tok3 = lambda b, i: (b, i, 0)
    const = lambda b, i: (0, 0)
    full = lambda a: pl.BlockSpec(a.shape, const)
    return pl.pallas_call(
        _post_attn_body, grid=(bsz, seq // tm),
        in_specs=[pl.BlockSpec((1, tm, d), tok3),
                  pl.BlockSpec((1, tm, Q_W), tok3), pl.BlockSpec((1, tm, Q_W), tok3),
                  pl.BlockSpec((1, tm, d), tok3), pl.BlockSpec((1, tm, d), tok3),
                  full(g0), full(b0), full(wa), full(wb), full(wo), full(g1), full(b1),
                  full(wr), full(br)],
        out_specs=[pl.BlockSpec((1, tm * ROW_TILE, LANES), tok3),
                   pl.BlockSpec((2 * TOP_K, tm), lambda b, i: (0, b * (seq // tm) + i)),
                   pl.BlockSpec((1, tm, LANES), tok3), pl.BlockSpec((N_EXPERTS, LANES), const)],
        out_shape=[jax.ShapeDtypeStruct((bsz, seq * ROW_TILE, LANES), F32),
                   jax.ShapeDtypeStruct((2 * TOP_K, bsz * seq), jnp.int32),
                   jax.ShapeDtypeStruct((bsz, seq, LANES), F32),
                   jax.ShapeDtypeStruct((N_EXPERTS, LANES), F32)],
        scratch_shapes=[pltpu.VMEM((N_EXPERTS, LANES), F32)],
        name="post_attn",
        compiler_params=pltpu.CompilerParams(
            dimension_semantics=("arbitrary", "arbitrary"), vmem_limit_bytes=48 * 1024 * 1024),
    )(x, oa, ob, sga, sgb, g0, b0, wa, wb, wo, g1, b1, wr, br)


def _dispatch_body(dest_ref, pend_ref, padded_ref, nu_ref, h1t_ref, xs_hbm, zbuf, sem, zsem):
    tm = TM_DISPATCH
    zrows = BM_EXPERT * ROW_TILE
    n_blocks = xs_hbm.shape[0] // zrows

    @pl.when(pl.program_id(0) == 0)
    def _():
        zbuf[...] = jnp.zeros_like(zbuf)
        zero_wait = pltpu.make_async_copy(zbuf, xs_hbm.at[pl.ds(0, zrows)], zsem).wait
        for e in range(N_EXPERTS):
            @pl.when(padded_ref[e] > 0)
            def _():
                start = pl.multiple_of((pend_ref[e] - BM_EXPERT) * ROW_TILE, ROW_TILE)
                pltpu.make_async_copy(zbuf, xs_hbm.at[pl.ds(start, zrows)], zsem).start()
        for b in range(n_blocks - N_EXPERTS, n_blocks):
            @pl.when(b >= nu_ref[0])
            def _():
                pltpu.make_async_copy(zbuf, xs_hbm.at[pl.ds(b * zrows, zrows)], zsem).start()
        for e in range(N_EXPERTS):
            pl.when(padded_ref[e] > 0)(zero_wait)
        for b in range(n_blocks - N_EXPERTS, n_blocks):
            pl.when(b >= nu_ref[0])(zero_wait)

    for t in range(tm):
        for kx in range(TOP_K):
            d = dest_ref[kx, t]
            pltpu.make_async_copy(_tile(h1t_ref, t), _tile(xs_hbm, d), sem).start(priority=kx % 2)
    for _ in range(TOP_K):
        pltpu.make_async_copy(h1t_ref, xs_hbm.at[pl.ds(0, tm * ROW_TILE)], sem).wait()


def _dispatch(dest_km, pad_end, padded, n_used, h1t, n_rows):
    n_tok = h1t.shape[0] // ROW_TILE
    tm = TM_DISPATCH
    return pl.pallas_call(
        _dispatch_body, grid=(n_tok // tm,),
        in_specs=[pl.BlockSpec((TOP_K, tm), lambda i: (0, i), memory_space=pltpu.SMEM),
                  pl.BlockSpec(memory_space=pltpu.SMEM), pl.BlockSpec(memory_space=pltpu.SMEM),
                  pl.BlockSpec(memory_space=pltpu.SMEM),
                  pl.BlockSpec((tm * ROW_TILE, LANES), lambda i: (i, 0))],
        out_specs=pl.BlockSpec(memory_space=pl.ANY),
        out_shape=jax.ShapeDtypeStruct((n_rows * ROW_TILE, LANES), F32),
        scratch_shapes=[pltpu.VMEM((BM_EXPERT * ROW_TILE, LANES), F32),
                        pltpu.SemaphoreType.DMA(()), pltpu.SemaphoreType.DMA(())],
        name="dispatch",
        compiler_params=pltpu.CompilerParams(dimension_semantics=("arbitrary",)),
    )(dest_km, pad_end, padded, n_used, h1t)


def _experts_body(be_ref, nu_ref, nx_ref, nv_ref, xs_ref, bg_ref, bu_ref, bd_ref, wg_hbm, wu_hbm, wd_hbm, ys_ref,
                  stage, wg_s, wu_s, wd_s, wsem):
    i = pl.program_id(0)
    bm = BM_EXPERT
    used = i < nu_ref[0]
    prev = be_ref[jnp.maximum(i - 1, 0)]
    fresh = (i == 0) | (be_ref[i] != prev)

    def weight_copies(e):
        return [pltpu.make_async_copy(w_hbm.at[e], stage.at[n], wsem)
                for n, w_hbm in enumerate((wg_hbm, wu_hbm, wd_hbm))]

    @pl.when(i == 0)
    def _():
        for cp in weight_copies(be_ref[0]):
            cp.start()

    @pl.when(used & fresh)
    def _():
        for cp in weight_copies(be_ref[i]):
            cp.wait()
        wg_s[...] = stage[0].astype(BF16)
        wu_s[...] = stage[1].astype(BF16)
        wd_s[...] = stage[2].astype(BF16)

        @pl.when(nx_ref[i] >= 0)
        def _():
            for cp in weight_copies(nx_ref[i]):
                cp.start()

    def ffn(rows):
        xb = _load_row_tiles(xs_ref, rows).astype(BF16)
        g = _dot(xb, wg_s[...]) + bg_ref[0]
        u = _dot(xb, wu_s[...]) + bu_ref[0]
        g = jnp.minimum(g, SWIGLU_LIMIT)
        u = jnp.clip(u, -SWIGLU_LIMIT, SWIGLU_LIMIT)
        act = g * jax.nn.sigmoid(SWIGLU_ALPHA * g) * (u + 1.0)
        _store_row_tiles(ys_ref, _dot(act.astype(BF16), wd_s[...]) + bd_ref[0], rows)

    half = bm // 2
    full = used & (nv_ref[i] > half)

    @pl.when(full)
    def _():
        ffn(bm)

    @pl.when(used & jnp.logical_not(full))
    def _():
        ffn(half)
        ys_ref[pl.ds(half * ROW_TILE, half * ROW_TILE), :] = jnp.zeros((half * ROW_TILE, LANES), F32)

    @pl.when(jnp.logical_not(used))
    def _():
        ys_ref[...] = jnp.zeros_like(ys_ref)


def _experts(block_e, n_used, next_e, n_valid, xs, wg, bg, wu, bu, wd, bd):
    n_rows = xs.shape[0] // ROW_TILE
    bm = BM_EXPERT
    n_e, d, d_ff = wg.shape
    assert d == d_ff and wd.shape == wg.shape
    row = lambda i, be, nu, nx, nv: (jnp.minimum(i, nu[0] - 1), 0)
    exp3 = lambda i, be, nu, nx, nv: (be[jnp.minimum(i, nu[0] - 1)], 0, 0)
    any_spec = pl.BlockSpec(memory_space=pl.ANY)
    grid_spec = pltpu.PrefetchScalarGridSpec(
        num_scalar_prefetch=4, grid=(n_rows // bm,),
        in_specs=[pl.BlockSpec((bm * ROW_TILE, LANES), row),
                  pl.BlockSpec((1, 1, d_ff), exp3), pl.BlockSpec((1, 1, d_ff), exp3),
                  pl.BlockSpec((1, 1, d), exp3), any_spec, any_spec, any_spec],
        out_specs=pl.BlockSpec((bm * ROW_TILE, LANES), lambda i, be, nu, nx, nv: (i, 0)),
        scratch_shapes=[pltpu.VMEM((3, d, d_ff), F32),
                        pltpu.VMEM((d, d_ff), BF16), pltpu.VMEM((d, d_ff), BF16), pltpu.VMEM((d_ff, d), BF16),
                        pltpu.SemaphoreType.DMA(())])
    return pl.pallas_call(
        _experts_body, grid_spec=grid_spec,
        out_shape=jax.ShapeDtypeStruct((n_rows * ROW_TILE, LANES), F32), name="experts",
        compiler_params=pltpu.CompilerParams(
            dimension_semantics=("arbitrary",), vmem_limit_bytes=48 * 1024 * 1024),
    )(block_e, n_used, next_e, n_valid, xs, bg.reshape(n_e, 1, d_ff), bu.reshape(n_e, 1, d_ff), bd.reshape(n_e, 1, d),
      wg, wu, wd)


def _combine_body(dest_ref, dest_next_ref, h1t_ref, tw_ref, g2_ref, b2_ref, ys_hbm, out_ref, buf, sems):
    tm = TM_COMBINE
    i = pl.program_id(0)
    slot = i % 2

    def start_row(d_ref, s, t):
        for kx in range(TOP_K):
            d = d_ref[kx, t]
            pltpu.make_async_copy(_tile(ys_hbm, d), _tile(buf.at[s, kx], t), sems.at[s]).start(priority=kx % 2)

    def wait_tile(s):
        for kx in range(TOP_K):
            pltpu.make_async_copy(ys_hbm.at[pl.ds(0, tm * ROW_TILE)], buf.at[s, kx], sems.at[s]).wait()

    @pl.when(i == 0)
    def _():
        lax.fori_loop(0, tm, lambda t, c: (start_row(dest_ref, 0, t), c)[1], 0)

    wait_tile(slot)
    rc = COMBINE_ROW_CHUNK
    for c in range(tm // rc):
        for t in range(c * rc, (c + 1) * rc):
            start_row(dest_next_ref, 1 - slot, t)
        tw = tw_ref[c * rc:(c + 1) * rc, :]
        ffn = tw[:, 0:1] * _load_row_tiles(buf.at[slot, 0], rc, c * rc)
        for kx in range(1, TOP_K):
            ffn = ffn + tw[:, kx:kx + 1] * _load_row_tiles(buf.at[slot, kx], rc, c * rc)
        h1 = _load_row_tiles(h1t_ref, rc, c * rc)
        out_ref[c * rc:(c + 1) * rc, :] = _ln(DN_ALPHA * h1 + ffn, g2_ref[...], b2_ref[...])

    @pl.when(i == pl.num_programs(0) - 1)
    def _():
        wait_tile(1 - slot)


def _combine(dest_km, h1t, tw, g2, b2, ys):
    n_tok = h1t.shape[0] // ROW_TILE
    d = ROW_TILE * LANES
    tm = TM_COMBINE
    n_steps = n_tok // tm
    const = lambda i: (0, 0)
    return pl.pallas_call(
        _combine_body, grid=(n_steps,),
        in_specs=[pl.BlockSpec((TOP_K, tm), lambda i: (0, i), memory_space=pltpu.SMEM),
                  pl.BlockSpec((TOP_K, tm), lambda i: (0, jnp.minimum(i + 1, n_steps - 1)),
                               memory_space=pltpu.SMEM),
                  pl.BlockSpec((tm * ROW_TILE, LANES), lambda i: (i, 0)),
                  pl.BlockSpec((tm, LANES), lambda i: (i, 0)),
                  pl.BlockSpec((1, d), const), pl.BlockSpec((1, d), const),
                  pl.BlockSpec(memory_space=pl.ANY)],
        out_specs=pl.BlockSpec((tm, d), lambda i: (i, 0)),
        out_shape=jax.ShapeDtypeStruct((n_tok, d), F32),
        scratch_shapes=[pltpu.VMEM((2, TOP_K, tm * ROW_TILE, LANES), F32), pltpu.SemaphoreType.DMA((2,))],
        name="combine",
        compiler_params=pltpu.CompilerParams(
            dimension_semantics=("arbitrary",), vmem_limit_bytes=40 * 1024 * 1024),
    )(dest_km, dest_km, h1t, tw, g2, b2, ys)


def _rope_tables(seq):
    t = np.arange(seq)
    row = (t // GRID_W).astype(np.float32)
    col = (t % GRID_W).astype(np.float32)
    half = HEAD_DIM // 2
    quarter = half // 2
    inv = (ROPE_THETA ** (-np.arange(quarter, dtype=np.float32) * np.float32(2.0 / half))).astype(np.float32)
    ang_r = row[:, None] * inv[None, :]
    ang_c = col[:, None] * inv[None, :]
    zeros = np.zeros_like(ang_r)
    cos = np.concatenate([np.cos(ang_r), np.cos(ang_r), np.cos(ang_c), np.cos(ang_c)], -1)
    s_lo = np.concatenate([-np.sin(ang_r), zeros, -np.sin(ang_c), zeros], -1)
    s_hi = np.concatenate([zeros, np.sin(ang_r), zeros, np.sin(ang_c)], -1)
    return tuple(jnp.asarray(np.tile(a, (1, LANES // HEAD_DIM)), F32) for a in (cos, s_lo, s_hi))


def _routing(top_i, rank, counts, bm):
    n_tok = top_i.shape[1]
    padded = (counts + bm - 1) // bm * bm
    pad_end = jnp.cumsum(padded)
    pad_start = pad_end - padded
    base = jnp.sum(jnp.where(top_i[:, :, None] == jnp.arange(N_EXPERTS, dtype=jnp.int32)[None, None, :],
                             pad_start[None, None, :], 0), axis=-1)
    dest = (base + rank).astype(jnp.int32)
    n_rows = n_tok * TOP_K + N_EXPERTS * bm
    n_blocks = n_rows // bm
    block_start = jnp.arange(n_blocks, dtype=jnp.int32) * bm
    block_e = jnp.minimum(jnp.sum((pad_end[None, :] <= block_start[:, None]).astype(jnp.int32), axis=1),
                          N_EXPERTS - 1).astype(jnp.int32)
    n_used = (pad_end[-1] // bm).astype(jnp.int32).reshape(1)
    e_ids = jnp.arange(N_EXPERTS, dtype=jnp.int32)
    later = (e_ids[None, :] > e_ids[:, None]) & (counts[None, :] > 0)
    next_of = jnp.min(jnp.where(later, e_ids[None, :], N_EXPERTS), axis=1)
    next_of = jnp.where(next_of == N_EXPERTS, -1, next_of).astype(jnp.int32)
    next_e = jnp.sum(jnp.where(block_e[:, None] == e_ids[None, :], next_of[None, :], 0), axis=1).astype(jnp.int32)
    start_of = jnp.sum(jnp.where(block_e[:, None] == e_ids[None, :], pad_start[None, :], 0), axis=1)
    count_of = jnp.sum(jnp.where(block_e[:, None] == e_ids[None, :], counts[None, :], 0), axis=1)
    n_valid = jnp.clip(count_of - (block_start - start_of), 0, bm).astype(jnp.int32)
    return (dest, block_e, n_used, next_e, n_valid, n_rows, pad_end.astype(jnp.int32),
            padded.astype(jnp.int32))


def kernel(x, ln0_g, ln0_b, w_in, a_sink, b_q_norm, b_k_norm, w_branch_a, w_branch_b, w_out,
           ln1_g, ln1_b, w_router, b_router, w_gate, b_gate, w_up, b_up, w_down, b_down,
           ln2_g, ln2_b):
    bsz, seq, d = x.shape
    assert w_in.shape[0] == DEPTH == 1
    assert seq % TM_PROJ == 0 and seq % TQ_GRID == 0 and seq == (seq // GRID_W) * GRID_W
    n_tok = bsz * seq
    row2 = lambda a: a.reshape(1, -1)

    w_perm = w_in[0].astype(BF16)
    head_id = np.arange(Q_W) // HEAD_DIM
    bd = jnp.asarray(head_id[:, None] == head_id[None, :], BF16)
    gq = jnp.tile(b_q_norm[0].astype(F32), N_HEADS).reshape(1, Q_W)
    gk = jnp.tile(b_k_norm[0].astype(F32), N_KV).reshape(1, KV_W)

    qa, ka, vat, qb, kb, vbt, sga, sgb = _in_proj(
        x, row2(ln0_g), row2(ln0_b), w_perm, bd, gq, gk, _rope_tables(seq))
    oa = _win_attn(a_sink[0].astype(F32), qa, ka, vat)
    ob = _grid_attn(qb, kb, vbt)

    wr = jnp.zeros((LANES, d), BF16).at[:N_EXPERTS].set(w_router[0].T.astype(BF16))
    br = jnp.zeros((LANES, 1), F32).at[:N_EXPERTS, 0].set(b_router[0])
    h1t, tr, tw, cnt = _post_attn(
        x, oa, ob, sga, sgb, row2(ln0_g), row2(ln0_b),
        w_branch_a[0].astype(BF16), w_branch_b[0].astype(BF16),
        w_out[0].astype(BF16), row2(ln1_g[0]), row2(ln1_b[0]), wr, br)
    h1t = h1t.reshape(n_tok * ROW_TILE, LANES)
    top_i, rank = tr[:TOP_K], tr[TOP_K:]
    tw = tw.reshape(n_tok, LANES)

    counts = cnt[:, 0].astype(jnp.int32)
    dest, block_e, n_used, next_e, n_valid, n_rows, pad_end, padded = _routing(top_i, rank, counts, BM_EXPERT)
    xs = _dispatch(dest, pad_end, padded, n_used, h1t, n_rows)
    ys = _experts(block_e, n_used, next_e, n_valid, xs, w_gate[0], b_gate[0], w_up[0], b_up[0], w_down[0], b_down[0])
    out = _combine(dest, h1t, tw, row2(ln2_g[0]), row2(ln2_b[0]), ys)
    return out.reshape(bsz, seq, d)
```

```python
import functools

import jax
import jax.numpy as jnp
import numpy as np
from jax import lax
from jax.experimental import pallas as pl
from jax.experimental.pallas import tpu as pltpu

HEAD_DIM = 64
N_HEADS = 8
N_KV = 2
WINDOW = 128
BLOCK = 128
GRID_W = 64
ROPE_THETA = 10000.0
N_EXPERTS = 32
TOP_K = 4
SWIGLU_LIMIT = 7.0
SWIGLU_ALPHA = 1.702
LN_EPS = 1e-5
RMS_EPS = 1e-6
NEG_INF = -1e30
DEPTH = 1
DN_ALPHA = (2.0 * DEPTH) ** 0.25
ALIBI_SLOPES = tuple(2.0 ** (-8.0 * (h + 1) / N_HEADS) for h in range(N_HEADS))
QK_SCALE = HEAD_DIM ** -0.5
LOG2_E = 1.4426950408889634

LANES = 128
ROW_TILE = 8
Q_W = N_HEADS * HEAD_DIM
KV_W = N_KV * HEAD_DIM
N_SLABS = Q_W // LANES
HEADS_PER_KV = N_HEADS // N_KV
ROPE_SHIFT = HEAD_DIM // 4

_MIB = 1024 * 1024
VMEM_LIMIT_PROJ = 48 * _MIB
VMEM_LIMIT_GRID_ATTN = 40 * _MIB
VMEM_LIMIT_EXPERTS = 48 * _MIB
VMEM_LIMIT_COMBINE = 40 * _MIB

TM_PROJ = 512
TM_IN_PROJ = 1024
PROJ_ROW_GROUPS = 2
WIN_BLOCKS = 8
TQ_GRID = 256
KEY_CHUNK = 256
BM_EXPERT = 512
TM_DISPATCH = 512
TM_COMBINE = 512
COMBINE_ROW_CHUNK = 32

F32 = jnp.float32
BF16 = jnp.bfloat16


def _ln(x, g, b):
    mu = jnp.mean(x, -1, keepdims=True)
    xc = x - mu
    var = jnp.mean(xc * xc, -1, keepdims=True)
    return xc * lax.rsqrt(var + LN_EPS) * g + b


def _dot(a, b):
    return jnp.dot(a, b, preferred_element_type=F32)


def _dot_nt(a, b):
    return lax.dot_general(a, b, (((1,), (1,)), ((), ())), preferred_element_type=F32)


def _load_row_tiles(ref, rows, first=0):
    return jnp.concatenate(
        [ref[pl.ds(first * ROW_TILE + c, rows, stride=ROW_TILE), :] for c in range(ROW_TILE)], axis=1)


def _store_row_tiles(ref, val, rows):
    for c in range(ROW_TILE):
        ref[pl.ds(c, rows, stride=ROW_TILE), :] = val[:, c * LANES:(c + 1) * LANES]


def _tile(ref, row):
    start = row * ROW_TILE
    if not isinstance(row, int):
        start = pl.multiple_of(start, ROW_TILE)
    return ref.at[pl.ds(start, ROW_TILE)]


def _in_proj_body(x_ref, g0_ref, b0_ref, w_ref, bd_ref, gq_ref, gk_ref, c_ref, s1_ref, s2_ref,
                  qa_ref, ka_ref, vat_ref, qb_ref, kb_ref, vbt_ref, sga_ref, sgb_ref):
    tm = x_ref.shape[1]
    d = sga_ref.shape[-1]
    o_kva, o_qb, o_kvb, o_g = Q_W, Q_W + 2 * KV_W, 2 * Q_W + 2 * KV_W, 2 * Q_W + 4 * KV_W
    both = lambda k: jnp.concatenate([k, pltpu.roll(k, HEAD_DIM, 1)], axis=1).astype(BF16)
    rg = tm // PROJ_ROW_GROUPS
    for grp in range(PROJ_ROW_GROUPS):
        rows = slice(grp * rg, (grp + 1) * rg)
        hb = _ln(x_ref[0, rows, :], g0_ref[...], b0_ref[...]).astype(BF16)

        def proj(lo, hi):
            return _dot(hb, w_ref[:, lo:hi])

        def norm_rope(t, width, g_ref):
            ss = _dot((t * t).astype(BF16), bd_ref[:width, :width])
            r = lax.rsqrt(ss * (1.0 / HEAD_DIM) + RMS_EPS)
            reps = width // LANES
            tab = lambda ref: jnp.concatenate([ref[rows, :]] * reps, axis=1)
            y = t * g_ref[...]
            rot = (y * tab(c_ref) + pltpu.roll(y, width - ROPE_SHIFT, 1) * tab(s1_ref)
                   + pltpu.roll(y, ROPE_SHIFT, 1) * tab(s2_ref))
            return rot * r

        qa_ref[0, rows, :] = (proj(0, o_kva) * (QK_SCALE * LOG2_E)).astype(BF16)
        kva = proj(o_kva, o_qb)
        ka_ref[0, rows, :] = both(kva[:, :KV_W])
        vat_ref[0, :, rows] = kva[:, KV_W:].T.astype(BF16)
        qb = norm_rope(proj(o_qb, o_kvb), Q_W, gq_ref)
        qb_ref[0, rows, :] = (qb * (QK_SCALE * LOG2_E)).astype(BF16)
        kvb = proj(o_kvb, o_g)
        kb_ref[0, rows, :] = both(norm_rope(kvb[:, :KV_W], KV_W, gk_ref))
        vbt_ref[0, :, rows] = kvb[:, KV_W:].T.astype(BF16)
        sga_ref[0, rows, :] = jax.nn.sigmoid(proj(o_g, o_g + d)).astype(BF16)
        sgb_ref[0, rows, :] = jax.nn.sigmoid(proj(o_g + d, o_g + 2 * d)).astype(BF16)


def _in_proj(x, g0, b0, w, bd, gq, gk, tabs):
    bsz, seq, d = x.shape
    tm = TM_IN_PROJ
    n_in = w.shape[1]
    const = lambda i, j: (0, 0)
    tok3 = lambda i, j: (j, i, 0)
    tab = lambda i, j: (i, 0)
    in_specs = [
        pl.BlockSpec((1, tm, d), tok3),
        pl.BlockSpec((1, d), const), pl.BlockSpec((1, d), const),
        pl.BlockSpec((d, n_in), const),
        pl.BlockSpec((Q_W, Q_W), const),
        pl.BlockSpec((1, Q_W), const), pl.BlockSpec((1, KV_W), const),
        pl.BlockSpec((tm, LANES), tab), pl.BlockSpec((tm, LANES), tab), pl.BlockSpec((tm, LANES), tab),
    ]
    tr3 = lambda i, j: (j, 0, i)
    out_specs = [
        pl.BlockSpec((1, tm, Q_W), tok3), pl.BlockSpec((1, tm, 2 * KV_W), tok3),
        pl.BlockSpec((1, KV_W, tm), tr3),
        pl.BlockSpec((1, tm, Q_W), tok3), pl.BlockSpec((1, tm, 2 * KV_W), tok3),
        pl.BlockSpec((1, KV_W, tm), tr3),
        pl.BlockSpec((1, tm, d), tok3), pl.BlockSpec((1, tm, d), tok3),
    ]
    out_shape = [
        jax.ShapeDtypeStruct((bsz, seq, Q_W), BF16), jax.ShapeDtypeStruct((bsz, seq, 2 * KV_W), BF16),
        jax.ShapeDtypeStruct((bsz, KV_W, seq), BF16),
        jax.ShapeDtypeStruct((bsz, seq, Q_W), BF16), jax.ShapeDtypeStruct((bsz, seq, 2 * KV_W), BF16),
        jax.ShapeDtypeStruct((bsz, KV_W, seq), BF16),
        jax.ShapeDtypeStruct((bsz, seq, d), BF16), jax.ShapeDtypeStruct((bsz, seq, d), BF16),
    ]
    return pl.pallas_call(
        _in_proj_body, grid=(seq // tm, bsz), in_specs=in_specs, out_specs=out_specs,
        out_shape=out_shape, name="in_proj",
        compiler_params=pltpu.CompilerParams(
            dimension_semantics=("arbitrary", "arbitrary"), vmem_limit_bytes=VMEM_LIMIT_PROJ),
    )(x, g0, b0, w, bd, gq, gk, *tabs)


def _half_mask(rows, c):
    lane = lax.broadcasted_iota(jnp.int32, (rows, LANES), 1)
    return (lane >= HEAD_DIM) if c == 1 else (lane < HEAD_DIM)


def _win_attn_body(sink_ref, q_ref, *refs, seq):
    nk = WIN_BLOCKS + 2
    k_refs, v_refs, o_ref = refs[:nk], refs[nk:2 * nk], refs[2 * nk]
    kk = lax.broadcasted_iota(jnp.int32, (3 * BLOCK, BLOCK), 0)
    qq = lax.broadcasted_iota(jnp.int32, (3 * BLOCK, BLOCK), 1)
    dist_i = jnp.abs(kk - BLOCK - qq)
    dist = dist_i.astype(F32)
    ones = jnp.ones((2 * ROW_TILE, 3 * BLOCK), BF16)
    for blk in range(WIN_BLOCKS):
        n = pl.program_id(1) * WIN_BLOCKS + blk
        rows = slice(blk * BLOCK, (blk + 1) * BLOCK)
        k2 = jnp.concatenate([r[0] for r in k_refs[blk:blk + 3]], axis=0)
        vt = jnp.concatenate([r[0] for r in v_refs[blk:blk + 3]], axis=1)
        k_pos = n * BLOCK - BLOCK + kk
        valid = (dist_i <= WINDOW) & (k_pos >= 0) & (k_pos < seq)
        slabs = [q_ref[0, rows, j * LANES:(j + 1) * LANES] for j in range(N_SLABS)]
        variant = [[h for h in range(N_HEADS) if (h // HEADS_PER_KV == h % 2) == straight]
                   for straight in (True, False)]
        scores = {}
        for v, heads in enumerate(variant):
            qm = jnp.concatenate([jnp.where(_half_mask(BLOCK, h % 2), slabs[h // 2], jnp.zeros_like(slabs[0]))
                                  for h in heads], axis=0)
            st_v = _dot_nt(k2[:, v * LANES:(v + 1) * LANES], qm)
            for col, h in enumerate(heads):
                scores[h] = st_v[:, col * BLOCK:(col + 1) * BLOCK]
        outs = {}
        for c in range(N_KV):
            heads = range(c * HEADS_PER_KV, (c + 1) * HEADS_PER_KV)
            ps, sinks = [], []
            for h in heads:
                st = scores[h] + jnp.where(valid, (-ALIBI_SLOPES[h] * LOG2_E) * dist, NEG_INF)
                sk = sink_ref[h] * LOG2_E
                m = jnp.maximum(jnp.max(st, axis=0, keepdims=True), sk)
                ps.append(jnp.exp2((st - m).astype(BF16)))
                sinks.append(jnp.exp2(sk - m))
            va = jnp.concatenate([vt[c * HEAD_DIM:(c + 1) * HEAD_DIM, :], ones], axis=0)
            ot = _dot(va, jnp.concatenate(ps, axis=1))
            ot = ot[:HEAD_DIM] / (ot[HEAD_DIM:HEAD_DIM + 1] + jnp.concatenate(sinks, axis=1))
            for col, h in enumerate(heads):
                outs[h] = ot[:, col * BLOCK:(col + 1) * BLOCK]
        for j in range(N_SLABS):
            pair = jnp.concatenate([outs[2 * j], outs[2 * j + 1]], axis=0)
            o_ref[0, rows, j * LANES:(j + 1) * LANES] = pair.T.astype(BF16)


def _win_attn(sink, qa, ka, vat):
    bsz, seq, _ = qa.shape
    nb = seq // BLOCK
    wb = WIN_BLOCKS
    qmap = lambda b, n: (b, n, 0)
    blk = lambda off: (lambda n: jnp.clip(n * wb + off, 0, nb - 1))
    kspec = lambda f: pl.BlockSpec((1, BLOCK, 2 * KV_W), lambda b, n: (b, f(n), 0))
    vspec = lambda f: pl.BlockSpec((1, KV_W, BLOCK), lambda b, n: (b, 0, f(n)))
    offs = range(-1, wb + 1)
    return pl.pallas_call(
        functools.partial(_win_attn_body, seq=seq), grid=(bsz, nb // wb),
        in_specs=[pl.BlockSpec(memory_space=pltpu.SMEM), pl.BlockSpec((1, wb * BLOCK, Q_W), qmap)]
        + [kspec(blk(o)) for o in offs] + [vspec(blk(o)) for o in offs],
        out_specs=pl.BlockSpec((1, wb * BLOCK, Q_W), qmap),
        out_shape=jax.ShapeDtypeStruct((bsz, seq, Q_W), BF16), name="win_attn",
        compiler_params=pltpu.CompilerParams(dimension_semantics=("arbitrary", "arbitrary")),
    )(sink, qa, *([ka] * (wb + 2)), *([vat] * (wb + 2)))


def _grid_attn_body(q_ref, k_ref, vt_ref, o_ref, s0_ref, s1_ref, p0_ref, p1_ref):
    tq = q_ref.shape[1]
    seq = k_ref.shape[1]
    kc = KEY_CHUNK
    n_chunks = seq // kc
    s_bufs, p_bufs = (s0_ref, s1_ref), (p0_ref, p1_ref)
    n_heads = N_HEADS
    ones = jnp.ones((2 * ROW_TILE, seq), BF16)

    def masked_q(h):
        slab = q_ref[0, :, (h // 2) * LANES:(h // 2 + 1) * LANES]
        return jnp.where(_half_mask(tq, h % 2), slab, jnp.zeros_like(slab))

    def score_chunk(h, qm, kb, m8):
        v = 0 if h // HEADS_PER_KV == h % 2 else 1
        sc = _dot_nt(k_ref[0, kb * kc:(kb + 1) * kc, v * LANES:(v + 1) * LANES], qm)
        s_bufs[h % 2][kb * kc:(kb + 1) * kc, :] = sc
        cm = jnp.max(sc.reshape(kc // ROW_TILE, ROW_TILE, tq), axis=0)
        return cm if m8 is None else jnp.maximum(m8, cm)

    def prob_chunk(h, kb, m):
        x = s_bufs[h % 2][kb * kc:(kb + 1) * kc, :] - m
        p_bufs[h % 2][kb * kc:(kb + 1) * kc, :] = jnp.exp2(x.astype(BF16))

    qm = masked_q(0)
    m8 = None
    for kb in range(n_chunks):
        m8 = score_chunk(0, qm, kb, m8)
    outs = []
    for h in range(n_heads):
        m = jnp.max(m8, axis=0, keepdims=True)
        nxt = h + 1 < n_heads
        if nxt:
            qm = masked_q(h + 1)
            m8 = None
        for kb in range(n_chunks):
            if nxt:
                m8 = score_chunk(h + 1, qm, kb, m8)
            prob_chunk(h, kb, m)
        c = h // HEADS_PER_KV
        va = jnp.concatenate([vt_ref[0, c * HEAD_DIM:(c + 1) * HEAD_DIM, :], ones], axis=0)
        ot = _dot(va, p_bufs[h % 2][...])
        outs.append(ot[:HEAD_DIM] / ot[HEAD_DIM:HEAD_DIM + 1])
        if h % 2 == 1:
            o_ref[0, :, (h // 2) * LANES:(h // 2 + 1) * LANES] = jnp.concatenate(outs, axis=0).T.astype(BF16)
            outs = []


def _grid_attn(qb, kb, vbt):
    bsz, seq, _ = qb.shape
    tq = TQ_GRID
    return pl.pallas_call(
        _grid_attn_body, grid=(bsz, seq // tq),
        in_specs=[pl.BlockSpec((1, tq, Q_W), lambda b, n: (b, n, 0)),
                  pl.BlockSpec((1, seq, 2 * KV_W), lambda b, n: (b, 0, 0)),
                  pl.BlockSpec((1, KV_W, seq), lambda b, n: (b, 0, 0))],
        out_specs=pl.BlockSpec((1, tq, Q_W), lambda b, n: (b, n, 0)),
        scratch_shapes=[pltpu.VMEM((seq, tq), F32), pltpu.VMEM((seq, tq), F32),
                        pltpu.VMEM((seq, tq), BF16), pltpu.VMEM((seq, tq), BF16)],
        out_shape=jax.ShapeDtypeStruct((bsz, seq, Q_W), BF16), name="grid_attn",
        compiler_params=pltpu.CompilerParams(
            dimension_semantics=("arbitrary", "arbitrary"), vmem_limit_bytes=VMEM_LIMIT_GRID_ATTN),
    )(qb, kb, vbt)


def _post_attn_body(x_ref, oa_ref, ob_ref, sga_ref, sgb_ref, g0_ref, b0_ref, wa_ref, wb_ref, wo_ref,
                    g1_ref, b1_ref, wr_ref, br_ref, h1t_ref, tr_ref, tw_ref, cnt_out_ref, cnt_ref):
    h0 = _ln(x_ref[0], g0_ref[...], b0_ref[...])
    out_a = _dot(oa_ref[0], wa_ref[...])
    out_b = _dot(ob_ref[0], wb_ref[...])
    merged = sga_ref[0].astype(F32) * out_a + sgb_ref[0].astype(F32) * out_b
    mix = _dot(merged.astype(BF16), wo_ref[...])
    h1 = _ln(DN_ALPHA * h0 + mix, g1_ref[...], b1_ref[...])
    tm = h1.shape[0]
    _store_row_tiles(h1t_ref.at[0], h1, tm)

    logits = (_dot_nt(wr_ref[...], h1.astype(BF16)) + br_ref[...])[:N_EXPERTS]
    sub = lax.broadcasted_iota(jnp.int32, (N_EXPERTS, tm), 0)
    cur = logits
    vals, idxs = [], []
    for _ in range(TOP_K):
        mv = jnp.max(cur, axis=0, keepdims=True)
        ix = jnp.min(jnp.where(cur == mv, sub, N_EXPERTS), axis=0, keepdims=True)
        vals.append(mv)
        idxs.append(ix)
        cur = jnp.where(sub == ix, -jnp.inf, cur)
    es = [jnp.exp(v - vals[0]) for v in vals]
    tot = es[0] + es[1] + es[2] + es[3]

    @pl.when((pl.program_id(0) == 0) & (pl.program_id(1) == 0))
    def _():
        cnt_ref[...] = jnp.zeros_like(cnt_ref)

    sel = jnp.zeros((N_EXPERTS, tm), F32)
    for kx in range(TOP_K):
        sel = sel + (sub == idxs[kx]).astype(F32)
    r_i = lax.broadcasted_iota(jnp.int32, (tm, tm), 0)
    c_i = lax.broadcasted_iota(jnp.int32, (tm, tm), 1)
    tri = (r_i < c_i).astype(BF16)
    rank = _dot(sel.astype(BF16), tri) + cnt_ref[:, 0:1]
    cnt_ref[...] = cnt_ref[...] + jnp.sum(sel, axis=1, keepdims=True)
    cnt_out_ref[...] = cnt_ref[...]

    rks = [jnp.sum(jnp.where(sub == ix, rank, 0.0), axis=0, keepdims=True).astype(jnp.int32) for ix in idxs]
    tr_ref[...] = jnp.concatenate(idxs + rks, axis=0)
    tw_t = jnp.concatenate([e / tot for e in es] + [jnp.zeros((LANES - TOP_K, tm), F32)], axis=0)
    tw_ref[0] = tw_t.T


def _post_attn(x, oa, ob, sga, sgb, g0, b0, wa, wb, wo, g1, b1, wr, br):
    bsz, seq, d = x.shape
    tm = TM_PROJ
    tok3 = lambda b, i: (b, i, 0)
    const = lambda b, i: (0, 0)
    full = lambda a: pl.BlockSpec(a.shape, const)
    return pl.pallas_call(
        _post_attn_body, grid=(bsz, seq // tm),
        in_specs=[pl.BlockSpec((1, tm, d), tok3),
                  pl.BlockSpec((1, tm, Q_W), tok3), pl.BlockSpec((1, tm, Q_W), tok3),
                  pl.BlockSpec((1, tm, d), tok3), pl.BlockSpec((1, tm, d), tok3),
                  full(g0), full(b0), full(wa), full(wb), full(wo), full(g1), full(b1),
                  full(wr), full(br)],
        out_specs=[pl.BlockSpec((1, tm * ROW_TILE, LANES), tok3),
                   pl.BlockSpec((2 * TOP_K, tm), lambda b, i: (0, b * (seq // tm) + i)),
                   pl.BlockSpec((1, tm, LANES), tok3), pl.BlockSpec((N_EXPERTS, LANES), const)],
        out_shape=[jax.ShapeDtypeStruct((bsz, seq * ROW_TILE, LANES), F32),
                   jax.ShapeDtypeStruct((2 * TOP_K, bsz * seq), jnp.int32),
                   jax.ShapeDtypeStruct((bsz, seq, LANES), F32),
                   jax.ShapeDtypeStruct((N_EXPERTS, LANES), F32)],
        scratch_shapes=[pltpu.VMEM((N_EXPERTS, LANES), F32)],
        name="post_attn",
        compiler_params=pltpu.CompilerParams(
            dimension_semantics=("arbitrary", "arbitrary"), vmem_limit_bytes=VMEM_LIMIT_PROJ),
    )(x, oa, ob, sga, sgb, g0, b0, wa, wb, wo, g1, b1, wr, br)


def _dispatch_body(dest_ref, pend_ref, padded_ref, nu_ref, h1t_ref, xs_hbm, zbuf, sem, zsem):
    tm = TM_DISPATCH
    zrows = BM_EXPERT * ROW_TILE
    n_blocks = xs_hbm.shape[0] // zrows

    @pl.when(pl.program_id(0) == 0)
    def _():
        zbuf[...] = jnp.zeros_like(zbuf)
        zero_wait = pltpu.make_async_copy(zbuf, xs_hbm.at[pl.ds(0, zrows)], zsem).wait
        for e in range(N_EXPERTS):
            @pl.when(padded_ref[e] > 0)
            def _():
                start = pl.multiple_of((pend_ref[e] - BM_EXPERT) * ROW_TILE, ROW_TILE)
                pltpu.make_async_copy(zbuf, xs_hbm.at[pl.ds(start, zrows)], zsem).start()
        for b in range(n_blocks - N_EXPERTS, n_blocks):
            @pl.when(b >= nu_ref[0])
            def _():
                pltpu.make_async_copy(zbuf, xs_hbm.at[pl.ds(b * zrows, zrows)], zsem).start()
        for e in range(N_EXPERTS):
            pl.when(padded_ref[e] > 0)(zero_wait)
        for b in range(n_blocks - N_EXPERTS, n_blocks):
            pl.when(b >= nu_ref[0])(zero_wait)

    for t in range(tm):
        for kx in range(TOP_K):
            d = dest_ref[kx, t]
            pltpu.make_async_copy(_tile(h1t_ref, t), _tile(xs_hbm, d), sem).start(priority=kx % 2)
    for _ in range(TOP_K):
        pltpu.make_async_copy(h1t_ref, xs_hbm.at[pl.ds(0, tm * ROW_TILE)], sem).wait()


def _dispatch(dest_km, pad_end, padded, n_used, h1t, n_rows):
    n_tok = h1t.shape[0] // ROW_TILE
    tm = TM_DISPATCH
    return pl.pallas_call(
        _dispatch_body, grid=(n_tok // tm,),
        in_specs=[pl.BlockSpec((TOP_K, tm), lambda i: (0, i), memory_space=pltpu.SMEM),
                  pl.BlockSpec(memory_space=pltpu.SMEM), pl.BlockSpec(memory_space=pltpu.SMEM),
                  pl.BlockSpec(memory_space=pltpu.SMEM),
                  pl.BlockSpec((tm * ROW_TILE, LANES), lambda i: (i, 0))],
        out_specs=pl.BlockSpec(memory_space=pl.ANY),
        out_shape=jax.ShapeDtypeStruct((n_rows * ROW_TILE, LANES), F32),
        scratch_shapes=[pltpu.VMEM((BM_EXPERT * ROW_TILE, LANES), F32),
                        pltpu.SemaphoreType.DMA(()), pltpu.SemaphoreType.DMA(())],
        name="dispatch",
        compiler_params=pltpu.CompilerParams(dimension_semantics=("arbitrary",)),
    )(dest_km, pad_end, padded, n_used, h1t)


def _experts_body(be_ref, nu_ref, nx_ref, nv_ref, xs_ref, bg_ref, bu_ref, bd_ref, wg_hbm, wu_hbm, wd_hbm, ys_ref,
                  stage, wg_s, wu_s, wd_s, wsem):
    i = pl.program_id(0)
    bm = BM_EXPERT
    used = i < nu_ref[0]
    prev = be_ref[jnp.maximum(i - 1, 0)]
    fresh = (i == 0) | (be_ref[i] != prev)

    def weight_copies(e):
        return [pltpu.make_async_copy(w_hbm.at[e], stage.at[n], wsem)
                for n, w_hbm in enumerate((wg_hbm, wu_hbm, wd_hbm))]

    @pl.when(i == 0)
    def _():
        for cp in weight_copies(be_ref[0]):
            cp.start()

    @pl.when(used & fresh)
    def _():
        for cp in weight_copies(be_ref[i]):
            cp.wait()
        wg_s[...] = stage[0].astype(BF16)
        wu_s[...] = stage[1].astype(BF16)
        wd_s[...] = stage[2].astype(BF16)

        @pl.when(nx_ref[i] >= 0)
        def _():
            for cp in weight_copies(nx_ref[i]):
                cp.start()

    def ffn(rows):
        xb = _load_row_tiles(xs_ref, rows).astype(BF16)
        g = _dot(xb, wg_s[...]) + bg_ref[0]
        u = _dot(xb, wu_s[...]) + bu_ref[0]
        g = jnp.minimum(g, SWIGLU_LIMIT)
        u = jnp.clip(u, -SWIGLU_LIMIT, SWIGLU_LIMIT)
        act = g * jax.nn.sigmoid(SWIGLU_ALPHA * g) * (u + 1.0)
        _store_row_tiles(ys_ref, _dot(act.astype(BF16), wd_s[...]) + bd_ref[0], rows)

    half = bm // 2
    full = used & (nv_ref[i] > half)

    @pl.when(full)
    def _():
        ffn(bm)

    @pl.when(used & jnp.logical_not(full))
    def _():
        ffn(half)
        ys_ref[pl.ds(half * ROW_TILE, half * ROW_TILE), :] = jnp.zeros((half * ROW_TILE, LANES), F32)

    @pl.when(jnp.logical_not(used))
    def _():
        ys_ref[...] = jnp.zeros_like(ys_ref)


def _experts(block_e, n_used, next_e, n_valid, xs, wg, bg, wu, bu, wd, bd):
    n_rows = xs.shape[0] // ROW_TILE
    bm = BM_EXPERT
    n_e, d, d_ff = wg.shape
    assert d == d_ff and wd.shape == wg.shape
    row = lambda i, be, nu, nx, nv: (jnp.minimum(i, nu[0] - 1), 0)
    exp3 = lambda i, be, nu, nx, nv: (be[jnp.minimum(i, nu[0] - 1)], 0, 0)
    any_spec = pl.BlockSpec(memory_space=pl.ANY)
    grid_spec = pltpu.PrefetchScalarGridSpec(
        num_scalar_prefetch=4, grid=(n_rows // bm,),
        in_specs=[pl.BlockSpec((bm * ROW_TILE, LANES), row),
                  pl.BlockSpec((1, 1, d_ff), exp3), pl.BlockSpec((1, 1, d_ff), exp3),
                  pl.BlockSpec((1, 1, d), exp3), any_spec, any_spec, any_spec],
        out_specs=pl.BlockSpec((bm * ROW_TILE, LANES), lambda i, be, nu, nx, nv: (i, 0)),
        scratch_shapes=[pltpu.VMEM((3, d, d_ff), F32),
                        pltpu.VMEM((d, d_ff), BF16), pltpu.VMEM((d, d_ff), BF16), pltpu.VMEM((d_ff, d), BF16),
                        pltpu.SemaphoreType.DMA(())])
    return pl.pallas_call(
        _experts_body, grid_spec=grid_spec,
        out_shape=jax.ShapeDtypeStruct((n_rows * ROW_TILE, LANES), F32), name="experts",
        compiler_params=pltpu.CompilerParams(
            dimension_semantics=("arbitrary",), vmem_limit_bytes=VMEM_LIMIT_EXPERTS),
    )(block_e, n_used, next_e, n_valid, xs, bg.reshape(n_e, 1, d_ff), bu.reshape(n_e, 1, d_ff), bd.reshape(n_e, 1, d),
      wg, wu, wd)


def _combine_body(dest_ref, dest_next_ref, h1t_ref, tw_ref, g2_ref, b2_ref, ys_hbm, out_ref, buf, sems):
    tm = TM_COMBINE
    i = pl.program_id(0)
    slot = i % 2

    def start_row(d_ref, s, t):
        for kx in range(TOP_K):
            d = d_ref[kx, t]
            pltpu.make_async_copy(_tile(ys_hbm, d), _tile(buf.at[s, kx], t), sems.at[s]).start(priority=kx % 2)

    def wait_tile(s):
        for kx in range(TOP_K):
            pltpu.make_async_copy(ys_hbm.at[pl.ds(0, tm * ROW_TILE)], buf.at[s, kx], sems.at[s]).wait()

    @pl.when(i == 0)
    def _():
        lax.fori_loop(0, tm, lambda t, c: (start_row(dest_ref, 0, t), c)[1], 0)

    wait_tile(slot)
    rc = COMBINE_ROW_CHUNK
    for c in range(tm // rc):
        for t in range(c * rc, (c + 1) * rc):
            start_row(dest_next_ref, 1 - slot, t)
        tw = tw_ref[c * rc:(c + 1) * rc, :]
        ffn = tw[:, 0:1] * _load_row_tiles(buf.at[slot, 0], rc, c * rc)
        for kx in range(1, TOP_K):
            ffn = ffn + tw[:, kx:kx + 1] * _load_row_tiles(buf.at[slot, kx], rc, c * rc)
        h1 = _load_row_tiles(h1t_ref, rc, c * rc)
        out_ref[c * rc:(c + 1) * rc, :] = _ln(DN_ALPHA * h1 + ffn, g2_ref[...], b2_ref[...])

    @pl.when(i == pl.num_programs(0) - 1)
    def _():
        wait_tile(1 - slot)


def _combine(dest_km, h1t, tw, g2, b2, ys):
    n_tok = h1t.shape[0] // ROW_TILE
    d = ROW_TILE * LANES
    tm = TM_COMBINE
    n_steps = n_tok // tm
    const = lambda i: (0, 0)
    return pl.pallas_call(
        _combine_body, grid=(n_steps,),
        in_specs=[pl.BlockSpec((TOP_K, tm), lambda i: (0, i), memory_space=pltpu.SMEM),
                  pl.BlockSpec((TOP_K, tm), lambda i: (0, jnp.minimum(i + 1, n_steps - 1)),
                               memory_space=pltpu.SMEM),
                  pl.BlockSpec((tm * ROW_TILE, LANES), lambda i: (i, 0)),
                  pl.BlockSpec((tm, LANES), lambda i: (i, 0)),
                  pl.BlockSpec((1, d), const), pl.BlockSpec((1, d), const),
                  pl.BlockSpec(memory_space=pl.ANY)],
        out_specs=pl.BlockSpec((tm, d), lambda i: (i, 0)),
        out_shape=jax.ShapeDtypeStruct((n_tok, d), F32),
        scratch_shapes=[pltpu.VMEM((2, TOP_K, tm * ROW_TILE, LANES), F32), pltpu.SemaphoreType.DMA((2,))],
        name="combine",
        compiler_params=pltpu.CompilerParams(
            dimension_semantics=("arbitrary",), vmem_limit_bytes=VMEM_LIMIT_COMBINE),
    )(dest_km, dest_km, h1t, tw, g2, b2, ys)


def _rope_tables(seq):
    t = np.arange(seq)
    row = (t // GRID_W).astype(np.float32)
    col = (t % GRID_W).astype(np.float32)
    half = HEAD_DIM // 2
    quarter = half // 2
    inv = (ROPE_THETA ** (-np.arange(quarter, dtype=np.float32) * np.float32(2.0 / half))).astype(np.float32)
    ang_r = row[:, None] * inv[None, :]
    ang_c = col[:, None] * inv[None, :]
    zeros = np.zeros_like(ang_r)
    cos = np.concatenate([np.cos(ang_r), np.cos(ang_r), np.cos(ang_c), np.cos(ang_c)], -1)
    s_lo = np.concatenate([-np.sin(ang_r), zeros, -np.sin(ang_c), zeros], -1)
    s_hi = np.concatenate([zeros, np.sin(ang_r), zeros, np.sin(ang_c)], -1)
    return tuple(jnp.asarray(np.tile(a, (1, LANES // HEAD_DIM)), F32) for a in (cos, s_lo, s_hi))


def _routing(top_i, rank, counts, bm):
    n_tok = top_i.shape[1]
    padded = (counts + bm - 1) // bm * bm
    pad_end = jnp.cumsum(padded)
    pad_start = pad_end - padded
    base = jnp.sum(jnp.where(top_i[:, :, None] == jnp.arange(N_EXPERTS, dtype=jnp.int32)[None, None, :],
                             pad_start[None, None, :], 0), axis=-1)
    dest = (base + rank).astype(jnp.int32)
    n_rows = n_tok * TOP_K + N_EXPERTS * bm
    n_blocks = n_rows // bm
    block_start = jnp.arange(n_blocks, dtype=jnp.int32) * bm
    block_e = jnp.minimum(jnp.sum((pad_end[None, :] <= block_start[:, None]).astype(jnp.int32), axis=1),
                          N_EXPERTS - 1).astype(jnp.int32)
    n_used = (pad_end[-1] // bm).astype(jnp.int32).reshape(1)
    e_ids = jnp.arange(N_EXPERTS, dtype=jnp.int32)
    later = (e_ids[None, :] > e_ids[:, None]) & (counts[None, :] > 0)
    next_of = jnp.min(jnp.where(later, e_ids[None, :], N_EXPERTS), axis=1)
    next_of = jnp.where(next_of == N_EXPERTS, -1, next_of).astype(jnp.int32)
    next_e = jnp.sum(jnp.where(block_e[:, None] == e_ids[None, :], next_of[None, :], 0), axis=1).astype(jnp.int32)
    start_of = jnp.sum(jnp.where(block_e[:, None] == e_ids[None, :], pad_start[None, :], 0), axis=1)
    count_of = jnp.sum(jnp.where(block_e[:, None] == e_ids[None, :], counts[None, :], 0), axis=1)
    n_valid = jnp.clip(count_of - (block_start - start_of), 0, bm).astype(jnp.int32)
    return (dest, block_e, n_used, next_e, n_valid, n_rows, pad_end.astype(jnp.int32),
            padded.astype(jnp.int32))


def kernel(x, ln0_g, ln0_b, w_in, a_sink, b_q_norm, b_k_norm, w_branch_a, w_branch_b, w_out,
           ln1_g, ln1_b, w_router, b_router, w_gate, b_gate, w_up, b_up, w_down, b_down,
           ln2_g, ln2_b):
    bsz, seq, d = x.shape
    assert w_in.shape[0] == DEPTH == 1
    assert seq % TM_PROJ == 0 and seq % TQ_GRID == 0 and seq == (seq // GRID_W) * GRID_W
    n_tok = bsz * seq
    row2 = lambda a: a.reshape(1, -1)

    w_perm = w_in[0].astype(BF16)
    head_id = np.arange(Q_W) // HEAD_DIM
    bd = jnp.asarray(head_id[:, None] == head_id[None, :], BF16)
    gq = jnp.tile(b_q_norm[0].astype(F32), N_HEADS).reshape(1, Q_W)
    gk = jnp.tile(b_k_norm[0].astype(F32), N_KV).reshape(1, KV_W)

    qa, ka, vat, qb, kb, vbt, sga, sgb = _in_proj(
        x, row2(ln0_g), row2(ln0_b), w_perm, bd, gq, gk, _rope_tables(seq))
    oa = _win_attn(a_sink[0].astype(F32), qa, ka, vat)
    ob = _grid_attn(qb, kb, vbt)

    wr = jnp.zeros((LANES, d), BF16).at[:N_EXPERTS].set(w_router[0].T.astype(BF16))
    br = jnp.zeros((LANES, 1), F32).at[:N_EXPERTS, 0].set(b_router[0])
    h1t, tr, tw, cnt = _post_attn(
        x, oa, ob, sga, sgb, row2(ln0_g), row2(ln0_b),
        w_branch_a[0].astype(BF16), w_branch_b[0].astype(BF16),
        w_out[0].astype(BF16), row2(ln1_g[0]), row2(ln1_b[0]), wr, br)
    h1t = h1t.reshape(n_tok * ROW_TILE, LANES)
    top_i, rank = tr[:TOP_K], tr[TOP_K:]
    tw = tw.reshape(n_tok, LANES)

    counts = cnt[:, 0].astype(jnp.int32)
    dest, block_e, n_used, next_e, n_valid, n_rows, pad_end, padded = _routing(top_i, rank, counts, BM_EXPERT)
    xs = _dispatch(dest, pad_end, padded, n_used, h1t, n_rows)
    ys = _experts(block_e, n_used, next_e, n_valid, xs, w_gate[0], b_gate[0], w_up[0], b_up[0], w_down[0], b_down[0])
    out = _combine(dest, h1t, tw, row2(ln2_g[0]), row2(ln2_b[0]), ys)
    return out.reshape(bsz, seq, d)
```

```python
import functools

import jax
import jax.numpy as jnp
import numpy as np
from jax import lax
from jax.experimental import pallas as pl
from jax.experimental.pallas import tpu as pltpu

HEAD_DIM = 64
N_HEADS = 8
N_KV = 2
WINDOW = 128
BLOCK = 128
GRID_W = 64
ROPE_THETA = 10000.0
N_EXPERTS = 32
TOP_K = 4
SWIGLU_LIMIT = 7.0
SWIGLU_ALPHA = 1.702
LN_EPS = 1e-5
RMS_EPS = 1e-6
NEG_INF = -1e30
DEPTH = 1
DN_ALPHA = (2.0 * DEPTH) ** 0.25
ALIBI_SLOPES = tuple(2.0 ** (-8.0 * (h + 1) / N_HEADS) for h in range(N_HEADS))
QK_SCALE = HEAD_DIM ** -0.5
LOG2_E = 1.4426950408889634

LANES = 128
ROW_TILE = 8
Q_W = N_HEADS * HEAD_DIM
KV_W = N_KV * HEAD_DIM
N_SLABS = Q_W // LANES
HEADS_PER_KV = N_HEADS // N_KV
ROPE_SHIFT = HEAD_DIM // 4

_MIB = 1024 * 1024
VMEM_LIMIT_PROJ = 48 * _MIB
VMEM_LIMIT_GRID_ATTN = 40 * _MIB
VMEM_LIMIT_EXPERTS = 48 * _MIB
VMEM_LIMIT_COMBINE = 40 * _MIB

TM_PROJ = 1024
TM_IN_PROJ = 1024
PROJ_ROW_GROUPS = 2
WIN_BLOCKS = 8
TQ_GRID = 256
KEY_CHUNK = 256
BM_EXPERT = 512
TM_DISPATCH = 1024
TM_COMBINE = 512
COMBINE_ROW_CHUNK = 32

F32 = jnp.float32
BF16 = jnp.bfloat16


def _ln(x, g, b):
    mu = jnp.mean(x, -1, keepdims=True)
    xc = x - mu
    var = jnp.mean(xc * xc, -1, keepdims=True)
    return xc * lax.rsqrt(var + LN_EPS) * g + b


def _dot(a, b):
    return jnp.dot(a, b, preferred_element_type=F32)


def _dot_nt(a, b):
    return lax.dot_general(a, b, (((1,), (1,)), ((), ())), preferred_element_type=F32)


def _load_row_tiles(ref, rows, first=0):
    return jnp.concatenate(
        [ref[pl.ds(first * ROW_TILE + c, rows, stride=ROW_TILE), :] for c in range(ROW_TILE)], axis=1)


def _store_row_tiles(ref, val, rows):
    for c in range(ROW_TILE):
        ref[pl.ds(c, rows, stride=ROW_TILE), :] = val[:, c * LANES:(c + 1) * LANES]


def _tile(ref, row):
    start = row * ROW_TILE
    if not isinstance(row, int):
        start = pl.multiple_of(start, ROW_TILE)
    return ref.at[pl.ds(start, ROW_TILE)]


def _in_proj_body(x_ref, g0_ref, b0_ref, w_ref, bd_ref, gq_ref, gk_ref, c_ref, s1_ref, s2_ref,
                  qa_ref, ka_ref, vat_ref, qb_ref, kb_ref, vbt_ref, sga_ref, sgb_ref):
    tm = x_ref.shape[1]
    d = sga_ref.shape[-1]
    o_kva, o_qb, o_kvb, o_g = Q_W, Q_W + 2 * KV_W, 2 * Q_W + 2 * KV_W, 2 * Q_W + 4 * KV_W
    both = lambda k: jnp.concatenate([k, pltpu.roll(k, HEAD_DIM, 1)], axis=1).astype(BF16)
    rg = tm // PROJ_ROW_GROUPS
    for grp in range(PROJ_ROW_GROUPS):
        rows = slice(grp * rg, (grp + 1) * rg)
        hb = _ln(x_ref[0, rows, :], g0_ref[...], b0_ref[...]).astype(BF16)

        def proj(lo, hi):
            return _dot(hb, w_ref[:, lo:hi])

        def norm_rope(t, width, g_ref):
            ss = _dot((t * t).astype(BF16), bd_ref[:width, :width])
            r = lax.rsqrt(ss * (1.0 / HEAD_DIM) + RMS_EPS)
            reps = width // LANES
            tab = lambda ref: jnp.concatenate([ref[rows, :]] * reps, axis=1)
            y = t * g_ref[...]
            rot = (y * tab(c_ref) + pltpu.roll(y, width - ROPE_SHIFT, 1) * tab(s1_ref)
                   + pltpu.roll(y, ROPE_SHIFT, 1) * tab(s2_ref))
            return rot * r

        qa_ref[0, rows, :] = (proj(0, o_kva) * (QK_SCALE * LOG2_E)).astype(BF16)
        kva = proj(o_kva, o_qb)
        ka_ref[0, rows, :] = both(kva[:, :KV_W])
        vat_ref[0, :, rows] = kva[:, KV_W:].T.astype(BF16)
        qb = norm_rope(proj(o_qb, o_kvb), Q_W, gq_ref)
        qb_ref[0, rows, :] = (qb * (QK_SCALE * LOG2_E)).astype(BF16)
        kvb = proj(o_kvb, o_g)
        kb_ref[0, rows, :] = both(norm_rope(kvb[:, :KV_W], KV_W, gk_ref))
        vbt_ref[0, :, rows] = kvb[:, KV_W:].T.astype(BF16)
        sga_ref[0, rows, :] = jax.nn.sigmoid(proj(o_g, o_g + d)).astype(BF16)
        sgb_ref[0, rows, :] = jax.nn.sigmoid(proj(o_g + d, o_g + 2 * d)).astype(BF16)


def _in_proj(x, g0, b0, w, bd, gq, gk, tabs):
    bsz, seq, d = x.shape
    tm = TM_IN_PROJ
    n_in = w.shape[1]
    const = lambda i, j: (0, 0)
    tok3 = lambda i, j: (j, i, 0)
    tab = lambda i, j: (i, 0)
    in_specs = [
        pl.BlockSpec((1, tm, d), tok3),
        pl.BlockSpec((1, d), const), pl.BlockSpec((1, d), const),
        pl.BlockSpec((d, n_in), const),
        pl.BlockSpec((Q_W, Q_W), const),
        pl.BlockSpec((1, Q_W), const), pl.BlockSpec((1, KV_W), const),
        pl.BlockSpec((tm, LANES), tab), pl.BlockSpec((tm, LANES), tab), pl.BlockSpec((tm, LANES), tab),
    ]
    tr3 = lambda i, j: (j, 0, i)
    out_specs = [
        pl.BlockSpec((1, tm, Q_W), tok3), pl.BlockSpec((1, tm, 2 * KV_W), tok3),
        pl.BlockSpec((1, KV_W, tm), tr3),
        pl.BlockSpec((1, tm, Q_W), tok3), pl.BlockSpec((1, tm, 2 * KV_W), tok3),
        pl.BlockSpec((1, KV_W, tm), tr3),
        pl.BlockSpec((1, tm, d), tok3), pl.BlockSpec((1, tm, d), tok3),
    ]
    out_shape = [
        jax.ShapeDtypeStruct((bsz, seq, Q_W), BF16), jax.ShapeDtypeStruct((bsz, seq, 2 * KV_W), BF16),
        jax.ShapeDtypeStruct((bsz, KV_W, seq), BF16),
        jax.ShapeDtypeStruct((bsz, seq, Q_W), BF16), jax.ShapeDtypeStruct((bsz, seq, 2 * KV_W), BF16),
        jax.ShapeDtypeStruct((bsz, KV_W, seq), BF16),
        jax.ShapeDtypeStruct((bsz, seq, d), BF16), jax.ShapeDtypeStruct((bsz, seq, d), BF16),
    ]
    return pl.pallas_call(
        _in_proj_body, grid=(seq // tm, bsz), in_specs=in_specs, out_specs=out_specs,
        out_shape=out_shape, name="in_proj",
        compiler_params=pltpu.CompilerParams(
            dimension_semantics=("arbitrary", "arbitrary"), vmem_limit_bytes=VMEM_LIMIT_PROJ),
    )(x, g0, b0, w, bd, gq, gk, *tabs)


def _half_mask(rows, c):
    lane = lax.broadcasted_iota(jnp.int32, (rows, LANES), 1)
    return (lane >= HEAD_DIM) if c == 1 else (lane < HEAD_DIM)


def _win_attn_body(sink_ref, q_ref, *refs, seq):
    nk = WIN_BLOCKS + 2
    k_refs, v_refs, o_ref = refs[:nk], refs[nk:2 * nk], refs[2 * nk]
    kk = lax.broadcasted_iota(jnp.int32, (3 * BLOCK, BLOCK), 0)
    qq = lax.broadcasted_iota(jnp.int32, (3 * BLOCK, BLOCK), 1)
    dist_i = jnp.abs(kk - BLOCK - qq)
    dist = dist_i.astype(F32)
    ones = jnp.ones((2 * ROW_TILE, 3 * BLOCK), BF16)
    for blk in range(WIN_BLOCKS):
        n = pl.program_id(1) * WIN_BLOCKS + blk
        rows = slice(blk * BLOCK, (blk + 1) * BLOCK)
        k2 = jnp.concatenate([r[0] for r in k_refs[blk:blk + 3]], axis=0)
        vt = jnp.concatenate([r[0] for r in v_refs[blk:blk + 3]], axis=1)
        k_pos = n * BLOCK - BLOCK + kk
        valid = (dist_i <= WINDOW) & (k_pos >= 0) & (k_pos < seq)
        slabs = [q_ref[0, rows, j * LANES:(j + 1) * LANES] for j in range(N_SLABS)]
        variant = [[h for h in range(N_HEADS) if (h // HEADS_PER_KV == h % 2) == straight]
                   for straight in (True, False)]
        scores = {}
        for v, heads in enumerate(variant):
            qm = jnp.concatenate([jnp.where(_half_mask(BLOCK, h % 2), slabs[h // 2], jnp.zeros_like(slabs[0]))
                                  for h in heads], axis=0)
            st_v = _dot_nt(k2[:, v * LANES:(v + 1) * LANES], qm)
            for col, h in enumerate(heads):
                scores[h] = st_v[:, col * BLOCK:(col + 1) * BLOCK]
        outs = {}
        for c in range(N_KV):
            heads = range(c * HEADS_PER_KV, (c + 1) * HEADS_PER_KV)
            ps, sinks = [], []
            for h in heads:
                st = scores[h] + jnp.where(valid, (-ALIBI_SLOPES[h] * LOG2_E) * dist, NEG_INF)
                sk = sink_ref[h] * LOG2_E
                m = jnp.maximum(jnp.max(st, axis=0, keepdims=True), sk)
                ps.append(jnp.exp2((st - m).astype(BF16)))
                sinks.append(jnp.exp2(sk - m))
            va = jnp.concatenate([vt[c * HEAD_DIM:(c + 1) * HEAD_DIM, :], ones], axis=0)
            ot = _dot(va, jnp.concatenate(ps, axis=1))
            ot = ot[:HEAD_DIM] / (ot[HEAD_DIM:HEAD_DIM + 1] + jnp.concatenate(sinks, axis=1))
            for col, h in enumerate(heads):
                outs[h] = ot[:, col * BLOCK:(col + 1) * BLOCK]
        for j in range(N_SLABS):
            pair = jnp.concatenate([outs[2 * j], outs[2 * j + 1]], axis=0)
            o_ref[0, rows, j * LANES:(j + 1) * LANES] = pair.T.astype(BF16)


def _win_attn(sink, qa, ka, vat):
    bsz, seq, _ = qa.shape
    nb = seq // BLOCK
    wb = WIN_BLOCKS
    qmap = lambda b, n: (b, n, 0)
    blk = lambda off: (lambda n: jnp.clip(n * wb + off, 0, nb - 1))
    kspec = lambda f: pl.BlockSpec((1, BLOCK, 2 * KV_W), lambda b, n: (b, f(n), 0))
    vspec = lambda f: pl.BlockSpec((1, KV_W, BLOCK), lambda b, n: (b, 0, f(n)))
    offs = range(-1, wb + 1)
    return pl.pallas_call(
        functools.partial(_win_attn_body, seq=seq), grid=(bsz, nb // wb),
        in_specs=[pl.BlockSpec(memory_space=pltpu.SMEM), pl.BlockSpec((1, wb * BLOCK, Q_W), qmap)]
        + [kspec(blk(o)) for o in offs] + [vspec(blk(o)) for o in offs],
        out_specs=pl.BlockSpec((1, wb * BLOCK, Q_W), qmap),
        out_shape=jax.ShapeDtypeStruct((bsz, seq, Q_W), BF16), name="win_attn",
        compiler_params=pltpu.CompilerParams(dimension_semantics=("arbitrary", "arbitrary")),
    )(sink, qa, *([ka] * (wb + 2)), *([vat] * (wb + 2)))


def _grid_attn_body(q_ref, k_ref, vt_ref, o_ref, s0_ref, s1_ref, p0_ref, p1_ref):
    tq = q_ref.shape[1]
    seq = k_ref.shape[1]
    kc = KEY_CHUNK
    n_chunks = seq // kc
    s_bufs, p_bufs = (s0_ref, s1_ref), (p0_ref, p1_ref)
    n_heads = N_HEADS
    ones = jnp.ones((2 * ROW_TILE, seq), BF16)

    def masked_q(h):
        slab = q_ref[0, :, (h // 2) * LANES:(h // 2 + 1) * LANES]
        return jnp.where(_half_mask(tq, h % 2), slab, jnp.zeros_like(slab))

    def score_chunk(h, qm, kb, m8):
        v = 0 if h // HEADS_PER_KV == h % 2 else 1
        sc = _dot_nt(k_ref[0, kb * kc:(kb + 1) * kc, v * LANES:(v + 1) * LANES], qm)
        s_bufs[h % 2][kb * kc:(kb + 1) * kc, :] = sc
        cm = jnp.max(sc.reshape(kc // ROW_TILE, ROW_TILE, tq), axis=0)
        return cm if m8 is None else jnp.maximum(m8, cm)

    def prob_chunk(h, kb, m):
        x = s_bufs[h % 2][kb * kc:(kb + 1) * kc, :] - m
        p_bufs[h % 2][kb * kc:(kb + 1) * kc, :] = jnp.exp2(x.astype(BF16))

    qm = masked_q(0)
    m8 = None
    for kb in range(n_chunks):
        m8 = score_chunk(0, qm, kb, m8)
    outs = []
    for h in range(n_heads):
        m = jnp.max(m8, axis=0, keepdims=True)
        nxt = h + 1 < n_heads
        if nxt:
            qm = masked_q(h + 1)
            m8 = None
        for kb in range(n_chunks):
            if nxt:
                m8 = score_chunk(h + 1, qm, kb, m8)
            prob_chunk(h, kb, m)
        c = h // HEADS_PER_KV
        va = jnp.concatenate([vt_ref[0, c * HEAD_DIM:(c + 1) * HEAD_DIM, :], ones], axis=0)
        ot = _dot(va, p_bufs[h % 2][...])
        outs.append(ot[:HEAD_DIM] / ot[HEAD_DIM:HEAD_DIM + 1])
        if h % 2 == 1:
            o_ref[0, :, (h // 2) * LANES:(h // 2 + 1) * LANES] = jnp.concatenate(outs, axis=0).T.astype(BF16)
            outs = []


def _grid_attn(qb, kb, vbt):
    bsz, seq, _ = qb.shape
    tq = TQ_GRID
    return pl.pallas_call(
        _grid_attn_body, grid=(bsz, seq // tq),
        in_specs=[pl.BlockSpec((1, tq, Q_W), lambda b, n: (b, n, 0)),
                  pl.BlockSpec((1, seq, 2 * KV_W), lambda b, n: (b, 0, 0)),
                  pl.BlockSpec((1, KV_W, seq), lambda b, n: (b, 0, 0))],
        out_specs=pl.BlockSpec((1, tq, Q_W), lambda b, n: (b, n, 0)),
        scratch_shapes=[pltpu.VMEM((seq, tq), F32), pltpu.VMEM((seq, tq), F32),
                        pltpu.VMEM((seq, tq), BF16), pltpu.VMEM((seq, tq), BF16)],
        out_shape=jax.ShapeDtypeStruct((bsz, seq, Q_W), BF16), name="grid_attn",
        compiler_params=pltpu.CompilerParams(
            dimension_semantics=("arbitrary", "arbitrary"), vmem_limit_bytes=VMEM_LIMIT_GRID_ATTN),
    )(qb, kb, vbt)


def _post_attn_body(x_ref, oa_ref, ob_ref, sga_ref, sgb_ref, g0_ref, b0_ref, wa_ref, wb_ref, wo_ref,
                    g1_ref, b1_ref, wr_ref, br_ref, h1t_ref, tr_ref, tw_ref, cnt_out_ref, cnt_ref):
    h0 = _ln(x_ref[0], g0_ref[...], b0_ref[...])
    out_a = _dot(oa_ref[0], wa_ref[...])
    out_b = _dot(ob_ref[0], wb_ref[...])
    merged = sga_ref[0].astype(F32) * out_a + sgb_ref[0].astype(F32) * out_b
    mix = _dot(merged.astype(BF16), wo_ref[...])
    h1 = _ln(DN_ALPHA * h0 + mix, g1_ref[...], b1_ref[...])
    tm = h1.shape[0]
    _store_row_tiles(h1t_ref.at[0], h1, tm)

    logits = (_dot_nt(wr_ref[...], h1.astype(BF16)) + br_ref[...])[:N_EXPERTS]
    sub = lax.broadcasted_iota(jnp.int32, (N_EXPERTS, tm), 0)
    cur = logits
    vals, idxs = [], []
    for _ in range(TOP_K):
        mv = jnp.max(cur, axis=0, keepdims=True)
        ix = jnp.min(jnp.where(cur == mv, sub, N_EXPERTS), axis=0, keepdims=True)
        vals.append(mv)
        idxs.append(ix)
        cur = jnp.where(sub == ix, -jnp.inf, cur)
    es = [jnp.exp(v - vals[0]) for v in vals]
    tot = es[0] + es[1] + es[2] + es[3]

    @pl.when((pl.program_id(0) == 0) & (pl.program_id(1) == 0))
    def _():
        cnt_ref[...] = jnp.zeros_like(cnt_ref)

    sel = jnp.zeros((N_EXPERTS, tm), F32)
    for kx in range(TOP_K):
        sel = sel + (sub == idxs[kx]).astype(F32)
    r_i = lax.broadcasted_iota(jnp.int32, (tm, tm), 0)
    c_i = lax.broadcasted_iota(jnp.int32, (tm, tm), 1)
    tri = (r_i < c_i).astype(BF16)
    rank = _dot(sel.astype(BF16), tri) + cnt_ref[:, 0:1]
    cnt_ref[...] = cnt_ref[...] + jnp.sum(sel, axis=1, keepdims=True)
    cnt_out_ref[...] = cnt_ref[...]

    rks = [jnp.sum(jnp.where(sub == ix, rank, 0.0), axis=0, keepdims=True).astype(jnp.int32) for ix in idxs]
    tr_ref[...] = jnp.concatenate(idxs + rks, axis=0)
    tw_t = jnp.concatenate([e / tot for e in es] + [jnp.zeros((LANES - TOP_K, tm), F32)], axis=0)
    tw_ref[0] = tw_t.T


def _post_attn(x, oa, ob, sga, sgb, g0, b0, wa, wb, wo, g1, b1, wr, br):
    bsz, seq, d = x.shape
    tm = TM_PROJ
    tok3 = lambda b, i: (b, i, 0)
    const = lambda b, i: (0, 0)
    full = lambda a: pl.BlockSpec(a.shape, const)
    return pl.pallas_call(
        _post_attn_body, grid=(bsz, seq // tm),
        in_specs=[pl.BlockSpec((1, tm, d), tok3),
                  pl.BlockSpec((1, tm, Q_W), tok3), pl.BlockSpec((1, tm, Q_W), tok3),
                  pl.BlockSpec((1, tm, d), tok3), pl.BlockSpec((1, tm, d), tok3),
                  full(g0), full(b0), full(wa), full(wb), full(wo), full(g1), full(b1),
                  full(wr), full(br)],
        out_specs=[pl.BlockSpec((1, tm * ROW_TILE, LANES), tok3),
                   pl.BlockSpec((2 * TOP_K, tm), lambda b, i: (0, b * (seq // tm) + i)),
                   pl.BlockSpec((1, tm, LANES), tok3), pl.BlockSpec((N_EXPERTS, LANES), const)],
        out_shape=[jax.ShapeDtypeStruct((bsz, seq * ROW_TILE, LANES), F32),
                   jax.ShapeDtypeStruct((2 * TOP_K, bsz * seq), jnp.int32),
                   jax.ShapeDtypeStruct((bsz, seq, LANES), F32),
                   jax.ShapeDtypeStruct((N_EXPERTS, LANES), F32)],
        scratch_shapes=[pltpu.VMEM((N_EXPERTS, LANES), F32)],
        name="post_attn",
        compiler_params=pltpu.CompilerParams(
            dimension_semantics=("arbitrary", "arbitrary"), vmem_limit_bytes=VMEM_LIMIT_PROJ),
    )(x, oa, ob, sga, sgb, g0, b0, wa, wb, wo, g1, b1, wr, br)


def _dispatch_body(dest_ref, pend_ref, padded_ref, nu_ref, h1t_ref, xs_hbm, zbuf, sem, zsem):
    tm = TM_DISPATCH
    zrows = BM_EXPERT * ROW_TILE
    n_blocks = xs_hbm.shape[0] // zrows

    @pl.when(pl.program_id(0) == 0)
    def _():
        zbuf[...] = jnp.zeros_like(zbuf)
        zero_wait = pltpu.make_async_copy(zbuf, xs_hbm.at[pl.ds(0, zrows)], zsem).wait
        for e in range(N_EXPERTS):
            @pl.when(padded_ref[e] > 0)
            def _():
                start = pl.multiple_of((pend_ref[e] - BM_EXPERT) * ROW_TILE, ROW_TILE)
                pltpu.make_async_copy(zbuf, xs_hbm.at[pl.ds(start, zrows)], zsem).start()
        for b in range(n_blocks - N_EXPERTS, n_blocks):
            @pl.when(b >= nu_ref[0])
            def _():
                pltpu.make_async_copy(zbuf, xs_hbm.at[pl.ds(b * zrows, zrows)], zsem).start()
        for e in range(N_EXPERTS):
            pl.when(padded_ref[e] > 0)(zero_wait)
        for b in range(n_blocks - N_EXPERTS, n_blocks):
            pl.when(b >= nu_ref[0])(zero_wait)

    for t in range(tm):
        for kx in range(TOP_K):
            d = dest_ref[kx, t]
            pltpu.make_async_copy(_tile(h1t_ref, t), _tile(xs_hbm, d), sem).start(priority=kx % 2)
    for _ in range(TOP_K):
        pltpu.make_async_copy(h1t_ref, xs_hbm.at[pl.ds(0, tm * ROW_TILE)], sem).wait()


def _dispatch(dest_km, pad_end, padded, n_used, h1t, n_rows):
    n_tok = h1t.shape[0] // ROW_TILE
    tm = TM_DISPATCH
    return pl.pallas_call(
        _dispatch_body, grid=(n_tok // tm,),
        in_specs=[pl.BlockSpec((TOP_K, tm), lambda i: (0, i), memory_space=pltpu.SMEM),
                  pl.BlockSpec(memory_space=pltpu.SMEM), pl.BlockSpec(memory_space=pltpu.SMEM),
                  pl.BlockSpec(memory_space=pltpu.SMEM),
                  pl.BlockSpec((tm * ROW_TILE, LANES), lambda i: (i, 0))],
        out_specs=pl.BlockSpec(memory_space=pl.ANY),
        out_shape=jax.ShapeDtypeStruct((n_rows * ROW_TILE, LANES), F32),
        scratch_shapes=[pltpu.VMEM((BM_EXPERT * ROW_TILE, LANES), F32),
                        pltpu.SemaphoreType.DMA(()), pltpu.SemaphoreType.DMA(())],
        name="dispatch",
        compiler_params=pltpu.CompilerParams(dimension_semantics=("arbitrary",)),
    )(dest_km, pad_end, padded, n_used, h1t)


def _experts_body(be_ref, nu_ref, nx_ref, nv_ref, xs_ref, bg_ref, bu_ref, bd_ref, wg_hbm, wu_hbm, wd_hbm, ys_ref,
                  stage, wg_s, wu_s, wd_s, wsem):
    i = pl.program_id(0)
    bm = BM_EXPERT
    used = i < nu_ref[0]
    prev = be_ref[jnp.maximum(i - 1, 0)]
    fresh = (i == 0) | (be_ref[i] != prev)

    def weight_copies(e):
        return [pltpu.make_async_copy(w_hbm.at[e], stage.at[n], wsem)
                for n, w_hbm in enumerate((wg_hbm, wu_hbm, wd_hbm))]

    @pl.when(i == 0)
    def _():
        for cp in weight_copies(be_ref[0]):
            cp.start()

    @pl.when(used & fresh)
    def _():
        for cp in weight_copies(be_ref[i]):
            cp.wait()
        wg_s[...] = stage[0].astype(BF16)
        wu_s[...] = stage[1].astype(BF16)
        wd_s[...] = stage[2].astype(BF16)

        @pl.when(nx_ref[i] >= 0)
        def _():
            for cp in weight_copies(nx_ref[i]):
                cp.start()

    def ffn(rows):
        xb = _load_row_tiles(xs_ref, rows).astype(BF16)
        g = _dot(xb, wg_s[...]) + bg_ref[0]
        u = _dot(xb, wu_s[...]) + bu_ref[0]
        g = jnp.minimum(g, SWIGLU_LIMIT)
        u = jnp.clip(u, -SWIGLU_LIMIT, SWIGLU_LIMIT)
        act = g * jax.nn.sigmoid(SWIGLU_ALPHA * g) * (u + 1.0)
        _store_row_tiles(ys_ref, _dot(act.astype(BF16), wd_s[...]) + bd_ref[0], rows)

    half = bm // 2
    full = used & (nv_ref[i] > half)

    @pl.when(full)
    def _():
        ffn(bm)

    @pl.when(used & jnp.logical_not(full))
    def _():
        ffn(half)
        ys_ref[pl.ds(half * ROW_TILE, half * ROW_TILE), :] = jnp.zeros((half * ROW_TILE, LANES), F32)

    @pl.when(jnp.logical_not(used))
    def _():
        ys_ref[...] = jnp.zeros_like(ys_ref)


def _experts(block_e, n_used, next_e, n_valid, xs, wg, bg, wu, bu, wd, bd):
    n_rows = xs.shape[0] // ROW_TILE
    bm = BM_EXPERT
    n_e, d, d_ff = wg.shape
    assert d == d_ff and wd.shape == wg.shape
    row = lambda i, be, nu, nx, nv: (jnp.minimum(i, nu[0] - 1), 0)
    exp3 = lambda i, be, nu, nx, nv: (be[jnp.minimum(i, nu[0] - 1)], 0, 0)
    any_spec = pl.BlockSpec(memory_space=pl.ANY)
    grid_spec = pltpu.PrefetchScalarGridSpec(
        num_scalar_prefetch=4, grid=(n_rows // bm,),
        in_specs=[pl.BlockSpec((bm * ROW_TILE, LANES), row),
                  pl.BlockSpec((1, 1, d_ff), exp3), pl.BlockSpec((1, 1, d_ff), exp3),
                  pl.BlockSpec((1, 1, d), exp3), any_spec, any_spec, any_spec],
        out_specs=pl.BlockSpec((bm * ROW_TILE, LANES), lambda i, be, nu, nx, nv: (i, 0)),
        scratch_shapes=[pltpu.VMEM((3, d, d_ff), F32),
                        pltpu.VMEM((d, d_ff), BF16), pltpu.VMEM((d, d_ff), BF16), pltpu.VMEM((d_ff, d), BF16),
                        pltpu.SemaphoreType.DMA(())])
    return pl.pallas_call(
        _experts_body, grid_spec=grid_spec,
        out_shape=jax.ShapeDtypeStruct((n_rows * ROW_TILE, LANES), F32), name="experts",
        compiler_params=pltpu.CompilerParams(
            dimension_semantics=("arbitrary",), vmem_limit_bytes=VMEM_LIMIT_EXPERTS),
    )(block_e, n_used, next_e, n_valid, xs, bg.reshape(n_e, 1, d_ff), bu.reshape(n_e, 1, d_ff), bd.reshape(n_e, 1, d),
      wg, wu, wd)


def _combine_body(dest_ref, dest_next_ref, h1t_ref, tw_ref, g2_ref, b2_ref, ys_hbm, out_ref, buf, sems):
    tm = TM_COMBINE
    i = pl.program_id(0)
    slot = i % 2

    def start_row(d_ref, s, t):
        for kx in range(TOP_K):
            d = d_ref[kx, t]
            pltpu.make_async_copy(_tile(ys_hbm, d), _tile(buf.at[s, kx], t), sems.at[s]).start(priority=kx % 2)

    def wait_tile(s):
        for kx in range(TOP_K):
            pltpu.make_async_copy(ys_hbm.at[pl.ds(0, tm * ROW_TILE)], buf.at[s, kx], sems.at[s]).wait()

    @pl.when(i == 0)
    def _():
        lax.fori_loop(0, tm, lambda t, c: (start_row(dest_ref, 0, t), c)[1], 0)

    wait_tile(slot)
    rc = COMBINE_ROW_CHUNK
    for c in range(tm // rc):
        for t in range(c * rc, (c + 1) * rc):
            start_row(dest_next_ref, 1 - slot, t)
        tw = tw_ref[c * rc:(c + 1) * rc, :]
        ffn = tw[:, 0:1] * _load_row_tiles(buf.at[slot, 0], rc, c * rc)
        for kx in range(1, TOP_K):
            ffn = ffn + tw[:, kx:kx + 1] * _load_row_tiles(buf.at[slot, kx], rc, c * rc)
        h1 = _load_row_tiles(h1t_ref, rc, c * rc)
        out_ref[c * rc:(c + 1) * rc, :] = _ln(DN_ALPHA * h1 + ffn, g2_ref[...], b2_ref[...])

    @pl.when(i == pl.num_programs(0) - 1)
    def _():
        wait_tile(1 - slot)


def _combine(dest_km, h1t, tw, g2, b2, ys):
    n_tok = h1t.shape[0] // ROW_TILE
    d = ROW_TILE * LANES
    tm = TM_COMBINE
    n_steps = n_tok // tm
    const = lambda i: (0, 0)
    return pl.pallas_call(
        _combine_body, grid=(n_steps,),
        in_specs=[pl.BlockSpec((TOP_K, tm), lambda i: (0, i), memory_space=pltpu.SMEM),
                  pl.BlockSpec((TOP_K, tm), lambda i: (0, jnp.minimum(i + 1, n_steps - 1)),
                               memory_space=pltpu.SMEM),
                  pl.BlockSpec((tm * ROW_TILE, LANES), lambda i: (i, 0)),
                  pl.BlockSpec((tm, LANES), lambda i: (i, 0)),
                  pl.BlockSpec((1, d), const), pl.BlockSpec((1, d), const),
                  pl.BlockSpec(memory_space=pl.ANY)],
        out_specs=pl.BlockSpec((tm, d), lambda i: (i, 0)),
        out_shape=jax.ShapeDtypeStruct((n_tok, d), F32),
        scratch_shapes=[pltpu.VMEM((2, TOP_K, tm * ROW_TILE, LANES), F32), pltpu.SemaphoreType.DMA((2,))],
        name="combine",
        compiler_params=pltpu.CompilerParams(
            dimension_semantics=("arbitrary",), vmem_limit_bytes=VMEM_LIMIT_COMBINE),
    )(dest_km, dest_km, h1t, tw, g2, b2, ys)


def _rope_tables(seq):
    t = np.arange(seq)
    row = (t // GRID_W).astype(np.float32)
    col = (t % GRID_W).astype(np.float32)
    half = HEAD_DIM // 2
    quarter = half // 2
    inv = (ROPE_THETA ** (-np.arange(quarter, dtype=np.float32) * np.float32(2.0 / half))).astype(np.float32)
    ang_r = row[:, None] * inv[None, :]
    ang_c = col[:, None] * inv[None, :]
    zeros = np.zeros_like(ang_r)
    cos = np.concatenate([np.cos(ang_r), np.cos(ang_r), np.cos(ang_c), np.cos(ang_c)], -1)
    s_lo = np.concatenate([-np.sin(ang_r), zeros, -np.sin(ang_c), zeros], -1)
    s_hi = np.concatenate([zeros, np.sin(ang_r), zeros, np.sin(ang_c)], -1)
    return tuple(jnp.asarray(np.tile(a, (1, LANES // HEAD_DIM)), F32) for a in (cos, s_lo, s_hi))


def _routing(top_i, rank, counts, bm):
    n_tok = top_i.shape[1]
    padded = (counts + bm - 1) // bm * bm
    pad_end = jnp.cumsum(padded)
    pad_start = pad_end - padded
    base = jnp.sum(jnp.where(top_i[:, :, None] == jnp.arange(N_EXPERTS, dtype=jnp.int32)[None, None, :],
                             pad_start[None, None, :], 0), axis=-1)
    dest = (base + rank).astype(jnp.int32)
    n_rows = n_tok * TOP_K + N_EXPERTS * bm
    n_blocks = n_rows // bm
    block_start = jnp.arange(n_blocks, dtype=jnp.int32) * bm
    block_e = jnp.minimum(jnp.sum((pad_end[None, :] <= block_start[:, None]).astype(jnp.int32), axis=1),
                          N_EXPERTS - 1).astype(jnp.int32)
    n_used = (pad_end[-1] // bm).astype(jnp.int32).reshape(1)
    e_ids = jnp.arange(N_EXPERTS, dtype=jnp.int32)
    later = (e_ids[None, :] > e_ids[:, None]) & (counts[None, :] > 0)
    next_of = jnp.min(jnp.where(later, e_ids[None, :], N_EXPERTS), axis=1)
    next_of = jnp.where(next_of == N_EXPERTS, -1, next_of).astype(jnp.int32)
    next_e = jnp.sum(jnp.where(block_e[:, None] == e_ids[None, :], next_of[None, :], 0), axis=1).astype(jnp.int32)
    start_of = jnp.sum(jnp.where(block_e[:, None] == e_ids[None, :], pad_start[None, :], 0), axis=1)
    count_of = jnp.sum(jnp.where(block_e[:, None] == e_ids[None, :], counts[None, :], 0), axis=1)
    n_valid = jnp.clip(count_of - (block_start - start_of), 0, bm).astype(jnp.int32)
    return (dest, block_e, n_used, next_e, n_valid, n_rows, pad_end.astype(jnp.int32),
            padded.astype(jnp.int32))


def kernel(x, ln0_g, ln0_b, w_in, a_sink, b_q_norm, b_k_norm, w_branch_a, w_branch_b, w_out,
           ln1_g, ln1_b, w_router, b_router, w_gate, b_gate, w_up, b_up, w_down, b_down,
           ln2_g, ln2_b):
    bsz, seq, d = x.shape
    assert w_in.shape[0] == DEPTH == 1
    assert seq % TM_PROJ == 0 and seq % TQ_GRID == 0 and seq == (seq // GRID_W) * GRID_W
    n_tok = bsz * seq
    row2 = lambda a: a.reshape(1, -1)

    w_perm = w_in[0].astype(BF16)
    head_id = np.arange(Q_W) // HEAD_DIM
    bd = jnp.asarray(head_id[:, None] == head_id[None, :], BF16)
    gq = jnp.tile(b_q_norm[0].astype(F32), N_HEADS).reshape(1, Q_W)
    gk = jnp.tile(b_k_norm[0].astype(F32), N_KV).reshape(1, KV_W)

    qa, ka, vat, qb, kb, vbt, sga, sgb = _in_proj(
        x, row2(ln0_g), row2(ln0_b), w_perm, bd, gq, gk, _rope_tables(seq))
    oa = _win_attn(a_sink[0].astype(F32), qa, ka, vat)
    ob = _grid_attn(qb, kb, vbt)

    wr = jnp.zeros((LANES, d), BF16).at[:N_EXPERTS].set(w_router[0].T.astype(BF16))
    br = jnp.zeros((LANES, 1), F32).at[:N_EXPERTS, 0].set(b_router[0])
    h1t, tr, tw, cnt = _post_attn(
        x, oa, ob, sga, sgb, row2(ln0_g), row2(ln0_b),
        w_branch_a[0].astype(BF16), w_branch_b[0].astype(BF16),
        w_out[0].astype(BF16), row2(ln1_g[0]), row2(ln1_b[0]), wr, br)
    h1t = h1t.reshape(n_tok * ROW_TILE, LANES)
    top_i, rank = tr[:TOP_K], tr[TOP_K:]
    tw = tw.reshape(n_tok, LANES)

    counts = cnt[:, 0].astype(jnp.int32)
    dest, block_e, n_used, next_e, n_valid, n_rows, pad_end, padded = _routing(top_i, rank, counts, BM_EXPERT)
    xs = _dispatch(dest, pad_end, padded, n_used, h1t, n_rows)
    ys = _experts(block_e, n_used, next_e, n_valid, xs, w_gate[0], b_gate[0], w_up[0], b_up[0], w_down[0], b_down[0])
    out = _combine(dest, h1t, tw, row2(ln2_g[0]), row2(ln2_b[0]), ys)
    return out.reshape(bsz, seq, d)
```

```python
import functools

import jax
import jax.numpy as jnp
import numpy as np
from jax import lax
from jax.experimental import pallas as pl
from jax.experimental.pallas import tpu as pltpu

HEAD_DIM = 64
N_HEADS = 8
N_KV = 2
WINDOW = 128
BLOCK = 128
GRID_W = 64
ROPE_THETA = 10000.0
N_EXPERTS = 32
TOP_K = 4
SWIGLU_LIMIT = 7.0
SWIGLU_ALPHA = 1.702
LN_EPS = 1e-5
RMS_EPS = 1e-6
NEG_INF = -1e30
DEPTH = 1
DN_ALPHA = (2.0 * DEPTH) ** 0.25
ALIBI_SLOPES = tuple(2.0 ** (-8.0 * (h + 1) / N_HEADS) for h in range(N_HEADS))
QK_SCALE = HEAD_DIM ** -0.5
LOG2_E = 1.4426950408889634

LANES = 128
ROW_TILE = 8
Q_W = N_HEADS * HEAD_DIM
KV_W = N_KV * HEAD_DIM
N_SLABS = Q_W // LANES
HEADS_PER_KV = N_HEADS // N_KV
ROPE_SHIFT = HEAD_DIM // 4

_MIB = 1024 * 1024
VMEM_LIMIT_PROJ = 48 * _MIB
VMEM_LIMIT_GRID_ATTN = 40 * _MIB
VMEM_LIMIT_EXPERTS = 48 * _MIB
VMEM_LIMIT_COMBINE = 40 * _MIB

TM_PROJ = 1024
TM_IN_PROJ = 1024
PROJ_ROW_GROUPS = 2
WIN_BLOCKS = 8
TQ_GRID = 256
KEY_CHUNK = 256
BM_EXPERT = 512
FFN_CHUNKS = 2
TM_DISPATCH = 1024
TM_COMBINE = 512
COMBINE_ROW_CHUNK = 32

F32 = jnp.float32
BF16 = jnp.bfloat16


def _ln(x, g, b):
    mu = jnp.mean(x, -1, keepdims=True)
    xc = x - mu
    var = jnp.mean(xc * xc, -1, keepdims=True)
    return xc * lax.rsqrt(var + LN_EPS) * g + b


def _dot(a, b):
    return jnp.dot(a, b, preferred_element_type=F32)


def _dot_nt(a, b):
    return lax.dot_general(a, b, (((1,), (1,)), ((), ())), preferred_element_type=F32)


def _load_row_tiles(ref, rows, first=0):
    return jnp.concatenate(
        [ref[pl.ds(first * ROW_TILE + c, rows, stride=ROW_TILE), :] for c in range(ROW_TILE)], axis=1)


def _store_row_tiles(ref, val, rows):
    for c in range(ROW_TILE):
        ref[pl.ds(c, rows, stride=ROW_TILE), :] = val[:, c * LANES:(c + 1) * LANES]


def _tile(ref, row):
    start = row * ROW_TILE
    if not isinstance(row, int):
        start = pl.multiple_of(start, ROW_TILE)
    return ref.at[pl.ds(start, ROW_TILE)]


def _in_proj_body(x_ref, g0_ref, b0_ref, w_ref, bd_ref, gq_ref, gk_ref, c_ref, s1_ref, s2_ref,
                  qa_ref, ka_ref, vat_ref, qb_ref, kb_ref, vbt_ref, sga_ref, sgb_ref):
    tm = x_ref.shape[1]
    d = sga_ref.shape[-1]
    o_kva, o_qb, o_kvb, o_g = Q_W, Q_W + 2 * KV_W, 2 * Q_W + 2 * KV_W, 2 * Q_W + 4 * KV_W
    both = lambda k: jnp.concatenate([k, pltpu.roll(k, HEAD_DIM, 1)], axis=1).astype(BF16)
    rg = tm // PROJ_ROW_GROUPS
    for grp in range(PROJ_ROW_GROUPS):
        rows = slice(grp * rg, (grp + 1) * rg)
        hb = _ln(x_ref[0, rows, :], g0_ref[...], b0_ref[...]).astype(BF16)

        def proj(lo, hi):
            return _dot(hb, w_ref[:, lo:hi])

        def norm_rope(t, width, g_ref):
            ss = _dot((t * t).astype(BF16), bd_ref[:width, :width])
            r = lax.rsqrt(ss * (1.0 / HEAD_DIM) + RMS_EPS)
            reps = width // LANES
            tab = lambda ref: jnp.concatenate([ref[rows, :]] * reps, axis=1)
            y = t * g_ref[...]
            rot = (y * tab(c_ref) + pltpu.roll(y, width - ROPE_SHIFT, 1) * tab(s1_ref)
                   + pltpu.roll(y, ROPE_SHIFT, 1) * tab(s2_ref))
            return rot * r

        qa_ref[0, rows, :] = (proj(0, o_kva) * (QK_SCALE * LOG2_E)).astype(BF16)
        kva = proj(o_kva, o_qb)
        ka_ref[0, rows, :] = both(kva[:, :KV_W])
        vat_ref[0, :, rows] = kva[:, KV_W:].T.astype(BF16)
        qb = norm_rope(proj(o_qb, o_kvb), Q_W, gq_ref)
        qb_ref[0, rows, :] = (qb * (QK_SCALE * LOG2_E)).astype(BF16)
        kvb = proj(o_kvb, o_g)
        kb_ref[0, rows, :] = both(norm_rope(kvb[:, :KV_W], KV_W, gk_ref))
        vbt_ref[0, :, rows] = kvb[:, KV_W:].T.astype(BF16)
        sga_ref[0, rows, :] = jax.nn.sigmoid(proj(o_g, o_g + d)).astype(BF16)
        sgb_ref[0, rows, :] = jax.nn.sigmoid(proj(o_g + d, o_g + 2 * d)).astype(BF16)


def _in_proj(x, g0, b0, w, bd, gq, gk, tabs):
    bsz, seq, d = x.shape
    tm = TM_IN_PROJ
    n_in = w.shape[1]
    const = lambda i, j: (0, 0)
    tok3 = lambda i, j: (j, i, 0)
    tab = lambda i, j: (i, 0)
    in_specs = [
        pl.BlockSpec((1, tm, d), tok3),
        pl.BlockSpec((1, d), const), pl.BlockSpec((1, d), const),
        pl.BlockSpec((d, n_in), const),
        pl.BlockSpec((Q_W, Q_W), const),
        pl.BlockSpec((1, Q_W), const), pl.BlockSpec((1, KV_W), const),
        pl.BlockSpec((tm, LANES), tab), pl.BlockSpec((tm, LANES), tab), pl.BlockSpec((tm, LANES), tab),
    ]
    tr3 = lambda i, j: (j, 0, i)
    out_specs = [
        pl.BlockSpec((1, tm, Q_W), tok3), pl.BlockSpec((1, tm, 2 * KV_W), tok3),
        pl.BlockSpec((1, KV_W, tm), tr3),
        pl.BlockSpec((1, tm, Q_W), tok3), pl.BlockSpec((1, tm, 2 * KV_W), tok3),
        pl.BlockSpec((1, KV_W, tm), tr3),
        pl.BlockSpec((1, tm, d), tok3), pl.BlockSpec((1, tm, d), tok3),
    ]
    out_shape = [
        jax.ShapeDtypeStruct((bsz, seq, Q_W), BF16), jax.ShapeDtypeStruct((bsz, seq, 2 * KV_W), BF16),
        jax.ShapeDtypeStruct((bsz, KV_W, seq), BF16),
        jax.ShapeDtypeStruct((bsz, seq, Q_W), BF16), jax.ShapeDtypeStruct((bsz, seq, 2 * KV_W), BF16),
        jax.ShapeDtypeStruct((bsz, KV_W, seq), BF16),
        jax.ShapeDtypeStruct((bsz, seq, d), BF16), jax.ShapeDtypeStruct((bsz, seq, d), BF16),
    ]
    return pl.pallas_call(
        _in_proj_body, grid=(seq // tm, bsz), in_specs=in_specs, out_specs=out_specs,
        out_shape=out_shape, name="in_proj",
        compiler_params=pltpu.CompilerParams(
            dimension_semantics=("arbitrary", "arbitrary"), vmem_limit_bytes=VMEM_LIMIT_PROJ),
    )(x, g0, b0, w, bd, gq, gk, *tabs)


def _half_mask(rows, c):
    lane = lax.broadcasted_iota(jnp.int32, (rows, LANES), 1)
    return (lane >= HEAD_DIM) if c == 1 else (lane < HEAD_DIM)


def _win_attn_body(sink_ref, q_ref, *refs, seq):
    nk = WIN_BLOCKS + 2
    k_refs, v_refs, o_ref = refs[:nk], refs[nk:2 * nk], refs[2 * nk]
    kk = lax.broadcasted_iota(jnp.int32, (3 * BLOCK, BLOCK), 0)
    qq = lax.broadcasted_iota(jnp.int32, (3 * BLOCK, BLOCK), 1)
    dist_i = jnp.abs(kk - BLOCK - qq)
    dist = dist_i.astype(F32)
    ones = jnp.ones((2 * ROW_TILE, 3 * BLOCK), BF16)
    for blk in range(WIN_BLOCKS):
        n = pl.program_id(1) * WIN_BLOCKS + blk
        rows = slice(blk * BLOCK, (blk + 1) * BLOCK)
        k2 = jnp.concatenate([r[0] for r in k_refs[blk:blk + 3]], axis=0)
        vt = jnp.concatenate([r[0] for r in v_refs[blk:blk + 3]], axis=1)
        k_pos = n * BLOCK - BLOCK + kk
        valid = (dist_i <= WINDOW) & (k_pos >= 0) & (k_pos < seq)
        slabs = [q_ref[0, rows, j * LANES:(j + 1) * LANES] for j in range(N_SLABS)]
        variant = [[h for h in range(N_HEADS) if (h // HEADS_PER_KV == h % 2) == straight]
                   for straight in (True, False)]
        scores = {}
        for v, heads in enumerate(variant):
            qm = jnp.concatenate([jnp.where(_half_mask(BLOCK, h % 2), slabs[h // 2], jnp.zeros_like(slabs[0]))
                                  for h in heads], axis=0)
            st_v = _dot_nt(k2[:, v * LANES:(v + 1) * LANES], qm)
            for col, h in enumerate(heads):
                scores[h] = st_v[:, col * BLOCK:(col + 1) * BLOCK]
        outs = {}
        for c in range(N_KV):
            heads = range(c * HEADS_PER_KV, (c + 1) * HEADS_PER_KV)
            ps, sinks = [], []
            for h in heads:
                st = scores[h] + jnp.where(valid, (-ALIBI_SLOPES[h] * LOG2_E) * dist, NEG_INF)
                sk = sink_ref[h] * LOG2_E
                m = jnp.maximum(jnp.max(st, axis=0, keepdims=True), sk)
                ps.append(jnp.exp2((st - m).astype(BF16)))
                sinks.append(jnp.exp2(sk - m))
            va = jnp.concatenate([vt[c * HEAD_DIM:(c + 1) * HEAD_DIM, :], ones], axis=0)
            ot = _dot(va, jnp.concatenate(ps, axis=1))
            ot = ot[:HEAD_DIM] / (ot[HEAD_DIM:HEAD_DIM + 1] + jnp.concatenate(sinks, axis=1))
            for col, h in enumerate(heads):
                outs[h] = ot[:, col * BLOCK:(col + 1) * BLOCK]
        for j in range(N_SLABS):
            pair = jnp.concatenate([outs[2 * j], outs[2 * j + 1]], axis=0)
            o_ref[0, rows, j * LANES:(j + 1) * LANES] = pair.T.astype(BF16)


def _win_attn(sink, qa, ka, vat):
    bsz, seq, _ = qa.shape
    nb = seq // BLOCK
    wb = WIN_BLOCKS
    qmap = lambda b, n: (b, n, 0)
    blk = lambda off: (lambda n: jnp.clip(n * wb + off, 0, nb - 1))
    kspec = lambda f: pl.BlockSpec((1, BLOCK, 2 * KV_W), lambda b, n: (b, f(n), 0))
    vspec = lambda f: pl.BlockSpec((1, KV_W, BLOCK), lambda b, n: (b, 0, f(n)))
    offs = range(-1, wb + 1)
    return pl.pallas_call(
        functools.partial(_win_attn_body, seq=seq), grid=(bsz, nb // wb),
        in_specs=[pl.BlockSpec(memory_space=pltpu.SMEM), pl.BlockSpec((1, wb * BLOCK, Q_W), qmap)]
        + [kspec(blk(o)) for o in offs] + [vspec(blk(o)) for o in offs],
        out_specs=pl.BlockSpec((1, wb * BLOCK, Q_W), qmap),
        out_shape=jax.ShapeDtypeStruct((bsz, seq, Q_W), BF16), name="win_attn",
        compiler_params=pltpu.CompilerParams(dimension_semantics=("arbitrary", "arbitrary")),
    )(sink, qa, *([ka] * (wb + 2)), *([vat] * (wb + 2)))


def _grid_attn_body(q_ref, k_ref, vt_ref, o_ref, s0_ref, s1_ref, p0_ref, p1_ref):
    tq = q_ref.shape[1]
    seq = k_ref.shape[1]
    kc = KEY_CHUNK
    n_chunks = seq // kc
    s_bufs, p_bufs = (s0_ref, s1_ref), (p0_ref, p1_ref)
    n_heads = N_HEADS
    ones = jnp.ones((2 * ROW_TILE, seq), BF16)

    def masked_q(h):
        slab = q_ref[0, :, (h // 2) * LANES:(h // 2 + 1) * LANES]
        return jnp.where(_half_mask(tq, h % 2), slab, jnp.zeros_like(slab))

    def score_chunk(h, qm, kb, m8):
        v = 0 if h // HEADS_PER_KV == h % 2 else 1
        sc = _dot_nt(k_ref[0, kb * kc:(kb + 1) * kc, v * LANES:(v + 1) * LANES], qm)
        s_bufs[h % 2][kb * kc:(kb + 1) * kc, :] = sc
        cm = jnp.max(sc.reshape(kc // ROW_TILE, ROW_TILE, tq), axis=0)
        return cm if m8 is None else jnp.maximum(m8, cm)

    def prob_chunk(h, kb, m):
        x = s_bufs[h % 2][kb * kc:(kb + 1) * kc, :] - m
        p_bufs[h % 2][kb * kc:(kb + 1) * kc, :] = jnp.exp2(x.astype(BF16))

    qm = masked_q(0)
    m8 = None
    for kb in range(n_chunks):
        m8 = score_chunk(0, qm, kb, m8)
    outs = []
    for h in range(n_heads):
        m = jnp.max(m8, axis=0, keepdims=True)
        nxt = h + 1 < n_heads
        if nxt:
            qm = masked_q(h + 1)
            m8 = None
        for kb in range(n_chunks):
            if nxt:
                m8 = score_chunk(h + 1, qm, kb, m8)
            prob_chunk(h, kb, m)
        c = h // HEADS_PER_KV
        va = jnp.concatenate([vt_ref[0, c * HEAD_DIM:(c + 1) * HEAD_DIM, :], ones], axis=0)
        ot = _dot(va, p_bufs[h % 2][...])
        outs.append(ot[:HEAD_DIM] / ot[HEAD_DIM:HEAD_DIM + 1])
        if h % 2 == 1:
            o_ref[0, :, (h // 2) * LANES:(h // 2 + 1) * LANES] = jnp.concatenate(outs, axis=0).T.astype(BF16)
            outs = []


def _grid_attn(qb, kb, vbt):
    bsz, seq, _ = qb.shape
    tq = TQ_GRID
    return pl.pallas_call(
        _grid_attn_body, grid=(bsz, seq // tq),
        in_specs=[pl.BlockSpec((1, tq, Q_W), lambda b, n: (b, n, 0)),
                  pl.BlockSpec((1, seq, 2 * KV_W), lambda b, n: (b, 0, 0)),
                  pl.BlockSpec((1, KV_W, seq), lambda b, n: (b, 0, 0))],
        out_specs=pl.BlockSpec((1, tq, Q_W), lambda b, n: (b, n, 0)),
        scratch_shapes=[pltpu.VMEM((seq, tq), F32), pltpu.VMEM((seq, tq), F32),
                        pltpu.VMEM((seq, tq), BF16), pltpu.VMEM((seq, tq), BF16)],
        out_shape=jax.ShapeDtypeStruct((bsz, seq, Q_W), BF16), name="grid_attn",
        compiler_params=pltpu.CompilerParams(
            dimension_semantics=("arbitrary", "arbitrary"), vmem_limit_bytes=VMEM_LIMIT_GRID_ATTN),
    )(qb, kb, vbt)


def _post_attn_body(x_ref, oa_ref, ob_ref, sga_ref, sgb_ref, g0_ref, b0_ref, wa_ref, wb_ref, wo_ref,
                    g1_ref, b1_ref, wr_ref, br_ref, h1t_ref, tr_ref, tw_ref, cnt_out_ref, cnt_ref):
    h0 = _ln(x_ref[0], g0_ref[...], b0_ref[...])
    out_a = _dot(oa_ref[0], wa_ref[...])
    out_b = _dot(ob_ref[0], wb_ref[...])
    merged = sga_ref[0].astype(F32) * out_a + sgb_ref[0].astype(F32) * out_b
    mix = _dot(merged.astype(BF16), wo_ref[...])
    h1 = _ln(DN_ALPHA * h0 + mix, g1_ref[...], b1_ref[...])
    tm = h1.shape[0]
    _store_row_tiles(h1t_ref.at[0], h1, tm)

    logits = (_dot_nt(wr_ref[...], h1.astype(BF16)) + br_ref[...])[:N_EXPERTS]
    sub = lax.broadcasted_iota(jnp.int32, (N_EXPERTS, tm), 0)
    cur = logits
    vals, idxs = [], []
    for _ in range(TOP_K):
        mv = jnp.max(cur, axis=0, keepdims=True)
        ix = jnp.min(jnp.where(cur == mv, sub, N_EXPERTS), axis=0, keepdims=True)
        vals.append(mv)
        idxs.append(ix)
        cur = jnp.where(sub == ix, -jnp.inf, cur)
    es = [jnp.exp(v - vals[0]) for v in vals]
    tot = es[0] + es[1] + es[2] + es[3]

    @pl.when((pl.program_id(0) == 0) & (pl.program_id(1) == 0))
    def _():
        cnt_ref[...] = jnp.zeros_like(cnt_ref)

    sel = jnp.zeros((N_EXPERTS, tm), F32)
    for kx in range(TOP_K):
        sel = sel + (sub == idxs[kx]).astype(F32)
    r_i = lax.broadcasted_iota(jnp.int32, (tm, tm), 0)
    c_i = lax.broadcasted_iota(jnp.int32, (tm, tm), 1)
    tri = (r_i < c_i).astype(BF16)
    rank = _dot(sel.astype(BF16), tri) + cnt_ref[:, 0:1]
    cnt_ref[...] = cnt_ref[...] + jnp.sum(sel, axis=1, keepdims=True)
    cnt_out_ref[...] = cnt_ref[...]

    rks = [jnp.sum(jnp.where(sub == ix, rank, 0.0), axis=0, keepdims=True).astype(jnp.int32) for ix in idxs]
    tr_ref[...] = jnp.concatenate(idxs + rks, axis=0)
    tw_t = jnp.concatenate([e / tot for e in es] + [jnp.zeros((LANES - TOP_K, tm), F32)], axis=0)
    tw_ref[0] = tw_t.T


def _post_attn(x, oa, ob, sga, sgb, g0, b0, wa, wb, wo, g1, b1, wr, br):
    bsz, seq, d = x.shape
    tm = TM_PROJ
    tok3 = lambda b, i: (b, i, 0)
    const = lambda b, i: (0, 0)
    full = lambda a: pl.BlockSpec(a.shape, const)
    return pl.pallas_call(
        _post_attn_body, grid=(bsz, seq // tm),
        in_specs=[pl.BlockSpec((1, tm, d), tok3),
                  pl.BlockSpec((1, tm, Q_W), tok3), pl.BlockSpec((1, tm, Q_W), tok3),
                  pl.BlockSpec((1, tm, d), tok3), pl.BlockSpec((1, tm, d), tok3),
                  full(g0), full(b0), full(wa), full(wb), full(wo), full(g1), full(b1),
                  full(wr), full(br)],
        out_specs=[pl.BlockSpec((1, tm * ROW_TILE, LANES), tok3),
                   pl.BlockSpec((2 * TOP_K, tm), lambda b, i: (0, b * (seq // tm) + i)),
                   pl.BlockSpec((1, tm, LANES), tok3), pl.BlockSpec((N_EXPERTS, LANES), const)],
        out_shape=[jax.ShapeDtypeStruct((bsz, seq * ROW_TILE, LANES), F32),
                   jax.ShapeDtypeStruct((2 * TOP_K, bsz * seq), jnp.int32),
                   jax.ShapeDtypeStruct((bsz, seq, LANES), F32),
                   jax.ShapeDtypeStruct((N_EXPERTS, LANES), F32)],
        scratch_shapes=[pltpu.VMEM((N_EXPERTS, LANES), F32)],
        name="post_attn",
        compiler_params=pltpu.CompilerParams(
            dimension_semantics=("arbitrary", "arbitrary"), vmem_limit_bytes=VMEM_LIMIT_PROJ),
    )(x, oa, ob, sga, sgb, g0, b0, wa, wb, wo, g1, b1, wr, br)


def _dispatch_body(dest_ref, pend_ref, padded_ref, nu_ref, h1t_ref, xs_hbm, zbuf, sem, zsem):
    tm = TM_DISPATCH
    zrows = BM_EXPERT * ROW_TILE
    n_blocks = xs_hbm.shape[0] // zrows

    @pl.when(pl.program_id(0) == 0)
    def _():
        zbuf[...] = jnp.zeros_like(zbuf)
        zero_wait = pltpu.make_async_copy(zbuf, xs_hbm.at[pl.ds(0, zrows)], zsem).wait
        for e in range(N_EXPERTS):
            @pl.when(padded_ref[e] > 0)
            def _():
                start = pl.multiple_of((pend_ref[e] - BM_EXPERT) * ROW_TILE, ROW_TILE)
                pltpu.make_async_copy(zbuf, xs_hbm.at[pl.ds(start, zrows)], zsem).start()
        for b in range(n_blocks - N_EXPERTS, n_blocks):
            @pl.when(b >= nu_ref[0])
            def _():
                pltpu.make_async_copy(zbuf, xs_hbm.at[pl.ds(b * zrows, zrows)], zsem).start()
        for e in range(N_EXPERTS):
            pl.when(padded_ref[e] > 0)(zero_wait)
        for b in range(n_blocks - N_EXPERTS, n_blocks):
            pl.when(b >= nu_ref[0])(zero_wait)

    for t in range(tm):
        for kx in range(TOP_K):
            d = dest_ref[kx, t]
            pltpu.make_async_copy(_tile(h1t_ref, t), _tile(xs_hbm, d), sem).start(priority=kx % 2)
    for _ in range(TOP_K):
        pltpu.make_async_copy(h1t_ref, xs_hbm.at[pl.ds(0, tm * ROW_TILE)], sem).wait()


def _dispatch(dest_km, pad_end, padded, n_used, h1t, n_rows):
    n_tok = h1t.shape[0] // ROW_TILE
    tm = TM_DISPATCH
    return pl.pallas_call(
        _dispatch_body, grid=(n_tok // tm,),
        in_specs=[pl.BlockSpec((TOP_K, tm), lambda i: (0, i), memory_space=pltpu.SMEM),
                  pl.BlockSpec(memory_space=pltpu.SMEM), pl.BlockSpec(memory_space=pltpu.SMEM),
                  pl.BlockSpec(memory_space=pltpu.SMEM),
                  pl.BlockSpec((tm * ROW_TILE, LANES), lambda i: (i, 0))],
        out_specs=pl.BlockSpec(memory_space=pl.ANY),
        out_shape=jax.ShapeDtypeStruct((n_rows * ROW_TILE, LANES), F32),
        scratch_shapes=[pltpu.VMEM((BM_EXPERT * ROW_TILE, LANES), F32),
                        pltpu.SemaphoreType.DMA(()), pltpu.SemaphoreType.DMA(())],
        name="dispatch",
        compiler_params=pltpu.CompilerParams(dimension_semantics=("arbitrary",)),
    )(dest_km, pad_end, padded, n_used, h1t)


def _experts_body(be_ref, nu_ref, nx_ref, nv_ref, xs_ref, bg_ref, bu_ref, bd_ref, wg_hbm, wu_hbm, wd_hbm, ys_ref,
                  stage, wg_s, wu_s, wd_s, wsem):
    i = pl.program_id(0)
    bm = BM_EXPERT
    used = i < nu_ref[0]
    prev = be_ref[jnp.maximum(i - 1, 0)]
    fresh = (i == 0) | (be_ref[i] != prev)

    def weight_copies(e):
        return [pltpu.make_async_copy(w_hbm.at[e], stage.at[n], wsem)
                for n, w_hbm in enumerate((wg_hbm, wu_hbm, wd_hbm))]

    @pl.when(i == 0)
    def _():
        for cp in weight_copies(be_ref[0]):
            cp.start()

    @pl.when(used & fresh)
    def _():
        for cp in weight_copies(be_ref[i]):
            cp.wait()
        wg_s[...] = stage[0].astype(BF16)
        wu_s[...] = stage[1].astype(BF16)
        wd_s[...] = stage[2].astype(BF16)

        @pl.when(nx_ref[i] >= 0)
        def _():
            for cp in weight_copies(nx_ref[i]):
                cp.start()

    def ffn(rows):
        xb = _load_row_tiles(xs_ref, rows).astype(BF16)
        d_ff = wg_s.shape[1]
        fc = d_ff // FFN_CHUNKS
        y = bd_ref[0]
        for ci in range(FFN_CHUNKS):
            cols = slice(ci * fc, (ci + 1) * fc)
            g = _dot(xb, wg_s[:, cols]) + bg_ref[0, :, cols]
            u = _dot(xb, wu_s[:, cols]) + bu_ref[0, :, cols]
            g = jnp.minimum(g, SWIGLU_LIMIT)
            u = jnp.clip(u, -SWIGLU_LIMIT, SWIGLU_LIMIT)
            act = g * jax.nn.sigmoid(SWIGLU_ALPHA * g) * (u + 1.0)
            y = y + _dot(act.astype(BF16), wd_s[cols, :])
        _store_row_tiles(ys_ref, y, rows)

    half = bm // 2
    full = used & (nv_ref[i] > half)

    @pl.when(full)
    def _():
        ffn(bm)

    @pl.when(used & jnp.logical_not(full))
    def _():
        ffn(half)
        ys_ref[pl.ds(half * ROW_TILE, half * ROW_TILE), :] = jnp.zeros((half * ROW_TILE, LANES), F32)

    @pl.when(jnp.logical_not(used))
    def _():
        ys_ref[...] = jnp.zeros_like(ys_ref)


def _experts(block_e, n_used, next_e, n_valid, xs, wg, bg, wu, bu, wd, bd):
    n_rows = xs.shape[0] // ROW_TILE
    bm = BM_EXPERT
    n_e, d, d_ff = wg.shape
    assert d == d_ff and wd.shape == wg.shape
    row = lambda i, be, nu, nx, nv: (jnp.minimum(i, nu[0] - 1), 0)
    exp3 = lambda i, be, nu, nx, nv: (be[jnp.minimum(i, nu[0] - 1)], 0, 0)
    any_spec = pl.BlockSpec(memory_space=pl.ANY)
    grid_spec = pltpu.PrefetchScalarGridSpec(
        num_scalar_prefetch=4, grid=(n_rows // bm,),
        in_specs=[pl.BlockSpec((bm * ROW_TILE, LANES), row),
                  pl.BlockSpec((1, 1, d_ff), exp3), pl.BlockSpec((1, 1, d_ff), exp3),
                  pl.BlockSpec((1, 1, d), exp3), any_spec, any_spec, any_spec],
        out_specs=pl.BlockSpec((bm * ROW_TILE, LANES), lambda i, be, nu, nx, nv: (i, 0)),
        scratch_shapes=[pltpu.VMEM((3, d, d_ff), F32),
                        pltpu.VMEM((d, d_ff), BF16), pltpu.VMEM((d, d_ff), BF16), pltpu.VMEM((d_ff, d), BF16),
                        pltpu.SemaphoreType.DMA(())])
    return pl.pallas_call(
        _experts_body, grid_spec=grid_spec,
        out_shape=jax.ShapeDtypeStruct((n_rows * ROW_TILE, LANES), F32), name="experts",
        compiler_params=pltpu.CompilerParams(
            dimension_semantics=("arbitrary",), vmem_limit_bytes=VMEM_LIMIT_EXPERTS),
    )(block_e, n_used, next_e, n_valid, xs, bg.reshape(n_e, 1, d_ff), bu.reshape(n_e, 1, d_ff), bd.reshape(n_e, 1, d),
      wg, wu, wd)


def _combine_body(dest_ref, dest_next_ref, h1t_ref, tw_ref, g2_ref, b2_ref, ys_hbm, out_ref, buf, sems):
    tm = TM_COMBINE
    i = pl.program_id(0)
    slot = i % 2

    def start_row(d_ref, s, t):
        for kx in range(TOP_K):
            d = d_ref[kx, t]
            pltpu.make_async_copy(_tile(ys_hbm, d), _tile(buf.at[s, kx], t), sems.at[s]).start(priority=kx % 2)

    def wait_tile(s):
        for kx in range(TOP_K):
            pltpu.make_async_copy(ys_hbm.at[pl.ds(0, tm * ROW_TILE)], buf.at[s, kx], sems.at[s]).wait()

    @pl.when(i == 0)
    def _():
        lax.fori_loop(0, tm, lambda t, c: (start_row(dest_ref, 0, t), c)[1], 0)

    wait_tile(slot)
    rc = COMBINE_ROW_CHUNK
    for c in range(tm // rc):
        for t in range(c * rc, (c + 1) * rc):
            start_row(dest_next_ref, 1 - slot, t)
        tw = tw_ref[c * rc:(c + 1) * rc, :]
        ffn = tw[:, 0:1] * _load_row_tiles(buf.at[slot, 0], rc, c * rc)
        for kx in range(1, TOP_K):
            ffn = ffn + tw[:, kx:kx + 1] * _load_row_tiles(buf.at[slot, kx], rc, c * rc)
        h1 = _load_row_tiles(h1t_ref, rc, c * rc)
        out_ref[c * rc:(c + 1) * rc, :] = _ln(DN_ALPHA * h1 + ffn, g2_ref[...], b2_ref[...])

    @pl.when(i == pl.num_programs(0) - 1)
    def _():
        wait_tile(1 - slot)


def _combine(dest_km, h1t, tw, g2, b2, ys):
    n_tok = h1t.shape[0] // ROW_TILE
    d = ROW_TILE * LANES
    tm = TM_COMBINE
    n_steps = n_tok // tm
    const = lambda i: (0, 0)
    return pl.pallas_call(
        _combine_body, grid=(n_steps,),
        in_specs=[pl.BlockSpec((TOP_K, tm), lambda i: (0, i), memory_space=pltpu.SMEM),
                  pl.BlockSpec((TOP_K, tm), lambda i: (0, jnp.minimum(i + 1, n_steps - 1)),
                               memory_space=pltpu.SMEM),
                  pl.BlockSpec((tm * ROW_TILE, LANES), lambda i: (i, 0)),
                  pl.BlockSpec((tm, LANES), lambda i: (i, 0)),
                  pl.BlockSpec((1, d), const), pl.BlockSpec((1, d), const),
                  pl.BlockSpec(memory_space=pl.ANY)],
        out_specs=pl.BlockSpec((tm, d), lambda i: (i, 0)),
        out_shape=jax.ShapeDtypeStruct((n_tok, d), F32),
        scratch_shapes=[pltpu.VMEM((2, TOP_K, tm * ROW_TILE, LANES), F32), pltpu.SemaphoreType.DMA((2,))],
        name="combine",
        compiler_params=pltpu.CompilerParams(
            dimension_semantics=("arbitrary",), vmem_limit_bytes=VMEM_LIMIT_COMBINE),
    )(dest_km, dest_km, h1t, tw, g2, b2, ys)


def _rope_tables(seq):
    t = np.arange(seq)
    row = (t // GRID_W).astype(np.float32)
    col = (t % GRID_W).astype(np.float32)
    half = HEAD_DIM // 2
    quarter = half // 2
    inv = (ROPE_THETA ** (-np.arange(quarter, dtype=np.float32) * np.float32(2.0 / half))).astype(np.float32)
    ang_r = row[:, None] * inv[None, :]
    ang_c = col[:, None] * inv[None, :]
    zeros = np.zeros_like(ang_r)
    cos = np.concatenate([np.cos(ang_r), np.cos(ang_r), np.cos(ang_c), np.cos(ang_c)], -1)
    s_lo = np.concatenate([-np.sin(ang_r), zeros, -np.sin(ang_c), zeros], -1)
    s_hi = np.concatenate([zeros, np.sin(ang_r), zeros, np.sin(ang_c)], -1)
    return tuple(jnp.asarray(np.tile(a, (1, LANES // HEAD_DIM)), F32) for a in (cos, s_lo, s_hi))


def _routing(top_i, rank, counts, bm):
    n_tok = top_i.shape[1]
    padded = (counts + bm - 1) // bm * bm
    pad_end = jnp.cumsum(padded)
    pad_start = pad_end - padded
    base = jnp.sum(jnp.where(top_i[:, :, None] == jnp.arange(N_EXPERTS, dtype=jnp.int32)[None, None, :],
                             pad_start[None, None, :], 0), axis=-1)
    dest = (base + rank).astype(jnp.int32)
    n_rows = n_tok * TOP_K + N_EXPERTS * bm
    n_blocks = n_rows // bm
    block_start = jnp.arange(n_blocks, dtype=jnp.int32) * bm
    block_e = jnp.minimum(jnp.sum((pad_end[None, :] <= block_start[:, None]).astype(jnp.int32), axis=1),
                          N_EXPERTS - 1).astype(jnp.int32)
    n_used = (pad_end[-1] // bm).astype(jnp.int32).reshape(1)
    e_ids = jnp.arange(N_EXPERTS, dtype=jnp.int32)
    later = (e_ids[None, :] > e_ids[:, None]) & (counts[None, :] > 0)
    next_of = jnp.min(jnp.where(later, e_ids[None, :], N_EXPERTS), axis=1)
    next_of = jnp.where(next_of == N_EXPERTS, -1, next_of).astype(jnp.int32)
    next_e = jnp.sum(jnp.where(block_e[:, None] == e_ids[None, :], next_of[None, :], 0), axis=1).astype(jnp.int32)
    start_of = jnp.sum(jnp.where(block_e[:, None] == e_ids[None, :], pad_start[None, :], 0), axis=1)
    count_of = jnp.sum(jnp.where(block_e[:, None] == e_ids[None, :], counts[None, :], 0), axis=1)
    n_valid = jnp.clip(count_of - (block_start - start_of), 0, bm).astype(jnp.int32)
    return (dest, block_e, n_used, next_e, n_valid, n_rows, pad_end.astype(jnp.int32),
            padded.astype(jnp.int32))


def kernel(x, ln0_g, ln0_b, w_in, a_sink, b_q_norm, b_k_norm, w_branch_a, w_branch_b, w_out,
           ln1_g, ln1_b, w_router, b_router, w_gate, b_gate, w_up, b_up, w_down, b_down,
           ln2_g, ln2_b):
    bsz, seq, d = x.shape
    assert w_in.shape[0] == DEPTH == 1
    assert seq % TM_PROJ == 0 and seq % TQ_GRID == 0 and seq == (seq // GRID_W) * GRID_W
    n_tok = bsz * seq
    row2 = lambda a: a.reshape(1, -1)

    w_perm = w_in[0].astype(BF16)
    head_id = np.arange(Q_W) // HEAD_DIM
    bd = jnp.asarray(head_id[:, None] == head_id[None, :], BF16)
    gq = jnp.tile(b_q_norm[0].astype(F32), N_HEADS).reshape(1, Q_W)
    gk = jnp.tile(b_k_norm[0].astype(F32), N_KV).reshape(1, KV_W)

    qa, ka, vat, qb, kb, vbt, sga, sgb = _in_proj(
        x, row2(ln0_g), row2(ln0_b), w_perm, bd, gq, gk, _rope_tables(seq))
    oa = _win_attn(a_sink[0].astype(F32), qa, ka, vat)
    ob = _grid_attn(qb, kb, vbt)

    wr = jnp.zeros((LANES, d), BF16).at[:N_EXPERTS].set(w_router[0].T.astype(BF16))
    br = jnp.zeros((LANES, 1), F32).at[:N_EXPERTS, 0].set(b_router[0])
    h1t, tr, tw, cnt = _post_attn(
        x, oa, ob, sga, sgb, row2(ln0_g), row2(ln0_b),
        w_branch_a[0].astype(BF16), w_branch_b[0].astype(BF16),
        w_out[0].astype(BF16), row2(ln1_g[0]), row2(ln1_b[0]), wr, br)
    h1t = h1t.reshape(n_tok * ROW_TILE, LANES)
    top_i, rank = tr[:TOP_K], tr[TOP_K:]
    tw = tw.reshape(n_tok, LANES)

    counts = cnt[:, 0].astype(jnp.int32)
    dest, block_e, n_used, next_e, n_valid, n_rows, pad_end, padded = _routing(top_i, rank, counts, BM_EXPERT)
    xs = _dispatch(dest, pad_end, padded, n_used, h1t, n_rows)
    ys = _experts(block_e, n_used, next_e, n_valid, xs, w_gate[0], b_gate[0], w_up[0], b_up[0], w_down[0], b_down[0])
    out = _combine(dest, h1t, tw, row2(ln2_g[0]), row2(ln2_b[0]), ys)
    return out.reshape(bsz, seq, d)
```
